```python
import jax, jax.numpy as jnp
from jax import lax
import numpy as np

D_MODEL = 1024
BATCH = 16
SEQ = 256
DEPTH = 1
DEC_BATCH = 8
DEC_SEQ = 1024
PAST_LEN = 256

GRID_W = 64
NORM_EPS = 1e-6
HEAD_R = 64
D_R = D_MODEL // 2
H_R = D_R // HEAD_R
DECAY_RANK = 64
AAA_RANK = 64
GATE_RANK = 128
GN_EPS = 64e-5
HEAD_DIM = 64
N_HEADS = (D_MODEL // 2) // HEAD_DIM
N_KV = N_HEADS // 4
GROUP = N_HEADS // N_KV
D_A = N_HEADS * HEAD_DIM
AXIS_DIM = HEAD_DIM // 2
ROPE_THETA = 10000.0
Q_BLOCK = 128
N_EXPERTS = 32
TOP_K = 4
D_FF = D_MODEL
SWIGLU_ALPHA = 1.702
SWIGLU_LIMIT = 7.0
MOE_BLOCK = 128
IN_SIZES = (D_R, D_R, D_R, DECAY_RANK, DECAY_RANK, AAA_RANK, AAA_RANK, GATE_RANK,
            D_A, N_KV * HEAD_DIM, N_KV * HEAD_DIM, D_MODEL, D_MODEL)
IN_COLS = sum(IN_SIZES)

kernel_name = "hybrid_rwkv7_gqa_moe_diffusion_step"


def rmsnorm(x, g):
    xf = x.astype(jnp.float32)
    y = xf * lax.rsqrt(jnp.mean(xf * xf, axis=-1, keepdims=True) + NORM_EPS)
    return (y * g.astype(jnp.float32)).astype(x.dtype)


def adaln(cond, w_ada, b_ada):
    mod = jax.nn.silu(cond) @ w_ada + b_ada
    return jnp.split(mod[:, None, :], 6, axis=-1)


def rope_2d(x):
    T = x.shape[1]
    rows = T // GRID_W
    row = jnp.repeat(jnp.arange(rows, dtype=jnp.float32), GRID_W)
    col = jnp.tile(jnp.arange(GRID_W, dtype=jnp.float32), rows)
    inv = ROPE_THETA ** (-jnp.arange(0, AXIS_DIM, 2, dtype=jnp.float32) / AXIS_DIM)

    def rot(xa, pos):
        ang = pos[:, None] * inv[None, :]
        cos = jnp.cos(ang)[None, :, None, :]
        sin = jnp.sin(ang)[None, :, None, :]
        x1, x2 = jnp.split(xa, 2, axis=-1)
        return jnp.concatenate([x1 * cos - x2 * sin, x2 * cos + x1 * sin], axis=-1)

    xr, xc = jnp.split(x.astype(jnp.float32), 2, axis=-1)
    return jnp.concatenate([rot(xr, row), rot(xc, col)], axis=-1).astype(x.dtype)


def block_attention(q, k, v):
    B, Sq, _, Dh = q.shape
    nb = Sq // Q_BLOCK
    scale = Dh ** -0.5
    qb = q.reshape(B, nb, Q_BLOCK, N_KV, GROUP, Dh).transpose(1, 0, 2, 3, 4, 5)

    def one_block(qblk):
        s = jnp.einsum("bqkgd,bskd->bkgqs", qblk, k).astype(jnp.float32) * scale
        p = jax.nn.softmax(s, axis=-1).astype(v.dtype)
        return jnp.einsum("bkgqs,bskd->bqkgd", p, v)

    o = lax.map(one_block, qb)
    return o.transpose(1, 0, 2, 3, 4, 5).reshape(B, Sq, N_HEADS * Dh)


def rwkv_scan(r, w, k, v, kk, a, s0, reverse):
    def step(S, xs):
        r_t, w_t, k_t, v_t, kk_t, a_t = xs
        sa = jnp.einsum("bhvk,bhk->bhv", S, kk_t)
        S = (S * w_t[:, :, None, :] - sa[..., None] * (kk_t * a_t)[:, :, None, :]
             + v_t[..., None] * k_t[:, :, None, :])
        return S, jnp.einsum("bhvk,bhk->bhv", S, r_t)

    xs = tuple(jnp.swapaxes(t, 0, 1) for t in (r, w, k, v, kk, a))
    s_fin, ys = lax.scan(step, s0.astype(jnp.float32), xs, reverse=reverse)
    return jnp.swapaxes(ys, 0, 1), s_fin


def rwkv7_bidir(r, k, v, wd, ad, gd, s0, lp):
    B, T, _ = r.shape

    def heads(t):
        return t.astype(jnp.float32).reshape(B, T, H_R, HEAD_R)

    rf, kf, vf = heads(r), heads(k), heads(v)
    kk = heads(k * lp["rw_kk"])
    kk = kk * lax.rsqrt(jnp.sum(kk * kk, axis=-1, keepdims=True) + 1e-12)
    k_a = lp["rw_ka"].astype(jnp.float32).reshape(H_R, HEAD_R)
    r_k = lp["rw_rk"].astype(jnp.float32)
    ys, bonus, finals = [], [], []
    for d in range(2):
        w_log = -jax.nn.softplus(-(lp["rw_w0"][d] + jnp.tanh(wd[d]) @ lp["rw_w2"][d])) - 0.5
        decay = jnp.exp(-jnp.exp(heads(w_log)))
        a = jax.nn.sigmoid(heads(lp["rw_a0"][d] + ad[d] @ lp["rw_a2"][d]))
        kd = kf * (1.0 + (a - 1.0) * k_a)
        y, s_fin = rwkv_scan(rf, decay, kd, vf, kk, a, s0[d], reverse=(d == 1))
        ys.append(y)
        bonus.append(jnp.sum(rf * kd * r_k, axis=-1, keepdims=True) * vf)
        finals.append(s_fin)
    y = ys[0] + ys[1]
    mu = jnp.mean(y, axis=-1, keepdims=True)
    var = jnp.mean(jnp.square(y - mu), axis=-1, keepdims=True)
    yn = ((y - mu) * lax.rsqrt(var + GN_EPS)).reshape(B, T, D_R)
    yn = yn * lp["rw_ln_w"] + lp["rw_ln_b"] + (bonus[0] + bonus[1]).reshape(B, T, D_R)
    g = jax.nn.sigmoid(gd) @ lp["rw_g2"]
    return (yn * g).astype(r.dtype), finals[0], finals[1]


def token_mixer(h, lp, s0_f, s0_b, ctx_k, ctx_v):
    B, T, _ = h.shape
    z = h @ lp["w_in"]
    offs = np.cumsum(IN_SIZES)[:-1].tolist()
    (r, k, v, wd_f, wd_b, ad_f, ad_b, gd, q, ka, va, gate_r, gate_a) = jnp.split(z, offs, axis=-1)
    y_r, s_f, s_b = rwkv7_bidir(r, k, v, (wd_f, wd_b), (ad_f, ad_b), gd, (s0_f, s0_b), lp)
    qh = rmsnorm(q.reshape(B, T, N_HEADS, HEAD_DIM), lp["q_norm"])
    kh = rmsnorm(ka.reshape(B, T, N_KV, HEAD_DIM), lp["k_norm"])
    vh = va.reshape(B, T, N_KV, HEAD_DIM)
    if ctx_k is None:
        y_a = block_attention(qh, kh, vh)
    else:
        k_all = jnp.concatenate([rope_2d(kh), ctx_k.astype(kh.dtype)], axis=1)
        v_all = jnp.concatenate([vh, ctx_v.astype(vh.dtype)], axis=1)
        y_a = block_attention(rope_2d(qh), k_all, v_all)
    merged = (jax.nn.sigmoid(gate_r) * (y_r @ lp["w_up_r"])
              + jax.nn.sigmoid(gate_a) * (y_a @ lp["w_up_a"]))
    return merged @ lp["w_out"], kh, vh, s_f, s_b


def routed_ffn(h, lp):
    n, d_model = h.shape
    nk = n * TOP_K
    logits = (h @ lp["w_router"] + lp["b_router"]).astype(jnp.float32)
    top_val, top_idx = lax.top_k(logits, TOP_K)
    gates = jax.nn.softmax(top_val, axis=-1)
    flat_e = top_idx.reshape(nk).astype(jnp.int32)
    flat_t = jnp.arange(nk, dtype=jnp.int32) // TOP_K
    flat_g = gates.reshape(nk)
    order = jnp.argsort(flat_e)
    e_s, t_s, g_s = flat_e[order], flat_t[order], flat_g[order]
    counts = jnp.bincount(flat_e, length=N_EXPERTS)
    start = jnp.cumsum(counts) - counts
    padded = (counts + MOE_BLOCK - 1) // MOE_BLOCK * MOE_BLOCK
    pstart = jnp.cumsum(padded) - padded
    dest = pstart[e_s] + jnp.arange(nk, dtype=jnp.int32) - start[e_s]
    n_blocks = -(-nk // MOE_BLOCK) + N_EXPERTS
    cap = n_blocks * MOE_BLOCK
    slot_t = jnp.zeros((cap,), jnp.int32).at[dest].set(t_s)
    slot_g = jnp.zeros((cap,), jnp.float32).at[dest].set(g_s)
    slot_e = jnp.full((cap,), N_EXPERTS - 1, jnp.int32).at[dest].set(e_s)
    block_e = slot_e[::MOE_BLOCK]
    xs = h[slot_t].reshape(n_blocks, MOE_BLOCK, d_model)
    w1, b1, w2, b2 = lp["w_moe_in"], lp["b_moe_in"], lp["w_moe_out"], lp["b_moe_out"]

    def run_block(args):
        xb, e = args
        hb = xb @ w1[e] + b1[e]
        glu, lin = jnp.split(hb, 2, axis=-1)
        glu = jnp.minimum(glu, SWIGLU_LIMIT)
        lin = jnp.clip(lin, -SWIGLU_LIMIT, SWIGLU_LIMIT)
        act = glu * jax.nn.sigmoid(SWIGLU_ALPHA * glu) * (lin + 1.0)
        return act @ w2[e] + b2[e]

    ys = lax.map(run_block, (xs, block_e)).reshape(cap, d_model)
    return jnp.zeros_like(h).at[slot_t].add(ys * slot_g[:, None].astype(ys.dtype))


def trunk_layer(x, mods, lp, s0_f, s0_b, ctx_k, ctx_v):
    sh1, sc1, g1, sh2, sc2, g2 = mods
    h = rmsnorm(x, lp["norm1"]) * (1.0 + sc1) + sh1
    mix, k_c, v_c, s_f, s_b = token_mixer(h, lp, s0_f, s0_b, ctx_k, ctx_v)
    x = x + g1 * mix
    h = rmsnorm(x, lp["norm2"]) * (1.0 + sc2) + sh2
    B, T, D = h.shape
    x = x + g2 * routed_ffn(h.reshape(B * T, D), lp).reshape(B, T, D)
    return x, k_c, v_c, s_f, s_b


def setup_inputs(seed: int = 0) -> dict:
    key = jax.random.key(seed)
    keys = list(jax.random.split(key, 40))

    def nrm(shape, scale):
        return scale * jax.random.normal(keys.pop(), shape, jnp.float32)

    def gain(shape):
        return 1.0 + nrm(shape, 0.05)

    L = DEPTH
    return {
        "x_prompt": nrm((BATCH, SEQ, D_MODEL), 1.0),
        "x_sample": nrm((DEC_BATCH, DEC_SEQ, D_MODEL), 1.0),
        "cache_k": nrm((DEC_BATCH, L, PAST_LEN, N_KV, HEAD_DIM), 1.0),
        "cache_v": nrm((DEC_BATCH, L, PAST_LEN, N_KV, HEAD_DIM), 1.0),
        "state_rwkv_fwd": nrm((DEC_BATCH, L, H_R, HEAD_R, HEAD_R), 0.5),
        "state_rwkv_bwd": nrm((DEC_BATCH, L, H_R, HEAD_R, HEAD_R), 0.5),
        "c": nrm((DEC_BATCH, D_MODEL), 1.0),
        "c_ctx": nrm((D_MODEL,), 1.0),
        "w_ada": nrm((L, D_MODEL, 6 * D_MODEL), 0.5 * D_MODEL ** -0.5),
        "b_ada": nrm((L, 6 * D_MODEL), 0.02),
        "norm1": gain((L, D_MODEL)),
        "norm2": gain((L, D_MODEL)),
        "w_in": nrm((L, D_MODEL, IN_COLS), D_MODEL ** -0.5),
        "rw_w0": jax.random.uniform(keys.pop(), (L, 2, D_R), jnp.float32, -6.0, 1.0),
        "rw_w2": nrm((L, 2, DECAY_RANK, D_R), 0.5 * DECAY_RANK ** -0.5),
        "rw_a0": nrm((L, 2, D_R), 0.5),
        "rw_a2": nrm((L, 2, AAA_RANK, D_R), 0.5 * AAA_RANK ** -0.5),
        "rw_g2": nrm((L, GATE_RANK, D_R), GATE_RANK ** -0.5),
        "rw_kk": 0.85 + nrm((L, D_R), 0.05),
        "rw_ka": gain((L, D_R)),
        "rw_rk": nrm((L, H_R, HEAD_R), 0.1),
        "rw_ln_w": gain((L, D_R)),
        "rw_ln_b": nrm((L, D_R), 0.02),
        "q_norm": gain((L, HEAD_DIM)),
        "k_norm": gain((L, HEAD_DIM)),
        "w_up_r": nrm((L, D_R, D_MODEL), D_R ** -0.5),
        "w_up_a": nrm((L, D_A, D_MODEL), D_A ** -0.5),
        "w_out": nrm((L, D_MODEL, D_MODEL), D_MODEL ** -0.5),
        "w_router": nrm((L, D_MODEL, N_EXPERTS), D_MODEL ** -0.5),
        "b_router": nrm((L, N_EXPERTS), 0.01),
        "w_moe_in": nrm((L, N_EXPERTS, D_MODEL, 2 * D_FF), D_MODEL ** -0.5),
        "b_moe_in": nrm((L, N_EXPERTS, 2 * D_FF), 0.02),
        "w_moe_out": nrm((L, N_EXPERTS, D_FF, D_MODEL), D_FF ** -0.5),
        "b_moe_out": nrm((L, N_EXPERTS, D_MODEL), 0.02),
        "norm_f": gain((D_MODEL,)),
    }


def reference(x_prompt, x_sample, cache_k, cache_v, state_rwkv_fwd, state_rwkv_bwd, c, c_ctx,
              w_ada, b_ada, norm1, norm2, w_in, rw_w0, rw_w2, rw_a0, rw_a2, rw_g2, rw_kk, rw_ka,
              rw_rk, rw_ln_w, rw_ln_b, q_norm, k_norm, w_up_r, w_up_a, w_out, w_router, b_router,
              w_moe_in, b_moe_in, w_moe_out, b_moe_out, norm_f):
    x_ctx, x_lat = x_prompt, x_sample
    new_k, new_v, new_sf, new_sb = [], [], [], []
    for l in range(DEPTH):
        lp = dict(norm1=norm1[l], norm2=norm2[l], w_in=w_in[l], rw_w0=rw_w0[l], rw_w2=rw_w2[l],
                  rw_a0=rw_a0[l], rw_a2=rw_a2[l], rw_g2=rw_g2[l], rw_kk=rw_kk[l], rw_ka=rw_ka[l],
                  rw_rk=rw_rk[l], rw_ln_w=rw_ln_w[l], rw_ln_b=rw_ln_b[l], q_norm=q_norm[l],
                  k_norm=k_norm[l], w_up_r=w_up_r[l], w_up_a=w_up_a[l], w_out=w_out[l],
                  w_router=w_router[l], b_router=b_router[l], w_moe_in=w_moe_in[l],
                  b_moe_in=b_moe_in[l], w_moe_out=w_moe_out[l], b_moe_out=b_moe_out[l])
        mods_ctx = adaln(c_ctx[None, :], w_ada[l], b_ada[l])
        mods_lat = adaln(c, w_ada[l], b_ada[l])
        zeros = jnp.zeros((x_ctx.shape[0], H_R, HEAD_R, HEAD_R), jnp.float32)
        x_ctx, k_l, v_l, sf_l, sb_l = trunk_layer(x_ctx, mods_ctx, lp, zeros, zeros, None, None)
        x_lat, _, _, _, _ = trunk_layer(x_lat, mods_lat, lp, state_rwkv_fwd[:, l], state_rwkv_bwd[:, l],
                                        cache_k[:, l], cache_v[:, l])
        new_k.append(k_l)
        new_v.append(v_l)
        new_sf.append(sf_l)
        new_sb.append(sb_l)
    y_prompt = rmsnorm(x_ctx, norm_f)
    y_sample = rmsnorm(x_lat, norm_f)
    new_cache_k = jnp.stack(new_k, axis=1)
    new_cache_v = jnp.stack(new_v, axis=1)
    new_state_fwd = jnp.stack(new_sf, axis=1)
    new_state_bwd = jnp.stack(new_sb, axis=1)
    return (y_prompt, y_sample, new_cache_k, new_cache_v, new_state_fwd, new_state_bwd)
```

```python
import functools

import numpy as np
import jax
import jax.numpy as jnp
from jax import lax
from jax.experimental import pallas as pl
from jax.experimental.pallas import tpu as pltpu

F32 = jnp.float32
BF16 = jnp.bfloat16

D_MODEL = 1024
GRID_W = 64
NORM_EPS = 1e-6
HEAD = 64
H_R = 8
D_R = H_R * HEAD
DECAY_RANK = 64
AAA_RANK = 64
GATE_RANK = 128
GN_EPS = 64e-5
N_HEADS = 8
N_KV = 2
D_A = N_HEADS * HEAD
D_KV = N_KV * HEAD
AXIS_DIM = HEAD // 2
ROPE_THETA = 10000.0
N_EXPERTS = 32
TOP_K = 4
D_FF = D_MODEL
SWIGLU_ALPHA = 1.702
SWIGLU_LIMIT = 7.0
LORA_COLS = 2 * DECAY_RANK + 2 * AAA_RANK + GATE_RANK
IN_SPLITS = (("r", D_R), ("k", D_R), ("v", D_R), ("lora", LORA_COLS), ("q", D_A),
             ("ka", D_KV), ("va", D_KV), ("gate_r", D_MODEL), ("gate_a", D_MODEL))

LANES = 128
TM = 256
CHUNK = 64
MOE_ROWS = 256
VMEM_LIMIT = 56 * 1024 * 1024


def _cparams(*sem):
    return pltpu.CompilerParams(dimension_semantics=sem, vmem_limit_bytes=VMEM_LIMIT)


def _dot(a, b):
    return jnp.dot(a, b, preferred_element_type=F32)


def _dot_nt(a, b):
    return lax.dot_general(a, b, (((1,), (1,)), ((), ())), preferred_element_type=F32)


def _dot_tn(a, b):
    return lax.dot_general(a, b, (((0,), (0,)), ((), ())), preferred_element_type=F32)


def _split2(x):
    hi = x.astype(BF16)
    lo = (x - hi.astype(F32)).astype(BF16)
    return hi, lo


def _split3(x):
    x1 = x.astype(BF16)
    r1 = x - x1.astype(F32)
    x2 = r1.astype(BF16)
    x3 = (r1 - x2.astype(F32)).astype(BF16)
    return x1, x2, x3


def _head_sum(x, ones_bd):
    hi, lo = _split2(x)
    return _dot(hi, ones_bd) + _dot(lo, ones_bd)


def _sigmoid(x):
    return 1.0 / (1.0 + jnp.exp(-x))


def _adaln_kernel(c_ref, w_ref, b_ref, o_ref):
    c = c_ref[...]
    s = (c * _sigmoid(c)).astype(BF16)
    o_ref[...] = _dot(s, w_ref[...].astype(BF16)) + b_ref[...]


def _adaln(cond, w_ada, b_ada):
    rows, d = cond.shape
    cols = w_ada.shape[1]
    tn = 512
    return pl.pallas_call(
        _adaln_kernel,
        out_shape=jax.ShapeDtypeStruct((rows, cols), F32),
        grid=(cols // tn,),
        in_specs=[pl.BlockSpec((rows, d), lambda j: (0, 0)),
                  pl.BlockSpec((d, tn), lambda j: (0, j)),
                  pl.BlockSpec((1, tn), lambda j: (0, j))],
        out_specs=pl.BlockSpec((rows, tn), lambda j: (0, j)),
        compiler_params=_cparams("arbitrary"),
        name="adaln",
    )(cond, w_ada, b_ada.reshape(1, cols))


def _proj_in_kernel(x_ref, mod_ref, n1_ref, w_ref, *out_refs):
    x = x_ref[...]
    ms = jnp.mean(x * x, axis=-1, keepdims=True)
    hn = x * lax.rsqrt(ms + NORM_EPS) * n1_ref[...]
    mod = mod_ref[0]
    sh1 = mod[:, 0:D_MODEL]
    sc1 = mod[:, D_MODEL:2 * D_MODEL]
    h = (hn * (1.0 + sc1) + sh1).astype(BF16)
    c0 = 0
    for ref, (_, width) in zip(out_refs, IN_SPLITS):
        ref[...] = _dot(h, w_ref[:, c0:c0 + width])
        c0 += width


def _proj_in(x_all, mods3, norm1, w_in_bf, mod_row):
    n, d = x_all.shape
    cols = w_in_bf.shape[1]
    row = lambda i: (i, 0)
    return pl.pallas_call(
        _proj_in_kernel,
        out_shape=[jax.ShapeDtypeStruct((n, width), F32) for _, width in IN_SPLITS],
        grid=(n // TM,),
        in_specs=[pl.BlockSpec((TM, d), row),
                  pl.BlockSpec((1, 1, 6 * d), lambda i: (mod_row(i), 0, 0)),
                  pl.BlockSpec((1, d), lambda i: (0, 0)),
                  pl.BlockSpec((d, cols), lambda i: (0, 0))],
        out_specs=[pl.BlockSpec((TM, width), row) for _, width in IN_SPLITS],
        compiler_params=_cparams("arbitrary"),
        name="proj_in",
    )(x_all, mods3, norm1.reshape(1, d), w_in_bf)


def _prep_kernel(r_ref, k_ref, v_ref, lora_ref, q_ref, ka_ref, cos_ref, sin_ref,
                 w0_ref, w2_ref, a0_ref, a2_ref, g2_ref, kkw_ref, kaw_ref, rk_ref,
                 qn_ref, kn_ref, ones_ref,
                 kk_o, lw_o, a_o, bonus_o, g_o, qr_o, kn_o, kr_o):
    ones = ones_ref[...]
    r = r_ref[...]
    k = k_ref[...]
    v = v_ref[...]
    lora = lora_ref[...]
    kk = k * kkw_ref[...]
    kk = kk * lax.rsqrt(_head_sum(kk * kk, ones) + 1e-12)
    kk_o[...] = kk
    th = jnp.tanh(lora[:, 0:LANES]).astype(BF16)
    al = lora[:, LANES:2 * LANES].astype(BF16)
    kd_sum = jnp.zeros_like(k)
    for d in range(2):
        u = w0_ref[d:d + 1, :] + _dot(th, w2_ref[d])
        w_log = -(jnp.maximum(-u, 0.0) + jnp.log1p(jnp.exp(-jnp.abs(u)))) - 0.5
        lw_o[d] = -jnp.exp(w_log)
        a = _sigmoid(a0_ref[d:d + 1, :] + _dot(al, a2_ref[d]))
        a_o[d] = a
        kd_sum = kd_sum + k * (1.0 + (a - 1.0) * kaw_ref[...])
    bonus_o[...] = _head_sum(r * kd_sum * rk_ref[...], ones) * v
    g_o[...] = _dot(_sigmoid(lora[:, 2 * LANES:3 * LANES]).astype(BF16), g2_ref[...])
    q = q_ref[...]
    qn = q * lax.rsqrt(_head_sum(q * q, ones) * (1.0 / HEAD) + NORM_EPS) * qn_ref[...]
    ka = ka_ref[...]
    ones_kv = ones[0:D_KV, 0:D_KV]
    kn = ka * lax.rsqrt(_head_sum(ka * ka, ones_kv) * (1.0 / HEAD) + NORM_EPS) * kn_ref[...]
    kn_o[...] = kn
    cos = cos_ref[...]
    sin = sin_ref[...]
    half = AXIS_DIM // 2

    def rope(t, cos_t, sin_t):
        width = t.shape[1]
        lane = lax.broadcasted_iota(jnp.int32, t.shape, 1)
        first = (lane % AXIS_DIM) < half
        swapped = jnp.where(first, pltpu.roll(t, width - half, 1), pltpu.roll(t, half, 1))
        return t * cos_t + swapped * sin_t

    cos4 = jnp.concatenate([cos] * (D_A // D_KV), axis=1)
    sin4 = jnp.concatenate([sin] * (D_A // D_KV), axis=1)
    qr_o[...] = (rope(qn, cos4, sin4) * (HEAD ** -0.5)).astype(BF16)
    kr_o[...] = rope(kn, cos, sin).astype(BF16)


def _prep(z, cos_tab, sin_tab, tab_row, p):
    n = z["r"].shape[0]
    row = lambda i: (i, 0)
    full = lambda *shape: pl.BlockSpec(shape, lambda i: (0,) * len(shape))
    tok = lambda width: pl.BlockSpec((TM, width), row)
    dirtok = pl.BlockSpec((2, TM, D_R), lambda i: (0, i, 0))
    tab = pl.BlockSpec((TM, D_KV), lambda i: (tab_row(i), 0))
    return pl.pallas_call(
        _prep_kernel,
        out_shape=[jax.ShapeDtypeStruct((n, D_R), F32),
                   jax.ShapeDtypeStruct((2, n, D_R), F32),
                   jax.ShapeDtypeStruct((2, n, D_R), F32),
                   jax.ShapeDtypeStruct((n, D_R), F32),
                   jax.ShapeDtypeStruct((n, D_R), F32),
                   jax.ShapeDtypeStruct((n, D_A), BF16),
                   jax.ShapeDtypeStruct((n, D_KV), F32),
                   jax.ShapeDtypeStruct((n, D_KV), BF16)],
        grid=(n // TM,),
        in_specs=[tok(D_R), tok(D_R), tok(D_R), tok(LORA_COLS), tok(D_A), tok(D_KV), tab, tab,
                  full(2, D_R), full(2, LANES, D_R), full(2, D_R), full(2, LANES, D_R),
                  full(GATE_RANK, D_R), full(1, D_R), full(1, D_R), full(1, D_R),
                  full(1, D_A), full(1, D_KV), full(D_R, D_R)],
        out_specs=[tok(D_R), dirtok, dirtok, tok(D_R), tok(D_R), tok(D_A), tok(D_KV), tok(D_KV)],
        compiler_params=_cparams("arbitrary"),
        name="mixer_prep",
    )(z["r"], z["k"], z["v"], z["lora"], z["q"], z["ka"], cos_tab, sin_tab,
      p["w0"], p["w2"], p["a0"], p["a2"], p["g2"], p["kkw"], p["kaw"], p["rk"],
      p["qn"], p["kn"], p["ones"])


def _scan_kernel(blk_ref, seq_ref, first_ref, last_ref,
                 r_ref, k_ref, v_ref, kk_ref, lw_ref, a_ref, kaw_ref, s0_ref,
                 y_ref, sfin_ref, state):
    d = pl.program_id(0)
    step = pl.program_id(1)
    C = CHUNK
    pairs = D_R // LANES

    lane = lax.broadcasted_iota(jnp.int32, (1, LANES), 1)
    low = lane < HEAD

    @pl.when(first_ref[step] == 1)
    def _():
        zero = jnp.zeros((HEAD, HEAD), F32)
        for p in range(pairs):
            top = jnp.concatenate([s0_ref[0, 0, 2 * p], zero], axis=1)
            bot = jnp.concatenate([zero, s0_ref[0, 0, 2 * p + 1]], axis=1)
            state[p] = jnp.concatenate([top, bot], axis=0)

    sign = 1 - 2 * d
    ti = lax.broadcasted_iota(jnp.int32, (C, C), 0)
    si = lax.broadcasted_iota(jnp.int32, (C, C), 1)
    tri = jnp.where((ti - si) * sign >= 0, 1.0, 0.0).astype(BF16)
    t2 = lax.broadcasted_iota(jnp.int32, (C, LANES), 0)
    s2 = lax.broadcasted_iota(jnp.int32, (C, LANES), 1) % C
    ahead = (t2 - s2) * sign
    strict = ahead > 0
    incl = ahead >= 0
    eye2 = jnp.where(ahead == 0, 1.0, 0.0)
    r2 = lax.broadcasted_iota(jnp.int32, (LANES, LANES), 0)
    c2 = lax.broadcasted_iota(jnp.int32, (LANES, LANES), 1)
    same_head = (r2 // HEAD) == (c2 // HEAD)

    r = r_ref[...]
    k = k_ref[...]
    v = v_ref[...]
    kk = kk_ref[...]
    lw = lw_ref[0]
    a = a_ref[0]
    kd = k * (1.0 + (a - 1.0) * kaw_ref[...])
    b = kk * a

    l1, l2, l3 = _split3(lw)
    L = _dot(tri, l1) + _dot(tri, l2) + _dot(tri, l3)
    Ltot = jnp.sum(lw, axis=0, keepdims=True)
    e_incl = jnp.exp(L)
    e_prev = jnp.exp(L - lw)
    e_inv = jnp.exp(-L)
    e_rest = jnp.exp(Ltot - L)
    g_tot = jnp.exp(Ltot)
    Qk = kk * e_prev
    Qr = r * e_incl
    Kb = b * e_inv
    Kk = kd * e_inv
    Kb_end = b * e_rest
    Kk_end = kd * e_rest

    def stack(x):
        zero = jnp.zeros_like(x)
        return jnp.concatenate([jnp.where(low, x, zero), jnp.where(low, zero, x)], axis=0)

    for p in range(pairs):
        sl = slice(p * LANES, (p + 1) * LANES)
        Qk_p, Qr_p, v_p = Qk[:, sl], Qr[:, sl], v[:, sl]
        v_st = stack(v_p.astype(BF16))
        qq = jnp.concatenate([Qk_p, Qr_p], axis=0).astype(BF16)
        kst = jnp.concatenate([stack(Kb[:, sl].astype(BF16)), stack(Kk[:, sl].astype(BF16))], axis=0)
        m = _dot_nt(qq, kst)
        A = jnp.where(strict, m[0:C, 0:LANES], 0.0)
        Bm = jnp.where(strict, m[0:C, LANES:2 * LANES], 0.0).astype(BF16)
        Pb = jnp.where(incl, m[C:2 * C, 0:LANES], 0.0).astype(BF16)
        Pk = jnp.where(incl, m[C:2 * C, LANES:2 * LANES], 0.0).astype(BF16)
        T = eye2 - A
        Ap = A.astype(BF16)
        n = 1
        while 2 * n < C:
            Ap = _dot(Ap, stack(Ap)).astype(BF16)
            T = T + _dot(T.astype(BF16), stack(Ap))
            n *= 2
        Tb = T.astype(BF16)
        BV = _dot(Bm, v_st)
        WU = _dot(Tb, jnp.concatenate([stack(Qk_p.astype(BF16)), stack(BV.astype(BF16))], axis=1))
        Wq = WU[:, 0:LANES]
        Uv = WU[:, LANES:2 * LANES]
        PW = _dot(Pb, jnp.concatenate([stack(Wq.astype(BF16)), stack(Uv.astype(BF16))], axis=1))
        Yq = Qr_p - PW[:, 0:LANES]
        Yv = _dot(Pk, v_st) - PW[:, LANES:2 * LANES]
        Z0 = state[p]
        UY = _dot_nt(jnp.concatenate([Wq, Yq], axis=0).astype(BF16), Z0.astype(BF16))
        U = UY[0:C] + Uv
        y_ref[0, :, sl] = UY[C:2 * C] + Yv
        lhs = jnp.concatenate([v_p, U], axis=0).astype(BF16)
        rhs = jnp.concatenate([Kk_end[:, sl], -Kb_end[:, sl]], axis=0).astype(BF16)
        upd = _dot_tn(lhs, rhs)
        state[p] = Z0 * g_tot[:, sl] + jnp.where(same_head, upd, 0.0)

    @pl.when(last_ref[step] == 1)
    def _():
        for p in range(pairs):
            z = state[p]
            sfin_ref[0, 0, 2 * p] = z[0:HEAD, 0:HEAD]
            sfin_ref[0, 0, 2 * p + 1] = z[HEAD:2 * HEAD, HEAD:2 * HEAD]


def _rwkv_scan(r, k, v, kk, lw, a, kaw, s0, seq_lens):
    n = r.shape[0]
    blk = [[], []]
    seq, first, last = [], [], []
    base = 0
    for i, t_len in enumerate(seq_lens):
        nch = t_len // CHUNK
        blk[0] += [base + cc for cc in range(nch)]
        blk[1] += [base + nch - 1 - cc for cc in range(nch)]
        seq += [i] * nch
        first += [1] + [0] * (nch - 1)
        last += [0] * (nch - 1) + [1]
        base += nch
    steps = len(seq)
    tabs = [jnp.asarray(np.array(t, np.int32)) for t in (blk, seq, first, last)]
    tok = pl.BlockSpec((CHUNK, D_R), lambda d, s, blk, *_: (blk[d, s], 0))
    dirtok = pl.BlockSpec((1, CHUNK, D_R), lambda d, s, blk, *_: (d, blk[d, s], 0))
    st = pl.BlockSpec((1, 1, H_R, HEAD, HEAD), lambda d, s, blk, seq, *_: (d, seq[s], 0, 0, 0))
    grid_spec = pltpu.PrefetchScalarGridSpec(
        num_scalar_prefetch=4,
        grid=(2, steps),
        in_specs=[tok, tok, tok, tok, dirtok, dirtok,
                  pl.BlockSpec((1, D_R), lambda d, s, *_: (0, 0)), st],
        out_specs=[dirtok, st],
        scratch_shapes=[pltpu.VMEM((D_R // LANES, LANES, LANES), F32)])
    y, sfin = pl.pallas_call(
        _scan_kernel,
        out_shape=[jax.ShapeDtypeStruct((2, n, D_R), F32),
                   jax.ShapeDtypeStruct((2, len(seq_lens), H_R, HEAD, HEAD), F32)],
        grid_spec=grid_spec,
        compiler_params=_cparams("arbitrary", "arbitrary"),
        name="rwkv_scan",
    )(*tabs, r, k, v, kk, lw, a, kaw, s0)
    return y, sfin


def _attn_kernel(*refs, tq, with_cache):
    if with_cache:
        q_ref, k_ref, v_ref, ck_ref, cv_ref, o_ref = refs
        kx = jnp.concatenate([k_ref[...], ck_ref[0].astype(BF16)], axis=0)
        vx = jnp.concatenate([v_ref[...], cv_ref[0]], axis=0).astype(BF16)
    else:
        q_ref, k_ref, v_ref, o_ref = refs
        kx = k_ref[...]
        vx = v_ref[...].astype(BF16)
    lane = lax.broadcasted_iota(jnp.int32, (1, D_KV), 1)
    low = lane < HEAD
    k_sw = pltpu.roll(kx.astype(F32), HEAD, 1).astype(BF16)
    v_sw = pltpu.roll(vx.astype(F32), HEAD, 1).astype(BF16)
    group = N_HEADS // N_KV
    for p in range(D_A // LANES):
        g = (2 * p) // group
        keep = low if g == 0 else jnp.logical_not(low)
        kd = jnp.where(keep, kx, k_sw)
        vd = jnp.where(keep, vx, v_sw)
        qp = q_ref[:, p * LANES:(p + 1) * LANES]
        zero = jnp.zeros_like(qp)
        qs = jnp.concatenate([jnp.where(low, qp, zero), jnp.where(low, zero, qp)], axis=0)
        s = _dot_nt(qs, kd)
        mx = jnp.max(s, axis=-1, keepdims=True)
        e = jnp.exp(s - mx)
        den = jnp.sum(e, axis=-1, keepdims=True)
        o = _dot(e.astype(BF16), vd) / den
        o_ref[:, p * LANES:(p + 1) * LANES] = jnp.where(low, o[0:tq], o[tq:2 * tq]).astype(BF16)


def _attention(qr, kr, va, row0, batch, t_len, cache_k=None, cache_v=None):
    tq = min(t_len, 256)
    nq = t_len // tq
    qblk0 = row0 // tq
    sblk0 = row0 // t_len
    with_cache = cache_k is not None
    in_specs = [pl.BlockSpec((tq, D_A), lambda b, i: (qblk0 + b * nq + i, 0)),
                pl.BlockSpec((t_len, D_KV), lambda b, i: (sblk0 + b, 0)),
                pl.BlockSpec((t_len, D_KV), lambda b, i: (sblk0 + b, 0))]
    args = [qr, kr, va]
    if with_cache:
        past = cache_k.shape[1]
        in_specs += [pl.BlockSpec((1, past, D_KV), lambda b, i: (b, 0, 0))] * 2
        args += [cache_k, cache_v]
    return pl.pallas_call(
        functools.partial(_attn_kernel, tq=tq, with_cache=with_cache),
        out_shape=jax.ShapeDtypeStruct((batch * t_len, D_A), BF16),
        grid=(batch, nq),
        in_specs=in_specs,
        out_specs=pl.BlockSpec((tq, D_A), lambda b, i: (b * nq + i, 0)),
        compiler_params=_cparams("arbitrary", "arbitrary"),
        name="attention_cache" if with_cache else "attention_ctx",
    )(*args)


def _post_kernel(y_ref, bonus_ref, g_ref, yac_ref, yal_ref, gr_ref, ga_ref, x_ref, mod_ref,
                 lnw_ref, lnb_ref, wur_ref, wua_ref, wo_ref, n2_ref, wr_ref, br_ref, ones_ref,
                 x1_o, h2_o, idx_o, gate_o, *, ctx_blocks):
    ones = ones_ref[...]
    y_a = jnp.where(pl.program_id(0) < ctx_blocks, yac_ref[...], yal_ref[...])
    y = y_ref[0] + y_ref[1]
    mu = _head_sum(y, ones) * (1.0 / HEAD)
    yc = y - mu
    var = _head_sum(yc * yc, ones) * (1.0 / HEAD)
    yn = yc * lax.rsqrt(var + GN_EPS) * lnw_ref[...] + lnb_ref[...] + bonus_ref[...]
    y_r = (yn * g_ref[...]).astype(BF16)
    merged = (_sigmoid(gr_ref[...]) * _dot(y_r, wur_ref[...])
              + _sigmoid(ga_ref[...]) * _dot(y_a, wua_ref[...]))
    mix = _dot(merged.astype(BF16), wo_ref[...])
    mod = mod_ref[0]
    g1 = mod[:, 2 * D_MODEL:3 * D_MODEL]
    sh2 = mod[:, 3 * D_MODEL:4 * D_MODEL]
    sc2 = mod[:, 4 * D_MODEL:5 * D_MODEL]
    x1 = x_ref[...] + g1 * mix
    x1_o[...] = x1
    ms = jnp.mean(x1 * x1, axis=-1, keepdims=True)
    h2 = x1 * lax.rsqrt(ms + NORM_EPS) * n2_ref[...] * (1.0 + sc2) + sh2
    h2_o[...] = h2
    hh, hl = _split2(h2)
    wh, wl = _split2(wr_ref[...])
    logits = _dot(hh, wh) + _dot(hh, wl) + _dot(hl, wh) + br_ref[...]
    lane_i = lax.broadcasted_iota(jnp.int32, logits.shape, 1)
    lane = lane_i.astype(F32)
    neg = jnp.float32(-jnp.inf)
    cur = jnp.where(lane_i < N_EXPERTS, logits, neg)
    vals, idxs = [], []
    for _ in range(TOP_K):
        mx = jnp.max(cur, axis=-1, keepdims=True)
        ix = jnp.min(jnp.where(cur == mx, lane, float(LANES)), axis=-1, keepdims=True)
        vals.append(mx)
        idxs.append(ix)
        cur = jnp.where(lane == ix, neg, cur)
    es = [jnp.exp(val - vals[0]) for val in vals]
    den = es[0] + es[1] + es[2] + es[3]
    idx_out = jnp.zeros(logits.shape, jnp.int32)
    gate_out = jnp.zeros(logits.shape, F32)
    for j in range(TOP_K):
        idx_out = jnp.where(lane_i == j, idxs[j].astype(jnp.int32), idx_out)
        gate_out = jnp.where(lane_i == j, es[j] / den, gate_out)
    idx_o[...] = idx_out
    gate_o[...] = gate_out


def _post(y2, bonus, g, ya_ctx, ya_lat, gate_r, gate_a, x_all, mods3, mod_row, p):
    n = x_all.shape[0]
    nbc = ya_ctx.shape[0] // TM
    row = lambda i: (i, 0)
    full = lambda *shape: pl.BlockSpec(shape, lambda i: (0,) * len(shape))
    tok = lambda width: pl.BlockSpec((TM, width), row)
    return pl.pallas_call(
        functools.partial(_post_kernel, ctx_blocks=nbc),
        out_shape=[jax.ShapeDtypeStruct((n, D_MODEL), F32),
                   jax.ShapeDtypeStruct((n, D_MODEL), F32),
                   jax.ShapeDtypeStruct((n, LANES), jnp.int32),
                   jax.ShapeDtypeStruct((n, LANES), F32)],
        grid=(n // TM,),
        in_specs=[pl.BlockSpec((2, TM, D_R), lambda i: (0, i, 0)), tok(D_R), tok(D_R),
                  pl.BlockSpec((TM, D_A), lambda i: (jnp.minimum(i, nbc - 1), 0)),
                  pl.BlockSpec((TM, D_A), lambda i: (jnp.maximum(i - nbc, 0), 0)),
                  tok(D_MODEL), tok(D_MODEL), tok(D_MODEL),
                  pl.BlockSpec((1, 1, 6 * D_MODEL), lambda i: (mod_row(i), 0, 0)),
                  full(1, D_R), full(1, D_R), full(D_R, D_MODEL), full(D_A, D_MODEL),
                  full(D_MODEL, D_MODEL), full(1, D_MODEL), full(D_MODEL, LANES), full(1, LANES),
                  full(D_R, D_R)],
        out_specs=[tok(D_MODEL), tok(D_MODEL), tok(LANES), tok(LANES)],
        compiler_params=_cparams("arbitrary"),
        name="mixer_post",
    )(y2, bonus, g, ya_ctx, ya_lat, gate_r, gate_a, x_all, mods3,
      p["lnw"], p["lnb"], p["wur"], p["wua"], p["wo"], p["n2"], p["wr"], p["br"], p["ones"])


def _moe_kernel(first_ref, nblk_ref, count_ref, src_hbm, dst_hbm, h_hbm, w1_ref, b1_ref, w2_ref, b2_ref,
                out_hbm, w1b, w2b, src_s, dst_s, xs, ys, sem_i, sem_g, sem_s):
    e = pl.program_id(0)
    w1b[...] = w1_ref[0].astype(BF16)
    w2b[...] = w2_ref[0].astype(BF16)
    first = first_ref[e]
    count = count_ref[e]

    @pl.when(e == 0)
    def _():
        xs[...] = jnp.zeros_like(xs)

    def row_in(i, t):
        return pltpu.make_async_copy(h_hbm.at[pl.ds(t, 1)], xs.at[pl.ds(i, 1)], sem_g)

    def row_out(i, t):
        return pltpu.make_async_copy(ys.at[pl.ds(i, 1)], out_hbm.at[pl.ds(t, 1)], sem_s)

    @pl.loop(0, nblk_ref[e])
    def _(blk):
        cp_src = pltpu.make_async_copy(src_hbm.at[pl.ds(first + blk, 1)], src_s, sem_i.at[0])
        cp_dst = pltpu.make_async_copy(dst_hbm.at[pl.ds(first + blk, 1)], dst_s, sem_i.at[1])
        cp_src.start()
        cp_dst.start()
        cp_src.wait()
        cp_dst.wait()
        rows = jnp.minimum(count - blk * MOE_ROWS, MOE_ROWS)

        @pl.loop(0, rows)
        def _(i):
            row_in(i, src_s[0, i]).start()

        @pl.loop(0, rows)
        def _(i):
            row_in(i, 0).wait()

        hb = _dot(xs[...].astype(BF16), w1b[...]) + b1_ref[0]
        glu = jnp.minimum(hb[:, 0:D_FF], SWIGLU_LIMIT)
        lin = jnp.clip(hb[:, D_FF:2 * D_FF], -SWIGLU_LIMIT, SWIGLU_LIMIT)
        act = glu * _sigmoid(SWIGLU_ALPHA * glu) * (lin + 1.0)
        ys[...] = _dot(act.astype(BF16), w2b[...]) + b2_ref[0]

        @pl.loop(0, rows)
        def _(i):
            row_out(i, dst_s[0, i]).start()

        @pl.loop(0, rows)
        def _(i):
            row_out(i, 0).wait()


def _moe(h2, src, dst, first_blk, n_blk, counts, w1, b1, w2, b2):
    n, d = h2.shape
    grid_spec = pltpu.PrefetchScalarGridSpec(
        num_scalar_prefetch=3,
        grid=(N_EXPERTS,),
        in_specs=[pl.BlockSpec(memory_space=pl.ANY),
                  pl.BlockSpec(memory_space=pl.ANY),
                  pl.BlockSpec(memory_space=pl.ANY),
                  pl.BlockSpec((1, d, 2 * D_FF), lambda e, *_: (e, 0, 0)),
                  pl.BlockSpec((1, 1, 2 * D_FF), lambda e, *_: (e, 0, 0)),
                  pl.BlockSpec((1, D_FF, d), lambda e, *_: (e, 0, 0)),
                  pl.BlockSpec((1, 1, d), lambda e, *_: (e, 0, 0))],
        out_specs=pl.BlockSpec(memory_space=pl.ANY),
        scratch_shapes=[pltpu.VMEM((d, 2 * D_FF), BF16),
                        pltpu.VMEM((D_FF, d), BF16),
                        pltpu.SMEM((1, MOE_ROWS), jnp.int32),
                        pltpu.SMEM((1, MOE_ROWS), jnp.int32),
                        pltpu.VMEM((MOE_ROWS, d), F32),
                        pltpu.VMEM((MOE_ROWS, d), F32),
                        pltpu.SemaphoreType.DMA((2,)),
                        pltpu.SemaphoreType.DMA(()),
                        pltpu.SemaphoreType.DMA(())])
    return pl.pallas_call(
        _moe_kernel,
        out_shape=jax.ShapeDtypeStruct((n * TOP_K, d), F32),
        grid_spec=grid_spec,
        compiler_params=pltpu.CompilerParams(dimension_semantics=("arbitrary",),
                                             vmem_limit_bytes=VMEM_LIMIT, has_side_effects=True),
        name="moe_experts",
    )(first_blk, n_blk, counts, src, dst, h2, w1, b1.reshape(N_EXPERTS, 1, 2 * D_FF),
      w2, b2.reshape(N_EXPERTS, 1, d))


def _route(top_idx, n):
    nk = n * TOP_K
    cap = nk + N_EXPERTS * MOE_ROWS
    flat_e = top_idx.reshape(nk)
    order = jnp.argsort(flat_e, stable=True).astype(jnp.int32)
    counts = jnp.sum((flat_e[:, None] == jnp.arange(N_EXPERTS, dtype=jnp.int32)[None, :])
                     .astype(jnp.int32), axis=0)
    start = jnp.cumsum(counts) - counts
    n_blk = (counts + MOE_ROWS - 1) // MOE_ROWS
    blk_end = jnp.cumsum(n_blk)
    first_blk = blk_end - n_blk
    slot = jnp.arange(cap, dtype=jnp.int32)
    slot_e = jnp.minimum(jnp.sum((slot[:, None] // MOE_ROWS >= blk_end[None, :]).astype(jnp.int32),
                                 axis=1), N_EXPERTS - 1)
    within = slot - first_blk[slot_e] * MOE_ROWS
    valid = within < counts[slot_e]
    j = order[jnp.clip(start[slot_e] + within, 0, nk - 1)]
    src = jnp.where(valid, j // TOP_K, 0).astype(jnp.int32)
    dst = jnp.where(valid, (j % TOP_K) * n + j // TOP_K, 0).astype(jnp.int32)
    shape = (cap // MOE_ROWS, MOE_ROWS)
    return (src.reshape(shape), dst.reshape(shape), first_blk.astype(jnp.int32),
            n_blk.astype(jnp.int32), counts.astype(jnp.int32))


def _final_kernel(x1_ref, y0_ref, y1_ref, y2_ref, y3_ref, gate_ref, mod_ref, nf_ref, o_ref):
    gates = gate_ref[...]
    ffn = gates[:, 0:1] * y0_ref[...]
    for j, ref in enumerate((y1_ref, y2_ref, y3_ref), start=1):
        ffn = ffn + gates[:, j:j + 1] * ref[...]
    g2 = mod_ref[0][:, 5 * D_MODEL:6 * D_MODEL]
    x2 = x1_ref[...] + g2 * ffn
    ms = jnp.mean(x2 * x2, axis=-1, keepdims=True)
    o_ref[...] = x2 * lax.rsqrt(ms + NORM_EPS) * nf_ref[...]


def _final(x1, ys, gates, mods3, mod_row, norm_f):
    n, d = x1.shape
    nb = n // TM
    row = lambda i: (i, 0)
    return pl.pallas_call(
        _final_kernel,
        out_shape=jax.ShapeDtypeStruct((n, d), F32),
        grid=(nb,),
        in_specs=[pl.BlockSpec((TM, d), row)]
                 + [pl.BlockSpec((TM, d), functools.partial(lambda j, i: (j * nb + i, 0), j))
                    for j in range(TOP_K)]
                 + [pl.BlockSpec((TM, LANES), row),
                    pl.BlockSpec((1, 1, 6 * d), lambda i: (mod_row(i), 0, 0)),
                    pl.BlockSpec((1, d), lambda i: (0, 0))],
        out_specs=pl.BlockSpec((TM, d), row),
        compiler_params=_cparams("arbitrary"),
        name="combine_final",
    )(x1, ys, ys, ys, ys, gates, mods3, norm_f.reshape(1, d))


def _rope_tables(t_ctx, t_lat):
    rows = np.arange(t_lat) // GRID_W
    cols = np.arange(t_lat) % GRID_W
    inv = ROPE_THETA ** (-np.arange(0, AXIS_DIM, 2, dtype=np.float32) / AXIS_DIM)
    inv = jnp.asarray(inv, F32)
    ang_r = jnp.asarray(rows, F32)[:, None] * inv[None, :]
    ang_c = jnp.asarray(cols, F32)[:, None] * inv[None, :]
    cos = jnp.concatenate([jnp.cos(ang_r)] * 2 + [jnp.cos(ang_c)] * 2, axis=1)
    sin = jnp.concatenate([-jnp.sin(ang_r), jnp.sin(ang_r), -jnp.sin(ang_c), jnp.sin(ang_c)], axis=1)
    cos = jnp.concatenate([jnp.ones((t_ctx, HEAD), F32), cos], axis=0)
    sin = jnp.concatenate([jnp.zeros((t_ctx, HEAD), F32), sin], axis=0)
    return jnp.concatenate([cos, cos], axis=1), jnp.concatenate([sin, sin], axis=1)


def kernel(x_prompt, x_sample, cache_k, cache_v, state_rwkv_fwd, state_rwkv_bwd, c, c_ctx, w_ada, b_ada, norm1, norm2, w_in, rw_w0, rw_w2, rw_a0, rw_a2, rw_g2, rw_kk, rw_ka, rw_rk, rw_ln_w, rw_ln_b, q_norm, k_norm, w_up_r, w_up_a, w_out, w_router, b_router, w_moe_in, b_moe_in, w_moe_out, b_moe_out, norm_f):
    depth = w_in.shape[0]
    assert depth == 1, "single trunk layer"
    bc, tc, d = x_prompt.shape
    bl, tl, _ = x_sample.shape
    nc, nl = bc * tc, bl * tl
    n = nc + nl
    assert d == D_MODEL and tc == TM and tl % TM == 0 and nc % tl == 0
    l = 0

    x_all = jnp.concatenate([x_prompt.reshape(nc, d), x_sample.reshape(nl, d)], axis=0)
    cond = jnp.concatenate([c_ctx[None, :], c, jnp.zeros((16 - 1 - bl, d), F32)], axis=0)
    mods3 = _adaln(cond, w_ada[l], b_ada[l]).reshape(16, 1, 6 * d)
    nbc = nc // TM
    per_seq = tl // TM
    mod_row = lambda i: jnp.where(i < nbc, 0, 1 + (i - nbc) // per_seq)
    tab_row = lambda i: jnp.where(i < nbc, 0, 1 + (i - nbc) % per_seq)

    z = dict(zip([name for name, _ in IN_SPLITS],
                 _proj_in(x_all, mods3, norm1[l], w_in[l].astype(BF16), mod_row)))

    ones_bd = jnp.asarray(np.kron(np.eye(H_R), np.ones((HEAD, HEAD))), BF16)
    pad_lo = lambda w: jnp.concatenate([w, jnp.zeros_like(w)], axis=0)
    pad_hi = lambda w: jnp.concatenate([jnp.zeros_like(w), w], axis=0)
    prep_p = dict(
        w0=rw_w0[l], a0=rw_a0[l],
        w2=jnp.stack([pad_lo(rw_w2[l, 0]), pad_hi(rw_w2[l, 1])]).astype(BF16),
        a2=jnp.stack([pad_lo(rw_a2[l, 0]), pad_hi(rw_a2[l, 1])]).astype(BF16),
        g2=rw_g2[l].astype(BF16),
        kkw=rw_kk[l].reshape(1, D_R), kaw=rw_ka[l].reshape(1, D_R), rk=rw_rk[l].reshape(1, D_R),
        qn=jnp.tile(q_norm[l], N_HEADS).reshape(1, D_A), kn=jnp.tile(k_norm[l], N_KV).reshape(1, D_KV),
        ones=ones_bd)
    cos_tab, sin_tab = _rope_tables(tc, tl)
    kk, lw, a, bonus, g, qr, kn, kr = _prep(z, cos_tab, sin_tab, tab_row, prep_p)

    s0 = jnp.concatenate([jnp.zeros((2, bc, H_R, HEAD, HEAD), F32),
                          jnp.stack([state_rwkv_fwd[:, l], state_rwkv_bwd[:, l]])], axis=1)
    y2, s_fin = _rwkv_scan(z["r"], z["k"], z["v"], kk, lw, a, prep_p["kaw"], s0,
                           [tc] * bc + [tl] * bl)

    ya_ctx = _attention(qr, kr, z["va"], 0, bc, tc)
    ya_lat = _attention(qr, kr, z["va"], nc, bl, tl,
                        cache_k[:, l].reshape(bl, -1, D_KV), cache_v[:, l].reshape(bl, -1, D_KV))

    post_p = dict(
        lnw=rw_ln_w[l].reshape(1, D_R), lnb=rw_ln_b[l].reshape(1, D_R),
        wur=w_up_r[l].astype(BF16), wua=w_up_a[l].astype(BF16), wo=w_out[l].astype(BF16),
        n2=norm2[l].reshape(1, d),
        wr=jnp.pad(w_router[l], ((0, 0), (0, LANES - N_EXPERTS))),
        br=jnp.pad(b_router[l], (0, LANES - N_EXPERTS)).reshape(1, LANES),
        ones=ones_bd)
    x1, h2, top_idx, gates = _post(y2, bonus, g, ya_ctx, ya_lat, z["gate_r"], z["gate_a"], x_all,
                                   mods3, mod_row, post_p)

    src, dst, first_blk, n_blk, counts = _route(top_idx[:, :TOP_K], n)
    ys = _moe(h2, src, dst, first_blk, n_blk, counts,
              w_moe_in[l], b_moe_in[l], w_moe_out[l], b_moe_out[l])
    y_all = _final(x1, ys, gates, mods3, mod_row, norm_f)

    y_prompt = y_all[:nc].reshape(bc, tc, d)
    y_sample = y_all[nc:].reshape(bl, tl, d)
    new_cache_k = kn[:nc].reshape(bc, 1, tc, N_KV, HEAD)
    new_cache_v = z["va"][:nc].reshape(bc, 1, tc, N_KV, HEAD)
    new_state_fwd = s_fin[0, :bc][:, None]
    new_state_bwd = s_fin[1, :bc][:, None]
    return (y_prompt, y_sample, new_cache_k, new_cache_v, new_state_fwd, new_state_bwd)
```

```python
import functools

import numpy as np
import jax
import jax.numpy as jnp
from jax import lax
from jax.experimental import pallas as pl
from jax.experimental.pallas import tpu as pltpu

F32 = jnp.float32
BF16 = jnp.bfloat16

D_MODEL = 1024
GRID_W = 64
NORM_EPS = 1e-6
HEAD = 64
H_R = 8
D_R = H_R * HEAD
DECAY_RANK = 64
AAA_RANK = 64
GATE_RANK = 128
GN_EPS = 64e-5
N_HEADS = 8
N_KV = 2
D_A = N_HEADS * HEAD
D_KV = N_KV * HEAD
AXIS_DIM = HEAD // 2
ROPE_THETA = 10000.0
N_EXPERTS = 32
TOP_K = 4
D_FF = D_MODEL
SWIGLU_ALPHA = 1.702
SWIGLU_LIMIT = 7.0
LORA_COLS = 2 * DECAY_RANK + 2 * AAA_RANK + GATE_RANK
IN_SPLITS = (("r", D_R), ("k", D_R), ("v", D_R), ("lora", LORA_COLS), ("q", D_A),
             ("ka", D_KV), ("va", D_KV), ("gate_r", D_MODEL), ("gate_a", D_MODEL))

LANES = 128
TM = 256
CHUNK = 64
SCAN_OPS = 6
MOE_ROWS = 256
VMEM_LIMIT = 56 * 1024 * 1024


def _cparams(*sem):
    return pltpu.CompilerParams(dimension_semantics=sem, vmem_limit_bytes=VMEM_LIMIT)


def _dot(a, b):
    return jnp.dot(a, b, preferred_element_type=F32)


def _dot_nt(a, b):
    return lax.dot_general(a, b, (((1,), (1,)), ((), ())), preferred_element_type=F32)


def _dot_tn(a, b):
    return lax.dot_general(a, b, (((0,), (0,)), ((), ())), preferred_element_type=F32)


def _split2(x):
    hi = x.astype(BF16)
    lo = (x - hi.astype(F32)).astype(BF16)
    return hi, lo


def _split3(x):
    x1 = x.astype(BF16)
    r1 = x - x1.astype(F32)
    x2 = r1.astype(BF16)
    x3 = (r1 - x2.astype(F32)).astype(BF16)
    return x1, x2, x3


def _head_sum(x, ones_bd):
    hi, lo = _split2(x)
    return _dot(hi, ones_bd) + _dot(lo, ones_bd)


def _sigmoid(x):
    return 1.0 / (1.0 + jnp.exp(-x))


def _adaln_kernel(c_ref, w_ref, b_ref, o_ref):
    c = c_ref[...]
    s = (c * _sigmoid(c)).astype(BF16)
    o_ref[...] = _dot(s, w_ref[...].astype(BF16)) + b_ref[...]


def _adaln(cond, w_ada, b_ada):
    rows, d = cond.shape
    cols = w_ada.shape[1]
    tn = 512
    return pl.pallas_call(
        _adaln_kernel,
        out_shape=jax.ShapeDtypeStruct((rows, cols), F32),
        grid=(cols // tn,),
        in_specs=[pl.BlockSpec((rows, d), lambda j: (0, 0)),
                  pl.BlockSpec((d, tn), lambda j: (0, j)),
                  pl.BlockSpec((1, tn), lambda j: (0, j))],
        out_specs=pl.BlockSpec((rows, tn), lambda j: (0, j)),
        compiler_params=_cparams("arbitrary"),
        name="adaln",
    )(cond, w_ada, b_ada.reshape(1, cols))


def _proj_in_kernel(x_ref, mod_ref, n1_ref, w_ref, *out_refs):
    x = x_ref[...]
    ms = jnp.mean(x * x, axis=-1, keepdims=True)
    hn = x * lax.rsqrt(ms + NORM_EPS) * n1_ref[...]
    mod = mod_ref[0]
    sh1 = mod[:, 0:D_MODEL]
    sc1 = mod[:, D_MODEL:2 * D_MODEL]
    h = (hn * (1.0 + sc1) + sh1).astype(BF16)
    c0 = 0
    for ref, (_, width) in zip(out_refs, IN_SPLITS):
        ref[...] = _dot(h, w_ref[:, c0:c0 + width])
        c0 += width


def _proj_in(x_all, mods3, norm1, w_in_bf, mod_row):
    n, d = x_all.shape
    cols = w_in_bf.shape[1]
    row = lambda i: (i, 0)
    return pl.pallas_call(
        _proj_in_kernel,
        out_shape=[jax.ShapeDtypeStruct((n, width), F32) for _, width in IN_SPLITS],
        grid=(n // TM,),
        in_specs=[pl.BlockSpec((TM, d), row),
                  pl.BlockSpec((1, 1, 6 * d), lambda i: (mod_row(i), 0, 0)),
                  pl.BlockSpec((1, d), lambda i: (0, 0)),
                  pl.BlockSpec((d, cols), lambda i: (0, 0))],
        out_specs=[pl.BlockSpec((TM, width), row) for _, width in IN_SPLITS],
        compiler_params=_cparams("arbitrary"),
        name="proj_in",
    )(x_all, mods3, norm1.reshape(1, d), w_in_bf)


def _prep_kernel(r_ref, k_ref, v_ref, lora_ref, q_ref, ka_ref, cos_ref, sin_ref,
                 w0_ref, w2_ref, a0_ref, a2_ref, g2_ref, kkw_ref, kaw_ref, rk_ref,
                 qn_ref, kn_ref, ones_ref, tri_ref, chunk_ones_ref,
                 sc_o, gt_o, vb_o, bonus_o, g_o, qr_o, kn_o, kr_o):
    ones = ones_ref[...]
    chunk_ones = chunk_ones_ref[...]
    r = r_ref[...]
    k = k_ref[...]
    v = v_ref[...]
    vb_o[...] = v.astype(BF16)
    lora = lora_ref[...]
    kk = k * kkw_ref[...]
    kk = kk * lax.rsqrt(_head_sum(kk * kk, ones) + 1e-12)
    th = jnp.tanh(lora[:, 0:LANES]).astype(BF16)
    al = lora[:, LANES:2 * LANES].astype(BF16)
    kd_sum = jnp.zeros_like(k)
    for d in range(2):
        u = w0_ref[d:d + 1, :] + _dot(th, w2_ref[d])
        w_log = -(jnp.maximum(-u, 0.0) + jnp.log1p(jnp.exp(-jnp.abs(u)))) - 0.5
        lw = -jnp.exp(w_log)
        a = _sigmoid(a0_ref[d:d + 1, :] + _dot(al, a2_ref[d]))
        kd = k * (1.0 + (a - 1.0) * kaw_ref[...])
        kd_sum = kd_sum + kd
        b = kk * a
        parts = _split3(lw)
        tri = tri_ref[d]
        L = _dot(tri, parts[0]) + _dot(tri, parts[1]) + _dot(tri, parts[2])
        Ltot = _dot(chunk_ones, parts[0]) + _dot(chunk_ones, parts[1]) + _dot(chunk_ones, parts[2])
        e_inv = jnp.exp(-L)
        e_rest = jnp.exp(Ltot - L)
        scan_ops = (kk * jnp.exp(L - lw), r * jnp.exp(L), b * e_inv, kd * e_inv, b * e_rest, kd * e_rest)
        for j, op in enumerate(scan_ops):
            sc_o[d, :, j * D_R:(j + 1) * D_R] = op.astype(BF16)
        g_tot = jnp.exp(Ltot)
        for j in range(TM // CHUNK):
            gt_o[d, j] = g_tot[j * CHUNK:j * CHUNK + 1, :]
    bonus_o[...] = _head_sum(r * kd_sum * rk_ref[...], ones) * v
    g_o[...] = _dot(_sigmoid(lora[:, 2 * LANES:3 * LANES]).astype(BF16), g2_ref[...])
    q = q_ref[...]
    qn = q * lax.rsqrt(_head_sum(q * q, ones) * (1.0 / HEAD) + NORM_EPS) * qn_ref[...]
    ka = ka_ref[...]
    ones_kv = ones[0:D_KV, 0:D_KV]
    kn = ka * lax.rsqrt(_head_sum(ka * ka, ones_kv) * (1.0 / HEAD) + NORM_EPS) * kn_ref[...]
    kn_o[...] = kn
    cos = cos_ref[...]
    sin = sin_ref[...]
    half = AXIS_DIM // 2

    def rope(t, cos_t, sin_t):
        width = t.shape[1]
        lane = lax.broadcasted_iota(jnp.int32, t.shape, 1)
        first = (lane % AXIS_DIM) < half
        swapped = jnp.where(first, pltpu.roll(t, width - half, 1), pltpu.roll(t, half, 1))
        return t * cos_t + swapped * sin_t

    cos4 = jnp.concatenate([cos] * (D_A // D_KV), axis=1)
    sin4 = jnp.concatenate([sin] * (D_A // D_KV), axis=1)
    qr_o[...] = (rope(qn, cos4, sin4) * (HEAD ** -0.5)).astype(BF16)
    kr_o[...] = rope(kn, cos, sin).astype(BF16)


def _prep(z, cos_tab, sin_tab, tab_row, p):
    n = z["r"].shape[0]
    row = lambda i: (i, 0)
    full = lambda *shape: pl.BlockSpec(shape, lambda i: (0,) * len(shape))
    tok = lambda width: pl.BlockSpec((TM, width), row)
    tab = pl.BlockSpec((TM, D_KV), lambda i: (tab_row(i), 0))
    cpb = TM // CHUNK
    return pl.pallas_call(
        _prep_kernel,
        out_shape=[jax.ShapeDtypeStruct((2, n, SCAN_OPS * D_R), BF16),
                   jax.ShapeDtypeStruct((2, n // CHUNK, 1, D_R), F32),
                   jax.ShapeDtypeStruct((n, D_R), BF16),
                   jax.ShapeDtypeStruct((n, D_R), F32),
                   jax.ShapeDtypeStruct((n, D_R), F32),
                   jax.ShapeDtypeStruct((n, D_A), BF16),
                   jax.ShapeDtypeStruct((n, D_KV), F32),
                   jax.ShapeDtypeStruct((n, D_KV), BF16)],
        grid=(n // TM,),
        in_specs=[tok(D_R), tok(D_R), tok(D_R), tok(LORA_COLS), tok(D_A), tok(D_KV), tab, tab,
                  full(2, D_R), full(2, LANES, D_R), full(2, D_R), full(2, LANES, D_R),
                  full(GATE_RANK, D_R), full(1, D_R), full(1, D_R), full(1, D_R),
                  full(1, D_A), full(1, D_KV), full(D_R, D_R), full(2, TM, TM), full(TM, TM)],
        out_specs=[pl.BlockSpec((2, TM, SCAN_OPS * D_R), lambda i: (0, i, 0)),
                   pl.BlockSpec((2, cpb, 1, D_R), lambda i: (0, i, 0, 0)),
                   tok(D_R), tok(D_R), tok(D_R), tok(D_A), tok(D_KV), tok(D_KV)],
        compiler_params=_cparams("arbitrary"),
        name="mixer_prep",
    )(z["r"], z["k"], z["v"], z["lora"], z["q"], z["ka"], cos_tab, sin_tab,
      p["w0"], p["w2"], p["a0"], p["a2"], p["g2"], p["kkw"], p["kaw"], p["rk"],
      p["qn"], p["kn"], p["ones"], p["tri"], p["chunk_ones"])


def _scan_kernel(blkf_ref, blkb_ref, seq_ref, first_ref, last_ref,
                 scf_ref, scb_ref, vf_ref, vb_ref, gf_ref, gb_ref, s0_ref,
                 yf_ref, yb_ref, sfin_ref, state):
    step = pl.program_id(0)
    C = CHUNK
    pairs = D_R // LANES
    chains = [(d, p) for d in range(2) for p in range(pairs)]

    lane = lax.broadcasted_iota(jnp.int32, (1, LANES), 1)
    low = lane < HEAD

    @pl.when(first_ref[step] == 1)
    def _():
        zero = jnp.zeros((HEAD, HEAD), F32)
        for d, p in chains:
            top = jnp.concatenate([s0_ref[d, 0, 2 * p], zero], axis=1)
            bot = jnp.concatenate([zero, s0_ref[d, 0, 2 * p + 1]], axis=1)
            state[d * pairs + p] = jnp.concatenate([top, bot], axis=0)

    t2 = lax.broadcasted_iota(jnp.int32, (C, LANES), 0)
    s2 = lax.broadcasted_iota(jnp.int32, (C, LANES), 1) % C
    strict = (s2 < t2, s2 > t2)
    incl = (s2 <= t2, s2 >= t2)
    eye2 = jnp.where(s2 == t2, 1.0, 0.0)
    r2 = lax.broadcasted_iota(jnp.int32, (LANES, LANES), 0)
    c2 = lax.broadcasted_iota(jnp.int32, (LANES, LANES), 1)
    same_head = (r2 // HEAD) == (c2 // HEAD)

    sc_refs = (scf_ref, scb_ref)
    v_refs = (vf_ref, vb_ref)
    g_refs = (gf_ref, gb_ref)
    y_refs = (yf_ref, yb_ref)

    def operand(d, p, j):
        c0 = j * D_R + p * LANES
        return sc_refs[d][0, :, c0:c0 + LANES]

    def stack(x):
        zero = jnp.zeros_like(x)
        return jnp.concatenate([jnp.where(low, x, zero), jnp.where(low, zero, x)], axis=0)

    Qk = [operand(d, p, 0) for d, p in chains]
    Qr = [operand(d, p, 1) for d, p in chains]
    v_p = [v_refs[d][:, p * LANES:(p + 1) * LANES] for d, p in chains]
    v_st = [stack(x) for x in v_p]
    Z0 = [state[i] for i in range(len(chains))]
    Z0b = [z.astype(BF16) for z in Z0]
    m = [_dot_nt(jnp.concatenate([Qk[i], Qr[i]], axis=0),
                 jnp.concatenate([stack(operand(d, p, 2)), stack(operand(d, p, 3))], axis=0))
         for i, (d, p) in enumerate(chains)]
    A = [jnp.where(strict[d], m[i][0:C, 0:LANES], 0.0) for i, (d, p) in enumerate(chains)]
    Bm = [jnp.where(strict[d], m[i][0:C, LANES:2 * LANES], 0.0).astype(BF16)
          for i, (d, p) in enumerate(chains)]
    Pb = [jnp.where(incl[d], m[i][C:2 * C, 0:LANES], 0.0).astype(BF16) for i, (d, p) in enumerate(chains)]
    Pk = [jnp.where(incl[d], m[i][C:2 * C, LANES:2 * LANES], 0.0).astype(BF16)
          for i, (d, p) in enumerate(chains)]
    BV = [_dot(Bm[i], v_st[i]) for i in range(len(chains))]
    T = [eye2 - a for a in A]
    Ap = [_dot(a.astype(BF16), stack(a.astype(BF16))).astype(BF16) for a in A]
    n = 2
    while 2 * n < C:
        R = [_dot(jnp.concatenate([Ap[i], T[i].astype(BF16)], axis=0), stack(Ap[i]))
             for i in range(len(chains))]
        Ap = [x[0:C].astype(BF16) for x in R]
        T = [T[i] + R[i][C:2 * C] for i in range(len(chains))]
        n *= 2
    T = [T[i] + _dot(T[i].astype(BF16), stack(Ap[i])) for i in range(len(chains))]
    WU = [_dot(T[i].astype(BF16),
               jnp.concatenate([stack(Qk[i]), stack(BV[i].astype(BF16))], axis=1))
          for i in range(len(chains))]
    Wq = [x[:, 0:LANES].astype(BF16) for x in WU]
    Uv = [x[:, LANES:2 * LANES] for x in WU]
    PW = [_dot(Pb[i], jnp.concatenate([stack(Wq[i]), stack(Uv[i].astype(BF16))], axis=1))
          for i in range(len(chains))]
    PkV = [_dot(Pk[i], v_st[i]) for i in range(len(chains))]
    U = [_dot_nt(Wq[i], Z0b[i]) + Uv[i] for i in range(len(chains))]
    Yq = [(Qr[i].astype(F32) - PW[i][:, 0:LANES]).astype(BF16) for i in range(len(chains))]
    for i, (d, p) in enumerate(chains):
        y_refs[d][:, p * LANES:(p + 1) * LANES] = (_dot_nt(Yq[i], Z0b[i]) + PkV[i]
                                                   - PW[i][:, LANES:2 * LANES])
    for i, (d, p) in enumerate(chains):
        lhs = jnp.concatenate([v_p[i], (-U[i]).astype(BF16)], axis=0)
        rhs = jnp.concatenate([operand(d, p, 5), operand(d, p, 4)], axis=0)
        upd = _dot_tn(lhs, rhs)
        g_tot = g_refs[d][0, 0][:, p * LANES:(p + 1) * LANES]
        state[i] = Z0[i] * g_tot + jnp.where(same_head, upd, 0.0)

    @pl.when(last_ref[step] == 1)
    def _():
        for d, p in chains:
            z = state[d * pairs + p]
            sfin_ref[d, 0, 2 * p] = z[0:HEAD, 0:HEAD]
            sfin_ref[d, 0, 2 * p + 1] = z[HEAD:2 * HEAD, HEAD:2 * HEAD]


def _rwkv_scan(sc, gt, vb, s0, seq_lens):
    n = vb.shape[0]
    blk_f, blk_b, seq, first, last = [], [], [], [], []
    base = 0
    for i, t_len in enumerate(seq_lens):
        nch = t_len // CHUNK
        blk_f += [base + cc for cc in range(nch)]
        blk_b += [base + nch - 1 - cc for cc in range(nch)]
        seq += [i] * nch
        first += [1] + [0] * (nch - 1)
        last += [0] * (nch - 1) + [1]
        base += nch
    steps = len(seq)
    tabs = [jnp.asarray(np.array(t, np.int32)) for t in (blk_f, blk_b, seq, first, last)]
    width = SCAN_OPS * D_R
    sc_f = pl.BlockSpec((1, CHUNK, width), lambda s, bf, bb, *_: (0, bf[s], 0))
    sc_b = pl.BlockSpec((1, CHUNK, width), lambda s, bf, bb, *_: (1, bb[s], 0))
    tok_f = pl.BlockSpec((CHUNK, D_R), lambda s, bf, bb, *_: (bf[s], 0))
    tok_b = pl.BlockSpec((CHUNK, D_R), lambda s, bf, bb, *_: (bb[s], 0))
    gt_f = pl.BlockSpec((1, 1, 1, D_R), lambda s, bf, bb, *_: (0, bf[s], 0, 0))
    gt_b = pl.BlockSpec((1, 1, 1, D_R), lambda s, bf, bb, *_: (1, bb[s], 0, 0))
    st = pl.BlockSpec((2, 1, H_R, HEAD, HEAD), lambda s, bf, bb, seq, *_: (0, seq[s], 0, 0, 0))
    grid_spec = pltpu.PrefetchScalarGridSpec(
        num_scalar_prefetch=5,
        grid=(steps,),
        in_specs=[sc_f, sc_b, tok_f, tok_b, gt_f, gt_b, st],
        out_specs=[tok_f, tok_b, st],
        scratch_shapes=[pltpu.VMEM((2 * D_R // LANES, LANES, LANES), F32)])
    y_f, y_b, sfin = pl.pallas_call(
        _scan_kernel,
        out_shape=[jax.ShapeDtypeStruct((n, D_R), F32),
                   jax.ShapeDtypeStruct((n, D_R), F32),
                   jax.ShapeDtypeStruct((2, len(seq_lens), H_R, HEAD, HEAD), F32)],
        grid_spec=grid_spec,
        compiler_params=_cparams("arbitrary"),
        name="rwkv_scan",
    )(*tabs, sc, sc, vb, vb, gt, gt, s0)
    return y_f, y_b, sfin


def _attn_kernel(*refs, tq, with_cache):
    if with_cache:
        q_ref, k_ref, v_ref, ck_ref, cv_ref, o_ref = refs
        kx = jnp.concatenate([k_ref[...], ck_ref[0].astype(BF16)], axis=0)
        vx = jnp.concatenate([v_ref[...], cv_ref[0]], axis=0).astype(BF16)
    else:
        q_ref, k_ref, v_ref, o_ref = refs
        kx = k_ref[...]
        vx = v_ref[...].astype(BF16)
    lane = lax.broadcasted_iota(jnp.int32, (1, D_KV), 1)
    low = lane < HEAD
    k_sw = pltpu.roll(kx.astype(F32), HEAD, 1).astype(BF16)
    v_sw = pltpu.roll(vx.astype(F32), HEAD, 1).astype(BF16)
    group = N_HEADS // N_KV
    for p in range(D_A // LANES):
        g = (2 * p) // group
        keep = low if g == 0 else jnp.logical_not(low)
        kd = jnp.where(keep, kx, k_sw)
        vd = jnp.where(keep, vx, v_sw)
        qp = q_ref[:, p * LANES:(p + 1) * LANES]
        zero = jnp.zeros_like(qp)
        qs = jnp.concatenate([jnp.where(low, qp, zero), jnp.where(low, zero, qp)], axis=0)
        s = _dot_nt(qs, kd)
        mx = jnp.max(s, axis=-1, keepdims=True)
        e = jnp.exp(s - mx)
        den = jnp.sum(e, axis=-1, keepdims=True)
        o = _dot(e.astype(BF16), vd) / den
        o_ref[:, p * LANES:(p + 1) * LANES] = jnp.where(low, o[0:tq], o[tq:2 * tq]).astype(BF16)


def _attention(qr, kr, va, row0, batch, t_len, cache_k=None, cache_v=None):
    tq = min(t_len, 256)
    nq = t_len // tq
    qblk0 = row0 // tq
    sblk0 = row0 // t_len
    with_cache = cache_k is not None
    in_specs = [pl.BlockSpec((tq, D_A), lambda b, i: (qblk0 + b * nq + i, 0)),
                pl.BlockSpec((t_len, D_KV), lambda b, i: (sblk0 + b, 0)),
                pl.BlockSpec((t_len, D_KV), lambda b, i: (sblk0 + b, 0))]
    args = [qr, kr, va]
    if with_cache:
        past = cache_k.shape[1]
        in_specs += [pl.BlockSpec((1, past, D_KV), lambda b, i: (b, 0, 0))] * 2
        args += [cache_k, cache_v]
    return pl.pallas_call(
        functools.partial(_attn_kernel, tq=tq, with_cache=with_cache),
        out_shape=jax.ShapeDtypeStruct((batch * t_len, D_A), BF16),
        grid=(batch, nq),
        in_specs=in_specs,
        out_specs=pl.BlockSpec((tq, D_A), lambda b, i: (b * nq + i, 0)),
        compiler_params=_cparams("arbitrary", "arbitrary"),
        name="attention_cache" if with_cache else "attention_ctx",
    )(*args)


def _post_kernel(yf_ref, yb_ref, bonus_ref, g_ref, yac_ref, yal_ref, gr_ref, ga_ref, x_ref, mod_ref,
                 lnw_ref, lnb_ref, wur_ref, wua_ref, wo_ref, n2_ref, wr_ref, br_ref, ones_ref,
                 x1_o, h2_o, idx_o, gate_o, *, ctx_blocks):
    ones = ones_ref[...]
    y_a = jnp.where(pl.program_id(0) < ctx_blocks, yac_ref[...], yal_ref[...])
    y = yf_ref[...] + yb_ref[...]
    mu = _head_sum(y, ones) * (1.0 / HEAD)
    yc = y - mu
    var = _head_sum(yc * yc, ones) * (1.0 / HEAD)
    yn = yc * lax.rsqrt(var + GN_EPS) * lnw_ref[...] + lnb_ref[...] + bonus_ref[...]
    y_r = (yn * g_ref[...]).astype(BF16)
    merged = (_sigmoid(gr_ref[...]) * _dot(y_r, wur_ref[...])
              + _sigmoid(ga_ref[...]) * _dot(y_a, wua_ref[...]))
    mix = _dot(merged.astype(BF16), wo_ref[...])
    mod = mod_ref[0]
    g1 = mod[:, 2 * D_MODEL:3 * D_MODEL]
    sh2 = mod[:, 3 * D_MODEL:4 * D_MODEL]
    sc2 = mod[:, 4 * D_MODEL:5 * D_MODEL]
    x1 = x_ref[...] + g1 * mix
    x1_o[...] = x1
    ms = jnp.mean(x1 * x1, axis=-1, keepdims=True)
    h2 = x1 * lax.rsqrt(ms + NORM_EPS) * n2_ref[...] * (1.0 + sc2) + sh2
    h2_o[...] = h2
    hh, hl = _split2(h2)
    wh, wl = _split2(wr_ref[...])
    logits = _dot(hh, wh) + _dot(hh, wl) + _dot(hl, wh) + br_ref[...]
    lane_i = lax.broadcasted_iota(jnp.int32, logits.shape, 1)
    lane = lane_i.astype(F32)
    neg = jnp.float32(-jnp.inf)
    cur = jnp.where(lane_i < N_EXPERTS, logits, neg)
    vals, idxs = [], []
    for _ in range(TOP_K):
        mx = jnp.max(cur, axis=-1, keepdims=True)
        ix = jnp.min(jnp.where(cur == mx, lane, float(LANES)), axis=-1, keepdims=True)
        vals.append(mx)
        idxs.append(ix)
        cur = jnp.where(lane == ix, neg, cur)
    es = [jnp.exp(val - vals[0]) for val in vals]
    den = es[0] + es[1] + es[2] + es[3]
    idx_out = jnp.zeros(logits.shape, jnp.int32)
    gate_out = jnp.zeros(logits.shape, F32)
    for j in range(TOP_K):
        idx_out = jnp.where(lane_i == j, idxs[j].astype(jnp.int32), idx_out)
        gate_out = jnp.where(lane_i == j, es[j] / den, gate_out)
    idx_o[...] = idx_out
    gate_o[...] = gate_out


def _post(y_f, y_b, bonus, g, ya_ctx, ya_lat, gate_r, gate_a, x_all, mods3, mod_row, p):
    n = x_all.shape[0]
    nbc = ya_ctx.shape[0] // TM
    row = lambda i: (i, 0)
    full = lambda *shape: pl.BlockSpec(shape, lambda i: (0,) * len(shape))
    tok = lambda width: pl.BlockSpec((TM, width), row)
    return pl.pallas_call(
        functools.partial(_post_kernel, ctx_blocks=nbc),
        out_shape=[jax.ShapeDtypeStruct((n, D_MODEL), F32),
                   jax.ShapeDtypeStruct((n, D_MODEL), F32),
                   jax.ShapeDtypeStruct((n, LANES), jnp.int32),
                   jax.ShapeDtypeStruct((n, LANES), F32)],
        grid=(n // TM,),
        in_specs=[tok(D_R), tok(D_R), tok(D_R), tok(D_R),
                  pl.BlockSpec((TM, D_A), lambda i: (jnp.minimum(i, nbc - 1), 0)),
                  pl.BlockSpec((TM, D_A), lambda i: (jnp.maximum(i - nbc, 0), 0)),
                  tok(D_MODEL), tok(D_MODEL), tok(D_MODEL),
                  pl.BlockSpec((1, 1, 6 * D_MODEL), lambda i: (mod_row(i), 0, 0)),
                  full(1, D_R), full(1, D_R), full(D_R, D_MODEL), full(D_A, D_MODEL),
                  full(D_MODEL, D_MODEL), full(1, D_MODEL), full(D_MODEL, LANES), full(1, LANES),
                  full(D_R, D_R)],
        out_specs=[tok(D_MODEL), tok(D_MODEL), tok(LANES), tok(LANES)],
        compiler_params=_cparams("arbitrary"),
        name="mixer_post",
    )(y_f, y_b, bonus, g, ya_ctx, ya_lat, gate_r, gate_a, x_all, mods3,
      p["lnw"], p["lnb"], p["wur"], p["wua"], p["wo"], p["n2"], p["wr"], p["br"], p["ones"])


def _moe_kernel(first_ref, nblk_ref, count_ref, src_hbm, dst_hbm, h_hbm, w1_ref, b1_ref, w2_ref, b2_ref,
                out_hbm, w1b, w2b, src_s, dst_s, xs, ys, sem_i, sem_g, sem_s):
    e = pl.program_id(0)
    w1b[...] = w1_ref[0].astype(BF16)
    w2b[...] = w2_ref[0].astype(BF16)
    first = first_ref[e]
    count = count_ref[e]

    @pl.when(e == 0)
    def _():
        xs[...] = jnp.zeros_like(xs)

    def row_in(i, t):
        return pltpu.make_async_copy(h_hbm.at[pl.ds(t, 1)], xs.at[pl.ds(i, 1)], sem_g)

    def row_out(i, t):
        return pltpu.make_async_copy(ys.at[pl.ds(i, 1)], out_hbm.at[pl.ds(t, 1)], sem_s)

    @pl.loop(0, nblk_ref[e])
    def _(blk):
        cp_src = pltpu.make_async_copy(src_hbm.at[pl.ds(first + blk, 1)], src_s, sem_i.at[0])
        cp_dst = pltpu.make_async_copy(dst_hbm.at[pl.ds(first + blk, 1)], dst_s, sem_i.at[1])
        cp_src.start()
        cp_dst.start()
        cp_src.wait()
        cp_dst.wait()
        rows = jnp.minimum(count - blk * MOE_ROWS, MOE_ROWS)

        @pl.loop(0, rows)
        def _(i):
            row_in(i, src_s[0, i]).start()

        @pl.loop(0, rows)
        def _(i):
            row_in(i, 0).wait()

        hb = _dot(xs[...].astype(BF16), w1b[...]) + b1_ref[0]
        glu = jnp.minimum(hb[:, 0:D_FF], SWIGLU_LIMIT)
        lin = jnp.clip(hb[:, D_FF:2 * D_FF], -SWIGLU_LIMIT, SWIGLU_LIMIT)
        act = glu * _sigmoid(SWIGLU_ALPHA * glu) * (lin + 1.0)
        ys[...] = _dot(act.astype(BF16), w2b[...]) + b2_ref[0]

        @pl.loop(0, rows)
        def _(i):
            row_out(i, dst_s[0, i]).start()

        @pl.loop(0, rows)
        def _(i):
            row_out(i, 0).wait()


def _moe(h2, src, dst, first_blk, n_blk, counts, w1, b1, w2, b2):
    n, d = h2.shape
    grid_spec = pltpu.PrefetchScalarGridSpec(
        num_scalar_prefetch=3,
        grid=(N_EXPERTS,),
        in_specs=[pl.BlockSpec(memory_space=pl.ANY),
                  pl.BlockSpec(memory_space=pl.ANY),
                  pl.BlockSpec(memory_space=pl.ANY),
                  pl.BlockSpec((1, d, 2 * D_FF), lambda e, *_: (e, 0, 0)),
                  pl.BlockSpec((1, 1, 2 * D_FF), lambda e, *_: (e, 0, 0)),
                  pl.BlockSpec((1, D_FF, d), lambda e, *_: (e, 0, 0)),
                  pl.BlockSpec((1, 1, d), lambda e, *_: (e, 0, 0))],
        out_specs=pl.BlockSpec(memory_space=pl.ANY),
        scratch_shapes=[pltpu.VMEM((d, 2 * D_FF), BF16),
                        pltpu.VMEM((D_FF, d), BF16),
                        pltpu.SMEM((1, MOE_ROWS), jnp.int32),
                        pltpu.SMEM((1, MOE_ROWS), jnp.int32),
                        pltpu.VMEM((MOE_ROWS, d), F32),
                        pltpu.VMEM((MOE_ROWS, d), F32),
                        pltpu.SemaphoreType.DMA((2,)),
                        pltpu.SemaphoreType.DMA(()),
                        pltpu.SemaphoreType.DMA(())])
    return pl.pallas_call(
        _moe_kernel,
        out_shape=jax.ShapeDtypeStruct((n * TOP_K, d), F32),
        grid_spec=grid_spec,
        compiler_params=pltpu.CompilerParams(dimension_semantics=("arbitrary",),
                                             vmem_limit_bytes=VMEM_LIMIT, has_side_effects=True),
        name="moe_experts",
    )(first_blk, n_blk, counts, src, dst, h2, w1, b1.reshape(N_EXPERTS, 1, 2 * D_FF),
      w2, b2.reshape(N_EXPERTS, 1, d))


def _route(top_idx, n):
    nk = n * TOP_K
    cap = nk + N_EXPERTS * MOE_ROWS
    flat_e = top_idx.reshape(nk)
    order = jnp.argsort(flat_e, stable=True).astype(jnp.int32)
    counts = jnp.sum((flat_e[:, None] == jnp.arange(N_EXPERTS, dtype=jnp.int32)[None, :])
                     .astype(jnp.int32), axis=0)
    start = jnp.cumsum(counts) - counts
    n_blk = (counts + MOE_ROWS - 1) // MOE_ROWS
    blk_end = jnp.cumsum(n_blk)
    first_blk = blk_end - n_blk
    slot = jnp.arange(cap, dtype=jnp.int32)
    slot_e = jnp.minimum(jnp.sum((slot[:, None] // MOE_ROWS >= blk_end[None, :]).astype(jnp.int32),
                                 axis=1), N_EXPERTS - 1)
    within = slot - first_blk[slot_e] * MOE_ROWS
    valid = within < counts[slot_e]
    j = order[jnp.clip(start[slot_e] + within, 0, nk - 1)]
    src = jnp.where(valid, j // TOP_K, 0).astype(jnp.int32)
    dst = jnp.where(valid, (j % TOP_K) * n + j // TOP_K, 0).astype(jnp.int32)
    shape = (cap // MOE_ROWS, MOE_ROWS)
    return (src.reshape(shape), dst.reshape(shape), first_blk.astype(jnp.int32),
            n_blk.astype(jnp.int32), counts.astype(jnp.int32))


def _final_kernel(x1_ref, y0_ref, y1_ref, y2_ref, y3_ref, gate_ref, mod_ref, nf_ref, o_ref):
    gates = gate_ref[...]
    ffn = gates[:, 0:1] * y0_ref[...]
    for j, ref in enumerate((y1_ref, y2_ref, y3_ref), start=1):
        ffn = ffn + gates[:, j:j + 1] * ref[...]
    g2 = mod_ref[0][:, 5 * D_MODEL:6 * D_MODEL]
    x2 = x1_ref[...] + g2 * ffn
    ms = jnp.mean(x2 * x2, axis=-1, keepdims=True)
    o_ref[...] = x2 * lax.rsqrt(ms + NORM_EPS) * nf_ref[...]


def _final(x1, ys, gates, mods3, mod_row, norm_f):
    n, d = x1.shape
    nb = n // TM
    row = lambda i: (i, 0)
    return pl.pallas_call(
        _final_kernel,
        out_shape=jax.ShapeDtypeStruct((n, d), F32),
        grid=(nb,),
        in_specs=[pl.BlockSpec((TM, d), row)]
                 + [pl.BlockSpec((TM, d), functools.partial(lambda j, i: (j * nb + i, 0), j))
                    for j in range(TOP_K)]
                 + [pl.BlockSpec((TM, LANES), row),
                    pl.BlockSpec((1, 1, 6 * d), lambda i: (mod_row(i), 0, 0)),
                    pl.BlockSpec((1, d), lambda i: (0, 0))],
        out_specs=pl.BlockSpec((TM, d), row),
        compiler_params=_cparams("arbitrary"),
        name="combine_final",
    )(x1, ys, ys, ys, ys, gates, mods3, norm_f.reshape(1, d))


def _rope_tables(t_ctx, t_lat):
    rows = np.arange(t_lat) // GRID_W
    cols = np.arange(t_lat) % GRID_W
    inv = ROPE_THETA ** (-np.arange(0, AXIS_DIM, 2, dtype=np.float32) / AXIS_DIM)
    inv = jnp.asarray(inv, F32)
    ang_r = jnp.asarray(rows, F32)[:, None] * inv[None, :]
    ang_c = jnp.asarray(cols, F32)[:, None] * inv[None, :]
    cos = jnp.concatenate([jnp.cos(ang_r)] * 2 + [jnp.cos(ang_c)] * 2, axis=1)
    sin = jnp.concatenate([-jnp.sin(ang_r), jnp.sin(ang_r), -jnp.sin(ang_c), jnp.sin(ang_c)], axis=1)
    cos = jnp.concatenate([jnp.ones((t_ctx, HEAD), F32), cos], axis=0)
    sin = jnp.concatenate([jnp.zeros((t_ctx, HEAD), F32), sin], axis=0)
    return jnp.concatenate([cos, cos], axis=1), jnp.concatenate([sin, sin], axis=1)


def kernel(x_prompt, x_sample, cache_k, cache_v, state_rwkv_fwd, state_rwkv_bwd, c, c_ctx, w_ada, b_ada, norm1, norm2, w_in, rw_w0, rw_w2, rw_a0, rw_a2, rw_g2, rw_kk, rw_ka, rw_rk, rw_ln_w, rw_ln_b, q_norm, k_norm, w_up_r, w_up_a, w_out, w_router, b_router, w_moe_in, b_moe_in, w_moe_out, b_moe_out, norm_f):
    depth = w_in.shape[0]
    assert depth == 1, "single trunk layer"
    bc, tc, d = x_prompt.shape
    bl, tl, _ = x_sample.shape
    nc, nl = bc * tc, bl * tl
    n = nc + nl
    assert d == D_MODEL and tc == TM and tl % TM == 0 and nc % tl == 0
    l = 0

    x_all = jnp.concatenate([x_prompt.reshape(nc, d), x_sample.reshape(nl, d)], axis=0)
    cond = jnp.concatenate([c_ctx[None, :], c, jnp.zeros((16 - 1 - bl, d), F32)], axis=0)
    mods3 = _adaln(cond, w_ada[l], b_ada[l]).reshape(16, 1, 6 * d)
    nbc = nc // TM
    per_seq = tl // TM
    mod_row = lambda i: jnp.where(i < nbc, 0, 1 + (i - nbc) // per_seq)
    tab_row = lambda i: jnp.where(i < nbc, 0, 1 + (i - nbc) % per_seq)

    z = dict(zip([name for name, _ in IN_SPLITS],
                 _proj_in(x_all, mods3, norm1[l], w_in[l].astype(BF16), mod_row)))

    ones_bd = jnp.asarray(np.kron(np.eye(H_R), np.ones((HEAD, HEAD))), BF16)
    pad_lo = lambda w: jnp.concatenate([w, jnp.zeros_like(w)], axis=0)
    pad_hi = lambda w: jnp.concatenate([jnp.zeros_like(w), w], axis=0)
    prep_p = dict(
        w0=rw_w0[l], a0=rw_a0[l],
        w2=jnp.stack([pad_lo(rw_w2[l, 0]), pad_hi(rw_w2[l, 1])]).astype(BF16),
        a2=jnp.stack([pad_lo(rw_a2[l, 0]), pad_hi(rw_a2[l, 1])]).astype(BF16),
        g2=rw_g2[l].astype(BF16),
        kkw=rw_kk[l].reshape(1, D_R), kaw=rw_ka[l].reshape(1, D_R), rk=rw_rk[l].reshape(1, D_R),
        qn=jnp.tile(q_norm[l], N_HEADS).reshape(1, D_A), kn=jnp.tile(k_norm[l], N_KV).reshape(1, D_KV),
        ones=ones_bd,
        tri=jnp.asarray(np.stack([np.kron(np.eye(TM // CHUNK), np.tril(np.ones((CHUNK, CHUNK)))),
                                  np.kron(np.eye(TM // CHUNK), np.triu(np.ones((CHUNK, CHUNK))))]), BF16),
        chunk_ones=jnp.asarray(np.kron(np.eye(TM // CHUNK), np.ones((CHUNK, CHUNK))), BF16))
    cos_tab, sin_tab = _rope_tables(tc, tl)
    sc, gt, vb, bonus, g, qr, kn, kr = _prep(z, cos_tab, sin_tab, tab_row, prep_p)

    s0 = jnp.concatenate([jnp.zeros((2, bc, H_R, HEAD, HEAD), F32),
                          jnp.stack([state_rwkv_fwd[:, l], state_rwkv_bwd[:, l]])], axis=1)
    y_f, y_b, s_fin = _rwkv_scan(sc, gt, vb, s0, [tc] * bc + [tl] * bl)

    ya_ctx = _attention(qr, kr, z["va"], 0, bc, tc)
    ya_lat = _attention(qr, kr, z["va"], nc, bl, tl,
                        cache_k[:, l].reshape(bl, -1, D_KV), cache_v[:, l].reshape(bl, -1, D_KV))

    post_p = dict(
        lnw=rw_ln_w[l].reshape(1, D_R), lnb=rw_ln_b[l].reshape(1, D_R),
        wur=w_up_r[l].astype(BF16), wua=w_up_a[l].astype(BF16), wo=w_out[l].astype(BF16),
        n2=norm2[l].reshape(1, d),
        wr=jnp.pad(w_router[l], ((0, 0), (0, LANES - N_EXPERTS))),
        br=jnp.pad(b_router[l], (0, LANES - N_EXPERTS)).reshape(1, LANES),
        ones=ones_bd)
    x1, h2, top_idx, gates = _post(y_f, y_b, bonus, g, ya_ctx, ya_lat, z["gate_r"], z["gate_a"],
                                   x_all, mods3, mod_row, post_p)

    src, dst, first_blk, n_blk, counts = _route(top_idx[:, :TOP_K], n)
    ys = _moe(h2, src, dst, first_blk, n_blk, counts,
              w_moe_in[l], b_moe_in[l], w_moe_out[l], b_moe_out[l])
    y_all = _final(x1, ys, gates, mods3, mod_row, norm_f)

    y_prompt = y_all[:nc].reshape(bc, tc, d)
    y_sample = y_all[nc:].reshape(bl, tl, d)
    new_cache_k = kn[:nc].reshape(bc, 1, tc, N_KV, HEAD)
    new_cache_v = z["va"][:nc].reshape(bc, 1, tc, N_KV, HEAD)
    new_state_fwd = s_fin[0, :bc][:, None]
    new_state_bwd = s_fin[1, :bc][:, None]
    return (y_prompt, y_sample, new_cache_k, new_cache_v, new_state_fwd, new_state_bwd)
```

```python
import functools

import numpy as np
import jax
import jax.numpy as jnp
from jax import lax
from jax.experimental import pallas as pl
from jax.experimental.pallas import tpu as pltpu

F32 = jnp.float32
BF16 = jnp.bfloat16

D_MODEL = 1024
GRID_W = 64
NORM_EPS = 1e-6
HEAD = 64
H_R = 8
D_R = H_R * HEAD
DECAY_RANK = 64
AAA_RANK = 64
GATE_RANK = 128
GN_EPS = 64e-5
N_HEADS = 8
N_KV = 2
D_A = N_HEADS * HEAD
D_KV = N_KV * HEAD
AXIS_DIM = HEAD // 2
ROPE_THETA = 10000.0
N_EXPERTS = 32
TOP_K = 4
D_FF = D_MODEL
SWIGLU_ALPHA = 1.702
SWIGLU_LIMIT = 7.0
LORA_COLS = 2 * DECAY_RANK + 2 * AAA_RANK + GATE_RANK
IN_SPLITS = (("r", D_R), ("k", D_R), ("v", D_R), ("lora", LORA_COLS), ("q", D_A),
             ("ka", D_KV), ("va", D_KV), ("gate_r", D_MODEL), ("gate_a", D_MODEL))

LANES = 128
TM = 256
CHUNK = 64
SCAN_OPS = 6
MOE_ROWS = 256
VMEM_LIMIT = 56 * 1024 * 1024


def _cparams(*sem):
    return pltpu.CompilerParams(dimension_semantics=sem, vmem_limit_bytes=VMEM_LIMIT)


def _dot(a, b):
    return jnp.dot(a, b, preferred_element_type=F32)


def _dot_nt(a, b):
    return lax.dot_general(a, b, (((1,), (1,)), ((), ())), preferred_element_type=F32)


def _dot_tn(a, b):
    return lax.dot_general(a, b, (((0,), (0,)), ((), ())), preferred_element_type=F32)


def _split2(x):
    hi = x.astype(BF16)
    lo = (x - hi.astype(F32)).astype(BF16)
    return hi, lo


def _split3(x):
    x1 = x.astype(BF16)
    r1 = x - x1.astype(F32)
    x2 = r1.astype(BF16)
    x3 = (r1 - x2.astype(F32)).astype(BF16)
    return x1, x2, x3


def _head_sum(x, ones_bd):
    hi, lo = _split2(x)
    return _dot(hi, ones_bd) + _dot(lo, ones_bd)


def _sigmoid(x):
    return 1.0 / (1.0 + jnp.exp(-x))


def _adaln_kernel(c_ref, w_ref, b_ref, o_ref):
    c = c_ref[...]
    s = (c * _sigmoid(c)).astype(BF16)
    o_ref[...] = _dot(s, w_ref[...].astype(BF16)) + b_ref[...]


def _adaln(cond, w_ada, b_ada):
    rows, d = cond.shape
    cols = w_ada.shape[1]
    tn = 512
    return pl.pallas_call(
        _adaln_kernel,
        out_shape=jax.ShapeDtypeStruct((rows, cols), F32),
        grid=(cols // tn,),
        in_specs=[pl.BlockSpec((rows, d), lambda j: (0, 0)),
                  pl.BlockSpec((d, tn), lambda j: (0, j)),
                  pl.BlockSpec((1, tn), lambda j: (0, j))],
        out_specs=pl.BlockSpec((rows, tn), lambda j: (0, j)),
        compiler_params=_cparams("arbitrary"),
        name="adaln",
    )(cond, w_ada, b_ada.reshape(1, cols))


def _proj_in_kernel(x_ref, mod_ref, n1_ref, w_ref, *out_refs):
    x = x_ref[...]
    ms = jnp.mean(x * x, axis=-1, keepdims=True)
    hn = x * lax.rsqrt(ms + NORM_EPS) * n1_ref[...]
    mod = mod_ref[0]
    sh1 = mod[:, 0:D_MODEL]
    sc1 = mod[:, D_MODEL:2 * D_MODEL]
    h = (hn * (1.0 + sc1) + sh1).astype(BF16)
    c0 = 0
    for ref, (_, width) in zip(out_refs, IN_SPLITS):
        ref[...] = _dot(h, w_ref[:, c0:c0 + width])
        c0 += width


def _proj_in(x_all, mods3, norm1, w_in_bf, mod_row):
    n, d = x_all.shape
    cols = w_in_bf.shape[1]
    row = lambda i: (i, 0)
    return pl.pallas_call(
        _proj_in_kernel,
        out_shape=[jax.ShapeDtypeStruct((n, width), F32) for _, width in IN_SPLITS],
        grid=(n // TM,),
        in_specs=[pl.BlockSpec((TM, d), row),
                  pl.BlockSpec((1, 1, 6 * d), lambda i: (mod_row(i), 0, 0)),
                  pl.BlockSpec((1, d), lambda i: (0, 0)),
                  pl.BlockSpec((d, cols), lambda i: (0, 0))],
        out_specs=[pl.BlockSpec((TM, width), row) for _, width in IN_SPLITS],
        compiler_params=_cparams("arbitrary"),
        name="proj_in",
    )(x_all, mods3, norm1.reshape(1, d), w_in_bf)


def _prep_kernel(r_ref, k_ref, v_ref, lora_ref, q_ref, ka_ref, cos_ref, sin_ref,
                 w0_ref, w2_ref, a0_ref, a2_ref, g2_ref, kkw_ref, kaw_ref, rk_ref,
                 qn_ref, kn_ref, ones_ref, tri_ref, chunk_ones_ref,
                 sc_o, gt_o, vb_o, bonus_o, g_o, qr_o, kn_o, kr_o):
    ones = ones_ref[...]
    chunk_ones = chunk_ones_ref[...]
    r = r_ref[...]
    k = k_ref[...]
    v = v_ref[...]
    vb_o[...] = v.astype(BF16)
    lora = lora_ref[...]
    kk = k * kkw_ref[...]
    kk = kk * lax.rsqrt(_head_sum(kk * kk, ones) + 1e-12)
    th = jnp.tanh(lora[:, 0:LANES]).astype(BF16)
    al = lora[:, LANES:2 * LANES].astype(BF16)
    kd_sum = jnp.zeros_like(k)
    for d in range(2):
        u = w0_ref[d:d + 1, :] + _dot(th, w2_ref[d])
        w_log = -(jnp.maximum(-u, 0.0) + jnp.log1p(jnp.exp(-jnp.abs(u)))) - 0.5
        lw = -jnp.exp(w_log)
        a = _sigmoid(a0_ref[d:d + 1, :] + _dot(al, a2_ref[d]))
        kd = k * (1.0 + (a - 1.0) * kaw_ref[...])
        kd_sum = kd_sum + kd
        b = kk * a
        parts = _split3(lw)
        tri = tri_ref[d]
        L = _dot(tri, parts[0]) + _dot(tri, parts[1]) + _dot(tri, parts[2])
        Ltot = _dot(chunk_ones, parts[0]) + _dot(chunk_ones, parts[1]) + _dot(chunk_ones, parts[2])
        e_inv = jnp.exp(-L)
        e_rest = jnp.exp(Ltot - L)
        scan_ops = (kk * jnp.exp(L - lw), r * jnp.exp(L), b * e_inv, kd * e_inv, b * e_rest, kd * e_rest)
        for j, op in enumerate(scan_ops):
            sc_o[d, :, j * D_R:(j + 1) * D_R] = op.astype(BF16)
        g_tot = jnp.exp(Ltot)
        for j in range(TM // CHUNK):
            gt_o[d, j] = g_tot[j * CHUNK:j * CHUNK + 1, :]
    bonus_o[...] = _head_sum(r * kd_sum * rk_ref[...], ones) * v
    g_o[...] = _dot(_sigmoid(lora[:, 2 * LANES:3 * LANES]).astype(BF16), g2_ref[...])
    q = q_ref[...]
    qn = q * lax.rsqrt(_head_sum(q * q, ones) * (1.0 / HEAD) + NORM_EPS) * qn_ref[...]
    ka = ka_ref[...]
    ones_kv = ones[0:D_KV, 0:D_KV]
    kn = ka * lax.rsqrt(_head_sum(ka * ka, ones_kv) * (1.0 / HEAD) + NORM_EPS) * kn_ref[...]
    kn_o[...] = kn
    cos = cos_ref[...]
    sin = sin_ref[...]
    half = AXIS_DIM // 2

    def rope(t, cos_t, sin_t):
        width = t.shape[1]
        lane = lax.broadcasted_iota(jnp.int32, t.shape, 1)
        first = (lane % AXIS_DIM) < half
        swapped = jnp.where(first, pltpu.roll(t, width - half, 1), pltpu.roll(t, half, 1))
        return t * cos_t + swapped * sin_t

    cos4 = jnp.concatenate([cos] * (D_A // D_KV), axis=1)
    sin4 = jnp.concatenate([sin] * (D_A // D_KV), axis=1)
    qr_o[...] = (rope(qn, cos4, sin4) * (HEAD ** -0.5)).astype(BF16)
    kr_o[...] = rope(kn, cos, sin).astype(BF16)


def _prep(z, cos_tab, sin_tab, tab_row, p):
    n = z["r"].shape[0]
    row = lambda i: (i, 0)
    full = lambda *shape: pl.BlockSpec(shape, lambda i: (0,) * len(shape))
    tok = lambda width: pl.BlockSpec((TM, width), row)
    tab = pl.BlockSpec((TM, D_KV), lambda i: (tab_row(i), 0))
    cpb = TM // CHUNK
    return pl.pallas_call(
        _prep_kernel,
        out_shape=[jax.ShapeDtypeStruct((2, n, SCAN_OPS * D_R), BF16),
                   jax.ShapeDtypeStruct((2, n // CHUNK, 1, D_R), F32),
                   jax.ShapeDtypeStruct((n, D_R), BF16),
                   jax.ShapeDtypeStruct((n, D_R), F32),
                   jax.ShapeDtypeStruct((n, D_R), F32),
                   jax.ShapeDtypeStruct((n, D_A), BF16),
                   jax.ShapeDtypeStruct((n, D_KV), F32),
                   jax.ShapeDtypeStruct((n, D_KV), BF16)],
        grid=(n // TM,),
        in_specs=[tok(D_R), tok(D_R), tok(D_R), tok(LORA_COLS), tok(D_A), tok(D_KV), tab, tab,
                  full(2, D_R), full(2, LANES, D_R), full(2, D_R), full(2, LANES, D_R),
                  full(GATE_RANK, D_R), full(1, D_R), full(1, D_R), full(1, D_R),
                  full(1, D_A), full(1, D_KV), full(D_R, D_R), full(2, TM, TM), full(TM, TM)],
        out_specs=[pl.BlockSpec((2, TM, SCAN_OPS * D_R), lambda i: (0, i, 0)),
                   pl.BlockSpec((2, cpb, 1, D_R), lambda i: (0, i, 0, 0)),
                   tok(D_R), tok(D_R), tok(D_R), tok(D_A), tok(D_KV), tok(D_KV)],
        compiler_params=_cparams("arbitrary"),
        name="mixer_prep",
    )(z["r"], z["k"], z["v"], z["lora"], z["q"], z["ka"], cos_tab, sin_tab,
      p["w0"], p["w2"], p["a0"], p["a2"], p["g2"], p["kkw"], p["kaw"], p["rk"],
      p["qn"], p["kn"], p["ones"], p["tri"], p["chunk_ones"])


def _scan_kernel(blkf_ref, blkb_ref, seq_ref, first_ref, last_ref,
                 scf_ref, scb_ref, vf_ref, vb_ref, gf_ref, gb_ref, s0_ref,
                 yf_ref, yb_ref, sfin_ref, state):
    step = pl.program_id(0)
    C = CHUNK
    pairs = D_R // LANES
    chains = [(d, p) for d in range(2) for p in range(pairs)]

    lane = lax.broadcasted_iota(jnp.int32, (1, LANES), 1)
    low = lane < HEAD

    @pl.when(first_ref[step] == 1)
    def _():
        zero = jnp.zeros((HEAD, HEAD), F32)
        for d, p in chains:
            top = jnp.concatenate([s0_ref[d, 0, 2 * p], zero], axis=1)
            bot = jnp.concatenate([zero, s0_ref[d, 0, 2 * p + 1]], axis=1)
            state[d * pairs + p] = jnp.concatenate([top, bot], axis=0)

    t2 = lax.broadcasted_iota(jnp.int32, (C, LANES), 0)
    s2 = lax.broadcasted_iota(jnp.int32, (C, LANES), 1) % C
    strict = (s2 < t2, s2 > t2)
    incl = (s2 <= t2, s2 >= t2)
    eye2 = jnp.where(s2 == t2, 1.0, 0.0)
    r2 = lax.broadcasted_iota(jnp.int32, (LANES, LANES), 0)
    c2 = lax.broadcasted_iota(jnp.int32, (LANES, LANES), 1)
    same_head = (r2 // HEAD) == (c2 // HEAD)

    sc_refs = (scf_ref, scb_ref)
    v_refs = (vf_ref, vb_ref)
    g_refs = (gf_ref, gb_ref)
    y_refs = (yf_ref, yb_ref)

    def operand(d, p, j):
        c0 = j * D_R + p * LANES
        return sc_refs[d][0, :, c0:c0 + LANES]

    def stack(x):
        zero = jnp.zeros_like(x)
        return jnp.concatenate([jnp.where(low, x, zero), jnp.where(low, zero, x)], axis=0)

    Qk = [operand(d, p, 0) for d, p in chains]
    Qr = [operand(d, p, 1) for d, p in chains]
    v_p = [v_refs[d][:, p * LANES:(p + 1) * LANES] for d, p in chains]
    v_st = [stack(x) for x in v_p]
    Z0 = [state[i] for i in range(len(chains))]
    Z0b = [z.astype(BF16) for z in Z0]
    m = [_dot_nt(jnp.concatenate([Qk[i], Qr[i]], axis=0),
                 jnp.concatenate([stack(operand(d, p, 2)), stack(operand(d, p, 3))], axis=0))
         for i, (d, p) in enumerate(chains)]
    A = [jnp.where(strict[d], m[i][0:C, 0:LANES], 0.0) for i, (d, p) in enumerate(chains)]
    Bm = [jnp.where(strict[d], m[i][0:C, LANES:2 * LANES], 0.0).astype(BF16)
          for i, (d, p) in enumerate(chains)]
    Pb = [jnp.where(incl[d], m[i][C:2 * C, 0:LANES], 0.0).astype(BF16) for i, (d, p) in enumerate(chains)]
    Pk = [jnp.where(incl[d], m[i][C:2 * C, LANES:2 * LANES], 0.0).astype(BF16)
          for i, (d, p) in enumerate(chains)]
    BV = [_dot(Bm[i], v_st[i]) for i in range(len(chains))]
    T = [eye2 - a for a in A]
    Ap = [_dot(a.astype(BF16), stack(a.astype(BF16))).astype(BF16) for a in A]
    n = 2
    while 2 * n < C:
        R = [_dot(jnp.concatenate([Ap[i], T[i].astype(BF16)], axis=0), stack(Ap[i]))
             for i in range(len(chains))]
        Ap = [x[0:C].astype(BF16) for x in R]
        T = [T[i] + R[i][C:2 * C] for i in range(len(chains))]
        n *= 2
    T = [T[i] + _dot(T[i].astype(BF16), stack(Ap[i])) for i in range(len(chains))]
    WU = [_dot(T[i].astype(BF16),
               jnp.concatenate([stack(Qk[i]), stack(BV[i].astype(BF16))], axis=1))
          for i in range(len(chains))]
    Wq = [x[:, 0:LANES].astype(BF16) for x in WU]
    Uv = [x[:, LANES:2 * LANES] for x in WU]
    PW = [_dot(Pb[i], jnp.concatenate([stack(Wq[i]), stack(Uv[i].astype(BF16))], axis=1))
          for i in range(len(chains))]
    PkV = [_dot(Pk[i], v_st[i]) for i in range(len(chains))]
    U = [_dot_nt(Wq[i], Z0b[i]) + Uv[i] for i in range(len(chains))]
    Yq = [(Qr[i].astype(F32) - PW[i][:, 0:LANES]).astype(BF16) for i in range(len(chains))]
    for i, (d, p) in enumerate(chains):
        y_refs[d][:, p * LANES:(p + 1) * LANES] = (_dot_nt(Yq[i], Z0b[i]) + PkV[i]
                                                   - PW[i][:, LANES:2 * LANES])
    for i, (d, p) in enumerate(chains):
        lhs = jnp.concatenate([v_p[i], (-U[i]).astype(BF16)], axis=0)
        rhs = jnp.concatenate([operand(d, p, 5), operand(d, p, 4)], axis=0)
        upd = _dot_tn(lhs, rhs)
        g_tot = g_refs[d][0, 0][:, p * LANES:(p + 1) * LANES]
        state[i] = Z0[i] * g_tot + jnp.where(same_head, upd, 0.0)

    @pl.when(last_ref[step] == 1)
    def _():
        for d, p in chains:
            z = state[d * pairs + p]
            sfin_ref[d, 0, 2 * p] = z[0:HEAD, 0:HEAD]
            sfin_ref[d, 0, 2 * p + 1] = z[HEAD:2 * HEAD, HEAD:2 * HEAD]


def _rwkv_scan(sc, gt, vb, s0, seq_lens):
    n = vb.shape[0]
    blk_f, blk_b, seq, first, last = [], [], [], [], []
    base = 0
    for i, t_len in enumerate(seq_lens):
        nch = t_len // CHUNK
        blk_f += [base + cc for cc in range(nch)]
        blk_b += [base + nch - 1 - cc for cc in range(nch)]
        seq += [i] * nch
        first += [1] + [0] * (nch - 1)
        last += [0] * (nch - 1) + [1]
        base += nch
    steps = len(seq)
    tabs = [jnp.asarray(np.array(t, np.int32)) for t in (blk_f, blk_b, seq, first, last)]
    width = SCAN_OPS * D_R
    sc_f = pl.BlockSpec((1, CHUNK, width), lambda s, bf, bb, *_: (0, bf[s], 0))
    sc_b = pl.BlockSpec((1, CHUNK, width), lambda s, bf, bb, *_: (1, bb[s], 0))
    tok_f = pl.BlockSpec((CHUNK, D_R), lambda s, bf, bb, *_: (bf[s], 0))
    tok_b = pl.BlockSpec((CHUNK, D_R), lambda s, bf, bb, *_: (bb[s], 0))
    gt_f = pl.BlockSpec((1, 1, 1, D_R), lambda s, bf, bb, *_: (0, bf[s], 0, 0))
    gt_b = pl.BlockSpec((1, 1, 1, D_R), lambda s, bf, bb, *_: (1, bb[s], 0, 0))
    st = pl.BlockSpec((2, 1, H_R, HEAD, HEAD), lambda s, bf, bb, seq, *_: (0, seq[s], 0, 0, 0))
    grid_spec = pltpu.PrefetchScalarGridSpec(
        num_scalar_prefetch=5,
        grid=(steps,),
        in_specs=[sc_f, sc_b, tok_f, tok_b, gt_f, gt_b, st],
        out_specs=[tok_f, tok_b, st],
        scratch_shapes=[pltpu.VMEM((2 * D_R // LANES, LANES, LANES), F32)])
    y_f, y_b, sfin = pl.pallas_call(
        _scan_kernel,
        out_shape=[jax.ShapeDtypeStruct((n, D_R), F32),
                   jax.ShapeDtypeStruct((n, D_R), F32),
                   jax.ShapeDtypeStruct((2, len(seq_lens), H_R, HEAD, HEAD), F32)],
        grid_spec=grid_spec,
        compiler_params=_cparams("arbitrary"),
        name="rwkv_scan",
    )(*tabs, sc, sc, vb, vb, gt, gt, s0)
    return y_f, y_b, sfin


def _attn_kernel(*refs, tq, with_cache):
    if with_cache:
        q_ref, k_ref, v_ref, ck_ref, cv_ref, o_ref = refs
        kx = jnp.concatenate([k_ref[...], ck_ref[0].astype(BF16)], axis=0)
        vx = jnp.concatenate([v_ref[...], cv_ref[0]], axis=0).astype(BF16)
    else:
        q_ref, k_ref, v_ref, o_ref = refs
        kx = k_ref[...]
        vx = v_ref[...].astype(BF16)
    lane = lax.broadcasted_iota(jnp.int32, (1, D_KV), 1)
    low = lane < HEAD
    k_sw = pltpu.roll(kx.astype(F32), HEAD, 1).astype(BF16)
    v_sw = pltpu.roll(vx.astype(F32), HEAD, 1).astype(BF16)
    group = N_HEADS // N_KV
    for p in range(D_A // LANES):
        g = (2 * p) // group
        keep = low if g == 0 else jnp.logical_not(low)
        kd = jnp.where(keep, kx, k_sw)
        vd = jnp.where(keep, vx, v_sw)
        qp = q_ref[:, p * LANES:(p + 1) * LANES]
        zero = jnp.zeros_like(qp)
        qs = jnp.concatenate([jnp.where(low, qp, zero), jnp.where(low, zero, qp)], axis=0)
        s = _dot_nt(qs, kd)
        mx = jnp.max(s, axis=-1, keepdims=True)
        e = jnp.exp(s - mx)
        den = jnp.sum(e, axis=-1, keepdims=True)
        o = _dot(e.astype(BF16), vd) / den
        o_ref[:, p * LANES:(p + 1) * LANES] = jnp.where(low, o[0:tq], o[tq:2 * tq]).astype(BF16)


def _attention(qr, kr, va, row0, batch, t_len, cache_k=None, cache_v=None):
    tq = min(t_len, 256)
    nq = t_len // tq
    qblk0 = row0 // tq
    sblk0 = row0 // t_len
    with_cache = cache_k is not None
    in_specs = [pl.BlockSpec((tq, D_A), lambda b, i: (qblk0 + b * nq + i, 0)),
                pl.BlockSpec((t_len, D_KV), lambda b, i: (sblk0 + b, 0)),
                pl.BlockSpec((t_len, D_KV), lambda b, i: (sblk0 + b, 0))]
    args = [qr, kr, va]
    if with_cache:
        past = cache_k.shape[1]
        in_specs += [pl.BlockSpec((1, past, D_KV), lambda b, i: (b, 0, 0))] * 2
        args += [cache_k, cache_v]
    return pl.pallas_call(
        functools.partial(_attn_kernel, tq=tq, with_cache=with_cache),
        out_shape=jax.ShapeDtypeStruct((batch * t_len, D_A), BF16),
        grid=(batch, nq),
        in_specs=in_specs,
        out_specs=pl.BlockSpec((tq, D_A), lambda b, i: (b * nq + i, 0)),
        compiler_params=_cparams("arbitrary", "arbitrary"),
        name="attention_cache" if with_cache else "attention_ctx",
    )(*args)


def _post_kernel(yf_ref, yb_ref, bonus_ref, g_ref, yac_ref, yal_ref, gr_ref, ga_ref, x_ref, mod_ref,
                 lnw_ref, lnb_ref, wur_ref, wua_ref, wo_ref, n2_ref, wr_ref, br_ref, ones_ref,
                 x1_o, h2_o, idx_o, gate_o, *, ctx_blocks):
    ones = ones_ref[...]
    y_a = jnp.where(pl.program_id(0) < ctx_blocks, yac_ref[...], yal_ref[...])
    y = yf_ref[...] + yb_ref[...]
    mu = _head_sum(y, ones) * (1.0 / HEAD)
    yc = y - mu
    var = _head_sum(yc * yc, ones) * (1.0 / HEAD)
    yn = yc * lax.rsqrt(var + GN_EPS) * lnw_ref[...] + lnb_ref[...] + bonus_ref[...]
    y_r = (yn * g_ref[...]).astype(BF16)
    merged = (_sigmoid(gr_ref[...]) * _dot(y_r, wur_ref[...])
              + _sigmoid(ga_ref[...]) * _dot(y_a, wua_ref[...]))
    mix = _dot(merged.astype(BF16), wo_ref[...])
    mod = mod_ref[0]
    g1 = mod[:, 2 * D_MODEL:3 * D_MODEL]
    sh2 = mod[:, 3 * D_MODEL:4 * D_MODEL]
    sc2 = mod[:, 4 * D_MODEL:5 * D_MODEL]
    x1 = x_ref[...] + g1 * mix
    x1_o[...] = x1
    ms = jnp.mean(x1 * x1, axis=-1, keepdims=True)
    h2 = x1 * lax.rsqrt(ms + NORM_EPS) * n2_ref[...] * (1.0 + sc2) + sh2
    h2_o[...] = h2
    hh, hl = _split2(h2)
    wh, wl = _split2(wr_ref[...])
    logits = _dot(hh, wh) + _dot(hh, wl) + _dot(hl, wh) + br_ref[...]
    lane_i = lax.broadcasted_iota(jnp.int32, logits.shape, 1)
    lane = lane_i.astype(F32)
    neg = jnp.float32(-jnp.inf)
    cur = jnp.where(lane_i < N_EXPERTS, logits, neg)
    vals, idxs = [], []
    for _ in range(TOP_K):
        mx = jnp.max(cur, axis=-1, keepdims=True)
        ix = jnp.min(jnp.where(cur == mx, lane, float(LANES)), axis=-1, keepdims=True)
        vals.append(mx)
        idxs.append(ix)
        cur = jnp.where(lane == ix, neg, cur)
    es = [jnp.exp(val - vals[0]) for val in vals]
    den = es[0] + es[1] + es[2] + es[3]
    idx_out = jnp.zeros(logits.shape, jnp.int32)
    gate_out = jnp.zeros(logits.shape, F32)
    for j in range(TOP_K):
        idx_out = jnp.where(lane_i == j, idxs[j].astype(jnp.int32), idx_out)
        gate_out = jnp.where(lane_i == j, es[j] / den, gate_out)
    idx_o[...] = idx_out
    gate_o[...] = gate_out


def _post(y_f, y_b, bonus, g, ya_ctx, ya_lat, gate_r, gate_a, x_all, mods3, mod_row, p):
    n = x_all.shape[0]
    nbc = ya_ctx.shape[0] // TM
    row = lambda i: (i, 0)
    full = lambda *shape: pl.BlockSpec(shape, lambda i: (0,) * len(shape))
    tok = lambda width: pl.BlockSpec((TM, width), row)
    return pl.pallas_call(
        functools.partial(_post_kernel, ctx_blocks=nbc),
        out_shape=[jax.ShapeDtypeStruct((n, D_MODEL), F32),
                   jax.ShapeDtypeStruct((n, D_MODEL), F32),
                   jax.ShapeDtypeStruct((n, LANES), jnp.int32),
                   jax.ShapeDtypeStruct((n, LANES), F32)],
        grid=(n // TM,),
        in_specs=[tok(D_R), tok(D_R), tok(D_R), tok(D_R),
                  pl.BlockSpec((TM, D_A), lambda i: (jnp.minimum(i, nbc - 1), 0)),
                  pl.BlockSpec((TM, D_A), lambda i: (jnp.maximum(i - nbc, 0), 0)),
                  tok(D_MODEL), tok(D_MODEL), tok(D_MODEL),
                  pl.BlockSpec((1, 1, 6 * D_MODEL), lambda i: (mod_row(i), 0, 0)),
                  full(1, D_R), full(1, D_R), full(D_R, D_MODEL), full(D_A, D_MODEL),
                  full(D_MODEL, D_MODEL), full(1, D_MODEL), full(D_MODEL, LANES), full(1, LANES),
                  full(D_R, D_R)],
        out_specs=[tok(D_MODEL), tok(D_MODEL), tok(LANES), tok(LANES)],
        compiler_params=_cparams("arbitrary"),
        name="mixer_post",
    )(y_f, y_b, bonus, g, ya_ctx, ya_lat, gate_r, gate_a, x_all, mods3,
      p["lnw"], p["lnb"], p["wur"], p["wua"], p["wo"], p["n2"], p["wr"], p["br"], p["ones"])


def _moe_kernel(blk_e_ref, blk_first_ref, used_ref,
                src0_ref, src_next_ref, dst_prev_ref, h_hbm, w1_ref, b1_ref, w2_ref, b2_ref,
                out_hbm, w1b, w2b, xs, ys, sem_g, sem_s, sem_t, *, n_rows):
    b = pl.program_id(0)
    used = used_ref[0]
    R = MOE_ROWS
    x_cur = b % 2
    y_cur = b % 3
    y_prev = (b + 2) % 3

    def gather(table_ref, buf):
        for i in range(R):
            pltpu.make_async_copy(h_hbm.at[pl.ds(table_ref[0, 0, i], 1)], xs.at[buf, pl.ds(i, 1)],
                                  sem_g.at[buf]).start(priority=i % 2)

    def scatter_prev():
        for i in range(R):
            pltpu.make_async_copy(ys.at[y_prev, pl.ds(i, 1)], out_hbm.at[pl.ds(dst_prev_ref[0, 0, i], 1)],
                                  sem_s.at[y_prev]).start(priority=i % 2)

    @pl.when(b == 0)
    def _():
        ys[...] = jnp.zeros_like(ys)
        for s in range(2):
            band = pltpu.make_async_copy(ys.at[s], out_hbm.at[pl.ds(n_rows + s * R, R)], sem_t)
            band.start()
            band.wait()
        gather(src0_ref, 0)

    @pl.when(jnp.logical_and(blk_first_ref[b] == 1, b < used))
    def _():
        w1b[...] = w1_ref[0].astype(BF16)
        w2b[...] = w2_ref[0].astype(BF16)

    @pl.when(b <= used)
    def _():
        pltpu.make_async_copy(h_hbm.at[pl.ds(0, R)], xs.at[x_cur], sem_g.at[x_cur]).wait()

    @pl.when(jnp.logical_and(b >= 2, b <= used + 2))
    def _():
        pltpu.make_async_copy(ys.at[y_cur], out_hbm.at[pl.ds(0, R)], sem_s.at[y_cur]).wait()

    @pl.when(b < used)
    def _():
        gather(src_next_ref, 1 - x_cur)
        scatter_prev()
        hb = _dot(xs[x_cur].astype(BF16), w1b[...]) + b1_ref[0]
        glu = jnp.minimum(hb[:, 0:D_FF], SWIGLU_LIMIT)
        lin = jnp.clip(hb[:, D_FF:2 * D_FF], -SWIGLU_LIMIT, SWIGLU_LIMIT)
        act = glu * _sigmoid(SWIGLU_ALPHA * glu) * (lin + 1.0)
        ys[y_cur] = _dot(act.astype(BF16), w2b[...]) + b2_ref[0]

    @pl.when(b == used)
    def _():
        scatter_prev()


def _moe(h2, src, dst_prev, blk_e, blk_first, used, w1, b1, w2, b2):
    n, d = h2.shape
    nb = src.shape[0]
    steps = nb + 3
    cur = lambda b, *_: (jnp.minimum(b, nb), 0, 0)
    nxt = lambda b, *_: (jnp.minimum(b + 1, nb - 1), 0, 0)
    table = lambda index_map: pl.BlockSpec((1, 1, MOE_ROWS), index_map, memory_space=pltpu.SMEM)
    expert = lambda b, blk_e, *_: (blk_e[jnp.minimum(b, nb - 1)], 0, 0)
    grid_spec = pltpu.PrefetchScalarGridSpec(
        num_scalar_prefetch=3,
        grid=(steps,),
        in_specs=[table(lambda b, *_: (0, 0, 0)), table(nxt), table(cur),
                  pl.BlockSpec(memory_space=pl.ANY),
                  pl.BlockSpec((1, d, 2 * D_FF), expert),
                  pl.BlockSpec((1, 1, 2 * D_FF), expert),
                  pl.BlockSpec((1, D_FF, d), expert),
                  pl.BlockSpec((1, 1, d), expert)],
        out_specs=pl.BlockSpec(memory_space=pl.ANY),
        scratch_shapes=[pltpu.VMEM((d, 2 * D_FF), BF16),
                        pltpu.VMEM((D_FF, d), BF16),
                        pltpu.VMEM((2, MOE_ROWS, d), F32),
                        pltpu.VMEM((3, MOE_ROWS, d), F32),
                        pltpu.SemaphoreType.DMA((2,)),
                        pltpu.SemaphoreType.DMA((3,)),
                        pltpu.SemaphoreType.DMA(())])
    tab3 = lambda t: t.reshape(-1, 1, MOE_ROWS)
    pad = lambda t: jnp.concatenate([t, jnp.zeros((steps - nb,), jnp.int32)])
    return pl.pallas_call(
        functools.partial(_moe_kernel, n_rows=n * TOP_K),
        out_shape=jax.ShapeDtypeStruct((n * TOP_K + 2 * MOE_ROWS, d), F32),
        grid_spec=grid_spec,
        compiler_params=pltpu.CompilerParams(dimension_semantics=("arbitrary",),
                                             vmem_limit_bytes=VMEM_LIMIT, has_side_effects=True),
        name="moe_experts",
    )(blk_e, pad(blk_first), used, tab3(src), tab3(src), tab3(dst_prev), h2, w1,
      b1.reshape(N_EXPERTS, 1, 2 * D_FF), w2, b2.reshape(N_EXPERTS, 1, d))


def _route(top_idx, n):
    nk = n * TOP_K
    cap = nk + N_EXPERTS * MOE_ROWS
    nb = cap // MOE_ROWS
    flat_e = top_idx.reshape(nk)
    order = jnp.argsort(flat_e, stable=True).astype(jnp.int32)
    counts = jnp.sum((flat_e[:, None] == jnp.arange(N_EXPERTS, dtype=jnp.int32)[None, :])
                     .astype(jnp.int32), axis=0)
    start = jnp.cumsum(counts) - counts
    n_blk = (counts + MOE_ROWS - 1) // MOE_ROWS
    blk_end = jnp.cumsum(n_blk)
    first_blk = blk_end - n_blk
    blk = jnp.arange(nb, dtype=jnp.int32)
    blk_e = jnp.minimum(jnp.sum((blk[:, None] >= blk_end[None, :]).astype(jnp.int32), axis=1),
                        N_EXPERTS - 1)
    blk_first = (blk == first_blk[blk_e]).astype(jnp.int32)
    slot = jnp.arange(cap, dtype=jnp.int32)
    slot_e = blk_e[slot // MOE_ROWS]
    within = slot - first_blk[slot_e] * MOE_ROWS
    valid = jnp.logical_and(within < counts[slot_e], slot // MOE_ROWS < blk_end[-1])
    j = order[jnp.clip(start[slot_e] + within, 0, nk - 1)]
    src = jnp.where(valid, j // TOP_K, 0).astype(jnp.int32)
    scratch_row = nk + ((slot // MOE_ROWS) % 2) * MOE_ROWS + slot % MOE_ROWS
    dst = jnp.where(valid, (j % TOP_K) * n + j // TOP_K, scratch_row).astype(jnp.int32)
    lead = nk + MOE_ROWS + jnp.arange(MOE_ROWS, dtype=jnp.int32)
    dst_prev = jnp.concatenate([lead, dst]).reshape(nb + 1, MOE_ROWS)
    return (src.reshape(nb, MOE_ROWS), dst_prev, blk_e.astype(jnp.int32), blk_first,
            blk_end[-1:].astype(jnp.int32))


def _final_kernel(x1_ref, y0_ref, y1_ref, y2_ref, y3_ref, gate_ref, mod_ref, nf_ref, o_ref):
    gates = gate_ref[...]
    ffn = gates[:, 0:1] * y0_ref[...]
    for j, ref in enumerate((y1_ref, y2_ref, y3_ref), start=1):
        ffn = ffn + gates[:, j:j + 1] * ref[...]
    g2 = mod_ref[0][:, 5 * D_MODEL:6 * D_MODEL]
    x2 = x1_ref[...] + g2 * ffn
    ms = jnp.mean(x2 * x2, axis=-1, keepdims=True)
    o_ref[...] = x2 * lax.rsqrt(ms + NORM_EPS) * nf_ref[...]


def _final(x1, ys, gates, mods3, mod_row, norm_f):
    n, d = x1.shape
    nb = n // TM
    row = lambda i: (i, 0)
    return pl.pallas_call(
        _final_kernel,
        out_shape=jax.ShapeDtypeStruct((n, d), F32),
        grid=(nb,),
        in_specs=[pl.BlockSpec((TM, d), row)]
                 + [pl.BlockSpec((TM, d), functools.partial(lambda j, i: (j * nb + i, 0), j))
                    for j in range(TOP_K)]
                 + [pl.BlockSpec((TM, LANES), row),
                    pl.BlockSpec((1, 1, 6 * d), lambda i: (mod_row(i), 0, 0)),
                    pl.BlockSpec((1, d), lambda i: (0, 0))],
        out_specs=pl.BlockSpec((TM, d), row),
        compiler_params=_cparams("arbitrary"),
        name="combine_final",
    )(x1, ys, ys, ys, ys, gates, mods3, norm_f.reshape(1, d))


def _rope_tables(t_ctx, t_lat):
    rows = np.arange(t_lat) // GRID_W
    cols = np.arange(t_lat) % GRID_W
    inv = ROPE_THETA ** (-np.arange(0, AXIS_DIM, 2, dtype=np.float32) / AXIS_DIM)
    inv = jnp.asarray(inv, F32)
    ang_r = jnp.asarray(rows, F32)[:, None] * inv[None, :]
    ang_c = jnp.asarray(cols, F32)[:, None] * inv[None, :]
    cos = jnp.concatenate([jnp.cos(ang_r)] * 2 + [jnp.cos(ang_c)] * 2, axis=1)
    sin = jnp.concatenate([-jnp.sin(ang_r), jnp.sin(ang_r), -jnp.sin(ang_c), jnp.sin(ang_c)], axis=1)
    cos = jnp.concatenate([jnp.ones((t_ctx, HEAD), F32), cos], axis=0)
    sin = jnp.concatenate([jnp.zeros((t_ctx, HEAD), F32), sin], axis=0)
    return jnp.concatenate([cos, cos], axis=1), jnp.concatenate([sin, sin], axis=1)


def kernel(x_prompt, x_sample, cache_k, cache_v, state_rwkv_fwd, state_rwkv_bwd, c, c_ctx, w_ada, b_ada, norm1, norm2, w_in, rw_w0, rw_w2, rw_a0, rw_a2, rw_g2, rw_kk, rw_ka, rw_rk, rw_ln_w, rw_ln_b, q_norm, k_norm, w_up_r, w_up_a, w_out, w_router, b_router, w_moe_in, b_moe_in, w_moe_out, b_moe_out, norm_f):
    depth = w_in.shape[0]
    assert depth == 1, "single trunk layer"
    bc, tc, d = x_prompt.shape
    bl, tl, _ = x_sample.shape
    nc, nl = bc * tc, bl * tl
    n = nc + nl
    assert d == D_MODEL and tc == TM and tl % TM == 0 and nc % tl == 0
    l = 0

    x_all = jnp.concatenate([x_prompt.reshape(nc, d), x_sample.reshape(nl, d)], axis=0)
    cond = jnp.concatenate([c_ctx[None, :], c, jnp.zeros((16 - 1 - bl, d), F32)], axis=0)
    mods3 = _adaln(cond, w_ada[l], b_ada[l]).reshape(16, 1, 6 * d)
    nbc = nc // TM
    per_seq = tl // TM
    mod_row = lambda i: jnp.where(i < nbc, 0, 1 + (i - nbc) // per_seq)
    tab_row = lambda i: jnp.where(i < nbc, 0, 1 + (i - nbc) % per_seq)

    z = dict(zip([name for name, _ in IN_SPLITS],
                 _proj_in(x_all, mods3, norm1[l], w_in[l].astype(BF16), mod_row)))

    ones_bd = jnp.asarray(np.kron(np.eye(H_R), np.ones((HEAD, HEAD))), BF16)
    pad_lo = lambda w: jnp.concatenate([w, jnp.zeros_like(w)], axis=0)
    pad_hi = lambda w: jnp.concatenate([jnp.zeros_like(w), w], axis=0)
    prep_p = dict(
        w0=rw_w0[l], a0=rw_a0[l],
        w2=jnp.stack([pad_lo(rw_w2[l, 0]), pad_hi(rw_w2[l, 1])]).astype(BF16),
        a2=jnp.stack([pad_lo(rw_a2[l, 0]), pad_hi(rw_a2[l, 1])]).astype(BF16),
        g2=rw_g2[l].astype(BF16),
        kkw=rw_kk[l].reshape(1, D_R), kaw=rw_ka[l].reshape(1, D_R), rk=rw_rk[l].reshape(1, D_R),
        qn=jnp.tile(q_norm[l], N_HEADS).reshape(1, D_A), kn=jnp.tile(k_norm[l], N_KV).reshape(1, D_KV),
        ones=ones_bd,
        tri=jnp.asarray(np.stack([np.kron(np.eye(TM // CHUNK), np.tril(np.ones((CHUNK, CHUNK)))),
                                  np.kron(np.eye(TM // CHUNK), np.triu(np.ones((CHUNK, CHUNK))))]), BF16),
        chunk_ones=jnp.asarray(np.kron(np.eye(TM // CHUNK), np.ones((CHUNK, CHUNK))), BF16))
    cos_tab, sin_tab = _rope_tables(tc, tl)
    sc, gt, vb, bonus, g, qr, kn, kr = _prep(z, cos_tab, sin_tab, tab_row, prep_p)

    s0 = jnp.concatenate([jnp.zeros((2, bc, H_R, HEAD, HEAD), F32),
                          jnp.stack([state_rwkv_fwd[:, l], state_rwkv_bwd[:, l]])], axis=1)
    y_f, y_b, s_fin = _rwkv_scan(sc, gt, vb, s0, [tc] * bc + [tl] * bl)

    ya_ctx = _attention(qr, kr, z["va"], 0, bc, tc)
    ya_lat = _attention(qr, kr, z["va"], nc, bl, tl,
                        cache_k[:, l].reshape(bl, -1, D_KV), cache_v[:, l].reshape(bl, -1, D_KV))

    post_p = dict(
        lnw=rw_ln_w[l].reshape(1, D_R), lnb=rw_ln_b[l].reshape(1, D_R),
        wur=w_up_r[l].astype(BF16), wua=w_up_a[l].astype(BF16), wo=w_out[l].astype(BF16),
        n2=norm2[l].reshape(1, d),
        wr=jnp.pad(w_router[l], ((0, 0), (0, LANES - N_EXPERTS))),
        br=jnp.pad(b_router[l], (0, LANES - N_EXPERTS)).reshape(1, LANES),
        ones=ones_bd)
    x1, h2, top_idx, gates = _post(y_f, y_b, bonus, g, ya_ctx, ya_lat, z["gate_r"], z["gate_a"],
                                   x_all, mods3, mod_row, post_p)

    src, dst, blk_e, blk_first, used = _route(top_idx[:, :TOP_K], n)
    ys = _moe(h2, src, dst, blk_e, blk_first, used,
              w_moe_in[l], b_moe_in[l], w_moe_out[l], b_moe_out[l])
    y_all = _final(x1, ys, gates, mods3, mod_row, norm_f)

    y_prompt = y_all[:nc].reshape(bc, tc, d)
    y_sample = y_all[nc:].reshape(bl, tl, d)
    new_cache_k = kn[:nc].reshape(bc, 1, tc, N_KV, HEAD)
    new_cache_v = z["va"][:nc].reshape(bc, 1, tc, N_KV, HEAD)
    new_state_fwd = s_fin[0, :bc][:, None]
    new_state_bwd = s_fin[1, :bc][:, None]
    return (y_prompt, y_sample, new_cache_k, new_cache_v, new_state_fwd, new_state_bwd)
```

```python
import functools

import numpy as np
import jax
import jax.numpy as jnp
from jax import lax
from jax.experimental import pallas as pl
from jax.experimental.pallas import tpu as pltpu

F32 = jnp.float32
BF16 = jnp.bfloat16

D_MODEL = 1024
GRID_W = 64
NORM_EPS = 1e-6
HEAD = 64
H_R = 8
D_R = H_R * HEAD
DECAY_RANK = 64
AAA_RANK = 64
GATE_RANK = 128
GN_EPS = 64e-5
N_HEADS = 8
N_KV = 2
D_A = N_HEADS * HEAD
D_KV = N_KV * HEAD
AXIS_DIM = HEAD // 2
ROPE_THETA = 10000.0
N_EXPERTS = 32
TOP_K = 4
D_FF = D_MODEL
SWIGLU_ALPHA = 1.702
SWIGLU_LIMIT = 7.0
LORA_COLS = 2 * DECAY_RANK + 2 * AAA_RANK + GATE_RANK
IN_SPLITS = (("r", D_R), ("k", D_R), ("v", D_R), ("lora", LORA_COLS), ("q", D_A),
             ("ka", D_KV), ("va", D_KV), ("gate_r", D_MODEL), ("gate_a", D_MODEL))

LANES = 128
TM = 256
CHUNK = 64
SCAN_OPS = 6
SCAN_GROUP = 2
MOE_ROWS = 256
VMEM_LIMIT = 56 * 1024 * 1024


def _cparams(*sem):
    return pltpu.CompilerParams(dimension_semantics=sem, vmem_limit_bytes=VMEM_LIMIT)


def _dot(a, b):
    return jnp.dot(a, b, preferred_element_type=F32)


def _dot_nt(a, b):
    return lax.dot_general(a, b, (((1,), (1,)), ((), ())), preferred_element_type=F32)


def _dot_tn(a, b):
    return lax.dot_general(a, b, (((0,), (0,)), ((), ())), preferred_element_type=F32)


def _split2(x):
    hi = x.astype(BF16)
    lo = (x - hi.astype(F32)).astype(BF16)
    return hi, lo


def _split3(x):
    x1 = x.astype(BF16)
    r1 = x - x1.astype(F32)
    x2 = r1.astype(BF16)
    x3 = (r1 - x2.astype(F32)).astype(BF16)
    return x1, x2, x3


def _head_sum(x, ones_bd):
    hi, lo = _split2(x)
    return _dot(hi, ones_bd) + _dot(lo, ones_bd)


def _sigmoid(x):
    return 1.0 / (1.0 + jnp.exp(-x))


def _adaln_kernel(c_ref, w_ref, b_ref, o_ref):
    c = c_ref[...]
    s = (c * _sigmoid(c)).astype(BF16)
    o_ref[...] = _dot(s, w_ref[...].astype(BF16)) + b_ref[...]


def _adaln(cond, w_ada, b_ada):
    rows, d = cond.shape
    cols = w_ada.shape[1]
    tn = 512
    return pl.pallas_call(
        _adaln_kernel,
        out_shape=jax.ShapeDtypeStruct((rows, cols), F32),
        grid=(cols // tn,),
        in_specs=[pl.BlockSpec((rows, d), lambda j: (0, 0)),
                  pl.BlockSpec((d, tn), lambda j: (0, j)),
                  pl.BlockSpec((1, tn), lambda j: (0, j))],
        out_specs=pl.BlockSpec((rows, tn), lambda j: (0, j)),
        compiler_params=_cparams("arbitrary"),
        name="adaln",
    )(cond, w_ada, b_ada.reshape(1, cols))


def _token_specs(nbc, width):
    return [pl.BlockSpec((TM, width), lambda i: (jnp.minimum(i, nbc - 1), 0)),
            pl.BlockSpec((TM, width), lambda i: (jnp.maximum(i - nbc, 0), 0))]


def _proj_in_kernel(xc_ref, xl_ref, mod_ref, n1_ref, w_ref, *out_refs, ctx_blocks):
    x = jnp.where(pl.program_id(0) < ctx_blocks, xc_ref[...], xl_ref[...])
    ms = jnp.mean(x * x, axis=-1, keepdims=True)
    hn = x * lax.rsqrt(ms + NORM_EPS) * n1_ref[...]
    mod = mod_ref[0]
    sh1 = mod[:, 0:D_MODEL]
    sc1 = mod[:, D_MODEL:2 * D_MODEL]
    h = (hn * (1.0 + sc1) + sh1).astype(BF16)
    c0 = 0
    for ref, (_, width) in zip(out_refs, IN_SPLITS):
        ref[...] = _dot(h, w_ref[:, c0:c0 + width])
        c0 += width


def _proj_in(x_ctx, x_lat, mods3, norm1, w_in_bf, mod_row):
    d = x_ctx.shape[1]
    n = x_ctx.shape[0] + x_lat.shape[0]
    nbc = x_ctx.shape[0] // TM
    cols = w_in_bf.shape[1]
    row = lambda i: (i, 0)
    return pl.pallas_call(
        functools.partial(_proj_in_kernel, ctx_blocks=nbc),
        out_shape=[jax.ShapeDtypeStruct((n, width), F32) for _, width in IN_SPLITS],
        grid=(n // TM,),
        in_specs=_token_specs(nbc, d) + [
                  pl.BlockSpec((1, 1, 6 * d), lambda i: (mod_row(i), 0, 0)),
                  pl.BlockSpec((1, d), lambda i: (0, 0)),
                  pl.BlockSpec((d, cols), lambda i: (0, 0))],
        out_specs=[pl.BlockSpec((TM, width), row) for _, width in IN_SPLITS],
        compiler_params=_cparams("arbitrary"),
        name="proj_in",
    )(x_ctx, x_lat, mods3, norm1.reshape(1, d), w_in_bf)


def _prep_kernel(r_ref, k_ref, v_ref, lora_ref, q_ref, ka_ref, cos_ref, sin_ref,
                 w0_ref, w2_ref, a0_ref, a2_ref, g2_ref, kkw_ref, kaw_ref, rk_ref,
                 qn_ref, kn_ref, ones_ref, tri_ref, chunk_ones_ref,
                 sc_o, gt_o, vb_o, bonus_o, g_o, qr_o, kn_o, kr_o):
    ones = ones_ref[...]
    chunk_ones = chunk_ones_ref[...]
    r = r_ref[...]
    k = k_ref[...]
    v = v_ref[...]
    vb_o[...] = v.astype(BF16)
    lora = lora_ref[...]
    kk = k * kkw_ref[...]
    kk = kk * lax.rsqrt(_head_sum(kk * kk, ones) + 1e-12)
    th = jnp.tanh(lora[:, 0:LANES]).astype(BF16)
    al = lora[:, LANES:2 * LANES].astype(BF16)
    kd_sum = jnp.zeros_like(k)
    for d in range(2):
        u = w0_ref[d:d + 1, :] + _dot(th, w2_ref[d])
        w_log = -(jnp.maximum(-u, 0.0) + jnp.log1p(jnp.exp(-jnp.abs(u)))) - 0.5
        lw = -jnp.exp(w_log)
        a = _sigmoid(a0_ref[d:d + 1, :] + _dot(al, a2_ref[d]))
        kd = k * (1.0 + (a - 1.0) * kaw_ref[...])
        kd_sum = kd_sum + kd
        b = kk * a
        parts = _split3(lw)
        tri = tri_ref[d]
        L = _dot(tri, parts[0]) + _dot(tri, parts[1]) + _dot(tri, parts[2])
        Ltot = _dot(chunk_ones, parts[0]) + _dot(chunk_ones, parts[1]) + _dot(chunk_ones, parts[2])
        e_inv = jnp.exp(-L)
        e_rest = jnp.exp(Ltot - L)
        scan_ops = (kk * jnp.exp(L - lw), r * jnp.exp(L), b * e_inv, kd * e_inv, b * e_rest, kd * e_rest)
        for j, op in enumerate(scan_ops):
            sc_o[d, :, j * D_R:(j + 1) * D_R] = op.astype(BF16)
        g_tot = jnp.exp(Ltot)
        for j in range(TM // CHUNK):
            gt_o[d, j] = g_tot[j * CHUNK:j * CHUNK + 1, :]
    bonus_o[...] = _head_sum(r * kd_sum * rk_ref[...], ones) * v
    g_o[...] = _dot(_sigmoid(lora[:, 2 * LANES:3 * LANES]).astype(BF16), g2_ref[...])
    q = q_ref[...]
    qn = q * lax.rsqrt(_head_sum(q * q, ones) * (1.0 / HEAD) + NORM_EPS) * qn_ref[...]
    ka = ka_ref[...]
    ones_kv = ones[0:D_KV, 0:D_KV]
    kn = ka * lax.rsqrt(_head_sum(ka * ka, ones_kv) * (1.0 / HEAD) + NORM_EPS) * kn_ref[...]
    kn_o[...] = kn
    cos = cos_ref[...]
    sin = sin_ref[...]
    half = AXIS_DIM // 2

    def rope(t, cos_t, sin_t):
        width = t.shape[1]
        lane = lax.broadcasted_iota(jnp.int32, t.shape, 1)
        first = (lane % AXIS_DIM) < half
        swapped = jnp.where(first, pltpu.roll(t, width - half, 1), pltpu.roll(t, half, 1))
        return t * cos_t + swapped * sin_t

    cos4 = jnp.concatenate([cos] * (D_A // D_KV), axis=1)
    sin4 = jnp.concatenate([sin] * (D_A // D_KV), axis=1)
    qr_o[...] = (rope(qn, cos4, sin4) * (HEAD ** -0.5)).astype(BF16)
    kr_o[...] = rope(kn, cos, sin).astype(BF16)


def _prep(z, cos_tab, sin_tab, tab_row, p):
    n = z["r"].shape[0]
    row = lambda i: (i, 0)
    full = lambda *shape: pl.BlockSpec(shape, lambda i: (0,) * len(shape))
    tok = lambda width: pl.BlockSpec((TM, width), row)
    tab = pl.BlockSpec((TM, D_KV), lambda i: (tab_row(i), 0))
    cpb = TM // CHUNK
    return pl.pallas_call(
        _prep_kernel,
        out_shape=[jax.ShapeDtypeStruct((2, n, SCAN_OPS * D_R), BF16),
                   jax.ShapeDtypeStruct((2, n // CHUNK, 1, D_R), F32),
                   jax.ShapeDtypeStruct((n, D_R), BF16),
                   jax.ShapeDtypeStruct((n, D_R), F32),
                   jax.ShapeDtypeStruct((n, D_R), F32),
                   jax.ShapeDtypeStruct((n, D_A), BF16),
                   jax.ShapeDtypeStruct((n, D_KV), F32),
                   jax.ShapeDtypeStruct((n, D_KV), BF16)],
        grid=(n // TM,),
        in_specs=[tok(D_R), tok(D_R), tok(D_R), tok(LORA_COLS), tok(D_A), tok(D_KV), tab, tab,
                  full(2, D_R), full(2, LANES, D_R), full(2, D_R), full(2, LANES, D_R),
                  full(GATE_RANK, D_R), full(1, D_R), full(1, D_R), full(1, D_R),
                  full(1, D_A), full(1, D_KV), full(D_R, D_R), full(2, TM, TM), full(TM, TM)],
        out_specs=[pl.BlockSpec((2, TM, SCAN_OPS * D_R), lambda i: (0, i, 0)),
                   pl.BlockSpec((2, cpb, 1, D_R), lambda i: (0, i, 0, 0)),
                   tok(D_R), tok(D_R), tok(D_R), tok(D_A), tok(D_KV), tok(D_KV)],
        compiler_params=_cparams("arbitrary"),
        name="mixer_prep",
    )(z["r"], z["k"], z["v"], z["lora"], z["q"], z["ka"], cos_tab, sin_tab,
      p["w0"], p["w2"], p["a0"], p["a2"], p["g2"], p["kkw"], p["kaw"], p["rk"],
      p["qn"], p["kn"], p["ones"], p["tri"], p["chunk_ones"])


def _scan_kernel(scf_ref, scb_ref, vf_ref, vb_ref, gf_ref, gb_ref, s0_ref,
                 yf_ref, yb_ref, sfin_ref, state):
    c = pl.program_id(1)
    C = CHUNK
    pairs = D_R // LANES
    chains = [(g, d, p) for g in range(SCAN_GROUP) for d in range(2) for p in range(pairs)]

    lane = lax.broadcasted_iota(jnp.int32, (1, LANES), 1)
    low = lane < HEAD

    @pl.when(c == 0)
    def _():
        zero = jnp.zeros((HEAD, HEAD), F32)
        for i, (g, d, p) in enumerate(chains):
            top = jnp.concatenate([s0_ref[d, g, 2 * p], zero], axis=1)
            bot = jnp.concatenate([zero, s0_ref[d, g, 2 * p + 1]], axis=1)
            state[i] = jnp.concatenate([top, bot], axis=0)

    t2 = lax.broadcasted_iota(jnp.int32, (C, LANES), 0)
    s2 = lax.broadcasted_iota(jnp.int32, (C, LANES), 1) % C
    strict = (s2 < t2, s2 > t2)
    incl = (s2 <= t2, s2 >= t2)
    eye2 = jnp.where(s2 == t2, 1.0, 0.0)
    r2 = lax.broadcasted_iota(jnp.int32, (LANES, LANES), 0)
    c2 = lax.broadcasted_iota(jnp.int32, (LANES, LANES), 1)
    same_head = (r2 // HEAD) == (c2 // HEAD)

    sc_refs = (scf_ref, scb_ref)
    v_refs = (vf_ref, vb_ref)
    g_refs = (gf_ref, gb_ref)
    y_refs = (yf_ref, yb_ref)

    def operand(chain, j):
        g, d, p = chain
        c0 = j * D_R + p * LANES
        return sc_refs[d][0, g, :, c0:c0 + LANES]

    def stack(x):
        zero = jnp.zeros_like(x)
        return jnp.concatenate([jnp.where(low, x, zero), jnp.where(low, zero, x)], axis=0)

    Qk = [operand(ch, 0) for ch in chains]
    Qr = [operand(ch, 1) for ch in chains]
    v_p = [v_refs[d][g, :, p * LANES:(p + 1) * LANES] for g, d, p in chains]
    v_st = [stack(x) for x in v_p]
    Z0 = [state[i] for i in range(len(chains))]
    Z0b = [z.astype(BF16) for z in Z0]
    m = [_dot_nt(jnp.concatenate([Qk[i], Qr[i]], axis=0),
                 jnp.concatenate([stack(operand(ch, 2)), stack(operand(ch, 3))], axis=0))
         for i, ch in enumerate(chains)]
    A = [jnp.where(strict[d], m[i][0:C, 0:LANES], 0.0) for i, (g, d, p) in enumerate(chains)]
    Bm = [jnp.where(strict[d], m[i][0:C, LANES:2 * LANES], 0.0).astype(BF16)
          for i, (g, d, p) in enumerate(chains)]
    Pb = [jnp.where(incl[d], m[i][C:2 * C, 0:LANES], 0.0).astype(BF16)
          for i, (g, d, p) in enumerate(chains)]
    Pk = [jnp.where(incl[d], m[i][C:2 * C, LANES:2 * LANES], 0.0).astype(BF16)
          for i, (g, d, p) in enumerate(chains)]
    BV = [_dot(Bm[i], v_st[i]) for i in range(len(chains))]
    T = [eye2 - a for a in A]
    Ap = [_dot(a.astype(BF16), stack(a.astype(BF16))).astype(BF16) for a in A]
    n = 2
    while 2 * n < C:
        R = [_dot(jnp.concatenate([Ap[i], T[i].astype(BF16)], axis=0), stack(Ap[i]))
             for i in range(len(chains))]
        Ap = [x[0:C].astype(BF16) for x in R]
        T = [T[i] + R[i][C:2 * C] for i in range(len(chains))]
        n *= 2
    T = [T[i] + _dot(T[i].astype(BF16), stack(Ap[i])) for i in range(len(chains))]
    WU = [_dot(T[i].astype(BF16),
               jnp.concatenate([stack(Qk[i]), stack(BV[i].astype(BF16))], axis=1))
          for i in range(len(chains))]
    Wq = [x[:, 0:LANES].astype(BF16) for x in WU]
    Uv = [x[:, LANES:2 * LANES] for x in WU]
    PW = [_dot(Pb[i], jnp.concatenate([stack(Wq[i]), stack(Uv[i].astype(BF16))], axis=1))
          for i in range(len(chains))]
    PkV = [_dot(Pk[i], v_st[i]) for i in range(len(chains))]
    U = [_dot_nt(Wq[i], Z0b[i]) + Uv[i] for i in range(len(chains))]
    Yq = [(Qr[i].astype(F32) - PW[i][:, 0:LANES]).astype(BF16) for i in range(len(chains))]
    for i, (g, d, p) in enumerate(chains):
        y_refs[d][g, :, p * LANES:(p + 1) * LANES] = (_dot_nt(Yq[i], Z0b[i]) + PkV[i]
                                                      - PW[i][:, LANES:2 * LANES])
    for i, (g, d, p) in enumerate(chains):
        lhs = jnp.concatenate([v_p[i], (-U[i]).astype(BF16)], axis=0)
        rhs = jnp.concatenate([operand(chains[i], 5), operand(chains[i], 4)], axis=0)
        upd = _dot_tn(lhs, rhs)
        g_tot = g_refs[d][0, g, 0][:, p * LANES:(p + 1) * LANES]
        state[i] = Z0[i] * g_tot + jnp.where(same_head, upd, 0.0)

    @pl.when(c == pl.num_programs(1) - 1)
    def _():
        for i, (g, d, p) in enumerate(chains):
            z = state[i]
            sfin_ref[d, g, 2 * p] = z[0:HEAD, 0:HEAD]
            sfin_ref[d, g, 2 * p + 1] = z[HEAD:2 * HEAD, HEAD:2 * HEAD]


def _rwkv_scan(sc, gt, vb, s0, row0, batch, t_len):
    n = vb.shape[0]
    nch = t_len // CHUNK
    assert row0 % (t_len * SCAN_GROUP) == 0 and batch % SCAN_GROUP == 0
    seq0 = row0 // (t_len * SCAN_GROUP)
    width = SCAN_OPS * D_R
    sc4 = sc.reshape(2, n // t_len, t_len, width)
    gt5 = gt.reshape(2, n // t_len, nch, 1, D_R)
    v3 = vb.reshape(n // t_len, t_len, D_R)
    G = SCAN_GROUP
    fwd = lambda c: c
    bwd = lambda c: nch - 1 - c
    sc_spec = lambda d, at: pl.BlockSpec((1, G, CHUNK, width), lambda s, c: (d, seq0 + s, at(c), 0))
    v_spec = lambda at: pl.BlockSpec((G, CHUNK, D_R), lambda s, c: (seq0 + s, at(c), 0))
    gt_spec = lambda d, at: pl.BlockSpec((1, G, 1, 1, D_R), lambda s, c: (d, seq0 + s, at(c), 0, 0))
    y_spec = lambda at: pl.BlockSpec((G, CHUNK, D_R), lambda s, c: (s, at(c), 0))
    st = pl.BlockSpec((2, G, H_R, HEAD, HEAD), lambda s, c: (0, s, 0, 0, 0))
    return pl.pallas_call(
        _scan_kernel,
        out_shape=[jax.ShapeDtypeStruct((batch, t_len, D_R), F32),
                   jax.ShapeDtypeStruct((batch, t_len, D_R), F32),
                   jax.ShapeDtypeStruct((2, batch, H_R, HEAD, HEAD), F32)],
        grid=(batch // G, nch),
        in_specs=[sc_spec(0, fwd), sc_spec(1, bwd), v_spec(fwd), v_spec(bwd),
                  gt_spec(0, fwd), gt_spec(1, bwd), st],
        out_specs=[y_spec(fwd), y_spec(bwd), st],
        scratch_shapes=[pltpu.VMEM((G * 2 * D_R // LANES, LANES, LANES), F32)],
        compiler_params=_cparams("arbitrary", "arbitrary"),
        name="rwkv_scan",
    )(sc4, sc4, v3, v3, gt5, gt5, s0)


def _attn_kernel(*refs, tq, with_cache):
    if with_cache:
        q_ref, k_ref, v_ref, ck_ref, cv_ref, o_ref = refs
        kx = jnp.concatenate([k_ref[...], ck_ref[0].astype(BF16)], axis=0)
        vx = jnp.concatenate([v_ref[...], cv_ref[0]], axis=0).astype(BF16)
    else:
        q_ref, k_ref, v_ref, o_ref = refs
        kx = k_ref[...]
        vx = v_ref[...].astype(BF16)
    lane = lax.broadcasted_iota(jnp.int32, (1, D_KV), 1)
    low = lane < HEAD
    k_sw = pltpu.roll(kx.astype(F32), HEAD, 1).astype(BF16)
    v_sw = pltpu.roll(vx.astype(F32), HEAD, 1).astype(BF16)
    group = N_HEADS // N_KV
    for p in range(D_A // LANES):
        g = (2 * p) // group
        keep = low if g == 0 else jnp.logical_not(low)
        kd = jnp.where(keep, kx, k_sw)
        vd = jnp.where(keep, vx, v_sw)
        qp = q_ref[:, p * LANES:(p + 1) * LANES]
        zero = jnp.zeros_like(qp)
        qs = jnp.concatenate([jnp.where(low, qp, zero), jnp.where(low, zero, qp)], axis=0)
        s = _dot_nt(qs, kd)
        mx = jnp.max(s, axis=-1, keepdims=True)
        e = jnp.exp(s - mx)
        den = jnp.sum(e, axis=-1, keepdims=True)
        o = _dot(e.astype(BF16), vd) / den
        o_ref[:, p * LANES:(p + 1) * LANES] = jnp.where(low, o[0:tq], o[tq:2 * tq]).astype(BF16)


def _attention(qr, kr, va, row0, batch, t_len, cache_k=None, cache_v=None):
    tq = min(t_len, 256)
    nq = t_len // tq
    qblk0 = row0 // tq
    sblk0 = row0 // t_len
    with_cache = cache_k is not None
    in_specs = [pl.BlockSpec((tq, D_A), lambda b, i: (qblk0 + b * nq + i, 0)),
                pl.BlockSpec((t_len, D_KV), lambda b, i: (sblk0 + b, 0)),
                pl.BlockSpec((t_len, D_KV), lambda b, i: (sblk0 + b, 0))]
    args = [qr, kr, va]
    if with_cache:
        past = cache_k.shape[1]
        in_specs += [pl.BlockSpec((1, past, D_KV), lambda b, i: (b, 0, 0))] * 2
        args += [cache_k, cache_v]
    return pl.pallas_call(
        functools.partial(_attn_kernel, tq=tq, with_cache=with_cache),
        out_shape=jax.ShapeDtypeStruct((batch * t_len, D_A), BF16),
        grid=(batch, nq),
        in_specs=in_specs,
        out_specs=pl.BlockSpec((tq, D_A), lambda b, i: (b * nq + i, 0)),
        compiler_params=_cparams("arbitrary", "arbitrary"),
        name="attention_cache" if with_cache else "attention_ctx",
    )(*args)


def _post_kernel(yfc_ref, yfl_ref, ybc_ref, ybl_ref, bonus_ref, g_ref, yac_ref, yal_ref, gr_ref, ga_ref,
                 xc_ref, xl_ref,
                 mod_ref, lnw_ref, lnb_ref, wur_ref, wua_ref, wo_ref, n2_ref, wr_ref, br_ref, ones_ref,
                 x1_o, h2_o, idx_o, gate_o, *, ctx_blocks):
    ones = ones_ref[...]
    is_ctx = pl.program_id(0) < ctx_blocks
    y_a = jnp.where(is_ctx, yac_ref[...], yal_ref[...])
    x = jnp.where(is_ctx, xc_ref[...], xl_ref[...])
    y = jnp.where(is_ctx, yfc_ref[...] + ybc_ref[...], yfl_ref[...] + ybl_ref[...])
    mu = _head_sum(y, ones) * (1.0 / HEAD)
    yc = y - mu
    var = _head_sum(yc * yc, ones) * (1.0 / HEAD)
    yn = yc * lax.rsqrt(var + GN_EPS) * lnw_ref[...] + lnb_ref[...] + bonus_ref[...]
    y_r = (yn * g_ref[...]).astype(BF16)
    merged = (_sigmoid(gr_ref[...]) * _dot(y_r, wur_ref[...])
              + _sigmoid(ga_ref[...]) * _dot(y_a, wua_ref[...]))
    mix = _dot(merged.astype(BF16), wo_ref[...])
    mod = mod_ref[0]
    g1 = mod[:, 2 * D_MODEL:3 * D_MODEL]
    sh2 = mod[:, 3 * D_MODEL:4 * D_MODEL]
    sc2 = mod[:, 4 * D_MODEL:5 * D_MODEL]
    x1 = x + g1 * mix
    x1_o[...] = x1
    ms = jnp.mean(x1 * x1, axis=-1, keepdims=True)
    h2 = x1 * lax.rsqrt(ms + NORM_EPS) * n2_ref[...] * (1.0 + sc2) + sh2
    h2_o[...] = h2
    hh, hl = _split2(h2)
    wh, wl = _split2(wr_ref[...])
    logits = _dot(hh, wh) + _dot(hh, wl) + _dot(hl, wh) + br_ref[...]
    lane_i = lax.broadcasted_iota(jnp.int32, logits.shape, 1)
    lane = lane_i.astype(F32)
    neg = jnp.float32(-jnp.inf)
    cur = jnp.where(lane_i < N_EXPERTS, logits, neg)
    vals, idxs = [], []
    for _ in range(TOP_K):
        mx = jnp.max(cur, axis=-1, keepdims=True)
        ix = jnp.min(jnp.where(cur == mx, lane, float(LANES)), axis=-1, keepdims=True)
        vals.append(mx)
        idxs.append(ix)
        cur = jnp.where(lane == ix, neg, cur)
    es = [jnp.exp(val - vals[0]) for val in vals]
    den = es[0] + es[1] + es[2] + es[3]
    idx_out = jnp.zeros(logits.shape, jnp.int32)
    gate_out = jnp.zeros(logits.shape, F32)
    for j in range(TOP_K):
        idx_out = jnp.where(lane_i == j, idxs[j].astype(jnp.int32), idx_out)
        gate_out = jnp.where(lane_i == j, es[j] / den, gate_out)
    idx_o[...] = idx_out
    gate_o[...] = gate_out


def _post(y_f, y_b, bonus, g, ya_ctx, ya_lat, gate_r, gate_a, x_ctx, x_lat, mods3, mod_row, p):
    n = x_ctx.shape[0] + x_lat.shape[0]
    nbc = ya_ctx.shape[0] // TM
    row = lambda i: (i, 0)
    full = lambda *shape: pl.BlockSpec(shape, lambda i: (0,) * len(shape))
    tok = lambda width: pl.BlockSpec((TM, width), row)
    return pl.pallas_call(
        functools.partial(_post_kernel, ctx_blocks=nbc),
        out_shape=[jax.ShapeDtypeStruct((n, D_MODEL), F32),
                   jax.ShapeDtypeStruct((n, D_MODEL), F32),
                   jax.ShapeDtypeStruct((n, LANES), jnp.int32),
                   jax.ShapeDtypeStruct((n, LANES), F32)],
        grid=(n // TM,),
        in_specs=_token_specs(nbc, D_R) + _token_specs(nbc, D_R) + [tok(D_R), tok(D_R)]
                 + _token_specs(nbc, D_A) + [tok(D_MODEL), tok(D_MODEL)] + _token_specs(nbc, D_MODEL) + [
                  pl.BlockSpec((1, 1, 6 * D_MODEL), lambda i: (mod_row(i), 0, 0)),
                  full(1, D_R), full(1, D_R), full(D_R, D_MODEL), full(D_A, D_MODEL),
                  full(D_MODEL, D_MODEL), full(1, D_MODEL), full(D_MODEL, LANES), full(1, LANES),
                  full(D_R, D_R)],
        out_specs=[tok(D_MODEL), tok(D_MODEL), tok(LANES), tok(LANES)],
        compiler_params=_cparams("arbitrary"),
        name="mixer_post",
    )(*y_f, *y_b, bonus, g, ya_ctx, ya_lat, gate_r, gate_a, x_ctx, x_lat, mods3,
      p["lnw"], p["lnb"], p["wur"], p["wua"], p["wo"], p["n2"], p["wr"], p["br"], p["ones"])


def _moe_kernel(blk_e_ref, blk_first_ref, used_ref,
                src0_ref, src_next_ref, dst_prev_ref, h_hbm, w1_ref, b1_ref, w2_ref, b2_ref,
                out_hbm, w1b, w2b, xs, ys, sem_g, sem_s, sem_t, *, n_rows):
    b = pl.program_id(0)
    used = used_ref[0]
    R = MOE_ROWS

    def gather(table_ref, buf):
        for i in range(R):
            pltpu.make_async_copy(h_hbm.at[pl.ds(table_ref[0, 0, i], 1)], xs.at[buf, pl.ds(i, 1)],
                                  sem_g.at[buf]).start(priority=i % 2)

    def scatter_prev(buf):
        for i in range(R):
            pltpu.make_async_copy(ys.at[buf, pl.ds(i, 1)], out_hbm.at[pl.ds(dst_prev_ref[0, 0, i], 1)],
                                  sem_s.at[buf]).start(priority=i % 2)

    @pl.when(b == 0)
    def _():
        ys[...] = jnp.zeros_like(ys)
        for s in range(2):
            band = pltpu.make_async_copy(ys.at[s], out_hbm.at[pl.ds(n_rows + s * R, R)], sem_t)
            band.start()
            band.wait()
        gather(src0_ref, 0)

    @pl.when(jnp.logical_and(blk_first_ref[b] == 1, b < used))
    def _():
        w1b[...] = w1_ref[0].astype(BF16)
        w2b[...] = w2_ref[0].astype(BF16)

    def step(p):
        @pl.when(b <= used)
        def _():
            pltpu.make_async_copy(h_hbm.at[pl.ds(0, R)], xs.at[p], sem_g.at[p]).wait()

        @pl.when(jnp.logical_and(b >= 1, b <= used + 1))
        def _():
            pltpu.make_async_copy(ys.at[p], out_hbm.at[pl.ds(0, R)], sem_s.at[p]).wait()

        @pl.when(b < used)
        def _():
            gather(src_next_ref, 1 - p)
            scatter_prev(1 - p)
            hb = _dot(xs[p].astype(BF16), w1b[...]) + b1_ref[0]
            glu = jnp.minimum(hb[:, 0:D_FF], SWIGLU_LIMIT)
            lin = jnp.clip(hb[:, D_FF:2 * D_FF], -SWIGLU_LIMIT, SWIGLU_LIMIT)
            act = glu * _sigmoid(SWIGLU_ALPHA * glu) * (lin + 1.0)
            ys[p] = _dot(act.astype(BF16), w2b[...]) + b2_ref[0]

        @pl.when(b == used)
        def _():
            scatter_prev(1 - p)

    for p in range(2):
        pl.when(b % 2 == p)(functools.partial(step, p))


def _moe(h2, src, dst_prev, blk_e, blk_first, used, w1, b1, w2, b2):
    n, d = h2.shape
    nb = src.shape[0]
    steps = nb + 2
    cur = lambda b, *_: (jnp.minimum(b, nb), 0, 0)
    nxt = lambda b, *_: (jnp.minimum(b + 1, nb - 1), 0, 0)
    table = lambda index_map: pl.BlockSpec((1, 1, MOE_ROWS), index_map, memory_space=pltpu.SMEM)
    expert = lambda b, blk_e, *_: (blk_e[jnp.minimum(b, nb - 1)], 0, 0)
    grid_spec = pltpu.PrefetchScalarGridSpec(
        num_scalar_prefetch=3,
        grid=(steps,),
        in_specs=[table(lambda b, *_: (0, 0, 0)), table(nxt), table(cur),
                  pl.BlockSpec(memory_space=pl.ANY),
                  pl.BlockSpec((1, d, 2 * D_FF), expert),
                  pl.BlockSpec((1, 1, 2 * D_FF), expert),
                  pl.BlockSpec((1, D_FF, d), expert),
                  pl.BlockSpec((1, 1, d), expert)],
        out_specs=pl.BlockSpec(memory_space=pl.ANY),
        scratch_shapes=[pltpu.VMEM((d, 2 * D_FF), BF16),
                        pltpu.VMEM((D_FF, d), BF16),
                        pltpu.VMEM((2, MOE_ROWS, d), F32),
                        pltpu.VMEM((2, MOE_ROWS, d), F32),
                        pltpu.SemaphoreType.DMA((2,)),
                        pltpu.SemaphoreType.DMA((2,)),
                        pltpu.SemaphoreType.DMA(())])
    tab3 = lambda t: t.reshape(-1, 1, MOE_ROWS)
    pad = lambda t: jnp.concatenate([t, jnp.zeros((steps - nb,), jnp.int32)])
    return pl.pallas_call(
        functools.partial(_moe_kernel, n_rows=n * TOP_K),
        out_shape=jax.ShapeDtypeStruct((n * TOP_K + 2 * MOE_ROWS, d), F32),
        grid_spec=grid_spec,
        compiler_params=pltpu.CompilerParams(dimension_semantics=("arbitrary",),
                                             vmem_limit_bytes=VMEM_LIMIT, has_side_effects=True),
        name="moe_experts",
    )(blk_e, pad(blk_first), used, tab3(src), tab3(src), tab3(dst_prev), h2, w1,
      b1.reshape(N_EXPERTS, 1, 2 * D_FF), w2, b2.reshape(N_EXPERTS, 1, d))


def _route(top_idx, n):
    nk = n * TOP_K
    cap = nk + N_EXPERTS * MOE_ROWS
    nb = cap // MOE_ROWS
    flat_e = top_idx.reshape(nk)
    order = jnp.argsort(flat_e, stable=True).astype(jnp.int32)
    counts = jnp.sum((flat_e[:, None] == jnp.arange(N_EXPERTS, dtype=jnp.int32)[None, :])
                     .astype(jnp.int32), axis=0)
    start = jnp.cumsum(counts) - counts
    n_blk = (counts + MOE_ROWS - 1) // MOE_ROWS
    blk_end = jnp.cumsum(n_blk)
    first_blk = blk_end - n_blk
    experts = jnp.arange(N_EXPERTS, dtype=jnp.int32)
    blk = jnp.arange(nb, dtype=jnp.int32)
    blk_e = jnp.minimum(jnp.sum((blk[:, None] >= blk_end[None, :]).astype(jnp.int32), axis=1),
                        N_EXPERTS - 1)
    onehot = (blk_e[:, None] == experts[None, :]).astype(jnp.int32)
    pick = lambda t: jnp.sum(onehot * t[None, :], axis=1)
    blk_first_of_e, blk_count, blk_start = pick(first_blk), pick(counts), pick(start)
    blk_first = (blk == blk_first_of_e).astype(jnp.int32)
    row = jnp.arange(MOE_ROWS, dtype=jnp.int32)[None, :]
    within = (blk - blk_first_of_e)[:, None] * MOE_ROWS + row
    valid = jnp.logical_and(within < blk_count[:, None], (blk < blk_end[-1])[:, None])
    j = order[jnp.clip(blk_start[:, None] + within, 0, nk - 1)]
    src = jnp.where(valid, j // TOP_K, 0).astype(jnp.int32)
    scratch_row = nk + (blk % 2)[:, None] * MOE_ROWS + row
    dst = jnp.where(valid, (j % TOP_K) * n + j // TOP_K, scratch_row).astype(jnp.int32)
    dst_prev = jnp.concatenate([nk + MOE_ROWS + row, dst], axis=0)
    return src, dst_prev, blk_e.astype(jnp.int32), blk_first, blk_end[-1:].astype(jnp.int32)


def _final_kernel(x1_ref, y0_ref, y1_ref, y2_ref, y3_ref, gate_ref, mod_ref, nf_ref, oc_ref, ol_ref,
                  *, ctx_blocks):
    gates = gate_ref[...]
    ffn = gates[:, 0:1] * y0_ref[...]
    for j, ref in enumerate((y1_ref, y2_ref, y3_ref), start=1):
        ffn = ffn + gates[:, j:j + 1] * ref[...]
    g2 = mod_ref[0][:, 5 * D_MODEL:6 * D_MODEL]
    x2 = x1_ref[...] + g2 * ffn
    ms = jnp.mean(x2 * x2, axis=-1, keepdims=True)
    y = x2 * lax.rsqrt(ms + NORM_EPS) * nf_ref[...]
    i = pl.program_id(0)

    @pl.when(i < ctx_blocks)
    def _():
        oc_ref[...] = y

    @pl.when(i >= ctx_blocks)
    def _():
        ol_ref[...] = y


def _final(x1, ys, gates, mods3, mod_row, norm_f, nc):
    n, d = x1.shape
    nb = n // TM
    nbc = nc // TM
    row = lambda i: (i, 0)
    return pl.pallas_call(
        functools.partial(_final_kernel, ctx_blocks=nbc),
        out_shape=[jax.ShapeDtypeStruct((nc, d), F32), jax.ShapeDtypeStruct((n - nc, d), F32)],
        grid=(nb,),
        in_specs=[pl.BlockSpec((TM, d), row)]
                 + [pl.BlockSpec((TM, d), functools.partial(lambda j, i: (j * nb + i, 0), j))
                    for j in range(TOP_K)]
                 + [pl.BlockSpec((TM, LANES), row),
                    pl.BlockSpec((1, 1, 6 * d), lambda i: (mod_row(i), 0, 0)),
                    pl.BlockSpec((1, d), lambda i: (0, 0))],
        out_specs=_token_specs(nbc, d),
        compiler_params=_cparams("arbitrary"),
        name="combine_final",
    )(x1, ys, ys, ys, ys, gates, mods3, norm_f.reshape(1, d))


def _rope_tables(t_ctx, t_lat):
    rows = np.arange(t_lat) // GRID_W
    cols = np.arange(t_lat) % GRID_W
    inv = ROPE_THETA ** (-np.arange(0, AXIS_DIM, 2, dtype=np.float32) / AXIS_DIM)
    inv = jnp.asarray(inv, F32)
    ang_r = jnp.asarray(rows, F32)[:, None] * inv[None, :]
    ang_c = jnp.asarray(cols, F32)[:, None] * inv[None, :]
    cos = jnp.concatenate([jnp.cos(ang_r)] * 2 + [jnp.cos(ang_c)] * 2, axis=1)
    sin = jnp.concatenate([-jnp.sin(ang_r), jnp.sin(ang_r), -jnp.sin(ang_c), jnp.sin(ang_c)], axis=1)
    cos = jnp.concatenate([jnp.ones((t_ctx, HEAD), F32), cos], axis=0)
    sin = jnp.concatenate([jnp.zeros((t_ctx, HEAD), F32), sin], axis=0)
    return jnp.concatenate([cos, cos], axis=1), jnp.concatenate([sin, sin], axis=1)


def kernel(x_prompt, x_sample, cache_k, cache_v, state_rwkv_fwd, state_rwkv_bwd, c, c_ctx, w_ada, b_ada, norm1, norm2, w_in, rw_w0, rw_w2, rw_a0, rw_a2, rw_g2, rw_kk, rw_ka, rw_rk, rw_ln_w, rw_ln_b, q_norm, k_norm, w_up_r, w_up_a, w_out, w_router, b_router, w_moe_in, b_moe_in, w_moe_out, b_moe_out, norm_f):
    depth = w_in.shape[0]
    assert depth == 1, "single trunk layer"
    bc, tc, d = x_prompt.shape
    bl, tl, _ = x_sample.shape
    nc, nl = bc * tc, bl * tl
    n = nc + nl
    assert d == D_MODEL and tc == TM and tl % TM == 0 and nc % tl == 0
    l = 0

    x_ctx, x_lat = x_prompt.reshape(nc, d), x_sample.reshape(nl, d)
    cond =jnp.concatenate([c_ctx[None, :], c, jnp.zeros((16 - 1 - bl, d), F32)], axis=0)
    mods3 = _adaln(cond, w_ada[l], b_ada[l]).reshape(16, 1, 6 * d)
    nbc = nc // TM
    per_seq = tl // TM
    mod_row = lambda i: jnp.where(i < nbc, 0, 1 + (i - nbc) // per_seq)
    tab_row = lambda i: jnp.where(i < nbc, 0, 1 + (i - nbc) % per_seq)

    z = dict(zip([name for name, _ in IN_SPLITS],
                 _proj_in(x_ctx, x_lat, mods3, norm1[l], w_in[l].astype(BF16), mod_row)))

    ones_bd = jnp.asarray(np.kron(np.eye(H_R), np.ones((HEAD, HEAD))), BF16)
    pad_lo = lambda w: jnp.concatenate([w, jnp.zeros_like(w)], axis=0)
    pad_hi = lambda w: jnp.concatenate([jnp.zeros_like(w), w], axis=0)
    prep_p = dict(
        w0=rw_w0[l], a0=rw_a0[l],
        w2=jnp.stack([pad_lo(rw_w2[l, 0]), pad_hi(rw_w2[l, 1])]).astype(BF16),
        a2=jnp.stack([pad_lo(rw_a2[l, 0]), pad_hi(rw_a2[l, 1])]).astype(BF16),
        g2=rw_g2[l].astype(BF16),
        kkw=rw_kk[l].reshape(1, D_R), kaw=rw_ka[l].reshape(1, D_R), rk=rw_rk[l].reshape(1, D_R),
        qn=jnp.tile(q_norm[l], N_HEADS).reshape(1, D_A), kn=jnp.tile(k_norm[l], N_KV).reshape(1, D_KV),
        ones=ones_bd,
        tri=jnp.asarray(np.stack([np.kron(np.eye(TM // CHUNK), np.tril(np.ones((CHUNK, CHUNK)))),
                                  np.kron(np.eye(TM // CHUNK), np.triu(np.ones((CHUNK, CHUNK))))]), BF16),
        chunk_ones=jnp.asarray(np.kron(np.eye(TM // CHUNK), np.ones((CHUNK, CHUNK))), BF16))
    cos_tab, sin_tab = _rope_tables(tc, tl)
    sc, gt, vb, bonus, g, qr, kn, kr = _prep(z, cos_tab, sin_tab, tab_row, prep_p)

    yf_c, yb_c, s_fin = _rwkv_scan(sc, gt, vb, jnp.zeros((2, bc, H_R, HEAD, HEAD), F32), 0, bc, tc)
    yf_l, yb_l, _ = _rwkv_scan(sc, gt, vb, jnp.stack([state_rwkv_fwd[:, l], state_rwkv_bwd[:, l]]),
                               nc, bl, tl)
    y_f = (yf_c.reshape(nc, D_R), yf_l.reshape(nl, D_R))
    y_b = (yb_c.reshape(nc, D_R), yb_l.reshape(nl, D_R))

    ya_ctx = _attention(qr, kr, z["va"], 0, bc, tc)
    ya_lat = _attention(qr, kr, z["va"], nc, bl, tl,
                        cache_k[:, l].reshape(bl, -1, D_KV), cache_v[:, l].reshape(bl, -1, D_KV))

    post_p = dict(
        lnw=rw_ln_w[l].reshape(1, D_R), lnb=rw_ln_b[l].reshape(1, D_R),
        wur=w_up_r[l].astype(BF16), wua=w_up_a[l].astype(BF16), wo=w_out[l].astype(BF16),
        n2=norm2[l].reshape(1, d),
        wr=jnp.pad(w_router[l], ((0, 0), (0, LANES - N_EXPERTS))),
        br=jnp.pad(b_router[l], (0, LANES - N_EXPERTS)).reshape(1, LANES),
        ones=ones_bd)
    x1, h2, top_idx, gates = _post(y_f, y_b, bonus, g, ya_ctx, ya_lat, z["gate_r"], z["gate_a"],
                                   x_ctx, x_lat, mods3, mod_row, post_p)

    src, dst, blk_e, blk_first, used = _route(top_idx[:, :TOP_K], n)
    ys = _moe(h2, src, dst, blk_e, blk_first, used,
              w_moe_in[l], b_moe_in[l], w_moe_out[l], b_moe_out[l])
    y_ctx, y_lat = _final(x1, ys, gates, mods3, mod_row, norm_f, nc)

    y_prompt = y_ctx.reshape(bc, tc, d)
    y_sample = y_lat.reshape(bl, tl, d)
    new_cache_k = kn[:nc].reshape(bc, 1, tc, N_KV, HEAD)
    new_cache_v = z["va"][:nc].reshape(bc, 1, tc, N_KV, HEAD)
    new_state_fwd = s_fin[0][:, None]
    new_state_bwd = s_fin[1][:, None]
    return (y_prompt, y_sample, new_cache_k, new_cache_v, new_state_fwd, new_state_bwd)
```

```python
import functools

import numpy as np
import jax
import jax.numpy as jnp
from jax import lax
from jax.experimental import pallas as pl
from jax.experimental.pallas import tpu as pltpu

F32 = jnp.float32
BF16 = jnp.bfloat16

D_MODEL = 1024
GRID_W = 64
NORM_EPS = 1e-6
HEAD = 64
H_R = 8
D_R = H_R * HEAD
DECAY_RANK = 64
AAA_RANK = 64
GATE_RANK = 128
GN_EPS = 64e-5
N_HEADS = 8
N_KV = 2
D_A = N_HEADS * HEAD
D_KV = N_KV * HEAD
AXIS_DIM = HEAD // 2
ROPE_THETA = 10000.0
N_EXPERTS = 32
TOP_K = 4
D_FF = D_MODEL
SWIGLU_ALPHA = 1.702
SWIGLU_LIMIT = 7.0
LORA_COLS = 2 * DECAY_RANK + 2 * AAA_RANK + GATE_RANK
IN_SPLITS = (("r", D_R), ("k", D_R), ("v", D_R), ("lora", LORA_COLS), ("q", D_A),
             ("ka", D_KV), ("va", D_KV), ("gate_r", D_MODEL), ("gate_a", D_MODEL))

LANES = 128
TM = 256
CHUNK = 64
SCAN_OPS = 6
SCAN_GROUP = 2
MOE_ROWS = 256
VMEM_LIMIT = 56 * 1024 * 1024


def _cparams(*sem):
    return pltpu.CompilerParams(dimension_semantics=sem, vmem_limit_bytes=VMEM_LIMIT)


def _dot(a, b):
    return jnp.dot(a, b, preferred_element_type=F32)


def _dot_nt(a, b):
    return lax.dot_general(a, b, (((1,), (1,)), ((), ())), preferred_element_type=F32)


def _dot_tn(a, b):
    return lax.dot_general(a, b, (((0,), (0,)), ((), ())), preferred_element_type=F32)


def _split2(x):
    hi = x.astype(BF16)
    lo = (x - hi.astype(F32)).astype(BF16)
    return hi, lo


def _split3(x):
    x1 = x.astype(BF16)
    r1 = x - x1.astype(F32)
    x2 = r1.astype(BF16)
    x3 = (r1 - x2.astype(F32)).astype(BF16)
    return x1, x2, x3


def _head_sum(x, ones_bd):
    hi, lo = _split2(x)
    return _dot(hi, ones_bd) + _dot(lo, ones_bd)


def _sigmoid(x):
    return 1.0 / (1.0 + jnp.exp(-x))


TILE_ROWS = D_MODEL // LANES


def _store_token_tiles(ref, x):
    rows = x.shape[0]
    for j in range(TILE_ROWS):
        ref[pl.ds(j, rows, stride=TILE_ROWS), :] = x[:, j * LANES:(j + 1) * LANES]


def _load_token_tiles(ref):
    rows = ref.shape[0] // TILE_ROWS
    return jnp.concatenate([ref[pl.ds(j, rows, stride=TILE_ROWS), :] for j in range(TILE_ROWS)], axis=1)


def _adaln_kernel(c_ref, w_ref, b_ref, o_ref):
    c = c_ref[...]
    s = (c * _sigmoid(c)).astype(BF16)
    o_ref[...] = _dot(s, w_ref[...].astype(BF16)) + b_ref[...]


def _adaln(cond, w_ada, b_ada):
    rows, d = cond.shape
    cols = w_ada.shape[1]
    tn = 512
    return pl.pallas_call(
        _adaln_kernel,
        out_shape=jax.ShapeDtypeStruct((rows, cols), F32),
        grid=(cols // tn,),
        in_specs=[pl.BlockSpec((rows, d), lambda j: (0, 0)),
                  pl.BlockSpec((d, tn), lambda j: (0, j)),
                  pl.BlockSpec((1, tn), lambda j: (0, j))],
        out_specs=pl.BlockSpec((rows, tn), lambda j: (0, j)),
        compiler_params=_cparams("arbitrary"),
        name="adaln",
    )(cond, w_ada, b_ada.reshape(1, cols))


def _token_specs(nbc, width):
    return [pl.BlockSpec((TM, width), lambda i: (jnp.minimum(i, nbc - 1), 0)),
            pl.BlockSpec((TM, width), lambda i: (jnp.maximum(i - nbc, 0), 0))]


def _proj_in_kernel(xc_ref, xl_ref, mod_ref, n1_ref, w_ref, *out_refs, ctx_blocks):
    x = jnp.where(pl.program_id(0) < ctx_blocks, xc_ref[...], xl_ref[...])
    ms = jnp.mean(x * x, axis=-1, keepdims=True)
    hn = x * lax.rsqrt(ms + NORM_EPS) * n1_ref[...]
    mod = mod_ref[0]
    sh1 = mod[:, 0:D_MODEL]
    sc1 = mod[:, D_MODEL:2 * D_MODEL]
    h = (hn * (1.0 + sc1) + sh1).astype(BF16)
    c0 = 0
    for ref, (_, width) in zip(out_refs, IN_SPLITS):
        ref[...] = _dot(h, w_ref[:, c0:c0 + width])
        c0 += width


def _proj_in(x_ctx, x_lat, mods3, norm1, w_in_bf, mod_row):
    d = x_ctx.shape[1]
    n = x_ctx.shape[0] + x_lat.shape[0]
    nbc = x_ctx.shape[0] // TM
    cols = w_in_bf.shape[1]
    row = lambda i: (i, 0)
    return pl.pallas_call(
        functools.partial(_proj_in_kernel, ctx_blocks=nbc),
        out_shape=[jax.ShapeDtypeStruct((n, width), F32) for _, width in IN_SPLITS],
        grid=(n // TM,),
        in_specs=_token_specs(nbc, d) + [
                  pl.BlockSpec((1, 1, 6 * d), lambda i: (mod_row(i), 0, 0)),
                  pl.BlockSpec((1, d), lambda i: (0, 0)),
                  pl.BlockSpec((d, cols), lambda i: (0, 0))],
        out_specs=[pl.BlockSpec((TM, width), row) for _, width in IN_SPLITS],
        compiler_params=_cparams("arbitrary"),
        name="proj_in",
    )(x_ctx, x_lat, mods3, norm1.reshape(1, d), w_in_bf)


def _prep_kernel(r_ref, k_ref, v_ref, lora_ref, q_ref, ka_ref, cos_ref, sin_ref,
                 w0_ref, w2_ref, a0_ref, a2_ref, g2_ref, kkw_ref, kaw_ref, rk_ref,
                 qn_ref, kn_ref, ones_ref, tri_ref, chunk_ones_ref,
                 sc_o, gt_o, vb_o, bonus_o, g_o, qr_o, kn_o, kr_o):
    ones = ones_ref[...]
    chunk_ones = chunk_ones_ref[...]
    r = r_ref[...]
    k = k_ref[...]
    v = v_ref[...]
    vb_o[...] = v.astype(BF16)
    lora = lora_ref[...]
    kk = k * kkw_ref[...]
    kk = kk * lax.rsqrt(_head_sum(kk * kk, ones) + 1e-12)
    th = jnp.tanh(lora[:, 0:LANES]).astype(BF16)
    al = lora[:, LANES:2 * LANES].astype(BF16)
    kd_sum = jnp.zeros_like(k)
    for d in range(2):
        u = w0_ref[d:d + 1, :] + _dot(th, w2_ref[d])
        w_log = -(jnp.maximum(-u, 0.0) + jnp.log1p(jnp.exp(-jnp.abs(u)))) - 0.5
        lw = -jnp.exp(w_log)
        a = _sigmoid(a0_ref[d:d + 1, :] + _dot(al, a2_ref[d]))
        kd = k * (1.0 + (a - 1.0) * kaw_ref[...])
        kd_sum = kd_sum + kd
        b = kk * a
        parts = _split3(lw)
        tri = tri_ref[d]
        L = _dot(tri, parts[0]) + _dot(tri, parts[1]) + _dot(tri, parts[2])
        Ltot = _dot(chunk_ones, parts[0]) + _dot(chunk_ones, parts[1]) + _dot(chunk_ones, parts[2])
        e_inv = jnp.exp(-L)
        e_rest = jnp.exp(Ltot - L)
        scan_ops = (kk * jnp.exp(L - lw), r * jnp.exp(L), b * e_inv, kd * e_inv, b * e_rest, kd * e_rest)
        for j, op in enumerate(scan_ops):
            sc_o[d, :, j * D_R:(j + 1) * D_R] = op.astype(BF16)
        g_tot = jnp.exp(Ltot)
        for j in range(TM // CHUNK):
            gt_o[d, j] = g_tot[j * CHUNK:j * CHUNK + 1, :]
    bonus_o[...] = _head_sum(r * kd_sum * rk_ref[...], ones) * v
    g_o[...] = _dot(_sigmoid(lora[:, 2 * LANES:3 * LANES]).astype(BF16), g2_ref[...])
    q = q_ref[...]
    qn = q * lax.rsqrt(_head_sum(q * q, ones) * (1.0 / HEAD) + NORM_EPS) * qn_ref[...]
    ka = ka_ref[...]
    ones_kv = ones[0:D_KV, 0:D_KV]
    kn = ka * lax.rsqrt(_head_sum(ka * ka, ones_kv) * (1.0 / HEAD) + NORM_EPS) * kn_ref[...]
    kn_o[...] = kn
    cos = cos_ref[...]
    sin = sin_ref[...]
    half = AXIS_DIM // 2

    def rope(t, cos_t, sin_t):
        width = t.shape[1]
        lane = lax.broadcasted_iota(jnp.int32, t.shape, 1)
        first = (lane % AXIS_DIM) < half
        swapped = jnp.where(first, pltpu.roll(t, width - half, 1), pltpu.roll(t, half, 1))
        return t * cos_t + swapped * sin_t

    cos4 = jnp.concatenate([cos] * (D_A // D_KV), axis=1)
    sin4 = jnp.concatenate([sin] * (D_A // D_KV), axis=1)
    qr_o[...] = (rope(qn, cos4, sin4) * (HEAD ** -0.5)).astype(BF16)
    kr_o[...] = rope(kn, cos, sin).astype(BF16)


def _prep(z, cos_tab, sin_tab, tab_row, p):
    n = z["r"].shape[0]
    row = lambda i: (i, 0)
    full = lambda *shape: pl.BlockSpec(shape, lambda i: (0,) * len(shape))
    tok = lambda width: pl.BlockSpec((TM, width), row)
    tab = pl.BlockSpec((TM, D_KV), lambda i: (tab_row(i), 0))
    cpb = TM // CHUNK
    return pl.pallas_call(
        _prep_kernel,
        out_shape=[jax.ShapeDtypeStruct((2, n, SCAN_OPS * D_R), BF16),
                   jax.ShapeDtypeStruct((2, n // CHUNK, 1, D_R), F32),
                   jax.ShapeDtypeStruct((n, D_R), BF16),
                   jax.ShapeDtypeStruct((n, D_R), F32),
                   jax.ShapeDtypeStruct((n, D_R), F32),
                   jax.ShapeDtypeStruct((n, D_A), BF16),
                   jax.ShapeDtypeStruct((n, D_KV), F32),
                   jax.ShapeDtypeStruct((n, D_KV), BF16)],
        grid=(n // TM,),
        in_specs=[tok(D_R), tok(D_R), tok(D_R), tok(LORA_COLS), tok(D_A), tok(D_KV), tab, tab,
                  full(2, D_R), full(2, LANES, D_R), full(2, D_R), full(2, LANES, D_R),
                  full(GATE_RANK, D_R), full(1, D_R), full(1, D_R), full(1, D_R),
                  full(1, D_A), full(1, D_KV), full(D_R, D_R), full(2, TM, TM), full(TM, TM)],
        out_specs=[pl.BlockSpec((2, TM, SCAN_OPS * D_R), lambda i: (0, i, 0)),
                   pl.BlockSpec((2, cpb, 1, D_R), lambda i: (0, i, 0, 0)),
                   tok(D_R), tok(D_R), tok(D_R), tok(D_A), tok(D_KV), tok(D_KV)],
        compiler_params=_cparams("arbitrary"),
        name="mixer_prep",
    )(z["r"], z["k"], z["v"], z["lora"], z["q"], z["ka"], cos_tab, sin_tab,
      p["w0"], p["w2"], p["a0"], p["a2"], p["g2"], p["kkw"], p["kaw"], p["rk"],
      p["qn"], p["kn"], p["ones"], p["tri"], p["chunk_ones"])


def _scan_kernel(scf_ref, scb_ref, vf_ref, vb_ref, gf_ref, gb_ref, s0_ref,
                 yf_ref, yb_ref, sfin_ref, state):
    c = pl.program_id(1)
    C = CHUNK
    pairs = D_R // LANES
    chains = [(g, d, p) for g in range(SCAN_GROUP) for d in range(2) for p in range(pairs)]

    lane = lax.broadcasted_iota(jnp.int32, (1, LANES), 1)
    low = lane < HEAD

    @pl.when(c == 0)
    def _():
        zero = jnp.zeros((HEAD, HEAD), F32)
        for i, (g, d, p) in enumerate(chains):
            top = jnp.concatenate([s0_ref[d, g, 2 * p], zero], axis=1)
            bot = jnp.concatenate([zero, s0_ref[d, g, 2 * p + 1]], axis=1)
            state[i] = jnp.concatenate([top, bot], axis=0)

    t2 = lax.broadcasted_iota(jnp.int32, (C, LANES), 0)
    s2 = lax.broadcasted_iota(jnp.int32, (C, LANES), 1) % C
    strict = (s2 < t2, s2 > t2)
    incl = (s2 <= t2, s2 >= t2)
    eye2 = jnp.where(s2 == t2, 1.0, 0.0)
    r2 = lax.broadcasted_iota(jnp.int32, (LANES, LANES), 0)
    c2 = lax.broadcasted_iota(jnp.int32, (LANES, LANES), 1)
    same_head = (r2 // HEAD) == (c2 // HEAD)

    sc_refs = (scf_ref, scb_ref)
    v_refs = (vf_ref, vb_ref)
    g_refs = (gf_ref, gb_ref)
    y_refs = (yf_ref, yb_ref)

    def operand(chain, j):
        g, d, p = chain
        c0 = j * D_R + p * LANES
        return sc_refs[d][0, g, :, c0:c0 + LANES]

    def stack(x):
        zero = jnp.zeros_like(x)
        return jnp.concatenate([jnp.where(low, x, zero), jnp.where(low, zero, x)], axis=0)

    Qk = [operand(ch, 0) for ch in chains]
    Qr = [operand(ch, 1) for ch in chains]
    v_p = [v_refs[d][g, :, p * LANES:(p + 1) * LANES] for g, d, p in chains]
    v_st = [stack(x) for x in v_p]
    Z0 = [state[i] for i in range(len(chains))]
    Z0b = [z.astype(BF16) for z in Z0]
    m = [_dot_nt(jnp.concatenate([Qk[i], Qr[i]], axis=0),
                 jnp.concatenate([stack(operand(ch, 2)), stack(operand(ch, 3))], axis=0))
         for i, ch in enumerate(chains)]
    A = [jnp.where(strict[d], m[i][0:C, 0:LANES], 0.0) for i, (g, d, p) in enumerate(chains)]
    Bm = [jnp.where(strict[d], m[i][0:C, LANES:2 * LANES], 0.0).astype(BF16)
          for i, (g, d, p) in enumerate(chains)]
    Pb = [jnp.where(incl[d], m[i][C:2 * C, 0:LANES], 0.0).astype(BF16)
          for i, (g, d, p) in enumerate(chains)]
    Pk = [jnp.where(incl[d], m[i][C:2 * C, LANES:2 * LANES], 0.0).astype(BF16)
          for i, (g, d, p) in enumerate(chains)]
    BV = [_dot(Bm[i], v_st[i]) for i in range(len(chains))]
    T = [eye2 - a for a in A]
    Ap = [_dot(a.astype(BF16), stack(a.astype(BF16))).astype(BF16) for a in A]
    n = 2
    while 2 * n < C:
        R = [_dot(jnp.concatenate([Ap[i], T[i].astype(BF16)], axis=0), stack(Ap[i]))
             for i in range(len(chains))]
        Ap = [x[0:C].astype(BF16) for x in R]
        T = [T[i] + R[i][C:2 * C] for i in range(len(chains))]
        n *= 2
    T = [T[i] + _dot(T[i].astype(BF16), stack(Ap[i])) for i in range(len(chains))]
    WU = [_dot(T[i].astype(BF16),
               jnp.concatenate([stack(Qk[i]), stack(BV[i].astype(BF16))], axis=1))
          for i in range(len(chains))]
    Wq = [x[:, 0:LANES].astype(BF16) for x in WU]
    Uv = [x[:, LANES:2 * LANES] for x in WU]
    PW = [_dot(Pb[i], jnp.concatenate([stack(Wq[i]), stack(Uv[i].astype(BF16))], axis=1))
          for i in range(len(chains))]
    PkV = [_dot(Pk[i], v_st[i]) for i in range(len(chains))]
    U = [_dot_nt(Wq[i], Z0b[i]) + Uv[i] for i in range(len(chains))]
    Yq = [(Qr[i].astype(F32) - PW[i][:, 0:LANES]).astype(BF16) for i in range(len(chains))]
    for i, (g, d, p) in enumerate(chains):
        y_refs[d][g, :, p * LANES:(p + 1) * LANES] = (_dot_nt(Yq[i], Z0b[i]) + PkV[i]
                                                      - PW[i][:, LANES:2 * LANES])
    for i, (g, d, p) in enumerate(chains):
        lhs = jnp.concatenate([v_p[i], (-U[i]).astype(BF16)], axis=0)
        rhs = jnp.concatenate([operand(chains[i], 5), operand(chains[i], 4)], axis=0)
        upd = _dot_tn(lhs, rhs)
        g_tot = g_refs[d][0, g, 0][:, p * LANES:(p + 1) * LANES]
        state[i] = Z0[i] * g_tot + jnp.where(same_head, upd, 0.0)

    @pl.when(c == pl.num_programs(1) - 1)
    def _():
        for i, (g, d, p) in enumerate(chains):
            z = state[i]
            sfin_ref[d, g, 2 * p] = z[0:HEAD, 0:HEAD]
            sfin_ref[d, g, 2 * p + 1] = z[HEAD:2 * HEAD, HEAD:2 * HEAD]


def _rwkv_scan(sc, gt, vb, s0, row0, batch, t_len):
    n = vb.shape[0]
    nch = t_len // CHUNK
    assert row0 % (t_len * SCAN_GROUP) == 0 and batch % SCAN_GROUP == 0
    seq0 = row0 // (t_len * SCAN_GROUP)
    width = SCAN_OPS * D_R
    sc4 = sc.reshape(2, n // t_len, t_len, width)
    gt5 = gt.reshape(2, n // t_len, nch, 1, D_R)
    v3 = vb.reshape(n // t_len, t_len, D_R)
    G = SCAN_GROUP
    fwd = lambda c: c
    bwd = lambda c: nch - 1 - c
    sc_spec = lambda d, at: pl.BlockSpec((1, G, CHUNK, width), lambda s, c: (d, seq0 + s, at(c), 0))
    v_spec = lambda at: pl.BlockSpec((G, CHUNK, D_R), lambda s, c: (seq0 + s, at(c), 0))
    gt_spec = lambda d, at: pl.BlockSpec((1, G, 1, 1, D_R), lambda s, c: (d, seq0 + s, at(c), 0, 0))
    y_spec = lambda at: pl.BlockSpec((G, CHUNK, D_R), lambda s, c: (s, at(c), 0))
    st = pl.BlockSpec((2, G, H_R, HEAD, HEAD), lambda s, c: (0, s, 0, 0, 0))
    return pl.pallas_call(
        _scan_kernel,
        out_shape=[jax.ShapeDtypeStruct((batch, t_len, D_R), F32),
                   jax.ShapeDtypeStruct((batch, t_len, D_R), F32),
                   jax.ShapeDtypeStruct((2, batch, H_R, HEAD, HEAD), F32)],
        grid=(batch // G, nch),
        in_specs=[sc_spec(0, fwd), sc_spec(1, bwd), v_spec(fwd), v_spec(bwd),
                  gt_spec(0, fwd), gt_spec(1, bwd), st],
        out_specs=[y_spec(fwd), y_spec(bwd), st],
        scratch_shapes=[pltpu.VMEM((G * 2 * D_R // LANES, LANES, LANES), F32)],
        compiler_params=_cparams("arbitrary", "arbitrary"),
        name="rwkv_scan",
    )(sc4, sc4, v3, v3, gt5, gt5, s0)


def _attn_kernel(*refs, tq, with_cache):
    if with_cache:
        q_ref, k_ref, v_ref, ck_ref, cv_ref, o_ref = refs
        kx = jnp.concatenate([k_ref[...], ck_ref[0].astype(BF16)], axis=0)
        vx = jnp.concatenate([v_ref[...], cv_ref[0]], axis=0).astype(BF16)
    else:
        q_ref, k_ref, v_ref, o_ref = refs
        kx = k_ref[...]
        vx = v_ref[...].astype(BF16)
    lane = lax.broadcasted_iota(jnp.int32, (1, D_KV), 1)
    low = lane < HEAD
    k_sw = pltpu.roll(kx.astype(F32), HEAD, 1).astype(BF16)
    v_sw = pltpu.roll(vx.astype(F32), HEAD, 1).astype(BF16)
    group = N_HEADS // N_KV
    for p in range(D_A // LANES):
        g = (2 * p) // group
        keep = low if g == 0 else jnp.logical_not(low)
        kd = jnp.where(keep, kx, k_sw)
        vd = jnp.where(keep, vx, v_sw)
        qp = q_ref[:, p * LANES:(p + 1) * LANES]
        zero = jnp.zeros_like(qp)
        qs = jnp.concatenate([jnp.where(low, qp, zero), jnp.where(low, zero, qp)], axis=0)
        s = _dot_nt(qs, kd)
        mx = jnp.max(s, axis=-1, keepdims=True)
        e = jnp.exp(s - mx)
        den = jnp.sum(e, axis=-1, keepdims=True)
        o = _dot(e.astype(BF16), vd) / den
        o_ref[:, p * LANES:(p + 1) * LANES] = jnp.where(low, o[0:tq], o[tq:2 * tq]).astype(BF16)


def _attention(qr, kr, va, row0, batch, t_len, cache_k=None, cache_v=None):
    tq = min(t_len, 256)
    nq = t_len // tq
    qblk0 = row0 // tq
    sblk0 = row0 // t_len
    with_cache = cache_k is not None
    in_specs = [pl.BlockSpec((tq, D_A), lambda b, i: (qblk0 + b * nq + i, 0)),
                pl.BlockSpec((t_len, D_KV), lambda b, i: (sblk0 + b, 0)),
                pl.BlockSpec((t_len, D_KV), lambda b, i: (sblk0 + b, 0))]
    args = [qr, kr, va]
    if with_cache:
        past = cache_k.shape[1]
        in_specs += [pl.BlockSpec((1, past, D_KV), lambda b, i: (b, 0, 0))] * 2
        args += [cache_k, cache_v]
    return pl.pallas_call(
        functools.partial(_attn_kernel, tq=tq, with_cache=with_cache),
        out_shape=jax.ShapeDtypeStruct((batch * t_len, D_A), BF16),
        grid=(batch, nq),
        in_specs=in_specs,
        out_specs=pl.BlockSpec((tq, D_A), lambda b, i: (b * nq + i, 0)),
        compiler_params=_cparams("arbitrary", "arbitrary"),
        name="attention_cache" if with_cache else "attention_ctx",
    )(*args)


def _post_kernel(yfc_ref, yfl_ref, ybc_ref, ybl_ref, bonus_ref, g_ref, yac_ref, yal_ref, gr_ref, ga_ref,
                 xc_ref, xl_ref,
                 mod_ref, lnw_ref, lnb_ref, wur_ref, wua_ref, wo_ref, n2_ref, wr_ref, br_ref, ones_ref,
                 x1_o, h2_o, idx_o, gate_o, *, ctx_blocks):
    ones = ones_ref[...]
    is_ctx = pl.program_id(0) < ctx_blocks
    y_a = jnp.where(is_ctx, yac_ref[...], yal_ref[...])
    x = jnp.where(is_ctx, xc_ref[...], xl_ref[...])
    y = jnp.where(is_ctx, yfc_ref[...] + ybc_ref[...], yfl_ref[...] + ybl_ref[...])
    mu = _head_sum(y, ones) * (1.0 / HEAD)
    yc = y - mu
    var = _head_sum(yc * yc, ones) * (1.0 / HEAD)
    yn = yc * lax.rsqrt(var + GN_EPS) * lnw_ref[...] + lnb_ref[...] + bonus_ref[...]
    y_r = (yn * g_ref[...]).astype(BF16)
    merged = (_sigmoid(gr_ref[...]) * _dot(y_r, wur_ref[...])
              + _sigmoid(ga_ref[...]) * _dot(y_a, wua_ref[...]))
    mix = _dot(merged.astype(BF16), wo_ref[...])
    mod = mod_ref[0]
    g1 = mod[:, 2 * D_MODEL:3 * D_MODEL]
    sh2 = mod[:, 3 * D_MODEL:4 * D_MODEL]
    sc2 = mod[:, 4 * D_MODEL:5 * D_MODEL]
    x1 = x + g1 * mix
    x1_o[...] = x1
    ms = jnp.mean(x1 * x1, axis=-1, keepdims=True)
    h2 = x1 * lax.rsqrt(ms + NORM_EPS) * n2_ref[...] * (1.0 + sc2) + sh2
    _store_token_tiles(h2_o, h2)
    hh, hl = _split2(h2)
    wh, wl = _split2(wr_ref[...])
    logits = _dot(hh, wh) + _dot(hh, wl) + _dot(hl, wh) + br_ref[...]
    lane_i = lax.broadcasted_iota(jnp.int32, logits.shape, 1)
    lane = lane_i.astype(F32)
    neg = jnp.float32(-jnp.inf)
    cur = jnp.where(lane_i < N_EXPERTS, logits, neg)
    vals, idxs = [], []
    for _ in range(TOP_K):
        mx = jnp.max(cur, axis=-1, keepdims=True)
        ix = jnp.min(jnp.where(cur == mx, lane, float(LANES)), axis=-1, keepdims=True)
        vals.append(mx)
        idxs.append(ix)
        cur = jnp.where(lane == ix, neg, cur)
    es = [jnp.exp(val - vals[0]) for val in vals]
    den = es[0] + es[1] + es[2] + es[3]
    idx_out = jnp.zeros(logits.shape, jnp.int32)
    gate_out = jnp.zeros(logits.shape, F32)
    for j in range(TOP_K):
        idx_out = jnp.where(lane_i == j, idxs[j].astype(jnp.int32), idx_out)
        gate_out = jnp.where(lane_i == j, es[j] / den, gate_out)
    idx_o[...] = idx_out
    gate_o[...] = gate_out


def _post(y_f, y_b, bonus, g, ya_ctx, ya_lat, gate_r, gate_a, x_ctx, x_lat, mods3, mod_row, p):
    n = x_ctx.shape[0] + x_lat.shape[0]
    nbc = ya_ctx.shape[0] // TM
    row = lambda i: (i, 0)
    full = lambda *shape: pl.BlockSpec(shape, lambda i: (0,) * len(shape))
    tok = lambda width: pl.BlockSpec((TM, width), row)
    return pl.pallas_call(
        functools.partial(_post_kernel, ctx_blocks=nbc),
        out_shape=[jax.ShapeDtypeStruct((n, D_MODEL), F32),
                   jax.ShapeDtypeStruct((n * TILE_ROWS, LANES), F32),
                   jax.ShapeDtypeStruct((n, LANES), jnp.int32),
                   jax.ShapeDtypeStruct((n, LANES), F32)],
        grid=(n // TM,),
        in_specs=_token_specs(nbc, D_R) + _token_specs(nbc, D_R) + [tok(D_R), tok(D_R)]
                 + _token_specs(nbc, D_A) + [tok(D_MODEL), tok(D_MODEL)] + _token_specs(nbc, D_MODEL) + [
                  pl.BlockSpec((1, 1, 6 * D_MODEL), lambda i: (mod_row(i), 0, 0)),
                  full(1, D_R), full(1, D_R), full(D_R, D_MODEL), full(D_A, D_MODEL),
                  full(D_MODEL, D_MODEL), full(1, D_MODEL), full(D_MODEL, LANES), full(1, LANES),
                  full(D_R, D_R)],
        out_specs=[tok(D_MODEL), pl.BlockSpec((TM * TILE_ROWS, LANES), row), tok(LANES), tok(LANES)],
        compiler_params=_cparams("arbitrary"),
        name="mixer_post",
    )(*y_f, *y_b, bonus, g, ya_ctx, ya_lat, gate_r, gate_a, x_ctx, x_lat, mods3,
      p["lnw"], p["lnb"], p["wur"], p["wua"], p["wo"], p["n2"], p["wr"], p["br"], p["ones"])


def _moe_kernel(blk_e_ref, blk_first_ref, used_ref,
                src0_ref, src_next_ref, dst_prev_ref, h_hbm, w1_ref, b1_ref, w2_ref, b2_ref,
                out_hbm, w1b, w2b, xs, ys, sem_g, sem_s, sem_t, *, n_rows):
    b = pl.program_id(0)
    used = used_ref[0]
    R = MOE_ROWS
    T = TILE_ROWS

    def tile(ref, i):
        return ref.at[pl.ds(pl.multiple_of(i * T, T), T)]

    def gather(table_ref, buf):
        for i in range(R):
            pltpu.make_async_copy(tile(h_hbm, table_ref[0, 0, i]), tile(xs.at[buf], i),
                                  sem_g.at[buf]).start(priority=i % 2)

    def scatter_prev(buf):
        for i in range(R):
            pltpu.make_async_copy(tile(ys.at[buf], i), tile(out_hbm, dst_prev_ref[0, 0, i]),
                                  sem_s.at[buf]).start(priority=i % 2)

    @pl.when(b == 0)
    def _():
        ys[...] = jnp.zeros_like(ys)
        for s in range(2):
            band = pltpu.make_async_copy(ys.at[s], out_hbm.at[pl.ds((n_rows + s * R) * T, R * T)], sem_t)
            band.start()
            band.wait()
        gather(src0_ref, 0)

    @pl.when(jnp.logical_and(blk_first_ref[b] == 1, b < used))
    def _():
        w1b[...] = w1_ref[0].astype(BF16)
        w2b[...] = w2_ref[0].astype(BF16)

    def step(p):
        @pl.when(b <= used)
        def _():
            pltpu.make_async_copy(h_hbm.at[pl.ds(0, R * T)], xs.at[p], sem_g.at[p]).wait()

        @pl.when(jnp.logical_and(b >= 1, b <= used + 1))
        def _():
            pltpu.make_async_copy(ys.at[p], out_hbm.at[pl.ds(0, R * T)], sem_s.at[p]).wait()

        @pl.when(b < used)
        def _():
            gather(src_next_ref, 1 - p)
            scatter_prev(1 - p)
            hb = _dot(_load_token_tiles(xs.at[p]).astype(BF16), w1b[...]) + b1_ref[0]
            glu = jnp.minimum(hb[:, 0:D_FF], SWIGLU_LIMIT)
            lin = jnp.clip(hb[:, D_FF:2 * D_FF], -SWIGLU_LIMIT, SWIGLU_LIMIT)
            act = glu * _sigmoid(SWIGLU_ALPHA * glu) * (lin + 1.0)
            _store_token_tiles(ys.at[p], _dot(act.astype(BF16), w2b[...]) + b2_ref[0])

        @pl.when(b == used)
        def _():
            scatter_prev(1 - p)

    for p in range(2):
        pl.when(b % 2 == p)(functools.partial(step, p))


def _moe(h2, src, dst_prev, blk_e, blk_first, used, w1, b1, w2, b2):
    d = D_MODEL
    n = h2.shape[0] // TILE_ROWS
    nb = src.shape[0]
    steps = nb + 2
    cur = lambda b, *_: (jnp.minimum(b, nb), 0, 0)
    nxt = lambda b, *_: (jnp.minimum(b + 1, nb - 1), 0, 0)
    table = lambda index_map: pl.BlockSpec((1, 1, MOE_ROWS), index_map, memory_space=pltpu.SMEM)
    expert = lambda b, blk_e, *_: (blk_e[jnp.minimum(b, nb - 1)], 0, 0)
    grid_spec = pltpu.PrefetchScalarGridSpec(
        num_scalar_prefetch=3,
        grid=(steps,),
        in_specs=[table(lambda b, *_: (0, 0, 0)), table(nxt), table(cur),
                  pl.BlockSpec(memory_space=pl.ANY),
                  pl.BlockSpec((1, d, 2 * D_FF), expert),
                  pl.BlockSpec((1, 1, 2 * D_FF), expert),
                  pl.BlockSpec((1, D_FF, d), expert),
                  pl.BlockSpec((1, 1, d), expert)],
        out_specs=pl.BlockSpec(memory_space=pl.ANY),
        scratch_shapes=[pltpu.VMEM((d, 2 * D_FF), BF16),
                        pltpu.VMEM((D_FF, d), BF16),
                        pltpu.VMEM((2, MOE_ROWS * TILE_ROWS, LANES), F32),
                        pltpu.VMEM((2, MOE_ROWS * TILE_ROWS, LANES), F32),
                        pltpu.SemaphoreType.DMA((2,)),
                        pltpu.SemaphoreType.DMA((2,)),
                        pltpu.SemaphoreType.DMA(())])
    tab3 = lambda t: t.reshape(-1, 1, MOE_ROWS)
    pad = lambda t: jnp.concatenate([t, jnp.zeros((steps - nb,), jnp.int32)])
    return pl.pallas_call(
        functools.partial(_moe_kernel, n_rows=n * TOP_K),
        out_shape=jax.ShapeDtypeStruct(((n * TOP_K + 2 * MOE_ROWS) * TILE_ROWS, LANES), F32),
        grid_spec=grid_spec,
        compiler_params=pltpu.CompilerParams(dimension_semantics=("arbitrary",),
                                             vmem_limit_bytes=VMEM_LIMIT, has_side_effects=True),
        name="moe_experts",
    )(blk_e, pad(blk_first), used, tab3(src), tab3(src), tab3(dst_prev), h2, w1,
      b1.reshape(N_EXPERTS, 1, 2 * D_FF), w2, b2.reshape(N_EXPERTS, 1, d))


def _route(top_idx, n):
    nk = n * TOP_K
    cap = nk + N_EXPERTS * MOE_ROWS
    nb = cap // MOE_ROWS
    flat_e = top_idx.reshape(nk)
    order = jnp.argsort(flat_e, stable=True).astype(jnp.int32)
    counts = jnp.sum((flat_e[:, None] == jnp.arange(N_EXPERTS, dtype=jnp.int32)[None, :])
                     .astype(jnp.int32), axis=0)
    start = jnp.cumsum(counts) - counts
    n_blk = (counts + MOE_ROWS - 1) // MOE_ROWS
    blk_end = jnp.cumsum(n_blk)
    first_blk = blk_end - n_blk
    experts = jnp.arange(N_EXPERTS, dtype=jnp.int32)
    blk = jnp.arange(nb, dtype=jnp.int32)
    blk_e = jnp.minimum(jnp.sum((blk[:, None] >= blk_end[None, :]).astype(jnp.int32), axis=1),
                        N_EXPERTS - 1)
    onehot = (blk_e[:, None] == experts[None, :]).astype(jnp.int32)
    pick = lambda t: jnp.sum(onehot * t[None, :], axis=1)
    blk_first_of_e, blk_count, blk_start = pick(first_blk), pick(counts), pick(start)
    blk_first = (blk == blk_first_of_e).astype(jnp.int32)
    row = jnp.arange(MOE_ROWS, dtype=jnp.int32)[None, :]
    within = (blk - blk_first_of_e)[:, None] * MOE_ROWS + row
    valid = jnp.logical_and(within < blk_count[:, None], (blk < blk_end[-1])[:, None])
    j = order[jnp.clip(blk_start[:, None] + within, 0, nk - 1)]
    src = jnp.where(valid, j // TOP_K, 0).astype(jnp.int32)
    scratch_row = nk + (blk % 2)[:, None] * MOE_ROWS + row
    dst = jnp.where(valid, (j % TOP_K) * n + j // TOP_K, scratch_row).astype(jnp.int32)
    dst_prev = jnp.concatenate([nk + MOE_ROWS + row, dst], axis=0)
    return src, dst_prev, blk_e.astype(jnp.int32), blk_first, blk_end[-1:].astype(jnp.int32)


def _final_kernel(x1_ref, y0_ref, y1_ref, y2_ref, y3_ref, gate_ref, mod_ref, nf_ref, oc_ref, ol_ref,
                  *, ctx_blocks):
    gates = gate_ref[...]
    ffn = gates[:, 0:1] * _load_token_tiles(y0_ref)
    for j, ref in enumerate((y1_ref, y2_ref, y3_ref), start=1):
        ffn = ffn + gates[:, j:j + 1] * _load_token_tiles(ref)
    g2 = mod_ref[0][:, 5 * D_MODEL:6 * D_MODEL]
    x2 = x1_ref[...] + g2 * ffn
    ms = jnp.mean(x2 * x2, axis=-1, keepdims=True)
    y = x2 * lax.rsqrt(ms + NORM_EPS) * nf_ref[...]
    i = pl.program_id(0)

    @pl.when(i < ctx_blocks)
    def _():
        oc_ref[...] = y

    @pl.when(i >= ctx_blocks)
    def _():
        ol_ref[...] = y


def _final(x1, ys, gates, mods3, mod_row, norm_f, nc):
    n, d = x1.shape
    nb = n // TM
    nbc = nc // TM
    row = lambda i: (i, 0)
    return pl.pallas_call(
        functools.partial(_final_kernel, ctx_blocks=nbc),
        out_shape=[jax.ShapeDtypeStruct((nc, d), F32), jax.ShapeDtypeStruct((n - nc, d), F32)],
        grid=(nb,),
        in_specs=[pl.BlockSpec((TM, d), row)]
                 + [pl.BlockSpec((TM * TILE_ROWS, LANES), functools.partial(lambda j, i: (j * nb + i, 0), j))
                    for j in range(TOP_K)]
                 + [pl.BlockSpec((TM, LANES), row),
                    pl.BlockSpec((1, 1, 6 * d), lambda i: (mod_row(i), 0, 0)),
                    pl.BlockSpec((1, d), lambda i: (0, 0))],
        out_specs=_token_specs(nbc, d),
        compiler_params=_cparams("arbitrary"),
        name="combine_final",
    )(x1, ys, ys, ys, ys, gates, mods3, norm_f.reshape(1, d))


def _rope_tables(t_ctx, t_lat):
    rows = np.arange(t_lat) // GRID_W
    cols = np.arange(t_lat) % GRID_W
    inv = ROPE_THETA ** (-np.arange(0, AXIS_DIM, 2, dtype=np.float32) / AXIS_DIM)
    inv = jnp.asarray(inv, F32)
    ang_r = jnp.asarray(rows, F32)[:, None] * inv[None, :]
    ang_c = jnp.asarray(cols, F32)[:, None] * inv[None, :]
    cos = jnp.concatenate([jnp.cos(ang_r)] * 2 + [jnp.cos(ang_c)] * 2, axis=1)
    sin = jnp.concatenate([-jnp.sin(ang_r), jnp.sin(ang_r), -jnp.sin(ang_c), jnp.sin(ang_c)], axis=1)
    cos = jnp.concatenate([jnp.ones((t_ctx, HEAD), F32), cos], axis=0)
    sin = jnp.concatenate([jnp.zeros((t_ctx, HEAD), F32), sin], axis=0)
    return jnp.concatenate([cos, cos], axis=1), jnp.concatenate([sin, sin], axis=1)


def kernel(x_prompt, x_sample, cache_k, cache_v, state_rwkv_fwd, state_rwkv_bwd, c, c_ctx, w_ada, b_ada, norm1, norm2, w_in, rw_w0, rw_w2, rw_a0, rw_a2, rw_g2, rw_kk, rw_ka, rw_rk, rw_ln_w, rw_ln_b, q_norm, k_norm, w_up_r, w_up_a, w_out, w_router, b_router, w_moe_in, b_moe_in, w_moe_out, b_moe_out, norm_f):
    depth = w_in.shape[0]
    assert depth == 1, "single trunk layer"
    bc, tc, d = x_prompt.shape
    bl, tl, _ = x_sample.shape
    nc, nl = bc * tc, bl * tl
    n = nc + nl
    assert d == D_MODEL and tc == TM and tl % TM == 0 and nc % tl == 0
    l = 0

    x_ctx, x_lat = x_prompt.reshape(nc, d), x_sample.reshape(nl, d)
    cond =jnp.concatenate([c_ctx[None, :], c, jnp.zeros((16 - 1 - bl, d), F32)], axis=0)
    mods3 = _adaln(cond, w_ada[l], b_ada[l]).reshape(16, 1, 6 * d)
    nbc = nc // TM
    per_seq = tl // TM
    mod_row = lambda i: jnp.where(i < nbc, 0, 1 + (i - nbc) // per_seq)
    tab_row = lambda i: jnp.where(i < nbc, 0, 1 + (i - nbc) % per_seq)

    z = dict(zip([name for name, _ in IN_SPLITS],
                 _proj_in(x_ctx, x_lat, mods3, norm1[l], w_in[l].astype(BF16), mod_row)))

    ones_bd = jnp.asarray(np.kron(np.eye(H_R), np.ones((HEAD, HEAD))), BF16)
    pad_lo = lambda w: jnp.concatenate([w, jnp.zeros_like(w)], axis=0)
    pad_hi = lambda w: jnp.concatenate([jnp.zeros_like(w), w], axis=0)
    prep_p = dict(
        w0=rw_w0[l], a0=rw_a0[l],
        w2=jnp.stack([pad_lo(rw_w2[l, 0]), pad_hi(rw_w2[l, 1])]).astype(BF16),
        a2=jnp.stack([pad_lo(rw_a2[l, 0]), pad_hi(rw_a2[l, 1])]).astype(BF16),
        g2=rw_g2[l].astype(BF16),
        kkw=rw_kk[l].reshape(1, D_R), kaw=rw_ka[l].reshape(1, D_R), rk=rw_rk[l].reshape(1, D_R),
        qn=jnp.tile(q_norm[l], N_HEADS).reshape(1, D_A), kn=jnp.tile(k_norm[l], N_KV).reshape(1, D_KV),
        ones=ones_bd,
        tri=jnp.asarray(np.stack([np.kron(np.eye(TM // CHUNK), np.tril(np.ones((CHUNK, CHUNK)))),
                                  np.kron(np.eye(TM // CHUNK), np.triu(np.ones((CHUNK, CHUNK))))]), BF16),
        chunk_ones=jnp.asarray(np.kron(np.eye(TM // CHUNK), np.ones((CHUNK, CHUNK))), BF16))
    cos_tab, sin_tab = _rope_tables(tc, tl)
    sc, gt, vb, bonus, g, qr, kn, kr = _prep(z, cos_tab, sin_tab, tab_row, prep_p)

    yf_c, yb_c, s_fin = _rwkv_scan(sc, gt, vb, jnp.zeros((2, bc, H_R, HEAD, HEAD), F32), 0, bc, tc)
    yf_l, yb_l, _ = _rwkv_scan(sc, gt, vb, jnp.stack([state_rwkv_fwd[:, l], state_rwkv_bwd[:, l]]),
                               nc, bl, tl)
    y_f = (yf_c.reshape(nc, D_R), yf_l.reshape(nl, D_R))
    y_b = (yb_c.reshape(nc, D_R), yb_l.reshape(nl, D_R))

    ya_ctx = _attention(qr, kr, z["va"], 0, bc, tc)
    ya_lat = _attention(qr, kr, z["va"], nc, bl, tl,
                        cache_k[:, l].reshape(bl, -1, D_KV), cache_v[:, l].reshape(bl, -1, D_KV))

    post_p = dict(
        lnw=rw_ln_w[l].reshape(1, D_R), lnb=rw_ln_b[l].reshape(1, D_R),
        wur=w_up_r[l].astype(BF16), wua=w_up_a[l].astype(BF16), wo=w_out[l].astype(BF16),
        n2=norm2[l].reshape(1, d),
        wr=jnp.pad(w_router[l], ((0, 0), (0, LANES - N_EXPERTS))),
        br=jnp.pad(b_router[l], (0, LANES - N_EXPERTS)).reshape(1, LANES),
        ones=ones_bd)
    x1, h2, top_idx, gates = _post(y_f, y_b, bonus, g, ya_ctx, ya_lat, z["gate_r"], z["gate_a"],
                                   x_ctx, x_lat, mods3, mod_row, post_p)

    src, dst, blk_e, blk_first, used = _route(top_idx[:, :TOP_K], n)
    ys = _moe(h2, src, dst, blk_e, blk_first, used,
              w_moe_in[l], b_moe_in[l], w_moe_out[l], b_moe_out[l])
    y_ctx, y_lat = _final(x1, ys, gates, mods3, mod_row, norm_f, nc)

    y_prompt = y_ctx.reshape(bc, tc, d)
    y_sample = y_lat.reshape(bl, tl, d)
    new_cache_k = kn[:nc].reshape(bc, 1, tc, N_KV, HEAD)
    new_cache_v = z["va"][:nc].reshape(bc, 1, tc, N_KV, HEAD)
    new_state_fwd = s_fin[0][:, None]
    new_state_bwd = s_fin[1][:, None]
    return (y_prompt, y_sample, new_cache_k, new_cache_v, new_state_fwd, new_state_bwd)
```

```python
import functools

import numpy as np
import jax
import jax.numpy as jnp
from jax import lax
from jax.experimental import pallas as pl
from jax.experimental.pallas import tpu as pltpu

F32 = jnp.float32
BF16 = jnp.bfloat16

D_MODEL = 1024
GRID_W = 64
NORM_EPS = 1e-6
HEAD = 64
H_R = 8
D_R = H_R * HEAD
DECAY_RANK = 64
AAA_RANK = 64
GATE_RANK = 128
GN_EPS = 64e-5
N_HEADS = 8
N_KV = 2
D_A = N_HEADS * HEAD
D_KV = N_KV * HEAD
AXIS_DIM = HEAD // 2
ROPE_THETA = 10000.0
N_EXPERTS = 32
TOP_K = 4
D_FF = D_MODEL
SWIGLU_ALPHA = 1.702
SWIGLU_LIMIT = 7.0
LORA_COLS = 2 * DECAY_RANK + 2 * AAA_RANK + GATE_RANK
IN_SPLITS = (("r", D_R), ("k", D_R), ("v", D_R), ("lora", LORA_COLS), ("q", D_A),
             ("ka", D_KV), ("va", D_KV), ("gate_r", D_MODEL), ("gate_a", D_MODEL))

LANES = 128
TM = 256
CHUNK = 64
SCAN_OPS = 6
SCAN_GROUP = 2
MOE_ROWS = 256
VMEM_LIMIT = 56 * 1024 * 1024


def _cparams(*sem):
    return pltpu.CompilerParams(dimension_semantics=sem, vmem_limit_bytes=VMEM_LIMIT)


def _dot(a, b):
    return jnp.dot(a, b, preferred_element_type=F32)


def _dot_nt(a, b):
    return lax.dot_general(a, b, (((1,), (1,)), ((), ())), preferred_element_type=F32)


def _dot_tn(a, b):
    return lax.dot_general(a, b, (((0,), (0,)), ((), ())), preferred_element_type=F32)


def _split2(x):
    hi = x.astype(BF16)
    lo = (x - hi.astype(F32)).astype(BF16)
    return hi, lo


def _split3(x):
    x1 = x.astype(BF16)
    r1 = x - x1.astype(F32)
    x2 = r1.astype(BF16)
    x3 = (r1 - x2.astype(F32)).astype(BF16)
    return x1, x2, x3


def _head_sum(x, ones_bd):
    hi, lo = _split2(x)
    return _dot(hi, ones_bd) + _dot(lo, ones_bd)


def _sigmoid(x):
    return 1.0 / (1.0 + jnp.exp(-x))


TILE_ROWS = D_MODEL // LANES


def _store_token_tiles(ref, x):
    rows = x.shape[0]
    for j in range(TILE_ROWS):
        ref[pl.ds(j, rows, stride=TILE_ROWS), :] = x[:, j * LANES:(j + 1) * LANES]


def _load_token_tiles(ref):
    rows = ref.shape[0] // TILE_ROWS
    return jnp.concatenate([ref[pl.ds(j, rows, stride=TILE_ROWS), :] for j in range(TILE_ROWS)], axis=1)


def _adaln_kernel(c_ref, w_ref, b_ref, o_ref):
    c = c_ref[...]
    s = (c * _sigmoid(c)).astype(BF16)
    o_ref[...] = _dot(s, w_ref[...].astype(BF16)) + b_ref[...]


def _adaln(cond, w_ada, b_ada):
    rows, d = cond.shape
    cols = w_ada.shape[1]
    tn = 512
    return pl.pallas_call(
        _adaln_kernel,
        out_shape=jax.ShapeDtypeStruct((rows, cols), F32),
        grid=(cols // tn,),
        in_specs=[pl.BlockSpec((rows, d), lambda j: (0, 0)),
                  pl.BlockSpec((d, tn), lambda j: (0, j)),
                  pl.BlockSpec((1, tn), lambda j: (0, j))],
        out_specs=pl.BlockSpec((rows, tn), lambda j: (0, j)),
        compiler_params=_cparams("arbitrary"),
        name="adaln",
    )(cond, w_ada, b_ada.reshape(1, cols))


def _token_specs(nbc, width):
    return [pl.BlockSpec((TM, width), lambda i: (jnp.minimum(i, nbc - 1), 0)),
            pl.BlockSpec((TM, width), lambda i: (jnp.maximum(i - nbc, 0), 0))]


def _proj_in_kernel(xc_ref, xl_ref, mod_ref, n1_ref, w_ref, *out_refs, ctx_blocks):
    x = jnp.where(pl.program_id(0) < ctx_blocks, xc_ref[...], xl_ref[...])
    ms = jnp.mean(x * x, axis=-1, keepdims=True)
    hn = x * lax.rsqrt(ms + NORM_EPS) * n1_ref[...]
    mod = mod_ref[0]
    sh1 = mod[:, 0:D_MODEL]
    sc1 = mod[:, D_MODEL:2 * D_MODEL]
    h = (hn * (1.0 + sc1) + sh1).astype(BF16)
    c0 = 0
    for ref, (_, width) in zip(out_refs, IN_SPLITS):
        ref[...] = _dot(h, w_ref[:, c0:c0 + width])
        c0 += width


def _proj_in(x_ctx, x_lat, mods3, norm1, w_in_bf, mod_row):
    d = x_ctx.shape[1]
    n = x_ctx.shape[0] + x_lat.shape[0]
    nbc = x_ctx.shape[0] // TM
    cols = w_in_bf.shape[1]
    row = lambda i: (i, 0)
    return pl.pallas_call(
        functools.partial(_proj_in_kernel, ctx_blocks=nbc),
        out_shape=[jax.ShapeDtypeStruct((n, width), F32) for _, width in IN_SPLITS],
        grid=(n // TM,),
        in_specs=_token_specs(nbc, d) + [
                  pl.BlockSpec((1, 1, 6 * d), lambda i: (mod_row(i), 0, 0)),
                  pl.BlockSpec((1, d), lambda i: (0, 0)),
                  pl.BlockSpec((d, cols), lambda i: (0, 0))],
        out_specs=[pl.BlockSpec((TM, width), row) for _, width in IN_SPLITS],
        compiler_params=_cparams("arbitrary"),
        name="proj_in",
    )(x_ctx, x_lat, mods3, norm1.reshape(1, d), w_in_bf)


def _prep_kernel(r_ref, k_ref, v_ref, lora_ref, q_ref, ka_ref, cos_ref, sin_ref,
                 w0_ref, w2_ref, a0_ref, a2_ref, g2_ref, kkw_ref, kaw_ref, rk_ref,
                 qn_ref, kn_ref, ones_ref, tri_ref, chunk_ones_ref,
                 sc_o, gt_o, vb_o, bonus_o, g_o, qr_o, kn_o, kr_o):
    ones = ones_ref[...]
    chunk_ones = chunk_ones_ref[...]
    r = r_ref[...]
    k = k_ref[...]
    v = v_ref[...]
    vb_o[...] = v.astype(BF16)
    lora = lora_ref[...]
    kk = k * kkw_ref[...]
    kk = kk * lax.rsqrt(_head_sum(kk * kk, ones) + 1e-12)
    th = jnp.tanh(lora[:, 0:LANES]).astype(BF16)
    al = lora[:, LANES:2 * LANES].astype(BF16)
    kd_sum = jnp.zeros_like(k)
    for d in range(2):
        u = w0_ref[d:d + 1, :] + _dot(th, w2_ref[d])
        w_log = -(jnp.maximum(-u, 0.0) + jnp.log1p(jnp.exp(-jnp.abs(u)))) - 0.5
        lw = -jnp.exp(w_log)
        a = _sigmoid(a0_ref[d:d + 1, :] + _dot(al, a2_ref[d]))
        kd = k * (1.0 + (a - 1.0) * kaw_ref[...])
        kd_sum = kd_sum + kd
        b = kk * a
        parts = _split3(lw)
        tri = tri_ref[d]
        L = _dot(tri, parts[0]) + _dot(tri, parts[1]) + _dot(tri, parts[2])
        Ltot = _dot(chunk_ones, parts[0]) + _dot(chunk_ones, parts[1]) + _dot(chunk_ones, parts[2])
        e_inv = jnp.exp(-L)
        e_rest = jnp.exp(Ltot - L)
        scan_ops = (kk * jnp.exp(L - lw), r * jnp.exp(L), b * e_inv, kd * e_inv, b * e_rest, kd * e_rest)
        for j, op in enumerate(scan_ops):
            sc_o[d, :, j * D_R:(j + 1) * D_R] = op.astype(BF16)
        g_tot = jnp.exp(Ltot)
        for j in range(TM // CHUNK):
            gt_o[d, j] = g_tot[j * CHUNK:j * CHUNK + 1, :]
    bonus_o[...] = _head_sum(r * kd_sum * rk_ref[...], ones) * v
    g_o[...] = _dot(_sigmoid(lora[:, 2 * LANES:3 * LANES]).astype(BF16), g2_ref[...])
    q = q_ref[...]
    qn = q * lax.rsqrt(_head_sum(q * q, ones) * (1.0 / HEAD) + NORM_EPS) * qn_ref[...]
    ka = ka_ref[...]
    ones_kv = ones[0:D_KV, 0:D_KV]
    kn = ka * lax.rsqrt(_head_sum(ka * ka, ones_kv) * (1.0 / HEAD) + NORM_EPS) * kn_ref[...]
    kn_o[...] = kn
    cos = cos_ref[...]
    sin = sin_ref[...]
    half = AXIS_DIM // 2

    def rope(t, cos_t, sin_t):
        width = t.shape[1]
        lane = lax.broadcasted_iota(jnp.int32, t.shape, 1)
        first = (lane % AXIS_DIM) < half
        swapped = jnp.where(first, pltpu.roll(t, width - half, 1), pltpu.roll(t, half, 1))
        return t * cos_t + swapped * sin_t

    cos4 = jnp.concatenate([cos] * (D_A // D_KV), axis=1)
    sin4 = jnp.concatenate([sin] * (D_A // D_KV), axis=1)
    qr_o[...] = (rope(qn, cos4, sin4) * (HEAD ** -0.5)).astype(BF16)
    kr_o[...] = rope(kn, cos, sin).astype(BF16)


def _prep(z, cos_tab, sin_tab, tab_row, p):
    n = z["r"].shape[0]
    row = lambda i: (i, 0)
    full = lambda *shape: pl.BlockSpec(shape, lambda i: (0,) * len(shape))
    tok = lambda width: pl.BlockSpec((TM, width), row)
    tab = pl.BlockSpec((TM, D_KV), lambda i: (tab_row(i), 0))
    cpb = TM // CHUNK
    return pl.pallas_call(
        _prep_kernel,
        out_shape=[jax.ShapeDtypeStruct((2, n, SCAN_OPS * D_R), BF16),
                   jax.ShapeDtypeStruct((2, n // CHUNK, 1, D_R), F32),
                   jax.ShapeDtypeStruct((n, D_R), BF16),
                   jax.ShapeDtypeStruct((n, D_R), F32),
                   jax.ShapeDtypeStruct((n, D_R), F32),
                   jax.ShapeDtypeStruct((n, D_A), BF16),
                   jax.ShapeDtypeStruct((n, D_KV), F32),
                   jax.ShapeDtypeStruct((n, D_KV), BF16)],
        grid=(n // TM,),
        in_specs=[tok(D_R), tok(D_R), tok(D_R), tok(LORA_COLS), tok(D_A), tok(D_KV), tab, tab,
                  full(2, D_R), full(2, LANES, D_R), full(2, D_R), full(2, LANES, D_R),
                  full(GATE_RANK, D_R), full(1, D_R), full(1, D_R), full(1, D_R),
                  full(1, D_A), full(1, D_KV), full(D_R, D_R), full(2, TM, TM), full(TM, TM)],
        out_specs=[pl.BlockSpec((2, TM, SCAN_OPS * D_R), lambda i: (0, i, 0)),
                   pl.BlockSpec((2, cpb, 1, D_R), lambda i: (0, i, 0, 0)),
                   tok(D_R), tok(D_R), tok(D_R), tok(D_A), tok(D_KV), tok(D_KV)],
        compiler_params=_cparams("arbitrary"),
        name="mixer_prep",
    )(z["r"], z["k"], z["v"], z["lora"], z["q"], z["ka"], cos_tab, sin_tab,
      p["w0"], p["w2"], p["a0"], p["a2"], p["g2"], p["kkw"], p["kaw"], p["rk"],
      p["qn"], p["kn"], p["ones"], p["tri"], p["chunk_ones"])


def _scan_kernel(scf_ref, scb_ref, vf_ref, vb_ref, gf_ref, gb_ref, s0_ref,
                 yf_ref, yb_ref, sfin_ref, state):
    c = pl.program_id(1)
    C = CHUNK
    pairs = D_R // LANES
    chains = [(g, d, p) for g in range(SCAN_GROUP) for d in range(2) for p in range(pairs)]

    lane = lax.broadcasted_iota(jnp.int32, (1, LANES), 1)
    low = lane < HEAD

    @pl.when(c == 0)
    def _():
        zero = jnp.zeros((HEAD, HEAD), F32)
        for i, (g, d, p) in enumerate(chains):
            top = jnp.concatenate([s0_ref[d, g, 2 * p], zero], axis=1)
            bot = jnp.concatenate([zero, s0_ref[d, g, 2 * p + 1]], axis=1)
            state[i] = jnp.concatenate([top, bot], axis=0)

    W = 2 * C
    t2 = lax.broadcasted_iota(jnp.int32, (C, W), 0)
    s2 = lax.broadcasted_iota(jnp.int32, (C, W), 1) % C
    strict = (s2 < t2, s2 > t2)
    incl = (s2 <= t2, s2 >= t2)
    eye2 = jnp.where(s2 == t2, 1.0, 0.0)
    first_mat = lax.broadcasted_iota(jnp.int32, (1, W), 1) < C
    r2 = lax.broadcasted_iota(jnp.int32, (LANES, LANES), 0)
    c2 = lax.broadcasted_iota(jnp.int32, (LANES, LANES), 1)
    same_head = (r2 // HEAD) == (c2 // HEAD)

    sc_refs = (scf_ref, scb_ref)
    v_refs = (vf_ref, vb_ref)
    g_refs = (gf_ref, gb_ref)
    y_refs = (yf_ref, yb_ref)

    def operand(chain, j):
        g, d, p = chain
        c0 = j * D_R + p * LANES
        return sc_refs[d][0, g, :, c0:c0 + LANES]

    def stack(x):
        first = low if x.shape[1] == LANES else first_mat
        zero = jnp.zeros_like(x)
        return jnp.concatenate([jnp.where(first, x, zero), jnp.where(first, zero, x)], axis=0)

    Qk = [operand(ch, 0) for ch in chains]
    Qr = [operand(ch, 1) for ch in chains]
    v_p = [v_refs[d][g, :, p * LANES:(p + 1) * LANES] for g, d, p in chains]
    v_st = [stack(x) for x in v_p]
    Z0 = [state[i] for i in range(len(chains))]
    Z0b = [z.astype(BF16) for z in Z0]
    m = [_dot_nt(jnp.concatenate([Qk[i], Qr[i]], axis=0),
                 jnp.concatenate([stack(operand(ch, 2)), stack(operand(ch, 3))], axis=0))
         for i, ch in enumerate(chains)]
    A = [jnp.where(strict[d], m[i][0:C, 0:W], 0.0) for i, (g, d, p) in enumerate(chains)]
    Bm = [jnp.where(strict[d], m[i][0:C, W:2 * W], 0.0).astype(BF16)
          for i, (g, d, p) in enumerate(chains)]
    Pb = [jnp.where(incl[d], m[i][C:2 * C, 0:W], 0.0).astype(BF16)
          for i, (g, d, p) in enumerate(chains)]
    Pk = [jnp.where(incl[d], m[i][C:2 * C, W:2 * W], 0.0).astype(BF16)
          for i, (g, d, p) in enumerate(chains)]
    BV = [_dot(Bm[i], v_st[i]) for i in range(len(chains))]
    T = [eye2 - a for a in A]
    Ap = [_dot(a.astype(BF16), stack(a.astype(BF16))).astype(BF16) for a in A]
    n = 2
    while 2 * n < C:
        R = [_dot(jnp.concatenate([Ap[i], T[i].astype(BF16)], axis=0), stack(Ap[i]))
             for i in range(len(chains))]
        Ap = [x[0:C].astype(BF16) for x in R]
        T = [T[i] + R[i][C:2 * C] for i in range(len(chains))]
        n *= 2
    T = [T[i] + _dot(T[i].astype(BF16), stack(Ap[i])) for i in range(len(chains))]
    WU = [_dot(T[i].astype(BF16),
               jnp.concatenate([stack(Qk[i]), stack(BV[i].astype(BF16))], axis=1))
          for i in range(len(chains))]
    Wq = [x[:, 0:LANES].astype(BF16) for x in WU]
    Uv = [x[:, LANES:2 * LANES] for x in WU]
    PW = [_dot(Pb[i], jnp.concatenate([stack(Wq[i]), stack(Uv[i].astype(BF16))], axis=1))
          for i in range(len(chains))]
    PkV = [_dot(Pk[i], v_st[i]) for i in range(len(chains))]
    U = [_dot_nt(Wq[i], Z0b[i]) + Uv[i] for i in range(len(chains))]
    Yq = [(Qr[i].astype(F32) - PW[i][:, 0:LANES]).astype(BF16) for i in range(len(chains))]
    for i, (g, d, p) in enumerate(chains):
        y_refs[d][g, :, p * LANES:(p + 1) * LANES] = (_dot_nt(Yq[i], Z0b[i]) + PkV[i]
                                                      - PW[i][:, LANES:2 * LANES])
    for i, (g, d, p) in enumerate(chains):
        lhs = jnp.concatenate([v_p[i], (-U[i]).astype(BF16)], axis=0)
        rhs = jnp.concatenate([operand(chains[i], 5), operand(chains[i], 4)], axis=0)
        upd = _dot_tn(lhs, rhs)
        g_tot = g_refs[d][0, g, 0][:, p * LANES:(p + 1) * LANES]
        state[i] = Z0[i] * g_tot + jnp.where(same_head, upd, 0.0)

    @pl.when(c == pl.num_programs(1) - 1)
    def _():
        for i, (g, d, p) in enumerate(chains):
            z = state[i]
            sfin_ref[d, g, 2 * p] = z[0:HEAD, 0:HEAD]
            sfin_ref[d, g, 2 * p + 1] = z[HEAD:2 * HEAD, HEAD:2 * HEAD]


def _rwkv_scan(sc, gt, vb, s0, row0, batch, t_len):
    n = vb.shape[0]
    nch = t_len // CHUNK
    assert row0 % (t_len * SCAN_GROUP) == 0 and batch % SCAN_GROUP == 0
    seq0 = row0 // (t_len * SCAN_GROUP)
    width = SCAN_OPS * D_R
    sc4 = sc.reshape(2, n // t_len, t_len, width)
    gt5 = gt.reshape(2, n // t_len, nch, 1, D_R)
    v3 = vb.reshape(n // t_len, t_len, D_R)
    G = SCAN_GROUP
    fwd = lambda c: c
    bwd = lambda c: nch - 1 - c
    sc_spec = lambda d, at: pl.BlockSpec((1, G, CHUNK, width), lambda s, c: (d, seq0 + s, at(c), 0))
    v_spec = lambda at: pl.BlockSpec((G, CHUNK, D_R), lambda s, c: (seq0 + s, at(c), 0))
    gt_spec = lambda d, at: pl.BlockSpec((1, G, 1, 1, D_R), lambda s, c: (d, seq0 + s, at(c), 0, 0))
    y_spec = lambda at: pl.BlockSpec((G, CHUNK, D_R), lambda s, c: (s, at(c), 0))
    st = pl.BlockSpec((2, G, H_R, HEAD, HEAD), lambda s, c: (0, s, 0, 0, 0))
    return pl.pallas_call(
        _scan_kernel,
        out_shape=[jax.ShapeDtypeStruct((batch, t_len, D_R), F32),
                   jax.ShapeDtypeStruct((batch, t_len, D_R), F32),
                   jax.ShapeDtypeStruct((2, batch, H_R, HEAD, HEAD), F32)],
        grid=(batch // G, nch),
        in_specs=[sc_spec(0, fwd), sc_spec(1, bwd), v_spec(fwd), v_spec(bwd),
                  gt_spec(0, fwd), gt_spec(1, bwd), st],
        out_specs=[y_spec(fwd), y_spec(bwd), st],
        scratch_shapes=[pltpu.VMEM((G * 2 * D_R // LANES, LANES, LANES), F32)],
        compiler_params=_cparams("arbitrary", "arbitrary"),
        name="rwkv_scan",
    )(sc4, sc4, v3, v3, gt5, gt5, s0)


def _attn_kernel(*refs, tq, with_cache):
    if with_cache:
        q_ref, k_ref, v_ref, ck_ref, cv_ref, o_ref = refs
        kx = jnp.concatenate([k_ref[...], ck_ref[0].astype(BF16)], axis=0)
        vx = jnp.concatenate([v_ref[...], cv_ref[0]], axis=0).astype(BF16)
    else:
        q_ref, k_ref, v_ref, o_ref = refs
        kx = k_ref[...]
        vx = v_ref[...].astype(BF16)
    lane = lax.broadcasted_iota(jnp.int32, (1, D_KV), 1)
    low = lane < HEAD
    k_sw = pltpu.roll(kx.astype(F32), HEAD, 1).astype(BF16)
    v_sw = pltpu.roll(vx.astype(F32), HEAD, 1).astype(BF16)
    group = N_HEADS // N_KV
    for p in range(D_A // LANES):
        g = (2 * p) // group
        keep = low if g == 0 else jnp.logical_not(low)
        kd = jnp.where(keep, kx, k_sw)
        vd = jnp.where(keep, vx, v_sw)
        qp = q_ref[:, p * LANES:(p + 1) * LANES]
        zero = jnp.zeros_like(qp)
        qs = jnp.concatenate([jnp.where(low, qp, zero), jnp.where(low, zero, qp)], axis=0)
        s = _dot_nt(qs, kd)
        mx = jnp.max(s, axis=-1, keepdims=True)
        e = jnp.exp(s - mx)
        den = jnp.sum(e, axis=-1, keepdims=True)
        o = _dot(e.astype(BF16), vd) / den
        o_ref[:, p * LANES:(p + 1) * LANES] = jnp.where(low, o[0:tq], o[tq:2 * tq]).astype(BF16)


def _attention(qr, kr, va, row0, batch, t_len, cache_k=None, cache_v=None):
    tq = min(t_len, 256)
    nq = t_len // tq
    qblk0 = row0 // tq
    sblk0 = row0 // t_len
    with_cache = cache_k is not None
    in_specs = [pl.BlockSpec((tq, D_A), lambda b, i: (qblk0 + b * nq + i, 0)),
                pl.BlockSpec((t_len, D_KV), lambda b, i: (sblk0 + b, 0)),
                pl.BlockSpec((t_len, D_KV), lambda b, i: (sblk0 + b, 0))]
    args = [qr, kr, va]
    if with_cache:
        past = cache_k.shape[1]
        in_specs += [pl.BlockSpec((1, past, D_KV), lambda b, i: (b, 0, 0))] * 2
        args += [cache_k, cache_v]
    return pl.pallas_call(
        functools.partial(_attn_kernel, tq=tq, with_cache=with_cache),
        out_shape=jax.ShapeDtypeStruct((batch * t_len, D_A), BF16),
        grid=(batch, nq),
        in_specs=in_specs,
        out_specs=pl.BlockSpec((tq, D_A), lambda b, i: (b * nq + i, 0)),
        compiler_params=_cparams("arbitrary", "arbitrary"),
        name="attention_cache" if with_cache else "attention_ctx",
    )(*args)


def _post_kernel(yfc_ref, yfl_ref, ybc_ref, ybl_ref, bonus_ref, g_ref, yac_ref, yal_ref, gr_ref, ga_ref,
                 xc_ref, xl_ref,
                 mod_ref, lnw_ref, lnb_ref, wur_ref, wua_ref, wo_ref, n2_ref, wr_ref, br_ref, ones_ref,
                 x1_o, h2_o, idx_o, gate_o, *, ctx_blocks):
    ones = ones_ref[...]
    is_ctx = pl.program_id(0) < ctx_blocks
    y_a = jnp.where(is_ctx, yac_ref[...], yal_ref[...])
    x = jnp.where(is_ctx, xc_ref[...], xl_ref[...])
    y = jnp.where(is_ctx, yfc_ref[...] + ybc_ref[...], yfl_ref[...] + ybl_ref[...])
    mu = _head_sum(y, ones) * (1.0 / HEAD)
    yc = y - mu
    var = _head_sum(yc * yc, ones) * (1.0 / HEAD)
    yn = yc * lax.rsqrt(var + GN_EPS) * lnw_ref[...] + lnb_ref[...] + bonus_ref[...]
    y_r = (yn * g_ref[...]).astype(BF16)
    merged = (_sigmoid(gr_ref[...]) * _dot(y_r, wur_ref[...])
              + _sigmoid(ga_ref[...]) * _dot(y_a, wua_ref[...]))
    mix = _dot(merged.astype(BF16), wo_ref[...])
    mod = mod_ref[0]
    g1 = mod[:, 2 * D_MODEL:3 * D_MODEL]
    sh2 = mod[:, 3 * D_MODEL:4 * D_MODEL]
    sc2 = mod[:, 4 * D_MODEL:5 * D_MODEL]
    x1 = x + g1 * mix
    x1_o[...] = x1
    ms = jnp.mean(x1 * x1, axis=-1, keepdims=True)
    h2 = x1 * lax.rsqrt(ms + NORM_EPS) * n2_ref[...] * (1.0 + sc2) + sh2
    _store_token_tiles(h2_o, h2)
    hh, hl = _split2(h2)
    wh, wl = _split2(wr_ref[...])
    logits = _dot(hh, wh) + _dot(hh, wl) + _dot(hl, wh) + br_ref[...]
    lane_i = lax.broadcasted_iota(jnp.int32, logits.shape, 1)
    lane = lane_i.astype(F32)
    neg = jnp.float32(-jnp.inf)
    cur = jnp.where(lane_i < N_EXPERTS, logits, neg)
    vals, idxs = [], []
    for _ in range(TOP_K):
        mx = jnp.max(cur, axis=-1, keepdims=True)
        ix = jnp.min(jnp.where(cur == mx, lane, float(LANES)), axis=-1, keepdims=True)
        vals.append(mx)
        idxs.append(ix)
        cur = jnp.where(lane == ix, neg, cur)
    es = [jnp.exp(val - vals[0]) for val in vals]
    den = es[0] + es[1] + es[2] + es[3]
    idx_out = jnp.zeros(logits.shape, jnp.int32)
    gate_out = jnp.zeros(logits.shape, F32)
    for j in range(TOP_K):
        idx_out = jnp.where(lane_i == j, idxs[j].astype(jnp.int32), idx_out)
        gate_out = jnp.where(lane_i == j, es[j] / den, gate_out)
    idx_o[...] = idx_out
    gate_o[...] = gate_out


def _post(y_f, y_b, bonus, g, ya_ctx, ya_lat, gate_r, gate_a, x_ctx, x_lat, mods3, mod_row, p):
    n = x_ctx.shape[0] + x_lat.shape[0]
    nbc = ya_ctx.shape[0] // TM
    row = lambda i: (i, 0)
    full = lambda *shape: pl.BlockSpec(shape, lambda i: (0,) * len(shape))
    tok = lambda width: pl.BlockSpec((TM, width), row)
    return pl.pallas_call(
        functools.partial(_post_kernel, ctx_blocks=nbc),
        out_shape=[jax.ShapeDtypeStruct((n, D_MODEL), F32),
                   jax.ShapeDtypeStruct((n * TILE_ROWS, LANES), F32),
                   jax.ShapeDtypeStruct((n, LANES), jnp.int32),
                   jax.ShapeDtypeStruct((n, LANES), F32)],
        grid=(n // TM,),
        in_specs=_token_specs(nbc, D_R) + _token_specs(nbc, D_R) + [tok(D_R), tok(D_R)]
                 + _token_specs(nbc, D_A) + [tok(D_MODEL), tok(D_MODEL)] + _token_specs(nbc, D_MODEL) + [
                  pl.BlockSpec((1, 1, 6 * D_MODEL), lambda i: (mod_row(i), 0, 0)),
                  full(1, D_R), full(1, D_R), full(D_R, D_MODEL), full(D_A, D_MODEL),
                  full(D_MODEL, D_MODEL), full(1, D_MODEL), full(D_MODEL, LANES), full(1, LANES),
                  full(D_R, D_R)],
        out_specs=[tok(D_MODEL), pl.BlockSpec((TM * TILE_ROWS, LANES), row), tok(LANES), tok(LANES)],
        compiler_params=_cparams("arbitrary"),
        name="mixer_post",
    )(*y_f, *y_b, bonus, g, ya_ctx, ya_lat, gate_r, gate_a, x_ctx, x_lat, mods3,
      p["lnw"], p["lnb"], p["wur"], p["wua"], p["wo"], p["n2"], p["wr"], p["br"], p["ones"])


def _moe_kernel(blk_e_ref, blk_first_ref, next_e_ref, used_ref,
                src0_ref, src_next_ref, dst_prev_ref, h_hbm, w1_hbm, b1_ref, w2_hbm, b2_ref,
                out_hbm, w1f, w2f, w1b, w2b, xs, ys, sem_g, sem_s, sem_t, sem_w, *, n_rows):
    b = pl.program_id(0)
    used = used_ref[0]
    R = MOE_ROWS
    T = TILE_ROWS

    def tile(ref, i):
        return ref.at[pl.ds(pl.multiple_of(i * T, T), T)]

    def gather(table_ref, buf):
        for i in range(R):
            pltpu.make_async_copy(tile(h_hbm, table_ref[0, 0, i]), tile(xs.at[buf], i),
                                  sem_g.at[buf]).start(priority=i % 2)

    def scatter_prev(buf):
        for i in range(R):
            pltpu.make_async_copy(tile(ys.at[buf], i), tile(out_hbm, dst_prev_ref[0, 0, i]),
                                  sem_s.at[buf]).start(priority=i % 2)

    def fetch_weights(e):
        pltpu.make_async_copy(w1_hbm.at[e], w1f, sem_w.at[0]).start()
        pltpu.make_async_copy(w2_hbm.at[e], w2f, sem_w.at[1]).start()

    @pl.when(b == 0)
    def _():
        ys[...] = jnp.zeros_like(ys)
        for s in range(2):
            band = pltpu.make_async_copy(ys.at[s], out_hbm.at[pl.ds((n_rows + s * R) * T, R * T)], sem_t)
            band.start()
            band.wait()
        gather(src0_ref, 0)
        fetch_weights(blk_e_ref[0])

    @pl.when(jnp.logical_and(blk_first_ref[b] == 1, b < used))
    def _():
        pltpu.make_async_copy(w1_hbm.at[0], w1f, sem_w.at[0]).wait()
        pltpu.make_async_copy(w2_hbm.at[0], w2f, sem_w.at[1]).wait()
        w1b[...] = w1f[...].astype(BF16)
        w2b[...] = w2f[...].astype(BF16)

        @pl.when(next_e_ref[b] >= 0)
        def _():
            fetch_weights(next_e_ref[b])

    def step(p):
        @pl.when(b <= used)
        def _():
            pltpu.make_async_copy(h_hbm.at[pl.ds(0, R * T)], xs.at[p], sem_g.at[p]).wait()

        @pl.when(jnp.logical_and(b >= 1, b <= used + 1))
        def _():
            pltpu.make_async_copy(ys.at[p], out_hbm.at[pl.ds(0, R * T)], sem_s.at[p]).wait()

        @pl.when(b < used)
        def _():
            gather(src_next_ref, 1 - p)
            scatter_prev(1 - p)
            hb = _dot(_load_token_tiles(xs.at[p]).astype(BF16), w1b[...]) + b1_ref[0]
            glu = jnp.minimum(hb[:, 0:D_FF], SWIGLU_LIMIT)
            lin = jnp.clip(hb[:, D_FF:2 * D_FF], -SWIGLU_LIMIT, SWIGLU_LIMIT)
            act = glu * _sigmoid(SWIGLU_ALPHA * glu) * (lin + 1.0)
            _store_token_tiles(ys.at[p], _dot(act.astype(BF16), w2b[...]) + b2_ref[0])

        @pl.when(b == used)
        def _():
            scatter_prev(1 - p)

    for p in range(2):
        pl.when(b % 2 == p)(functools.partial(step, p))


def _moe(h2, src, dst_prev, blk_e, blk_first, next_e, used, w1, b1, w2, b2):
    d = D_MODEL
    n = h2.shape[0] // TILE_ROWS
    nb = src.shape[0]
    steps = nb + 2
    cur = lambda b, *_: (jnp.minimum(b, nb), 0, 0)
    nxt = lambda b, *_: (jnp.minimum(b + 1, nb - 1), 0, 0)
    table = lambda index_map: pl.BlockSpec((1, 1, MOE_ROWS), index_map, memory_space=pltpu.SMEM)
    expert = lambda b, blk_e, *_: (blk_e[jnp.minimum(b, nb - 1)], 0, 0)
    grid_spec = pltpu.PrefetchScalarGridSpec(
        num_scalar_prefetch=4,
        grid=(steps,),
        in_specs=[table(lambda b, *_: (0, 0, 0)), table(nxt), table(cur),
                  pl.BlockSpec(memory_space=pl.ANY),
                  pl.BlockSpec(memory_space=pl.ANY),
                  pl.BlockSpec((1, 1, 2 * D_FF), expert),
                  pl.BlockSpec(memory_space=pl.ANY),
                  pl.BlockSpec((1, 1, d), expert)],
        out_specs=pl.BlockSpec(memory_space=pl.ANY),
        scratch_shapes=[pltpu.VMEM((d, 2 * D_FF), F32),
                        pltpu.VMEM((D_FF, d), F32),
                        pltpu.VMEM((d, 2 * D_FF), BF16),
                        pltpu.VMEM((D_FF, d), BF16),
                        pltpu.VMEM((2, MOE_ROWS * TILE_ROWS, LANES), F32),
                        pltpu.VMEM((2, MOE_ROWS * TILE_ROWS, LANES), F32),
                        pltpu.SemaphoreType.DMA((2,)),
                        pltpu.SemaphoreType.DMA((2,)),
                        pltpu.SemaphoreType.DMA(()),
                        pltpu.SemaphoreType.DMA((2,))])
    tab3 = lambda t: t.reshape(-1, 1, MOE_ROWS)
    pad = lambda t: jnp.concatenate([t, jnp.zeros((steps - nb,), jnp.int32)])
    return pl.pallas_call(
        functools.partial(_moe_kernel, n_rows=n * TOP_K),
        out_shape=jax.ShapeDtypeStruct(((n * TOP_K + 2 * MOE_ROWS) * TILE_ROWS, LANES), F32),
        grid_spec=grid_spec,
        compiler_params=pltpu.CompilerParams(dimension_semantics=("arbitrary",),
                                             vmem_limit_bytes=VMEM_LIMIT, has_side_effects=True),
        name="moe_experts",
    )(blk_e, pad(blk_first), next_e, used, tab3(src), tab3(src), tab3(dst_prev), h2, w1,
      b1.reshape(N_EXPERTS, 1, 2 * D_FF), w2, b2.reshape(N_EXPERTS, 1, d))


def _route(top_idx, n):
    nk = n * TOP_K
    cap = nk + N_EXPERTS * MOE_ROWS
    nb = cap // MOE_ROWS
    flat_e = top_idx.reshape(nk)
    order = jnp.argsort(flat_e, stable=True).astype(jnp.int32)
    counts = jnp.sum((flat_e[:, None] == jnp.arange(N_EXPERTS, dtype=jnp.int32)[None, :])
                     .astype(jnp.int32), axis=0)
    start = jnp.cumsum(counts) - counts
    n_blk = (counts + MOE_ROWS - 1) // MOE_ROWS
    blk_end = jnp.cumsum(n_blk)
    first_blk = blk_end - n_blk
    experts = jnp.arange(N_EXPERTS, dtype=jnp.int32)
    blk = jnp.arange(nb, dtype=jnp.int32)
    blk_e = jnp.minimum(jnp.sum((blk[:, None] >= blk_end[None, :]).astype(jnp.int32), axis=1),
                        N_EXPERTS - 1)
    onehot = (blk_e[:, None] == experts[None, :]).astype(jnp.int32)
    pick = lambda t: jnp.sum(onehot * t[None, :], axis=1)
    blk_first_of_e, blk_count, blk_start = pick(first_blk), pick(counts), pick(start)
    blk_first = (blk == blk_first_of_e).astype(jnp.int32)
    blk_after = pick(blk_end)
    next_e = jnp.where(blk_after < blk_end[-1], blk_e[jnp.minimum(blk_after, nb - 1)], -1).astype(jnp.int32)
    row = jnp.arange(MOE_ROWS, dtype=jnp.int32)[None, :]
    within = (blk - blk_first_of_e)[:, None] * MOE_ROWS + row
    valid = jnp.logical_and(within < blk_count[:, None], (blk < blk_end[-1])[:, None])
    j = order[jnp.clip(blk_start[:, None] + within, 0, nk - 1)]
    src = jnp.where(valid, j // TOP_K, 0).astype(jnp.int32)
    scratch_row = nk + (blk % 2)[:, None] * MOE_ROWS + row
    dst = jnp.where(valid, (j % TOP_K) * n + j // TOP_K, scratch_row).astype(jnp.int32)
    dst_prev = jnp.concatenate([nk + MOE_ROWS + row, dst], axis=0)
    return (src, dst_prev, blk_e.astype(jnp.int32), blk_first, next_e,
            blk_end[-1:].astype(jnp.int32))


def _final_kernel(x1_ref, y0_ref, y1_ref, y2_ref, y3_ref, gate_ref, mod_ref, nf_ref, oc_ref, ol_ref,
                  *, ctx_blocks):
    gates = gate_ref[...]
    ffn = gates[:, 0:1] * _load_token_tiles(y0_ref)
    for j, ref in enumerate((y1_ref, y2_ref, y3_ref), start=1):
        ffn = ffn + gates[:, j:j + 1] * _load_token_tiles(ref)
    g2 = mod_ref[0][:, 5 * D_MODEL:6 * D_MODEL]
    x2 = x1_ref[...] + g2 * ffn
    ms = jnp.mean(x2 * x2, axis=-1, keepdims=True)
    y = x2 * lax.rsqrt(ms + NORM_EPS) * nf_ref[...]
    i = pl.program_id(0)

    @pl.when(i < ctx_blocks)
    def _():
        oc_ref[...] = y

    @pl.when(i >= ctx_blocks)
    def _():
        ol_ref[...] = y


def _final(x1, ys, gates, mods3, mod_row, norm_f, nc):
    n, d = x1.shape
    nb = n // TM
    nbc = nc // TM
    row = lambda i: (i, 0)
    return pl.pallas_call(
        functools.partial(_final_kernel, ctx_blocks=nbc),
        out_shape=[jax.ShapeDtypeStruct((nc, d), F32), jax.ShapeDtypeStruct((n - nc, d), F32)],
        grid=(nb,),
        in_specs=[pl.BlockSpec((TM, d), row)]
                 + [pl.BlockSpec((TM * TILE_ROWS, LANES), functools.partial(lambda j, i: (j * nb + i, 0), j))
                    for j in range(TOP_K)]
                 + [pl.BlockSpec((TM, LANES), row),
                    pl.BlockSpec((1, 1, 6 * d), lambda i: (mod_row(i), 0, 0)),
                    pl.BlockSpec((1, d), lambda i: (0, 0))],
        out_specs=_token_specs(nbc, d),
        compiler_params=_cparams("arbitrary"),
        name="combine_final",
    )(x1, ys, ys, ys, ys, gates, mods3, norm_f.reshape(1, d))


def _rope_tables(t_ctx, t_lat):
    rows = np.arange(t_lat) // GRID_W
    cols = np.arange(t_lat) % GRID_W
    inv = ROPE_THETA ** (-np.arange(0, AXIS_DIM, 2, dtype=np.float32) / AXIS_DIM)
    inv = jnp.asarray(inv, F32)
    ang_r = jnp.asarray(rows, F32)[:, None] * inv[None, :]
    ang_c = jnp.asarray(cols, F32)[:, None] * inv[None, :]
    cos = jnp.concatenate([jnp.cos(ang_r)] * 2 + [jnp.cos(ang_c)] * 2, axis=1)
    sin = jnp.concatenate([-jnp.sin(ang_r), jnp.sin(ang_r), -jnp.sin(ang_c), jnp.sin(ang_c)], axis=1)
    cos = jnp.concatenate([jnp.ones((t_ctx, HEAD), F32), cos], axis=0)
    sin = jnp.concatenate([jnp.zeros((t_ctx, HEAD), F32), sin], axis=0)
    return jnp.concatenate([cos, cos], axis=1), jnp.concatenate([sin, sin], axis=1)


def kernel(x_prompt, x_sample, cache_k, cache_v, state_rwkv_fwd, state_rwkv_bwd, c, c_ctx, w_ada, b_ada, norm1, norm2, w_in, rw_w0, rw_w2, rw_a0, rw_a2, rw_g2, rw_kk, rw_ka, rw_rk, rw_ln_w, rw_ln_b, q_norm, k_norm, w_up_r, w_up_a, w_out, w_router, b_router, w_moe_in, b_moe_in, w_moe_out, b_moe_out, norm_f):
    depth = w_in.shape[0]
    assert depth == 1, "single trunk layer"
    bc, tc, d = x_prompt.shape
    bl, tl, _ = x_sample.shape
    nc, nl = bc * tc, bl * tl
    n = nc + nl
    assert d == D_MODEL and tc == TM and tl % TM == 0 and nc % tl == 0
    l = 0

    x_ctx, x_lat = x_prompt.reshape(nc, d), x_sample.reshape(nl, d)
    cond =jnp.concatenate([c_ctx[None, :], c, jnp.zeros((16 - 1 - bl, d), F32)], axis=0)
    mods3 = _adaln(cond, w_ada[l], b_ada[l]).reshape(16, 1, 6 * d)
    nbc = nc // TM
    per_seq = tl // TM
    mod_row = lambda i: jnp.where(i < nbc, 0, 1 + (i - nbc) // per_seq)
    tab_row = lambda i: jnp.where(i < nbc, 0, 1 + (i - nbc) % per_seq)

    z = dict(zip([name for name, _ in IN_SPLITS],
                 _proj_in(x_ctx, x_lat, mods3, norm1[l], w_in[l].astype(BF16), mod_row)))

    ones_bd = jnp.asarray(np.kron(np.eye(H_R), np.ones((HEAD, HEAD))), BF16)
    pad_lo = lambda w: jnp.concatenate([w, jnp.zeros_like(w)], axis=0)
    pad_hi = lambda w: jnp.concatenate([jnp.zeros_like(w), w], axis=0)
    prep_p = dict(
        w0=rw_w0[l], a0=rw_a0[l],
        w2=jnp.stack([pad_lo(rw_w2[l, 0]), pad_hi(rw_w2[l, 1])]).astype(BF16),
        a2=jnp.stack([pad_lo(rw_a2[l, 0]), pad_hi(rw_a2[l, 1])]).astype(BF16),
        g2=rw_g2[l].astype(BF16),
        kkw=rw_kk[l].reshape(1, D_R), kaw=rw_ka[l].reshape(1, D_R), rk=rw_rk[l].reshape(1, D_R),
        qn=jnp.tile(q_norm[l], N_HEADS).reshape(1, D_A), kn=jnp.tile(k_norm[l], N_KV).reshape(1, D_KV),
        ones=ones_bd,
        tri=jnp.asarray(np.stack([np.kron(np.eye(TM // CHUNK), np.tril(np.ones((CHUNK, CHUNK)))),
                                  np.kron(np.eye(TM // CHUNK), np.triu(np.ones((CHUNK, CHUNK))))]), BF16),
        chunk_ones=jnp.asarray(np.kron(np.eye(TM // CHUNK), np.ones((CHUNK, CHUNK))), BF16))
    cos_tab, sin_tab = _rope_tables(tc, tl)
    sc, gt, vb, bonus, g, qr, kn, kr = _prep(z, cos_tab, sin_tab, tab_row, prep_p)

    yf_c, yb_c, s_fin = _rwkv_scan(sc, gt, vb, jnp.zeros((2, bc, H_R, HEAD, HEAD), F32), 0, bc, tc)
    yf_l, yb_l, _ = _rwkv_scan(sc, gt, vb, jnp.stack([state_rwkv_fwd[:, l], state_rwkv_bwd[:, l]]),
                               nc, bl, tl)
    y_f = (yf_c.reshape(nc, D_R), yf_l.reshape(nl, D_R))
    y_b = (yb_c.reshape(nc, D_R), yb_l.reshape(nl, D_R))

    ya_ctx = _attention(qr, kr, z["va"], 0, bc, tc)
    ya_lat = _attention(qr, kr, z["va"], nc, bl, tl,
                        cache_k[:, l].reshape(bl, -1, D_KV), cache_v[:, l].reshape(bl, -1, D_KV))

    post_p = dict(
        lnw=rw_ln_w[l].reshape(1, D_R), lnb=rw_ln_b[l].reshape(1, D_R),
        wur=w_up_r[l].astype(BF16), wua=w_up_a[l].astype(BF16), wo=w_out[l].astype(BF16),
        n2=norm2[l].reshape(1, d),
        wr=jnp.pad(w_router[l], ((0, 0), (0, LANES - N_EXPERTS))),
        br=jnp.pad(b_router[l], (0, LANES - N_EXPERTS)).reshape(1, LANES),
        ones=ones_bd)
    x1, h2, top_idx, gates = _post(y_f, y_b, bonus, g, ya_ctx, ya_lat, z["gate_r"], z["gate_a"],
                                   x_ctx, x_lat, mods3, mod_row, post_p)

    src, dst, blk_e, blk_first, next_e, used = _route(top_idx[:, :TOP_K], n)
    ys = _moe(h2, src, dst, blk_e, blk_first, next_e, used,
              w_moe_in[l], b_moe_in[l], w_moe_out[l], b_moe_out[l])
    y_ctx, y_lat = _final(x1, ys, gates, mods3, mod_row, norm_f, nc)

    y_prompt = y_ctx.reshape(bc, tc, d)
    y_sample = y_lat.reshape(bl, tl, d)
    new_cache_k = kn[:nc].reshape(bc, 1, tc, N_KV, HEAD)
    new_cache_v = z["va"][:nc].reshape(bc, 1, tc, N_KV, HEAD)
    new_state_fwd = s_fin[0][:, None]
    new_state_bwd = s_fin[1][:, None]
    return (y_prompt, y_sample, new_cache_k, new_cache_v, new_state_fwd, new_state_bwd)
```

```python
import functools

import numpy as np
import jax
import jax.numpy as jnp
from jax import lax
from jax.experimental import pallas as pl
from jax.experimental.pallas import tpu as pltpu

F32 = jnp.float32
BF16 = jnp.bfloat16

D_MODEL = 1024
GRID_W = 64
NORM_EPS = 1e-6
HEAD = 64
H_R = 8
D_R = H_R * HEAD
DECAY_RANK = 64
AAA_RANK = 64
GATE_RANK = 128
GN_EPS = 64e-5
N_HEADS = 8
N_KV = 2
D_A = N_HEADS * HEAD
D_KV = N_KV * HEAD
AXIS_DIM = HEAD // 2
ROPE_THETA = 10000.0
N_EXPERTS = 32
TOP_K = 4
D_FF = D_MODEL
SWIGLU_ALPHA = 1.702
SWIGLU_LIMIT = 7.0
LORA_COLS = 2 * DECAY_RANK + 2 * AAA_RANK + GATE_RANK
IN_SPLITS = (("r", D_R), ("k", D_R), ("v", D_R), ("lora", LORA_COLS), ("q", D_A),
             ("ka", D_KV), ("va", D_KV), ("gate_r", D_MODEL), ("gate_a", D_MODEL))

LANES = 128
TM = 256
TM_PROJ = 512
CHUNK = 64
SCAN_OPS = 6
SCAN_GROUP = 2
MOE_ROWS = 512
VMEM_LIMIT = 56 * 1024 * 1024


def _cparams(*sem):
    return pltpu.CompilerParams(dimension_semantics=sem, vmem_limit_bytes=VMEM_LIMIT)


def _dot(a, b):
    return jnp.dot(a, b, preferred_element_type=F32)


def _dot_nt(a, b):
    return lax.dot_general(a, b, (((1,), (1,)), ((), ())), preferred_element_type=F32)


def _dot_tn(a, b):
    return lax.dot_general(a, b, (((0,), (0,)), ((), ())), preferred_element_type=F32)


def _split2(x):
    hi = x.astype(BF16)
    lo = (x - hi.astype(F32)).astype(BF16)
    return hi, lo


def _split3(x):
    x1 = x.astype(BF16)
    r1 = x - x1.astype(F32)
    x2 = r1.astype(BF16)
    x3 = (r1 - x2.astype(F32)).astype(BF16)
    return x1, x2, x3


def _head_sum(x, ones_bd):
    hi, lo = _split2(x)
    return _dot(hi, ones_bd) + _dot(lo, ones_bd)


def _sigmoid(x):
    return 1.0 / (1.0 + jnp.exp(-x))


TILE_ROWS = D_MODEL // LANES


def _store_token_tiles(ref, x):
    rows = x.shape[0]
    for j in range(TILE_ROWS):
        ref[pl.ds(j, rows, stride=TILE_ROWS), :] = x[:, j * LANES:(j + 1) * LANES]


def _load_token_tiles(ref):
    rows = ref.shape[0] // TILE_ROWS
    return jnp.concatenate([ref[pl.ds(j, rows, stride=TILE_ROWS), :] for j in range(TILE_ROWS)], axis=1)


def _adaln_kernel(c_ref, w_ref, b_ref, o_ref):
    c = c_ref[...]
    s = (c * _sigmoid(c)).astype(BF16)
    o_ref[...] = _dot(s, w_ref[...].astype(BF16)) + b_ref[...]


def _adaln(cond, w_ada, b_ada):
    rows, d = cond.shape
    cols = w_ada.shape[1]
    tn = 512
    return pl.pallas_call(
        _adaln_kernel,
        out_shape=jax.ShapeDtypeStruct((rows, cols), F32),
        grid=(cols // tn,),
        in_specs=[pl.BlockSpec((rows, d), lambda j: (0, 0)),
                  pl.BlockSpec((d, tn), lambda j: (0, j)),
                  pl.BlockSpec((1, tn), lambda j: (0, j))],
        out_specs=pl.BlockSpec((rows, tn), lambda j: (0, j)),
        compiler_params=_cparams("arbitrary"),
        name="adaln",
    )(cond, w_ada, b_ada.reshape(1, cols))


def _token_specs(nbc, width, tm=TM):
    return [pl.BlockSpec((tm, width), lambda i: (jnp.minimum(i, nbc - 1), 0)),
            pl.BlockSpec((tm, width), lambda i: (jnp.maximum(i - nbc, 0), 0))]


def _proj_in_kernel(xc_ref, xl_ref, mod_ref, n1_ref, w_ref, *out_refs, ctx_blocks):
    x = jnp.where(pl.program_id(0) < ctx_blocks, xc_ref[...], xl_ref[...])
    ms = jnp.mean(x * x, axis=-1, keepdims=True)
    hn = x * lax.rsqrt(ms + NORM_EPS) * n1_ref[...]
    mod = mod_ref[0]
    sh1 = mod[:, 0:D_MODEL]
    sc1 = mod[:, D_MODEL:2 * D_MODEL]
    h = (hn * (1.0 + sc1) + sh1).astype(BF16)
    c0 = 0
    for ref, (_, width) in zip(out_refs, IN_SPLITS):
        ref[...] = _dot(h, w_ref[:, c0:c0 + width])
        c0 += width


def _proj_in(x_ctx, x_lat, mods3, norm1, w_in_bf, t_lat):
    d = x_ctx.shape[1]
    n = x_ctx.shape[0] + x_lat.shape[0]
    tm = TM_PROJ
    nbc = x_ctx.shape[0] // tm
    per_seq = t_lat // tm
    cols = w_in_bf.shape[1]
    row = lambda i: (i, 0)
    mod_row = lambda i: jnp.where(i < nbc, 0, 1 + (i - nbc) // per_seq)
    return pl.pallas_call(
        functools.partial(_proj_in_kernel, ctx_blocks=nbc),
        out_shape=[jax.ShapeDtypeStruct((n, width), F32) for _, width in IN_SPLITS],
        grid=(n // tm,),
        in_specs=_token_specs(nbc, d, tm) + [
                  pl.BlockSpec((1, 1, 6 * d), lambda i: (mod_row(i), 0, 0)),
                  pl.BlockSpec((1, d), lambda i: (0, 0)),
                  pl.BlockSpec((d, cols), lambda i: (0, 0))],
        out_specs=[pl.BlockSpec((tm, width), row) for _, width in IN_SPLITS],
        compiler_params=_cparams("arbitrary"),
        name="proj_in",
    )(x_ctx, x_lat, mods3, norm1.reshape(1, d), w_in_bf)


def _prep_kernel(r_ref, k_ref, v_ref, lora_ref, q_ref, ka_ref, cos_ref, sin_ref,
                 w0_ref, w2_ref, a0_ref, a2_ref, g2_ref, kkw_ref, kaw_ref, rk_ref,
                 qn_ref, kn_ref, ones_ref, tri_ref, chunk_ones_ref,
                 sc_o, gt_o, vb_o, bonus_o, g_o, qr_o, kn_o, kr_o):
    ones = ones_ref[...]
    chunk_ones = chunk_ones_ref[...]
    r = r_ref[...]
    k = k_ref[...]
    v = v_ref[...]
    vb_o[...] = v.astype(BF16)
    lora = lora_ref[...]
    kk = k * kkw_ref[...]
    kk = kk * lax.rsqrt(_head_sum(kk * kk, ones) + 1e-12)
    th = jnp.tanh(lora[:, 0:LANES]).astype(BF16)
    al = lora[:, LANES:2 * LANES].astype(BF16)
    kd_sum = jnp.zeros_like(k)
    for d in range(2):
        u = w0_ref[d:d + 1, :] + _dot(th, w2_ref[d])
        w_log = -(jnp.maximum(-u, 0.0) + jnp.log1p(jnp.exp(-jnp.abs(u)))) - 0.5
        lw = -jnp.exp(w_log)
        a = _sigmoid(a0_ref[d:d + 1, :] + _dot(al, a2_ref[d]))
        kd = k * (1.0 + (a - 1.0) * kaw_ref[...])
        kd_sum = kd_sum + kd
        b = kk * a
        parts = _split3(lw)
        tri = tri_ref[d]
        L = _dot(tri, parts[0]) + _dot(tri, parts[1]) + _dot(tri, parts[2])
        Ltot = _dot(chunk_ones, parts[0]) + _dot(chunk_ones, parts[1]) + _dot(chunk_ones, parts[2])
        e_inv = jnp.exp(-L)
        e_rest = jnp.exp(Ltot - L)
        scan_ops = (kk * jnp.exp(L - lw), r * jnp.exp(L), b * e_inv, kd * e_inv, b * e_rest, kd * e_rest)
        for j, op in enumerate(scan_ops):
            sc_o[d, :, j * D_R:(j + 1) * D_R] = op.astype(BF16)
        g_tot = jnp.exp(Ltot)
        for j in range(TM // CHUNK):
            gt_o[d, j] = g_tot[j * CHUNK:j * CHUNK + 1, :]
    bonus_o[...] = _head_sum(r * kd_sum * rk_ref[...], ones) * v
    g_o[...] = _dot(_sigmoid(lora[:, 2 * LANES:3 * LANES]).astype(BF16), g2_ref[...])
    q = q_ref[...]
    qn = q * lax.rsqrt(_head_sum(q * q, ones) * (1.0 / HEAD) + NORM_EPS) * qn_ref[...]
    ka = ka_ref[...]
    ones_kv = ones[0:D_KV, 0:D_KV]
    kn = ka * lax.rsqrt(_head_sum(ka * ka, ones_kv) * (1.0 / HEAD) + NORM_EPS) * kn_ref[...]
    kn_o[...] = kn
    cos = cos_ref[...]
    sin = sin_ref[...]
    half = AXIS_DIM // 2

    def rope(t, cos_t, sin_t):
        width = t.shape[1]
        lane = lax.broadcasted_iota(jnp.int32, t.shape, 1)
        first = (lane % AXIS_DIM) < half
        swapped = jnp.where(first, pltpu.roll(t, width - half, 1), pltpu.roll(t, half, 1))
        return t * cos_t + swapped * sin_t

    cos4 = jnp.concatenate([cos] * (D_A // D_KV), axis=1)
    sin4 = jnp.concatenate([sin] * (D_A // D_KV), axis=1)
    qr_o[...] = (rope(qn, cos4, sin4) * (HEAD ** -0.5)).astype(BF16)
    kr_o[...] = rope(kn, cos, sin).astype(BF16)


def _prep(z, cos_tab, sin_tab, tab_row, p):
    n = z["r"].shape[0]
    row = lambda i: (i, 0)
    full = lambda *shape: pl.BlockSpec(shape, lambda i: (0,) * len(shape))
    tok = lambda width: pl.BlockSpec((TM, width), row)
    tab = pl.BlockSpec((TM, D_KV), lambda i: (tab_row(i), 0))
    cpb = TM // CHUNK
    return pl.pallas_call(
        _prep_kernel,
        out_shape=[jax.ShapeDtypeStruct((2, n, SCAN_OPS * D_R), BF16),
                   jax.ShapeDtypeStruct((2, n // CHUNK, 1, D_R), F32),
                   jax.ShapeDtypeStruct((n, D_R), BF16),
                   jax.ShapeDtypeStruct((n, D_R), F32),
                   jax.ShapeDtypeStruct((n, D_R), F32),
                   jax.ShapeDtypeStruct((n, D_A), BF16),
                   jax.ShapeDtypeStruct((n, D_KV), F32),
                   jax.ShapeDtypeStruct((n, D_KV), BF16)],
        grid=(n // TM,),
        in_specs=[tok(D_R), tok(D_R), tok(D_R), tok(LORA_COLS), tok(D_A), tok(D_KV), tab, tab,
                  full(2, D_R), full(2, LANES, D_R), full(2, D_R), full(2, LANES, D_R),
                  full(GATE_RANK, D_R), full(1, D_R), full(1, D_R), full(1, D_R),
                  full(1, D_A), full(1, D_KV), full(D_R, D_R), full(2, TM, TM), full(TM, TM)],
        out_specs=[pl.BlockSpec((2, TM, SCAN_OPS * D_R), lambda i: (0, i, 0)),
                   pl.BlockSpec((2, cpb, 1, D_R), lambda i: (0, i, 0, 0)),
                   tok(D_R), tok(D_R), tok(D_R), tok(D_A), tok(D_KV), tok(D_KV)],
        compiler_params=_cparams("arbitrary"),
        name="mixer_prep",
    )(z["r"], z["k"], z["v"], z["lora"], z["q"], z["ka"], cos_tab, sin_tab,
      p["w0"], p["w2"], p["a0"], p["a2"], p["g2"], p["kkw"], p["kaw"], p["rk"],
      p["qn"], p["kn"], p["ones"], p["tri"], p["chunk_ones"])


def _scan_kernel(scf_ref, scb_ref, vf_ref, vb_ref, gf_ref, gb_ref, s0_ref,
                 yf_ref, yb_ref, sfin_ref, state):
    c = pl.program_id(1)
    C = CHUNK
    pairs = D_R // LANES
    chains = [(g, d, p) for g in range(SCAN_GROUP) for d in range(2) for p in range(pairs)]

    lane = lax.broadcasted_iota(jnp.int32, (1, LANES), 1)
    low = lane < HEAD

    @pl.when(c == 0)
    def _():
        zero = jnp.zeros((HEAD, HEAD), F32)
        for i, (g, d, p) in enumerate(chains):
            top = jnp.concatenate([s0_ref[d, g, 2 * p], zero], axis=1)
            bot = jnp.concatenate([zero, s0_ref[d, g, 2 * p + 1]], axis=1)
            state[i] = jnp.concatenate([top, bot], axis=0)

    W = 2 * C
    t2 = lax.broadcasted_iota(jnp.int32, (C, W), 0)
    s2 = lax.broadcasted_iota(jnp.int32, (C, W), 1) % C
    strict = (s2 < t2, s2 > t2)
    incl = (s2 <= t2, s2 >= t2)
    eye2 = jnp.where(s2 == t2, 1.0, 0.0)
    first_mat = lax.broadcasted_iota(jnp.int32, (1, W), 1) < C
    r2 = lax.broadcasted_iota(jnp.int32, (LANES, LANES), 0)
    c2 = lax.broadcasted_iota(jnp.int32, (LANES, LANES), 1)
    same_head = (r2 // HEAD) == (c2 // HEAD)

    sc_refs = (scf_ref, scb_ref)
    v_refs = (vf_ref, vb_ref)
    g_refs = (gf_ref, gb_ref)
    y_refs = (yf_ref, yb_ref)

    def operand(chain, j):
        g, d, p = chain
        c0 = j * D_R + p * LANES
        return sc_refs[d][0, g, :, c0:c0 + LANES]

    def stack(x):
        first = low if x.shape[1] == LANES else first_mat
        zero = jnp.zeros_like(x)
        return jnp.concatenate([jnp.where(first, x, zero), jnp.where(first, zero, x)], axis=0)

    Qk = [operand(ch, 0) for ch in chains]
    Qr = [operand(ch, 1) for ch in chains]
    v_p = [v_refs[d][g, :, p * LANES:(p + 1) * LANES] for g, d, p in chains]
    v_st = [stack(x) for x in v_p]
    Z0 = [state[i] for i in range(len(chains))]
    Z0b = [z.astype(BF16) for z in Z0]
    m = [_dot_nt(jnp.concatenate([Qk[i], Qr[i]], axis=0),
                 jnp.concatenate([stack(operand(ch, 2)), stack(operand(ch, 3))], axis=0))
         for i, ch in enumerate(chains)]
    A = [jnp.where(strict[d], m[i][0:C, 0:W], 0.0) for i, (g, d, p) in enumerate(chains)]
    Bm = [jnp.where(strict[d], m[i][0:C, W:2 * W], 0.0).astype(BF16)
          for i, (g, d, p) in enumerate(chains)]
    Pb = [jnp.where(incl[d], m[i][C:2 * C, 0:W], 0.0).astype(BF16)
          for i, (g, d, p) in enumerate(chains)]
    Pk = [jnp.where(incl[d], m[i][C:2 * C, W:2 * W], 0.0).astype(BF16)
          for i, (g, d, p) in enumerate(chains)]
    BV = [_dot(Bm[i], v_st[i]) for i in range(len(chains))]
    T = [eye2 - a for a in A]
    Ap = [_dot(a.astype(BF16), stack(a.astype(BF16))).astype(BF16) for a in A]
    n = 2
    while 2 * n < C:
        R = [_dot(jnp.concatenate([Ap[i], T[i].astype(BF16)], axis=0), stack(Ap[i]))
             for i in range(len(chains))]
        Ap = [x[0:C].astype(BF16) for x in R]
        T = [T[i] + R[i][C:2 * C] for i in range(len(chains))]
        n *= 2
    T = [T[i] + _dot(T[i].astype(BF16), stack(Ap[i])) for i in range(len(chains))]
    WU = [_dot(T[i].astype(BF16),
               jnp.concatenate([stack(Qk[i]), stack(BV[i].astype(BF16))], axis=1))
          for i in range(len(chains))]
    Wq = [x[:, 0:LANES].astype(BF16) for x in WU]
    Uv = [x[:, LANES:2 * LANES] for x in WU]
    PW = [_dot(Pb[i], jnp.concatenate([stack(Wq[i]), stack(Uv[i].astype(BF16))], axis=1))
          for i in range(len(chains))]
    PkV = [_dot(Pk[i], v_st[i]) for i in range(len(chains))]
    U = [_dot_nt(Wq[i], Z0b[i]) + Uv[i] for i in range(len(chains))]
    Yq = [(Qr[i].astype(F32) - PW[i][:, 0:LANES]).astype(BF16) for i in range(len(chains))]
    for i, (g, d, p) in enumerate(chains):
        y_refs[d][g, :, p * LANES:(p + 1) * LANES] = (_dot_nt(Yq[i], Z0b[i]) + PkV[i]
                                                      - PW[i][:, LANES:2 * LANES])
    for i, (g, d, p) in enumerate(chains):
        lhs = jnp.concatenate([v_p[i], (-U[i]).astype(BF16)], axis=0)
        rhs = jnp.concatenate([operand(chains[i], 5), operand(chains[i], 4)], axis=0)
        upd = _dot_tn(lhs, rhs)
        g_tot = g_refs[d][0, g, 0][:, p * LANES:(p + 1) * LANES]
        state[i] = Z0[i] * g_tot + jnp.where(same_head, upd, 0.0)

    @pl.when(c == pl.num_programs(1) - 1)
    def _():
        for i, (g, d, p) in enumerate(chains):
            z = state[i]
            sfin_ref[d, g, 2 * p] = z[0:HEAD, 0:HEAD]
            sfin_ref[d, g, 2 * p + 1] = z[HEAD:2 * HEAD, HEAD:2 * HEAD]


def _rwkv_scan(sc, gt, vb, s0, row0, batch, t_len):
    n = vb.shape[0]
    nch = t_len // CHUNK
    assert row0 % (t_len * SCAN_GROUP) == 0 and batch % SCAN_GROUP == 0
    seq0 = row0 // (t_len * SCAN_GROUP)
    width = SCAN_OPS * D_R
    sc4 = sc.reshape(2, n // t_len, t_len, width)
    gt5 = gt.reshape(2, n // t_len, nch, 1, D_R)
    v3 = vb.reshape(n // t_len, t_len, D_R)
    G = SCAN_GROUP
    fwd = lambda c: c
    bwd = lambda c: nch - 1 - c
    sc_spec = lambda d, at: pl.BlockSpec((1, G, CHUNK, width), lambda s, c: (d, seq0 + s, at(c), 0))
    v_spec = lambda at: pl.BlockSpec((G, CHUNK, D_R), lambda s, c: (seq0 + s, at(c), 0))
    gt_spec = lambda d, at: pl.BlockSpec((1, G, 1, 1, D_R), lambda s, c: (d, seq0 + s, at(c), 0, 0))
    y_spec = lambda at: pl.BlockSpec((G, CHUNK, D_R), lambda s, c: (s, at(c), 0))
    st = pl.BlockSpec((2, G, H_R, HEAD, HEAD), lambda s, c: (0, s, 0, 0, 0))
    return pl.pallas_call(
        _scan_kernel,
        out_shape=[jax.ShapeDtypeStruct((batch, t_len, D_R), F32),
                   jax.ShapeDtypeStruct((batch, t_len, D_R), F32),
                   jax.ShapeDtypeStruct((2, batch, H_R, HEAD, HEAD), F32)],
        grid=(batch // G, nch),
        in_specs=[sc_spec(0, fwd), sc_spec(1, bwd), v_spec(fwd), v_spec(bwd),
                  gt_spec(0, fwd), gt_spec(1, bwd), st],
        out_specs=[y_spec(fwd), y_spec(bwd), st],
        scratch_shapes=[pltpu.VMEM((G * 2 * D_R // LANES, LANES, LANES), F32)],
        compiler_params=_cparams("arbitrary", "arbitrary"),
        name="rwkv_scan",
    )(sc4, sc4, v3, v3, gt5, gt5, s0)


def _attn_kernel(*refs, tq, with_cache):
    if with_cache:
        q_ref, k_ref, v_ref, ck_ref, cv_ref, o_ref = refs
        kx = jnp.concatenate([k_ref[...], ck_ref[0].astype(BF16)], axis=0)
        vx = jnp.concatenate([v_ref[...], cv_ref[0]], axis=0).astype(BF16)
    else:
        q_ref, k_ref, v_ref, o_ref = refs
        kx = k_ref[...]
        vx = v_ref[...].astype(BF16)
    lane = lax.broadcasted_iota(jnp.int32, (1, D_KV), 1)
    low = lane < HEAD
    k_sw = pltpu.roll(kx.astype(F32), HEAD, 1).astype(BF16)
    v_sw = pltpu.roll(vx.astype(F32), HEAD, 1).astype(BF16)
    group = N_HEADS // N_KV
    for p in range(D_A // LANES):
        g = (2 * p) // group
        keep = low if g == 0 else jnp.logical_not(low)
        kd = jnp.where(keep, kx, k_sw)
        vd = jnp.where(keep, vx, v_sw)
        qp = q_ref[:, p * LANES:(p + 1) * LANES]
        zero = jnp.zeros_like(qp)
        qs = jnp.concatenate([jnp.where(low, qp, zero), jnp.where(low, zero, qp)], axis=0)
        s = _dot_nt(qs, kd)
        mx = jnp.max(s, axis=-1, keepdims=True)
        e = jnp.exp(s - mx)
        den = jnp.sum(e, axis=-1, keepdims=True)
        o = _dot(e.astype(BF16), vd) / den
        o_ref[:, p * LANES:(p + 1) * LANES] = jnp.where(low, o[0:tq], o[tq:2 * tq]).astype(BF16)


def _attention(qr, kr, va, row0, batch, t_len, cache_k=None, cache_v=None):
    tq = min(t_len, 256)
    nq = t_len // tq
    qblk0 = row0 // tq
    sblk0 = row0 // t_len
    with_cache = cache_k is not None
    in_specs = [pl.BlockSpec((tq, D_A), lambda b, i: (qblk0 + b * nq + i, 0)),
                pl.BlockSpec((t_len, D_KV), lambda b, i: (sblk0 + b, 0)),
                pl.BlockSpec((t_len, D_KV), lambda b, i: (sblk0 + b, 0))]
    args = [qr, kr, va]
    if with_cache:
        past = cache_k.shape[1]
        in_specs += [pl.BlockSpec((1, past, D_KV), lambda b, i: (b, 0, 0))] * 2
        args += [cache_k, cache_v]
    return pl.pallas_call(
        functools.partial(_attn_kernel, tq=tq, with_cache=with_cache),
        out_shape=jax.ShapeDtypeStruct((batch * t_len, D_A), BF16),
        grid=(batch, nq),
        in_specs=in_specs,
        out_specs=pl.BlockSpec((tq, D_A), lambda b, i: (b * nq + i, 0)),
        compiler_params=_cparams("arbitrary", "arbitrary"),
        name="attention_cache" if with_cache else "attention_ctx",
    )(*args)


def _post_kernel(yfc_ref, yfl_ref, ybc_ref, ybl_ref, bonus_ref, g_ref, yac_ref, yal_ref, gr_ref, ga_ref,
                 xc_ref, xl_ref,
                 mod_ref, lnw_ref, lnb_ref, wur_ref, wua_ref, wo_ref, n2_ref, wr_ref, br_ref, ones_ref,
                 x1_o, h2_o, idx_o, gate_o, *, ctx_blocks):
    ones = ones_ref[...]
    is_ctx = pl.program_id(0) < ctx_blocks
    y_a = jnp.where(is_ctx, yac_ref[...], yal_ref[...])
    x = jnp.where(is_ctx, xc_ref[...], xl_ref[...])
    y = jnp.where(is_ctx, yfc_ref[...] + ybc_ref[...], yfl_ref[...] + ybl_ref[...])
    mu = _head_sum(y, ones) * (1.0 / HEAD)
    yc = y - mu
    var = _head_sum(yc * yc, ones) * (1.0 / HEAD)
    yn = yc * lax.rsqrt(var + GN_EPS) * lnw_ref[...] + lnb_ref[...] + bonus_ref[...]
    y_r = (yn * g_ref[...]).astype(BF16)
    merged = (_sigmoid(gr_ref[...]) * _dot(y_r, wur_ref[...])
              + _sigmoid(ga_ref[...]) * _dot(y_a, wua_ref[...]))
    mix = _dot(merged.astype(BF16), wo_ref[...])
    mod = mod_ref[0]
    g1 = mod[:, 2 * D_MODEL:3 * D_MODEL]
    sh2 = mod[:, 3 * D_MODEL:4 * D_MODEL]
    sc2 = mod[:, 4 * D_MODEL:5 * D_MODEL]
    x1 = x + g1 * mix
    x1_o[...] = x1
    ms = jnp.mean(x1 * x1, axis=-1, keepdims=True)
    h2 = x1 * lax.rsqrt(ms + NORM_EPS) * n2_ref[...] * (1.0 + sc2) + sh2
    _store_token_tiles(h2_o, h2)
    hh, hl = _split2(h2)
    wh, wl = _split2(wr_ref[...])
    logits = _dot(hh, wh) + _dot(hh, wl) + _dot(hl, wh) + br_ref[...]
    lane_i = lax.broadcasted_iota(jnp.int32, logits.shape, 1)
    lane = lane_i.astype(F32)
    neg = jnp.float32(-jnp.inf)
    cur = jnp.where(lane_i < N_EXPERTS, logits, neg)
    vals, idxs = [], []
    for _ in range(TOP_K):
        mx = jnp.max(cur, axis=-1, keepdims=True)
        ix = jnp.min(jnp.where(cur == mx, lane, float(LANES)), axis=-1, keepdims=True)
        vals.append(mx)
        idxs.append(ix)
        cur = jnp.where(lane == ix, neg, cur)
    es = [jnp.exp(val - vals[0]) for val in vals]
    den = es[0] + es[1] + es[2] + es[3]
    idx_out = jnp.zeros(logits.shape, jnp.int32)
    gate_out = jnp.zeros(logits.shape, F32)
    for j in range(TOP_K):
        idx_out = jnp.where(lane_i == j, idxs[j].astype(jnp.int32), idx_out)
        gate_out = jnp.where(lane_i == j, es[j] / den, gate_out)
    idx_o[...] = idx_out
    gate_o[...] = gate_out


def _post(y_f, y_b, bonus, g, ya_ctx, ya_lat, gate_r, gate_a, x_ctx, x_lat, mods3, mod_row, p):
    n = x_ctx.shape[0] + x_lat.shape[0]
    nbc = ya_ctx.shape[0] // TM
    row = lambda i: (i, 0)
    full = lambda *shape: pl.BlockSpec(shape, lambda i: (0,) * len(shape))
    tok = lambda width: pl.BlockSpec((TM, width), row)
    return pl.pallas_call(
        functools.partial(_post_kernel, ctx_blocks=nbc),
        out_shape=[jax.ShapeDtypeStruct((n, D_MODEL), F32),
                   jax.ShapeDtypeStruct((n * TILE_ROWS, LANES), F32),
                   jax.ShapeDtypeStruct((n, LANES), jnp.int32),
                   jax.ShapeDtypeStruct((n, LANES), F32)],
        grid=(n // TM,),
        in_specs=_token_specs(nbc, D_R) + _token_specs(nbc, D_R) + [tok(D_R), tok(D_R)]
                 + _token_specs(nbc, D_A) + [tok(D_MODEL), tok(D_MODEL)] + _token_specs(nbc, D_MODEL) + [
                  pl.BlockSpec((1, 1, 6 * D_MODEL), lambda i: (mod_row(i), 0, 0)),
                  full(1, D_R), full(1, D_R), full(D_R, D_MODEL), full(D_A, D_MODEL),
                  full(D_MODEL, D_MODEL), full(1, D_MODEL), full(D_MODEL, LANES), full(1, LANES),
                  full(D_R, D_R)],
        out_specs=[tok(D_MODEL), pl.BlockSpec((TM * TILE_ROWS, LANES), row), tok(LANES), tok(LANES)],
        compiler_params=_cparams("arbitrary"),
        name="mixer_post",
    )(*y_f, *y_b, bonus, g, ya_ctx, ya_lat, gate_r, gate_a, x_ctx, x_lat, mods3,
      p["lnw"], p["lnb"], p["wur"], p["wua"], p["wo"], p["n2"], p["wr"], p["br"], p["ones"])


def _moe_kernel(blk_e_ref, blk_first_ref, next_e_ref, used_ref,
                src0_ref, src_next_ref, dst_prev_ref, h_hbm, w1_hbm, b1_ref, w2_hbm, b2_ref,
                out_hbm, w1f, w2f, w1b, w2b, xs, ys, sem_g, sem_s, sem_t, sem_w, *, n_rows):
    b = pl.program_id(0)
    used = used_ref[0]
    R = MOE_ROWS
    T = TILE_ROWS

    def tile(ref, i):
        return ref.at[pl.ds(pl.multiple_of(i * T, T), T)]

    def gather(table_ref, buf):
        for i in range(R):
            pltpu.make_async_copy(tile(h_hbm, table_ref[0, 0, i]), tile(xs.at[buf], i),
                                  sem_g.at[buf]).start(priority=i % 2)

    def scatter_prev(buf):
        for i in range(R):
            pltpu.make_async_copy(tile(ys.at[buf], i), tile(out_hbm, dst_prev_ref[0, 0, i]),
                                  sem_s.at[buf]).start(priority=i % 2)

    def fetch_weights(e):
        pltpu.make_async_copy(w1_hbm.at[e], w1f, sem_w.at[0]).start()
        pltpu.make_async_copy(w2_hbm.at[e], w2f, sem_w.at[1]).start()

    @pl.when(b == 0)
    def _():
        ys[...] = jnp.zeros_like(ys)
        for s in range(2):
            band = pltpu.make_async_copy(ys.at[s], out_hbm.at[pl.ds((n_rows + s * R) * T, R * T)], sem_t)
            band.start()
            band.wait()
        gather(src0_ref, 0)
        fetch_weights(blk_e_ref[0])

    @pl.when(jnp.logical_and(blk_first_ref[b] == 1, b < used))
    def _():
        pltpu.make_async_copy(w1_hbm.at[0], w1f, sem_w.at[0]).wait()
        pltpu.make_async_copy(w2_hbm.at[0], w2f, sem_w.at[1]).wait()
        w1b[...] = w1f[...].astype(BF16)
        w2b[...] = w2f[...].astype(BF16)

        @pl.when(next_e_ref[b] >= 0)
        def _():
            fetch_weights(next_e_ref[b])

    def step(p):
        @pl.when(b <= used)
        def _():
            pltpu.make_async_copy(h_hbm.at[pl.ds(0, R * T)], xs.at[p], sem_g.at[p]).wait()

        @pl.when(jnp.logical_and(b >= 1, b <= used + 1))
        def _():
            pltpu.make_async_copy(ys.at[p], out_hbm.at[pl.ds(0, R * T)], sem_s.at[p]).wait()

        @pl.when(b < used)
        def _():
            gather(src_next_ref, 1 - p)
            scatter_prev(1 - p)
            hb = _dot(_load_token_tiles(xs.at[p]).astype(BF16), w1b[...]) + b1_ref[0]
            glu = jnp.minimum(hb[:, 0:D_FF], SWIGLU_LIMIT)
            lin = jnp.clip(hb[:, D_FF:2 * D_FF], -SWIGLU_LIMIT, SWIGLU_LIMIT)
            act = glu * _sigmoid(SWIGLU_ALPHA * glu) * (lin + 1.0)
            _store_token_tiles(ys.at[p], _dot(act.astype(BF16), w2b[...]) + b2_ref[0])

        @pl.when(b == used)
        def _():
            scatter_prev(1 - p)

    for p in range(2):
        pl.when(b % 2 == p)(functools.partial(step, p))


def _moe(h2, src, dst_prev, blk_e, blk_first, next_e, used, w1, b1, w2, b2):
    d = D_MODEL
    n = h2.shape[0] // TILE_ROWS
    nb = src.shape[0]
    steps = nb + 2
    cur = lambda b, *_: (jnp.minimum(b, nb), 0, 0)
    nxt = lambda b, *_: (jnp.minimum(b + 1, nb - 1), 0, 0)
    table = lambda index_map: pl.BlockSpec((1, 1, MOE_ROWS), index_map, memory_space=pltpu.SMEM)
    expert = lambda b, blk_e, *_: (blk_e[jnp.minimum(b, nb - 1)], 0, 0)
    grid_spec = pltpu.PrefetchScalarGridSpec(
        num_scalar_prefetch=4,
        grid=(steps,),
        in_specs=[table(lambda b, *_: (0, 0, 0)), table(nxt), table(cur),
                  pl.BlockSpec(memory_space=pl.ANY),
                  pl.BlockSpec(memory_space=pl.ANY),
                  pl.BlockSpec((1, 1, 2 * D_FF), expert),
                  pl.BlockSpec(memory_space=pl.ANY),
                  pl.BlockSpec((1, 1, d), expert)],
        out_specs=pl.BlockSpec(memory_space=pl.ANY),
        scratch_shapes=[pltpu.VMEM((d, 2 * D_FF), F32),
                        pltpu.VMEM((D_FF, d), F32),
                        pltpu.VMEM((d, 2 * D_FF), BF16),
                        pltpu.VMEM((D_FF, d), BF16),
                        pltpu.VMEM((2, MOE_ROWS * TILE_ROWS, LANES), F32),
                        pltpu.VMEM((2, MOE_ROWS * TILE_ROWS, LANES), F32),
                        pltpu.SemaphoreType.DMA((2,)),
                        pltpu.SemaphoreType.DMA((2,)),
                        pltpu.SemaphoreType.DMA(()),
                        pltpu.SemaphoreType.DMA((2,))])
    tab3 = lambda t: t.reshape(-1, 1, MOE_ROWS)
    pad = lambda t: jnp.concatenate([t, jnp.zeros((steps - nb,), jnp.int32)])
    return pl.pallas_call(
        functools.partial(_moe_kernel, n_rows=n * TOP_K),
        out_shape=jax.ShapeDtypeStruct(((n * TOP_K + 2 * MOE_ROWS) * TILE_ROWS, LANES), F32),
        grid_spec=grid_spec,
        compiler_params=pltpu.CompilerParams(dimension_semantics=("arbitrary",),
                                             vmem_limit_bytes=VMEM_LIMIT, has_side_effects=True),
        name="moe_experts",
    )(blk_e, pad(blk_first), next_e, used, tab3(src), tab3(src), tab3(dst_prev), h2, w1,
      b1.reshape(N_EXPERTS, 1, 2 * D_FF), w2, b2.reshape(N_EXPERTS, 1, d))


def _route(top_idx, n):
    nk = n * TOP_K
    cap = nk + N_EXPERTS * MOE_ROWS
    nb = cap // MOE_ROWS
    flat_e = top_idx.reshape(nk)
    order = jnp.argsort(flat_e, stable=True).astype(jnp.int32)
    counts = jnp.sum((flat_e[:, None] == jnp.arange(N_EXPERTS, dtype=jnp.int32)[None, :])
                     .astype(jnp.int32), axis=0)
    start = jnp.cumsum(counts) - counts
    n_blk = (counts + MOE_ROWS - 1) // MOE_ROWS
    blk_end = jnp.cumsum(n_blk)
    first_blk = blk_end - n_blk
    experts = jnp.arange(N_EXPERTS, dtype=jnp.int32)
    blk = jnp.arange(nb, dtype=jnp.int32)
    blk_e = jnp.minimum(jnp.sum((blk[:, None] >= blk_end[None, :]).astype(jnp.int32), axis=1),
                        N_EXPERTS - 1)
    onehot = (blk_e[:, None] == experts[None, :]).astype(jnp.int32)
    pick = lambda t: jnp.sum(onehot * t[None, :], axis=1)
    blk_first_of_e, blk_count, blk_start = pick(first_blk), pick(counts), pick(start)
    blk_first = (blk == blk_first_of_e).astype(jnp.int32)
    blk_after = pick(blk_end)
    next_e = jnp.where(blk_after < blk_end[-1], blk_e[jnp.minimum(blk_after, nb - 1)], -1).astype(jnp.int32)
    row = jnp.arange(MOE_ROWS, dtype=jnp.int32)[None, :]
    within = (blk - blk_first_of_e)[:, None] * MOE_ROWS + row
    valid = jnp.logical_and(within < blk_count[:, None], (blk < blk_end[-1])[:, None])
    j = order[jnp.clip(blk_start[:, None] + within, 0, nk - 1)]
    src = jnp.where(valid, j // TOP_K, 0).astype(jnp.int32)
    scratch_row = nk + (blk % 2)[:, None] * MOE_ROWS + row
    dst = jnp.where(valid, (j % TOP_K) * n + j // TOP_K, scratch_row).astype(jnp.int32)
    dst_prev = jnp.concatenate([nk + MOE_ROWS + row, dst], axis=0)
    return (src, dst_prev, blk_e.astype(jnp.int32), blk_first, next_e,
            blk_end[-1:].astype(jnp.int32))


def _final_kernel(x1_ref, y0_ref, y1_ref, y2_ref, y3_ref, gate_ref, mod_ref, nf_ref, oc_ref, ol_ref,
                  *, ctx_blocks):
    gates = gate_ref[...]
    ffn = gates[:, 0:1] * _load_token_tiles(y0_ref)
    for j, ref in enumerate((y1_ref, y2_ref, y3_ref), start=1):
        ffn = ffn + gates[:, j:j + 1] * _load_token_tiles(ref)
    g2 = mod_ref[0][:, 5 * D_MODEL:6 * D_MODEL]
    x2 = x1_ref[...] + g2 * ffn
    ms = jnp.mean(x2 * x2, axis=-1, keepdims=True)
    y = x2 * lax.rsqrt(ms + NORM_EPS) * nf_ref[...]
    i = pl.program_id(0)

    @pl.when(i < ctx_blocks)
    def _():
        oc_ref[...] = y

    @pl.when(i >= ctx_blocks)
    def _():
        ol_ref[...] = y


def _final(x1, ys, gates, mods3, mod_row, norm_f, nc):
    n, d = x1.shape
    nb = n // TM
    nbc = nc // TM
    row = lambda i: (i, 0)
    return pl.pallas_call(
        functools.partial(_final_kernel, ctx_blocks=nbc),
        out_shape=[jax.ShapeDtypeStruct((nc, d), F32), jax.ShapeDtypeStruct((n - nc, d), F32)],
        grid=(nb,),
        in_specs=[pl.BlockSpec((TM, d), row)]
                 + [pl.BlockSpec((TM * TILE_ROWS, LANES), functools.partial(lambda j, i: (j * nb + i, 0), j))
                    for j in range(TOP_K)]
                 + [pl.BlockSpec((TM, LANES), row),
                    pl.BlockSpec((1, 1, 6 * d), lambda i: (mod_row(i), 0, 0)),
                    pl.BlockSpec((1, d), lambda i: (0, 0))],
        out_specs=_token_specs(nbc, d),
        compiler_params=_cparams("arbitrary"),
        name="combine_final",
    )(x1, ys, ys, ys, ys, gates, mods3, norm_f.reshape(1, d))


def _rope_tables(t_ctx, t_lat):
    rows = np.arange(t_lat) // GRID_W
    cols = np.arange(t_lat) % GRID_W
    inv = ROPE_THETA ** (-np.arange(0, AXIS_DIM, 2, dtype=np.float32) / AXIS_DIM)
    inv = jnp.asarray(inv, F32)
    ang_r = jnp.asarray(rows, F32)[:, None] * inv[None, :]
    ang_c = jnp.asarray(cols, F32)[:, None] * inv[None, :]
    cos = jnp.concatenate([jnp.cos(ang_r)] * 2 + [jnp.cos(ang_c)] * 2, axis=1)
    sin = jnp.concatenate([-jnp.sin(ang_r), jnp.sin(ang_r), -jnp.sin(ang_c), jnp.sin(ang_c)], axis=1)
    cos = jnp.concatenate([jnp.ones((t_ctx, HEAD), F32), cos], axis=0)
    sin = jnp.concatenate([jnp.zeros((t_ctx, HEAD), F32), sin], axis=0)
    return jnp.concatenate([cos, cos], axis=1), jnp.concatenate([sin, sin], axis=1)


def kernel(x_prompt, x_sample, cache_k, cache_v, state_rwkv_fwd, state_rwkv_bwd, c, c_ctx, w_ada, b_ada, norm1, norm2, w_in, rw_w0, rw_w2, rw_a0, rw_a2, rw_g2, rw_kk, rw_ka, rw_rk, rw_ln_w, rw_ln_b, q_norm, k_norm, w_up_r, w_up_a, w_out, w_router, b_router, w_moe_in, b_moe_in, w_moe_out, b_moe_out, norm_f):
    depth = w_in.shape[0]
    assert depth == 1, "single trunk layer"
    bc, tc, d = x_prompt.shape
    bl, tl, _ = x_sample.shape
    nc, nl = bc * tc, bl * tl
    n = nc + nl
    assert d == D_MODEL and tc == TM and tl % TM == 0 and nc % tl == 0
    assert nc % TM_PROJ == 0 and tl % TM_PROJ == 0
    l = 0

    x_ctx, x_lat = x_prompt.reshape(nc, d), x_sample.reshape(nl, d)
    cond =jnp.concatenate([c_ctx[None, :], c, jnp.zeros((16 - 1 - bl, d), F32)], axis=0)
    mods3 = _adaln(cond, w_ada[l], b_ada[l]).reshape(16, 1, 6 * d)
    nbc = nc // TM
    per_seq = tl // TM
    mod_row = lambda i: jnp.where(i < nbc, 0, 1 + (i - nbc) // per_seq)
    tab_row = lambda i: jnp.where(i < nbc, 0, 1 + (i - nbc) % per_seq)

    z = dict(zip([name for name, _ in IN_SPLITS],
                 _proj_in(x_ctx, x_lat, mods3, norm1[l], w_in[l].astype(BF16), tl)))

    ones_bd = jnp.asarray(np.kron(np.eye(H_R), np.ones((HEAD, HEAD))), BF16)
    pad_lo = lambda w: jnp.concatenate([w, jnp.zeros_like(w)], axis=0)
    pad_hi = lambda w: jnp.concatenate([jnp.zeros_like(w), w], axis=0)
    prep_p = dict(
        w0=rw_w0[l], a0=rw_a0[l],
        w2=jnp.stack([pad_lo(rw_w2[l, 0]), pad_hi(rw_w2[l, 1])]).astype(BF16),
        a2=jnp.stack([pad_lo(rw_a2[l, 0]), pad_hi(rw_a2[l, 1])]).astype(BF16),
        g2=rw_g2[l].astype(BF16),
        kkw=rw_kk[l].reshape(1, D_R), kaw=rw_ka[l].reshape(1, D_R), rk=rw_rk[l].reshape(1, D_R),
        qn=jnp.tile(q_norm[l], N_HEADS).reshape(1, D_A), kn=jnp.tile(k_norm[l], N_KV).reshape(1, D_KV),
        ones=ones_bd,
        tri=jnp.asarray(np.stack([np.kron(np.eye(TM // CHUNK), np.tril(np.ones((CHUNK, CHUNK)))),
                                  np.kron(np.eye(TM // CHUNK), np.triu(np.ones((CHUNK, CHUNK))))]), BF16),
        chunk_ones=jnp.asarray(np.kron(np.eye(TM // CHUNK), np.ones((CHUNK, CHUNK))), BF16))
    cos_tab, sin_tab = _rope_tables(tc, tl)
    sc, gt, vb, bonus, g, qr, kn, kr = _prep(z, cos_tab, sin_tab, tab_row, prep_p)

    yf_c, yb_c, s_fin = _rwkv_scan(sc, gt, vb, jnp.zeros((2, bc, H_R, HEAD, HEAD), F32), 0, bc, tc)
    yf_l, yb_l, _ = _rwkv_scan(sc, gt, vb, jnp.stack([state_rwkv_fwd[:, l], state_rwkv_bwd[:, l]]),
                               nc, bl, tl)
    y_f = (yf_c.reshape(nc, D_R), yf_l.reshape(nl, D_R))
    y_b = (yb_c.reshape(nc, D_R), yb_l.reshape(nl, D_R))

    ya_ctx = _attention(qr, kr, z["va"], 0, bc, tc)
    ya_lat = _attention(qr, kr, z["va"], nc, bl, tl,
                        cache_k[:, l].reshape(bl, -1, D_KV), cache_v[:, l].reshape(bl, -1, D_KV))

    post_p = dict(
        lnw=rw_ln_w[l].reshape(1, D_R), lnb=rw_ln_b[l].reshape(1, D_R),
        wur=w_up_r[l].astype(BF16), wua=w_up_a[l].astype(BF16), wo=w_out[l].astype(BF16),
        n2=norm2[l].reshape(1, d),
        wr=jnp.pad(w_router[l], ((0, 0), (0, LANES - N_EXPERTS))),
        br=jnp.pad(b_router[l], (0, LANES - N_EXPERTS)).reshape(1, LANES),
        ones=ones_bd)
    x1, h2, top_idx, gates = _post(y_f, y_b, bonus, g, ya_ctx, ya_lat, z["gate_r"], z["gate_a"],
                                   x_ctx, x_lat, mods3, mod_row, post_p)

    src, dst, blk_e, blk_first, next_e, used = _route(top_idx[:, :TOP_K], n)
    ys = _moe(h2, src, dst, blk_e, blk_first, next_e, used,
              w_moe_in[l], b_moe_in[l], w_moe_out[l], b_moe_out[l])
    y_ctx, y_lat = _final(x1, ys, gates, mods3, mod_row, norm_f, nc)

    y_prompt = y_ctx.reshape(bc, tc, d)
    y_sample = y_lat.reshape(bl, tl, d)
    new_cache_k = kn[:nc].reshape(bc, 1, tc, N_KV, HEAD)
    new_cache_v = z["va"][:nc].reshape(bc, 1, tc, N_KV, HEAD)
    new_state_fwd = s_fin[0][:, None]
    new_state_bwd = s_fin[1][:, None]
    return (y_prompt, y_sample, new_cache_k, new_cache_v, new_state_fwd, new_state_bwd)
```

```python
import functools

import numpy as np
import jax
import jax.numpy as jnp
from jax import lax
from jax.experimental import pallas as pl
from jax.experimental.pallas import tpu as pltpu

F32 = jnp.float32
BF16 = jnp.bfloat16

D_MODEL = 1024
GRID_W = 64
NORM_EPS = 1e-6
HEAD = 64
H_R = 8
D_R = H_R * HEAD
DECAY_RANK = 64
AAA_RANK = 64
GATE_RANK = 128
GN_EPS = 64e-5
N_HEADS = 8
N_KV = 2
D_A = N_HEADS * HEAD
D_KV = N_KV * HEAD
AXIS_DIM = HEAD // 2
ROPE_THETA = 10000.0
N_EXPERTS = 32
TOP_K = 4
D_FF = D_MODEL
SWIGLU_ALPHA = 1.702
SWIGLU_LIMIT = 7.0
LORA_COLS = 2 * DECAY_RANK + 2 * AAA_RANK + GATE_RANK
IN_SPLITS = (("r", D_R), ("k", D_R), ("v", D_R), ("lora", LORA_COLS), ("q", D_A),
             ("ka", D_KV), ("va", D_KV), ("gate_r", D_MODEL), ("gate_a", D_MODEL))

LANES = 128
TM = 256
TM_PROJ = 512
CHUNK = 64
SCAN_OPS = 6
SCAN_GROUP = 2
MOE_ROWS = 256
VMEM_LIMIT = 56 * 1024 * 1024


def _cparams(*sem):
    return pltpu.CompilerParams(dimension_semantics=sem, vmem_limit_bytes=VMEM_LIMIT)


def _dot(a, b):
    return jnp.dot(a, b, preferred_element_type=F32)


def _dot_nt(a, b):
    return lax.dot_general(a, b, (((1,), (1,)), ((), ())), preferred_element_type=F32)


def _dot_tn(a, b):
    return lax.dot_general(a, b, (((0,), (0,)), ((), ())), preferred_element_type=F32)


def _split2(x):
    hi = x.astype(BF16)
    lo = (x - hi.astype(F32)).astype(BF16)
    return hi, lo


def _head_sum(x, ones_bd):
    return _dot(x.astype(BF16), ones_bd)


def _sigmoid(x):
    return 1.0 / (1.0 + jnp.exp(-x))


TILE_ROWS = D_MODEL // LANES


def _store_token_tiles(ref, x):
    rows = x.shape[0]
    for j in range(TILE_ROWS):
        ref[pl.ds(j, rows, stride=TILE_ROWS), :] = x[:, j * LANES:(j + 1) * LANES]


def _load_token_tiles(ref):
    rows = ref.shape[0] // TILE_ROWS
    return jnp.concatenate([ref[pl.ds(j, rows, stride=TILE_ROWS), :] for j in range(TILE_ROWS)], axis=1)


def _adaln_kernel(c_ref, w_ref, b_ref, o_ref):
    c = c_ref[...]
    s = (c * _sigmoid(c)).astype(BF16)
    o_ref[...] = _dot(s, w_ref[...].astype(BF16)) + b_ref[...]


def _adaln(cond, w_ada, b_ada):
    rows, d = cond.shape
    cols = w_ada.shape[1]
    tn = 512
    return pl.pallas_call(
        _adaln_kernel,
        out_shape=jax.ShapeDtypeStruct((rows, cols), F32),
        grid=(cols // tn,),
        in_specs=[pl.BlockSpec((rows, d), lambda j: (0, 0)),
                  pl.BlockSpec((d, tn), lambda j: (0, j)),
                  pl.BlockSpec((1, tn), lambda j: (0, j))],
        out_specs=pl.BlockSpec((rows, tn), lambda j: (0, j)),
        compiler_params=_cparams("arbitrary"),
        name="adaln",
    )(cond, w_ada, b_ada.reshape(1, cols))


def _token_specs(nbc, width, tm=TM):
    return [pl.BlockSpec((tm, width), lambda i: (jnp.minimum(i, nbc - 1), 0)),
            pl.BlockSpec((tm, width), lambda i: (jnp.maximum(i - nbc, 0), 0))]


def _proj_in_kernel(xc_ref, xl_ref, mod_ref, n1_ref, w_ref, *out_refs, ctx_blocks):
    x = jnp.where(pl.program_id(0) < ctx_blocks, xc_ref[...], xl_ref[...])
    ms = jnp.mean(x * x, axis=-1, keepdims=True)
    hn = x * lax.rsqrt(ms + NORM_EPS) * n1_ref[...]
    mod = mod_ref[0]
    sh1 = mod[:, 0:D_MODEL]
    sc1 = mod[:, D_MODEL:2 * D_MODEL]
    h = (hn * (1.0 + sc1) + sh1).astype(BF16)
    c0 = 0
    for ref, (_, width) in zip(out_refs, IN_SPLITS):
        ref[...] = _dot(h, w_ref[:, c0:c0 + width])
        c0 += width


def _proj_in(x_ctx, x_lat, mods3, norm1, w_in_bf, t_lat):
    d = x_ctx.shape[1]
    n = x_ctx.shape[0] + x_lat.shape[0]
    tm = TM_PROJ
    nbc = x_ctx.shape[0] // tm
    per_seq = t_lat // tm
    cols = w_in_bf.shape[1]
    row = lambda i: (i, 0)
    mod_row = lambda i: jnp.where(i < nbc, 0, 1 + (i - nbc) // per_seq)
    return pl.pallas_call(
        functools.partial(_proj_in_kernel, ctx_blocks=nbc),
        out_shape=[jax.ShapeDtypeStruct((n, width), F32) for _, width in IN_SPLITS],
        grid=(n // tm,),
        in_specs=_token_specs(nbc, d, tm) + [
                  pl.BlockSpec((1, 1, 6 * d), lambda i: (mod_row(i), 0, 0)),
                  pl.BlockSpec((1, d), lambda i: (0, 0)),
                  pl.BlockSpec((d, cols), lambda i: (0, 0))],
        out_specs=[pl.BlockSpec((tm, width), row) for _, width in IN_SPLITS],
        compiler_params=_cparams("arbitrary"),
        name="proj_in",
    )(x_ctx, x_lat, mods3, norm1.reshape(1, d), w_in_bf)


def _prep_kernel(r_ref, k_ref, v_ref, lora_ref, q_ref, ka_ref, cos_ref, sin_ref,
                 w0_ref, w2_ref, a0_ref, a2_ref, g2_ref, kkw_ref, kaw_ref, rk_ref,
                 qn_ref, kn_ref, ones_ref, tri_ref, chunk_ones_ref,
                 sc_o, gt_o, vb_o, bonus_o, g_o, qr_o, kn_o, kr_o):
    ones = ones_ref[...]
    chunk_ones = chunk_ones_ref[...]
    r = r_ref[...]
    k = k_ref[...]
    v = v_ref[...]
    vb_o[...] = v.astype(BF16)
    lora = lora_ref[...]
    kk = k * kkw_ref[...]
    kk = kk * lax.rsqrt(_head_sum(kk * kk, ones) + 1e-12)
    th = jnp.tanh(lora[:, 0:LANES]).astype(BF16)
    al = lora[:, LANES:2 * LANES].astype(BF16)
    kd_sum = jnp.zeros_like(k)
    for d in range(2):
        u = w0_ref[d:d + 1, :] + _dot(th, w2_ref[d])
        w_log = -(jnp.maximum(-u, 0.0) + jnp.log1p(jnp.exp(-jnp.abs(u)))) - 0.5
        lw = -jnp.exp(w_log)
        a = _sigmoid(a0_ref[d:d + 1, :] + _dot(al, a2_ref[d]))
        kd = k * (1.0 + (a - 1.0) * kaw_ref[...])
        kd_sum = kd_sum + kd
        b = kk * a
        parts = _split2(lw)
        tri = tri_ref[d]
        L = _dot(tri, parts[0]) + _dot(tri, parts[1])
        Ltot = _dot(chunk_ones, parts[0]) + _dot(chunk_ones, parts[1])
        e_inv = jnp.exp(-L)
        e_rest = jnp.exp(Ltot - L)
        scan_ops = (kk * jnp.exp(L - lw), r * jnp.exp(L), b * e_inv, kd * e_inv, b * e_rest, kd * e_rest)
        for j, op in enumerate(scan_ops):
            sc_o[d, :, j * D_R:(j + 1) * D_R] = op.astype(BF16)
        g_tot = jnp.exp(Ltot)
        for j in range(TM // CHUNK):
            gt_o[d, j] = g_tot[j * CHUNK:j * CHUNK + 1, :]
    bonus_o[...] = _head_sum(r * kd_sum * rk_ref[...], ones) * v
    g_o[...] = _dot(_sigmoid(lora[:, 2 * LANES:3 * LANES]).astype(BF16), g2_ref[...])
    q = q_ref[...]
    qn = q * lax.rsqrt(_head_sum(q * q, ones) * (1.0 / HEAD) + NORM_EPS) * qn_ref[...]
    ka = ka_ref[...]
    ones_kv = ones[0:D_KV, 0:D_KV]
    kn = ka * lax.rsqrt(_head_sum(ka * ka, ones_kv) * (1.0 / HEAD) + NORM_EPS) * kn_ref[...]
    kn_o[...] = kn
    cos = cos_ref[...]
    sin = sin_ref[...]
    half = AXIS_DIM // 2

    def rope(t, cos_t, sin_t):
        width = t.shape[1]
        lane = lax.broadcasted_iota(jnp.int32, t.shape, 1)
        first = (lane % AXIS_DIM) < half
        swapped = jnp.where(first, pltpu.roll(t, width - half, 1), pltpu.roll(t, half, 1))
        return t * cos_t + swapped * sin_t

    cos4 = jnp.concatenate([cos] * (D_A // D_KV), axis=1)
    sin4 = jnp.concatenate([sin] * (D_A // D_KV), axis=1)
    qr_o[...] = (rope(qn, cos4, sin4) * (HEAD ** -0.5)).astype(BF16)
    kr_o[...] = rope(kn, cos, sin).astype(BF16)


def _prep(z, cos_tab, sin_tab, tab_row, p):
    n = z["r"].shape[0]
    row = lambda i: (i, 0)
    full = lambda *shape: pl.BlockSpec(shape, lambda i: (0,) * len(shape))
    tok = lambda width: pl.BlockSpec((TM, width), row)
    tab = pl.BlockSpec((TM, D_KV), lambda i: (tab_row(i), 0))
    cpb = TM // CHUNK
    return pl.pallas_call(
        _prep_kernel,
        out_shape=[jax.ShapeDtypeStruct((2, n, SCAN_OPS * D_R), BF16),
                   jax.ShapeDtypeStruct((2, n // CHUNK, 1, D_R), F32),
                   jax.ShapeDtypeStruct((n, D_R), BF16),
                   jax.ShapeDtypeStruct((n, D_R), F32),
                   jax.ShapeDtypeStruct((n, D_R), F32),
                   jax.ShapeDtypeStruct((n, D_A), BF16),
                   jax.ShapeDtypeStruct((n, D_KV), F32),
                   jax.ShapeDtypeStruct((n, D_KV), BF16)],
        grid=(n // TM,),
        in_specs=[tok(D_R), tok(D_R), tok(D_R), tok(LORA_COLS), tok(D_A), tok(D_KV), tab, tab,
                  full(2, D_R), full(2, LANES, D_R), full(2, D_R), full(2, LANES, D_R),
                  full(GATE_RANK, D_R), full(1, D_R), full(1, D_R), full(1, D_R),
                  full(1, D_A), full(1, D_KV), full(D_R, D_R), full(2, TM, TM), full(TM, TM)],
        out_specs=[pl.BlockSpec((2, TM, SCAN_OPS * D_R), lambda i: (0, i, 0)),
                   pl.BlockSpec((2, cpb, 1, D_R), lambda i: (0, i, 0, 0)),
                   tok(D_R), tok(D_R), tok(D_R), tok(D_A), tok(D_KV), tok(D_KV)],
        compiler_params=_cparams("arbitrary"),
        name="mixer_prep",
    )(z["r"], z["k"], z["v"], z["lora"], z["q"], z["ka"], cos_tab, sin_tab,
      p["w0"], p["w2"], p["a0"], p["a2"], p["g2"], p["kkw"], p["kaw"], p["rk"],
      p["qn"], p["kn"], p["ones"], p["tri"], p["chunk_ones"])


def _scan_kernel(scf_ref, scb_ref, vf_ref, vb_ref, gf_ref, gb_ref, s0_ref,
                 yf_ref, yb_ref, sfin_ref, state):
    c = pl.program_id(1)
    C = CHUNK
    pairs = D_R // LANES
    chains = [(g, d, p) for g in range(SCAN_GROUP) for d in range(2) for p in range(pairs)]

    lane = lax.broadcasted_iota(jnp.int32, (1, LANES), 1)
    low = lane < HEAD

    @pl.when(c == 0)
    def _():
        zero = jnp.zeros((HEAD, HEAD), F32)
        for i, (g, d, p) in enumerate(chains):
            top = jnp.concatenate([s0_ref[d, g, 2 * p], zero], axis=1)
            bot = jnp.concatenate([zero, s0_ref[d, g, 2 * p + 1]], axis=1)
            state[i] = jnp.concatenate([top, bot], axis=0)

    W = 2 * C
    t2 = lax.broadcasted_iota(jnp.int32, (C, W), 0)
    s2 = lax.broadcasted_iota(jnp.int32, (C, W), 1) % C
    strict = (s2 < t2, s2 > t2)
    incl = (s2 <= t2, s2 >= t2)
    eye2 = jnp.where(s2 == t2, 1.0, 0.0)
    first_mat = lax.broadcasted_iota(jnp.int32, (1, W), 1) < C
    r2 = lax.broadcasted_iota(jnp.int32, (LANES, LANES), 0)
    c2 = lax.broadcasted_iota(jnp.int32, (LANES, LANES), 1)
    same_head = (r2 // HEAD) == (c2 // HEAD)

    sc_refs = (scf_ref, scb_ref)
    v_refs = (vf_ref, vb_ref)
    g_refs = (gf_ref, gb_ref)
    y_refs = (yf_ref, yb_ref)

    def operand(chain, j):
        g, d, p = chain
        c0 = j * D_R + p * LANES
        return sc_refs[d][0, g, :, c0:c0 + LANES]

    def stack(x):
        first = low if x.shape[1] == LANES else first_mat
        zero = jnp.zeros_like(x)
        return jnp.concatenate([jnp.where(first, x, zero), jnp.where(first, zero, x)], axis=0)

    Qk = [operand(ch, 0) for ch in chains]
    Qr = [operand(ch, 1) for ch in chains]
    v_p = [v_refs[d][g, :, p * LANES:(p + 1) * LANES] for g, d, p in chains]
    v_st = [stack(x) for x in v_p]
    Z0 = [state[i] for i in range(len(chains))]
    Z0b = [z.astype(BF16) for z in Z0]
    m = [_dot_nt(jnp.concatenate([Qk[i], Qr[i]], axis=0),
                 jnp.concatenate([stack(operand(ch, 2)), stack(operand(ch, 3))], axis=0))
         for i, ch in enumerate(chains)]
    A = [jnp.where(strict[d], m[i][0:C, 0:W], 0.0) for i, (g, d, p) in enumerate(chains)]
    Bm = [jnp.where(strict[d], m[i][0:C, W:2 * W], 0.0).astype(BF16)
          for i, (g, d, p) in enumerate(chains)]
    Pb = [jnp.where(incl[d], m[i][C:2 * C, 0:W], 0.0).astype(BF16)
          for i, (g, d, p) in enumerate(chains)]
    Pk = [jnp.where(incl[d], m[i][C:2 * C, W:2 * W], 0.0).astype(BF16)
          for i, (g, d, p) in enumerate(chains)]
    BV = [_dot(Bm[i], v_st[i]) for i in range(len(chains))]
    T = [eye2 - a for a in A]
    Ap = [_dot(a.astype(BF16), stack(a.astype(BF16))).astype(BF16) for a in A]
    n = 2
    while 2 * n < C:
        R = [_dot(jnp.concatenate([Ap[i], T[i].astype(BF16)], axis=0), stack(Ap[i]))
             for i in range(len(chains))]
        Ap = [x[0:C].astype(BF16) for x in R]
        T = [T[i] + R[i][C:2 * C] for i in range(len(chains))]
        n *= 2
    T = [T[i] + _dot(T[i].astype(BF16), stack(Ap[i])) for i in range(len(chains))]
    WU = [_dot(T[i].astype(BF16),
               jnp.concatenate([stack(Qk[i]), stack(BV[i].astype(BF16))], axis=1))
          for i in range(len(chains))]
    Wq = [x[:, 0:LANES].astype(BF16) for x in WU]
    Uv = [x[:, LANES:2 * LANES] for x in WU]
    PW = [_dot(Pb[i], jnp.concatenate([stack(Wq[i]), stack(Uv[i].astype(BF16))], axis=1))
          for i in range(len(chains))]
    PkV = [_dot(Pk[i], v_st[i]) for i in range(len(chains))]
    U = [_dot_nt(Wq[i], Z0b[i]) + Uv[i] for i in range(len(chains))]
    Yq = [(Qr[i].astype(F32) - PW[i][:, 0:LANES]).astype(BF16) for i in range(len(chains))]
    for i, (g, d, p) in enumerate(chains):
        y_refs[d][g, :, p * LANES:(p + 1) * LANES] = (_dot_nt(Yq[i], Z0b[i]) + PkV[i]
                                                      - PW[i][:, LANES:2 * LANES])
    for i, (g, d, p) in enumerate(chains):
        lhs = jnp.concatenate([v_p[i], (-U[i]).astype(BF16)], axis=0)
        rhs = jnp.concatenate([operand(chains[i], 5), operand(chains[i], 4)], axis=0)
        upd = _dot_tn(lhs, rhs)
        g_tot = g_refs[d][0, g, 0][:, p * LANES:(p + 1) * LANES]
        state[i] = Z0[i] * g_tot + jnp.where(same_head, upd, 0.0)

    @pl.when(c == pl.num_programs(1) - 1)
    def _():
        for i, (g, d, p) in enumerate(chains):
            z = state[i]
            sfin_ref[d, g, 2 * p] = z[0:HEAD, 0:HEAD]
            sfin_ref[d, g, 2 * p + 1] = z[HEAD:2 * HEAD, HEAD:2 * HEAD]


def _rwkv_scan(sc, gt, vb, s0, row0, batch, t_len):
    n = vb.shape[0]
    nch = t_len // CHUNK
    assert row0 % (t_len * SCAN_GROUP) == 0 and batch % SCAN_GROUP == 0
    seq0 = row0 // (t_len * SCAN_GROUP)
    width = SCAN_OPS * D_R
    sc4 = sc.reshape(2, n // t_len, t_len, width)
    gt5 = gt.reshape(2, n // t_len, nch, 1, D_R)
    v3 = vb.reshape(n // t_len, t_len, D_R)
    G = SCAN_GROUP
    fwd = lambda c: c
    bwd = lambda c: nch - 1 - c
    sc_spec = lambda d, at: pl.BlockSpec((1, G, CHUNK, width), lambda s, c: (d, seq0 + s, at(c), 0))
    v_spec = lambda at: pl.BlockSpec((G, CHUNK, D_R), lambda s, c: (seq0 + s, at(c), 0))
    gt_spec = lambda d, at: pl.BlockSpec((1, G, 1, 1, D_R), lambda s, c: (d, seq0 + s, at(c), 0, 0))
    y_spec = lambda at: pl.BlockSpec((G, CHUNK, D_R), lambda s, c: (s, at(c), 0))
    st = pl.BlockSpec((2, G, H_R, HEAD, HEAD), lambda s, c: (0, s, 0, 0, 0))
    return pl.pallas_call(
        _scan_kernel,
        out_shape=[jax.ShapeDtypeStruct((batch, t_len, D_R), F32),
                   jax.ShapeDtypeStruct((batch, t_len, D_R), F32),
                   jax.ShapeDtypeStruct((2, batch, H_R, HEAD, HEAD), F32)],
        grid=(batch // G, nch),
        in_specs=[sc_spec(0, fwd), sc_spec(1, bwd), v_spec(fwd), v_spec(bwd),
                  gt_spec(0, fwd), gt_spec(1, bwd), st],
        out_specs=[y_spec(fwd), y_spec(bwd), st],
        scratch_shapes=[pltpu.VMEM((G * 2 * D_R // LANES, LANES, LANES), F32)],
        compiler_params=_cparams("arbitrary", "arbitrary"),
        name="rwkv_scan",
    )(sc4, sc4, v3, v3, gt5, gt5, s0)


def _attn_kernel(*refs, tq, with_cache):
    if with_cache:
        q_ref, k_ref, v_ref, ck_ref, cv_ref, o_ref = refs
        kx = jnp.concatenate([k_ref[...], ck_ref[0].astype(BF16)], axis=0)
        vx = jnp.concatenate([v_ref[...], cv_ref[0]], axis=0).astype(BF16)
    else:
        q_ref, k_ref, v_ref, o_ref = refs
        kx = k_ref[...]
        vx = v_ref[...].astype(BF16)
    lane = lax.broadcasted_iota(jnp.int32, (1, D_KV), 1)
    low = lane < HEAD
    k_sw = pltpu.roll(kx.astype(F32), HEAD, 1).astype(BF16)
    v_sw = pltpu.roll(vx.astype(F32), HEAD, 1).astype(BF16)
    group = N_HEADS // N_KV
    for p in range(D_A // LANES):
        g = (2 * p) // group
        keep = low if g == 0 else jnp.logical_not(low)
        kd = jnp.where(keep, kx, k_sw)
        vd = jnp.where(keep, vx, v_sw)
        qp = q_ref[:, p * LANES:(p + 1) * LANES]
        zero = jnp.zeros_like(qp)
        qs = jnp.concatenate([jnp.where(low, qp, zero), jnp.where(low, zero, qp)], axis=0)
        s = _dot_nt(qs, kd)
        mx = jnp.max(s, axis=-1, keepdims=True)
        e = jnp.exp(s - mx)
        den = jnp.sum(e, axis=-1, keepdims=True)
        o = _dot(e.astype(BF16), vd) / den
        o_ref[:, p * LANES:(p + 1) * LANES] = jnp.where(low, o[0:tq], o[tq:2 * tq]).astype(BF16)


def _attention(qr, kr, va, row0, batch, t_len, cache_k=None, cache_v=None):
    tq = min(t_len, 256)
    nq = t_len // tq
    qblk0 = row0 // tq
    sblk0 = row0 // t_len
    with_cache = cache_k is not None
    in_specs = [pl.BlockSpec((tq, D_A), lambda b, i: (qblk0 + b * nq + i, 0)),
                pl.BlockSpec((t_len, D_KV), lambda b, i: (sblk0 + b, 0)),
                pl.BlockSpec((t_len, D_KV), lambda b, i: (sblk0 + b, 0))]
    args = [qr, kr, va]
    if with_cache:
        past = cache_k.shape[1]
        in_specs += [pl.BlockSpec((1, past, D_KV), lambda b, i: (b, 0, 0))] * 2
        args += [cache_k, cache_v]
    return pl.pallas_call(
        functools.partial(_attn_kernel, tq=tq, with_cache=with_cache),
        out_shape=jax.ShapeDtypeStruct((batch * t_len, D_A), BF16),
        grid=(batch, nq),
        in_specs=in_specs,
        out_specs=pl.BlockSpec((tq, D_A), lambda b, i: (b * nq + i, 0)),
        compiler_params=_cparams("arbitrary", "arbitrary"),
        name="attention_cache" if with_cache else "attention_ctx",
    )(*args)


def _post_kernel(yfc_ref, yfl_ref, ybc_ref, ybl_ref, bonus_ref, g_ref, yac_ref, yal_ref, gr_ref, ga_ref,
                 xc_ref, xl_ref,
                 mod_ref, lnw_ref, lnb_ref, wur_ref, wua_ref, wo_ref, n2_ref, wr_ref, br_ref, ones_ref,
                 x1_o, h2_o, idx_o, gate_o, *, ctx_blocks):
    ones = ones_ref[...]
    is_ctx = pl.program_id(0) < ctx_blocks
    y_a = jnp.where(is_ctx, yac_ref[...], yal_ref[...])
    x = jnp.where(is_ctx, xc_ref[...], xl_ref[...])
    y = jnp.where(is_ctx, yfc_ref[...] + ybc_ref[...], yfl_ref[...] + ybl_ref[...])
    mu = _head_sum(y, ones) * (1.0 / HEAD)
    yc = y - mu
    var = _head_sum(yc * yc, ones) * (1.0 / HEAD)
    yn = yc * lax.rsqrt(var + GN_EPS) * lnw_ref[...] + lnb_ref[...] + bonus_ref[...]
    y_r = (yn * g_ref[...]).astype(BF16)
    merged = (_sigmoid(gr_ref[...]) * _dot(y_r, wur_ref[...])
              + _sigmoid(ga_ref[...]) * _dot(y_a, wua_ref[...]))
    mix = _dot(merged.astype(BF16), wo_ref[...])
    mod = mod_ref[0]
    g1 = mod[:, 2 * D_MODEL:3 * D_MODEL]
    sh2 = mod[:, 3 * D_MODEL:4 * D_MODEL]
    sc2 = mod[:, 4 * D_MODEL:5 * D_MODEL]
    x1 = x + g1 * mix
    x1_o[...] = x1
    ms = jnp.mean(x1 * x1, axis=-1, keepdims=True)
    h2 = x1 * lax.rsqrt(ms + NORM_EPS) * n2_ref[...] * (1.0 + sc2) + sh2
    _store_token_tiles(h2_o, h2)
    hh, hl = _split2(h2)
    wh, wl = _split2(wr_ref[...])
    hi_terms = _dot(hh, jnp.concatenate([wh, wl], axis=1))
    logits = hi_terms[:, 0:LANES] + hi_terms[:, LANES:2 * LANES] + _dot(hl, wh) + br_ref[...]
    lane_i = lax.broadcasted_iota(jnp.int32, logits.shape, 1)
    lane = lane_i.astype(F32)
    neg = jnp.float32(-jnp.inf)
    cur = jnp.where(lane_i < N_EXPERTS, logits, neg)
    vals, idxs = [], []
    for _ in range(TOP_K):
        mx = jnp.max(cur, axis=-1, keepdims=True)
        ix = jnp.min(jnp.where(cur == mx, lane, float(LANES)), axis=-1, keepdims=True)
        vals.append(mx)
        idxs.append(ix)
        cur = jnp.where(lane == ix, neg, cur)
    es = [jnp.exp(val - vals[0]) for val in vals]
    den = es[0] + es[1] + es[2] + es[3]
    idx_out = jnp.zeros(logits.shape, jnp.int32)
    gate_out = jnp.zeros(logits.shape, F32)
    for j in range(TOP_K):
        idx_out = jnp.where(lane_i == j, idxs[j].astype(jnp.int32), idx_out)
        gate_out = jnp.where(lane_i == j, es[j] / den, gate_out)
    idx_o[...] = idx_out
    gate_o[...] = gate_out


def _post(y_f, y_b, bonus, g, ya_ctx, ya_lat, gate_r, gate_a, x_ctx, x_lat, mods3, mod_row, p):
    n = x_ctx.shape[0] + x_lat.shape[0]
    nbc = ya_ctx.shape[0] // TM
    row = lambda i: (i, 0)
    full = lambda *shape: pl.BlockSpec(shape, lambda i: (0,) * len(shape))
    tok = lambda width: pl.BlockSpec((TM, width), row)
    return pl.pallas_call(
        functools.partial(_post_kernel, ctx_blocks=nbc),
        out_shape=[jax.ShapeDtypeStruct((n, D_MODEL), F32),
                   jax.ShapeDtypeStruct((n * TILE_ROWS, LANES), F32),
                   jax.ShapeDtypeStruct((n, LANES), jnp.int32),
                   jax.ShapeDtypeStruct((n, LANES), F32)],
        grid=(n // TM,),
        in_specs=_token_specs(nbc, D_R) + _token_specs(nbc, D_R) + [tok(D_R), tok(D_R)]
                 + _token_specs(nbc, D_A) + [tok(D_MODEL), tok(D_MODEL)] + _token_specs(nbc, D_MODEL) + [
                  pl.BlockSpec((1, 1, 6 * D_MODEL), lambda i: (mod_row(i), 0, 0)),
                  full(1, D_R), full(1, D_R), full(D_R, D_MODEL), full(D_A, D_MODEL),
                  full(D_MODEL, D_MODEL), full(1, D_MODEL), full(D_MODEL, LANES), full(1, LANES),
                  full(D_R, D_R)],
        out_specs=[tok(D_MODEL), pl.BlockSpec((TM * TILE_ROWS, LANES), row), tok(LANES), tok(LANES)],
        compiler_params=_cparams("arbitrary"),
        name="mixer_post",
    )(*y_f, *y_b, bonus, g, ya_ctx, ya_lat, gate_r, gate_a, x_ctx, x_lat, mods3,
      p["lnw"], p["lnb"], p["wur"], p["wua"], p["wo"], p["n2"], p["wr"], p["br"], p["ones"])


def _moe_kernel(blk_e_ref, blk_first_ref, next_e_ref, used_ref,
                src0_ref, src_next_ref, dst_prev_ref, h_hbm, w1_hbm, b1_ref, w2_hbm, b2_ref,
                out_hbm, w1f, w2f, w1b, w2b, xs, ys, sem_g, sem_s, sem_t, sem_w, *, n_rows):
    b = pl.program_id(0)
    used = used_ref[0]
    R = MOE_ROWS
    T = TILE_ROWS

    def tile(ref, i):
        return ref.at[pl.ds(pl.multiple_of(i * T, T), T)]

    def gather(table_ref, buf):
        for i in range(R):
            pltpu.make_async_copy(tile(h_hbm, table_ref[0, 0, i]), tile(xs.at[buf], i),
                                  sem_g.at[buf]).start(priority=i % 2)

    def scatter_prev(buf):
        for i in range(R):
            pltpu.make_async_copy(tile(ys.at[buf], i), tile(out_hbm, dst_prev_ref[0, 0, i]),
                                  sem_s.at[buf]).start(priority=i % 2)

    def fetch_weights(e):
        pltpu.make_async_copy(w1_hbm.at[e], w1f, sem_w.at[0]).start()
        pltpu.make_async_copy(w2_hbm.at[e], w2f, sem_w.at[1]).start()

    @pl.when(b == 0)
    def _():
        ys[...] = jnp.zeros_like(ys)
        for s in range(2):
            band = pltpu.make_async_copy(ys.at[s], out_hbm.at[pl.ds((n_rows + s * R) * T, R * T)], sem_t)
            band.start()
            band.wait()
        gather(src0_ref, 0)
        fetch_weights(blk_e_ref[0])

    @pl.when(jnp.logical_and(blk_first_ref[b] == 1, b < used))
    def _():
        pltpu.make_async_copy(w1_hbm.at[0], w1f, sem_w.at[0]).wait()
        pltpu.make_async_copy(w2_hbm.at[0], w2f, sem_w.at[1]).wait()
        w1b[...] = w1f[...].astype(BF16)
        w2b[...] = w2f[...].astype(BF16)

        @pl.when(next_e_ref[b] >= 0)
        def _():
            fetch_weights(next_e_ref[b])

    def step(p):
        @pl.when(b <= used)
        def _():
            pltpu.make_async_copy(h_hbm.at[pl.ds(0, R * T)], xs.at[p], sem_g.at[p]).wait()

        @pl.when(jnp.logical_and(b >= 1, b <= used + 1))
        def _():
            pltpu.make_async_copy(ys.at[p], out_hbm.at[pl.ds(0, R * T)], sem_s.at[p]).wait()

        @pl.when(b < used)
        def _():
            gather(src_next_ref, 1 - p)
            scatter_prev(1 - p)
            hb = _dot(_load_token_tiles(xs.at[p]).astype(BF16), w1b[...]) + b1_ref[0]
            glu = jnp.minimum(hb[:, 0:D_FF], SWIGLU_LIMIT)
            lin = jnp.clip(hb[:, D_FF:2 * D_FF], -SWIGLU_LIMIT, SWIGLU_LIMIT)
            act = glu * _sigmoid(SWIGLU_ALPHA * glu) * (lin + 1.0)
            _store_token_tiles(ys.at[p], _dot(act.astype(BF16), w2b[...]) + b2_ref[0])

        @pl.when(b == used)
        def _():
            scatter_prev(1 - p)

    for p in range(2):
        pl.when(b % 2 == p)(functools.partial(step, p))


def _moe(h2, src, dst_prev, blk_e, blk_first, next_e, used, w1, b1, w2, b2):
    d = D_MODEL
    n = h2.shape[0] // TILE_ROWS
    nb = src.shape[0]
    steps = nb + 2
    cur = lambda b, *_: (jnp.minimum(b, nb), 0, 0)
    nxt = lambda b, *_: (jnp.minimum(b + 1, nb - 1), 0, 0)
    table = lambda index_map: pl.BlockSpec((1, 1, MOE_ROWS), index_map, memory_space=pltpu.SMEM)
    expert = lambda b, blk_e, *_: (blk_e[jnp.minimum(b, nb - 1)], 0, 0)
    grid_spec = pltpu.PrefetchScalarGridSpec(
        num_scalar_prefetch=4,
        grid=(steps,),
        in_specs=[table(lambda b, *_: (0, 0, 0)), table(nxt), table(cur),
                  pl.BlockSpec(memory_space=pl.ANY),
                  pl.BlockSpec(memory_space=pl.ANY),
                  pl.BlockSpec((1, 1, 2 * D_FF), expert),
                  pl.BlockSpec(memory_space=pl.ANY),
                  pl.BlockSpec((1, 1, d), expert)],
        out_specs=pl.BlockSpec(memory_space=pl.ANY),
        scratch_shapes=[pltpu.VMEM((d, 2 * D_FF), F32),
                        pltpu.VMEM((D_FF, d), F32),
                        pltpu.VMEM((d, 2 * D_FF), BF16),
                        pltpu.VMEM((D_FF, d), BF16),
                        pltpu.VMEM((2, MOE_ROWS * TILE_ROWS, LANES), F32),
                        pltpu.VMEM((2, MOE_ROWS * TILE_ROWS, LANES), F32),
                        pltpu.SemaphoreType.DMA((2,)),
                        pltpu.SemaphoreType.DMA((2,)),
                        pltpu.SemaphoreType.DMA(()),
                        pltpu.SemaphoreType.DMA((2,))])
    tab3 = lambda t: t.reshape(-1, 1, MOE_ROWS)
    pad = lambda t: jnp.concatenate([t, jnp.zeros((steps - nb,), jnp.int32)])
    return pl.pallas_call(
        functools.partial(_moe_kernel, n_rows=n * TOP_K),
        out_shape=jax.ShapeDtypeStruct(((n * TOP_K + 2 * MOE_ROWS) * TILE_ROWS, LANES), F32),
        grid_spec=grid_spec,
        compiler_params=pltpu.CompilerParams(dimension_semantics=("arbitrary",),
                                             vmem_limit_bytes=VMEM_LIMIT, has_side_effects=True),
        name="moe_experts",
    )(blk_e, pad(blk_first), next_e, used, tab3(src), tab3(src), tab3(dst_prev), h2, w1,
      b1.reshape(N_EXPERTS, 1, 2 * D_FF), w2, b2.reshape(N_EXPERTS, 1, d))


def _route(top_idx, n):
    nk = n * TOP_K
    cap = nk + N_EXPERTS * MOE_ROWS
    nb = cap // MOE_ROWS
    flat_e = top_idx.reshape(nk)
    order = jnp.argsort(flat_e, stable=True).astype(jnp.int32)
    counts = jnp.sum((flat_e[:, None] == jnp.arange(N_EXPERTS, dtype=jnp.int32)[None, :])
                     .astype(jnp.int32), axis=0)
    start = jnp.cumsum(counts) - counts
    n_blk = (counts + MOE_ROWS - 1) // MOE_ROWS
    blk_end = jnp.cumsum(n_blk)
    first_blk = blk_end - n_blk
    experts = jnp.arange(N_EXPERTS, dtype=jnp.int32)
    blk = jnp.arange(nb, dtype=jnp.int32)
    blk_e = jnp.minimum(jnp.sum((blk[:, None] >= blk_end[None, :]).astype(jnp.int32), axis=1),
                        N_EXPERTS - 1)
    onehot = (blk_e[:, None] == experts[None, :]).astype(jnp.int32)
    pick = lambda t: jnp.sum(onehot * t[None, :], axis=1)
    blk_first_of_e, blk_count, blk_start = pick(first_blk), pick(counts), pick(start)
    blk_first = (blk == blk_first_of_e).astype(jnp.int32)
    blk_after = pick(blk_end)
    next_e = jnp.where(blk_after < blk_end[-1], blk_e[jnp.minimum(blk_after, nb - 1)], -1).astype(jnp.int32)
    row = jnp.arange(MOE_ROWS, dtype=jnp.int32)[None, :]
    within = (blk - blk_first_of_e)[:, None] * MOE_ROWS + row
    valid = jnp.logical_and(within < blk_count[:, None], (blk < blk_end[-1])[:, None])
    j = order[jnp.clip(blk_start[:, None] + within, 0, nk - 1)]
    src = jnp.where(valid, j // TOP_K, 0).astype(jnp.int32)
    scratch_row = nk + (blk % 2)[:, None] * MOE_ROWS + row
    dst = jnp.where(valid, (j % TOP_K) * n + j // TOP_K, scratch_row).astype(jnp.int32)
    dst_prev = jnp.concatenate([nk + MOE_ROWS + row, dst], axis=0)
    return (src, dst_prev, blk_e.astype(jnp.int32), blk_first, next_e,
            blk_end[-1:].astype(jnp.int32))


def _final_kernel(x1_ref, y0_ref, y1_ref, y2_ref, y3_ref, gate_ref, mod_ref, nf_ref, oc_ref, ol_ref,
                  *, ctx_blocks):
    gates = gate_ref[...]
    ffn = gates[:, 0:1] * _load_token_tiles(y0_ref)
    for j, ref in enumerate((y1_ref, y2_ref, y3_ref), start=1):
        ffn = ffn + gates[:, j:j + 1] * _load_token_tiles(ref)
    g2 = mod_ref[0][:, 5 * D_MODEL:6 * D_MODEL]
    x2 = x1_ref[...] + g2 * ffn
    ms = jnp.mean(x2 * x2, axis=-1, keepdims=True)
    y = x2 * lax.rsqrt(ms + NORM_EPS) * nf_ref[...]
    i = pl.program_id(0)

    @pl.when(i < ctx_blocks)
    def _():
        oc_ref[...] = y

    @pl.when(i >= ctx_blocks)
    def _():
        ol_ref[...] = y


def _final(x1, ys, gates, mods3, mod_row, norm_f, nc):
    n, d = x1.shape
    nb = n // TM
    nbc = nc // TM
    row = lambda i: (i, 0)
    return pl.pallas_call(
        functools.partial(_final_kernel, ctx_blocks=nbc),
        out_shape=[jax.ShapeDtypeStruct((nc, d), F32), jax.ShapeDtypeStruct((n - nc, d), F32)],
        grid=(nb,),
        in_specs=[pl.BlockSpec((TM, d), row)]
                 + [pl.BlockSpec((TM * TILE_ROWS, LANES), functools.partial(lambda j, i: (j * nb + i, 0), j))
                    for j in range(TOP_K)]
                 + [pl.BlockSpec((TM, LANES), row),
                    pl.BlockSpec((1, 1, 6 * d), lambda i: (mod_row(i), 0, 0)),
                    pl.BlockSpec((1, d), lambda i: (0, 0))],
        out_specs=_token_specs(nbc, d),
        compiler_params=_cparams("arbitrary"),
        name="combine_final",
    )(x1, ys, ys, ys, ys, gates, mods3, norm_f.reshape(1, d))


def _rope_tables(t_ctx, t_lat):
    rows = np.arange(t_lat) // GRID_W
    cols = np.arange(t_lat) % GRID_W
    inv = ROPE_THETA ** (-np.arange(0, AXIS_DIM, 2, dtype=np.float32) / AXIS_DIM)
    inv = jnp.asarray(inv, F32)
    ang_r = jnp.asarray(rows, F32)[:, None] * inv[None, :]
    ang_c = jnp.asarray(cols, F32)[:, None] * inv[None, :]
    cos = jnp.concatenate([jnp.cos(ang_r)] * 2 + [jnp.cos(ang_c)] * 2, axis=1)
    sin = jnp.concatenate([-jnp.sin(ang_r), jnp.sin(ang_r), -jnp.sin(ang_c), jnp.sin(ang_c)], axis=1)
    cos = jnp.concatenate([jnp.ones((t_ctx, HEAD), F32), cos], axis=0)
    sin = jnp.concatenate([jnp.zeros((t_ctx, HEAD), F32), sin], axis=0)
    return jnp.concatenate([cos, cos], axis=1), jnp.concatenate([sin, sin], axis=1)


def kernel(x_prompt, x_sample, cache_k, cache_v, state_rwkv_fwd, state_rwkv_bwd, c, c_ctx, w_ada, b_ada, norm1, norm2, w_in, rw_w0, rw_w2, rw_a0, rw_a2, rw_g2, rw_kk, rw_ka, rw_rk, rw_ln_w, rw_ln_b, q_norm, k_norm, w_up_r, w_up_a, w_out, w_router, b_router, w_moe_in, b_moe_in, w_moe_out, b_moe_out, norm_f):
    depth = w_in.shape[0]
    assert depth == 1, "single trunk layer"
    bc, tc, d = x_prompt.shape
    bl, tl, _ = x_sample.shape
    nc, nl = bc * tc, bl * tl
    n = nc + nl
    assert d == D_MODEL and tc == TM and tl % TM == 0 and nc % tl == 0
    assert nc % TM_PROJ == 0 and tl % TM_PROJ == 0
    l = 0

    x_ctx, x_lat = x_prompt.reshape(nc, d), x_sample.reshape(nl, d)
    cond =jnp.concatenate([c_ctx[None, :], c, jnp.zeros((16 - 1 - bl, d), F32)], axis=0)
    mods3 = _adaln(cond, w_ada[l], b_ada[l]).reshape(16, 1, 6 * d)
    nbc = nc // TM
    per_seq = tl // TM
    mod_row = lambda i: jnp.where(i < nbc, 0, 1 + (i - nbc) // per_seq)
    tab_row = lambda i: jnp.where(i < nbc, 0, 1 + (i - nbc) % per_seq)

    z = dict(zip([name for name, _ in IN_SPLITS],
                 _proj_in(x_ctx, x_lat, mods3, norm1[l], w_in[l].astype(BF16), tl)))

    ones_bd = jnp.asarray(np.kron(np.eye(H_R), np.ones((HEAD, HEAD))), BF16)
    pad_lo = lambda w: jnp.concatenate([w, jnp.zeros_like(w)], axis=0)
    pad_hi = lambda w: jnp.concatenate([jnp.zeros_like(w), w], axis=0)
    prep_p = dict(
        w0=rw_w0[l], a0=rw_a0[l],
        w2=jnp.stack([pad_lo(rw_w2[l, 0]), pad_hi(rw_w2[l, 1])]).astype(BF16),
        a2=jnp.stack([pad_lo(rw_a2[l, 0]), pad_hi(rw_a2[l, 1])]).astype(BF16),
        g2=rw_g2[l].astype(BF16),
        kkw=rw_kk[l].reshape(1, D_R), kaw=rw_ka[l].reshape(1, D_R), rk=rw_rk[l].reshape(1, D_R),
        qn=jnp.tile(q_norm[l], N_HEADS).reshape(1, D_A), kn=jnp.tile(k_norm[l], N_KV).reshape(1, D_KV),
        ones=ones_bd,
        tri=jnp.asarray(np.stack([np.kron(np.eye(TM // CHUNK), np.tril(np.ones((CHUNK, CHUNK)))),
                                  np.kron(np.eye(TM // CHUNK), np.triu(np.ones((CHUNK, CHUNK))))]), BF16),
        chunk_ones=jnp.asarray(np.kron(np.eye(TM // CHUNK), np.ones((CHUNK, CHUNK))), BF16))
    cos_tab, sin_tab = _rope_tables(tc, tl)
    sc, gt, vb, bonus, g, qr, kn, kr = _prep(z, cos_tab, sin_tab, tab_row, prep_p)

    yf_c, yb_c, s_fin = _rwkv_scan(sc, gt, vb, jnp.zeros((2, bc, H_R, HEAD, HEAD), F32), 0, bc, tc)
    yf_l, yb_l, _ = _rwkv_scan(sc, gt, vb, jnp.stack([state_rwkv_fwd[:, l], state_rwkv_bwd[:, l]]),
                               nc, bl, tl)
    y_f = (yf_c.reshape(nc, D_R), yf_l.reshape(nl, D_R))
    y_b = (yb_c.reshape(nc, D_R), yb_l.reshape(nl, D_R))

    ya_ctx = _attention(qr, kr, z["va"], 0, bc, tc)
    ya_lat = _attention(qr, kr, z["va"], nc, bl, tl,
                        cache_k[:, l].reshape(bl, -1, D_KV), cache_v[:, l].reshape(bl, -1, D_KV))

    post_p = dict(
        lnw=rw_ln_w[l].reshape(1, D_R), lnb=rw_ln_b[l].reshape(1, D_R),
        wur=w_up_r[l].astype(BF16), wua=w_up_a[l].astype(BF16), wo=w_out[l].astype(BF16),
        n2=norm2[l].reshape(1, d),
        wr=jnp.pad(w_router[l], ((0, 0), (0, LANES - N_EXPERTS))),
        br=jnp.pad(b_router[l], (0, LANES - N_EXPERTS)).reshape(1, LANES),
        ones=ones_bd)
    x1, h2, top_idx, gates = _post(y_f, y_b, bonus, g, ya_ctx, ya_lat, z["gate_r"], z["gate_a"],
                                   x_ctx, x_lat, mods3, mod_row, post_p)

    src, dst, blk_e, blk_first, next_e, used = _route(top_idx[:, :TOP_K], n)
    ys = _moe(h2, src, dst, blk_e, blk_first, next_e, used,
              w_moe_in[l], b_moe_in[l], w_moe_out[l], b_moe_out[l])
    y_ctx, y_lat = _final(x1, ys, gates, mods3, mod_row, norm_f, nc)

    y_prompt = y_ctx.reshape(bc, tc, d)
    y_sample = y_lat.reshape(bl, tl, d)
    new_cache_k = kn[:nc].reshape(bc, 1, tc, N_KV, HEAD)
    new_cache_v = z["va"][:nc].reshape(bc, 1, tc, N_KV, HEAD)
    new_state_fwd = s_fin[0][:, None]
    new_state_bwd = s_fin[1][:, None]
    return (y_prompt, y_sample, new_cache_k, new_cache_v, new_state_fwd, new_state_bwd)
```

```python
import functools

import numpy as np
import jax
import jax.numpy as jnp
from jax import lax
from jax.experimental import pallas as pl
from jax.experimental.pallas import tpu as pltpu

F32 = jnp.float32
BF16 = jnp.bfloat16

D_MODEL = 1024
GRID_W = 64
NORM_EPS = 1e-6
HEAD = 64
H_R = 8
D_R = H_R * HEAD
DECAY_RANK = 64
AAA_RANK = 64
GATE_RANK = 128
GN_EPS = 64e-5
N_HEADS = 8
N_KV = 2
D_A = N_HEADS * HEAD
D_KV = N_KV * HEAD
AXIS_DIM = HEAD // 2
ROPE_THETA = 10000.0
N_EXPERTS = 32
TOP_K = 4
D_FF = D_MODEL
SWIGLU_ALPHA = 1.702
SWIGLU_LIMIT = 7.0
LORA_COLS = 2 * DECAY_RANK + 2 * AAA_RANK + GATE_RANK
IN_SPLITS = (("r", D_R), ("k", D_R), ("v", D_R), ("lora", LORA_COLS), ("q", D_A),
             ("ka", D_KV), ("va", D_KV), ("gate_r", D_MODEL), ("gate_a", D_MODEL))

LANES = 128
TM = 256
TM_PROJ = 512
CHUNK = 64
SCAN_OPS = 6
SCAN_GROUP = 2
MOE_ROWS = 256
VMEM_LIMIT = 56 * 1024 * 1024


def _cparams(*sem):
    return pltpu.CompilerParams(dimension_semantics=sem, vmem_limit_bytes=VMEM_LIMIT)


def _dot(a, b):
    return jnp.dot(a, b, preferred_element_type=F32)


def _dot_nt(a, b):
    return lax.dot_general(a, b, (((1,), (1,)), ((), ())), preferred_element_type=F32)


def _dot_tn(a, b):
    return lax.dot_general(a, b, (((0,), (0,)), ((), ())), preferred_element_type=F32)


def _split2(x):
    hi = x.astype(BF16)
    lo = (x - hi.astype(F32)).astype(BF16)
    return hi, lo


def _head_sum(x, ones_bd):
    return _dot(x.astype(BF16), ones_bd)


def _sigmoid(x):
    return 0.5 * jnp.tanh(0.5 * x) + 0.5


TILE_ROWS = D_MODEL // LANES


def _store_token_tiles(ref, x):
    rows = x.shape[0]
    for j in range(TILE_ROWS):
        ref[pl.ds(j, rows, stride=TILE_ROWS), :] = x[:, j * LANES:(j + 1) * LANES]


def _load_token_tiles(ref):
    rows = ref.shape[0] // TILE_ROWS
    return jnp.concatenate([ref[pl.ds(j, rows, stride=TILE_ROWS), :] for j in range(TILE_ROWS)], axis=1)


def _adaln_kernel(c_ref, w_ref, b_ref, o_ref):
    c = c_ref[...]
    s = (c * _sigmoid(c)).astype(BF16)
    o_ref[...] = _dot(s, w_ref[...].astype(BF16)) + b_ref[...]


def _adaln(cond, w_ada, b_ada):
    rows, d = cond.shape
    cols = w_ada.shape[1]
    tn = 512
    return pl.pallas_call(
        _adaln_kernel,
        out_shape=jax.ShapeDtypeStruct((rows, cols), F32),
        grid=(cols // tn,),
        in_specs=[pl.BlockSpec((rows, d), lambda j: (0, 0)),
                  pl.BlockSpec((d, tn), lambda j: (0, j)),
                  pl.BlockSpec((1, tn), lambda j: (0, j))],
        out_specs=pl.BlockSpec((rows, tn), lambda j: (0, j)),
        compiler_params=_cparams("arbitrary"),
        name="adaln",
    )(cond, w_ada, b_ada.reshape(1, cols))


def _token_specs(nbc, width, tm=TM):
    return [pl.BlockSpec((tm, width), lambda i: (jnp.minimum(i, nbc - 1), 0)),
            pl.BlockSpec((tm, width), lambda i: (jnp.maximum(i - nbc, 0), 0))]


def _proj_in_kernel(xc_ref, xl_ref, mod_ref, n1_ref, w_ref, *out_refs, ctx_blocks):
    x = jnp.where(pl.program_id(0) < ctx_blocks, xc_ref[...], xl_ref[...])
    ms = jnp.mean(x * x, axis=-1, keepdims=True)
    hn = x * lax.rsqrt(ms + NORM_EPS) * n1_ref[...]
    mod = mod_ref[0]
    sh1 = mod[:, 0:D_MODEL]
    sc1 = mod[:, D_MODEL:2 * D_MODEL]
    h = (hn * (1.0 + sc1) + sh1).astype(BF16)
    c0 = 0
    for ref, (_, width) in zip(out_refs, IN_SPLITS):
        ref[...] = _dot(h, w_ref[:, c0:c0 + width])
        c0 += width


def _proj_in(x_ctx, x_lat, mods3, norm1, w_in_bf, t_lat):
    d = x_ctx.shape[1]
    n = x_ctx.shape[0] + x_lat.shape[0]
    tm = TM_PROJ
    nbc = x_ctx.shape[0] // tm
    per_seq = t_lat // tm
    cols = w_in_bf.shape[1]
    row = lambda i: (i, 0)
    mod_row = lambda i: jnp.where(i < nbc, 0, 1 + (i - nbc) // per_seq)
    return pl.pallas_call(
        functools.partial(_proj_in_kernel, ctx_blocks=nbc),
        out_shape=[jax.ShapeDtypeStruct((n, width), F32) for _, width in IN_SPLITS],
        grid=(n // tm,),
        in_specs=_token_specs(nbc, d, tm) + [
                  pl.BlockSpec((1, 1, 6 * d), lambda i: (mod_row(i), 0, 0)),
                  pl.BlockSpec((1, d), lambda i: (0, 0)),
                  pl.BlockSpec((d, cols), lambda i: (0, 0))],
        out_specs=[pl.BlockSpec((tm, width), row) for _, width in IN_SPLITS],
        compiler_params=_cparams("arbitrary"),
        name="proj_in",
    )(x_ctx, x_lat, mods3, norm1.reshape(1, d), w_in_bf)


def _prep_kernel(r_ref, k_ref, v_ref, lora_ref, q_ref, ka_ref, cos_ref, sin_ref,
                 w0_ref, w2_ref, a0_ref, a2_ref, g2_ref, kkw_ref, kaw_ref, rk_ref,
                 qn_ref, kn_ref, ones_ref, tri_ref, chunk_ones_ref,
                 sc_o, gt_o, vb_o, bonus_o, g_o, qr_o, kn_o, kr_o):
    ones = ones_ref[...]
    chunk_ones = chunk_ones_ref[...]
    r = r_ref[...]
    k = k_ref[...]
    v = v_ref[...]
    vb_o[...] = v.astype(BF16)
    lora = lora_ref[...]
    kk = k * kkw_ref[...]
    kk = kk * lax.rsqrt(_head_sum(kk * kk, ones) + 1e-12)
    th = jnp.tanh(lora[:, 0:LANES]).astype(BF16)
    al = lora[:, LANES:2 * LANES].astype(BF16)
    kd_sum = jnp.zeros_like(k)
    for d in range(2):
        u = w0_ref[d:d + 1, :] + _dot(th, w2_ref[d])
        lw = -float(np.exp(-0.5)) * _sigmoid(u)
        a = _sigmoid(a0_ref[d:d + 1, :] + _dot(al, a2_ref[d]))
        kd = k * (1.0 + (a - 1.0) * kaw_ref[...])
        kd_sum = kd_sum + kd
        b = kk * a
        parts = _split2(lw)
        tri = tri_ref[d]
        L = _dot(tri, parts[0]) + _dot(tri, parts[1])
        Ltot = _dot(chunk_ones, parts[0]) + _dot(chunk_ones, parts[1])
        e_inv = jnp.exp(-L)
        e_rest = jnp.exp(Ltot - L)
        scan_ops = (kk * jnp.exp(L - lw), r * jnp.exp(L), b * e_inv, kd * e_inv, b * e_rest, kd * e_rest)
        for j, op in enumerate(scan_ops):
            sc_o[d, :, j * D_R:(j + 1) * D_R] = op.astype(BF16)
        g_tot = jnp.exp(Ltot)
        for j in range(TM // CHUNK):
            gt_o[d, j] = g_tot[j * CHUNK:j * CHUNK + 1, :]
    bonus_o[...] = _head_sum(r * kd_sum * rk_ref[...], ones) * v
    g_o[...] = _dot(_sigmoid(lora[:, 2 * LANES:3 * LANES]).astype(BF16), g2_ref[...])
    q = q_ref[...]
    qn = q * lax.rsqrt(_head_sum(q * q, ones) * (1.0 / HEAD) + NORM_EPS) * qn_ref[...]
    ka = ka_ref[...]
    ones_kv = ones[0:D_KV, 0:D_KV]
    kn = ka * lax.rsqrt(_head_sum(ka * ka, ones_kv) * (1.0 / HEAD) + NORM_EPS) * kn_ref[...]
    kn_o[...] = kn
    cos = cos_ref[...]
    sin = sin_ref[...]
    half = AXIS_DIM // 2

    def rope(t, cos_t, sin_t):
        width = t.shape[1]
        lane = lax.broadcasted_iota(jnp.int32, t.shape, 1)
        first = (lane % AXIS_DIM) < half
        swapped = jnp.where(first, pltpu.roll(t, width - half, 1), pltpu.roll(t, half, 1))
        return t * cos_t + swapped * sin_t

    cos4 = jnp.concatenate([cos] * (D_A // D_KV), axis=1)
    sin4 = jnp.concatenate([sin] * (D_A // D_KV), axis=1)
    qr_o[...] = (rope(qn, cos4, sin4) * (HEAD ** -0.5)).astype(BF16)
    kr_o[...] = rope(kn, cos, sin).astype(BF16)


def _prep(z, cos_tab, sin_tab, tab_row, p):
    n = z["r"].shape[0]
    row = lambda i: (i, 0)
    full = lambda *shape: pl.BlockSpec(shape, lambda i: (0,) * len(shape))
    tok = lambda width: pl.BlockSpec((TM, width), row)
    tab = pl.BlockSpec((TM, D_KV), lambda i: (tab_row(i), 0))
    cpb = TM // CHUNK
    return pl.pallas_call(
        _prep_kernel,
        out_shape=[jax.ShapeDtypeStruct((2, n, SCAN_OPS * D_R), BF16),
                   jax.ShapeDtypeStruct((2, n // CHUNK, 1, D_R), F32),
                   jax.ShapeDtypeStruct((n, D_R), BF16),
                   jax.ShapeDtypeStruct((n, D_R), F32),
                   jax.ShapeDtypeStruct((n, D_R), F32),
                   jax.ShapeDtypeStruct((n, D_A), BF16),
                   jax.ShapeDtypeStruct((n, D_KV), F32),
                   jax.ShapeDtypeStruct((n, D_KV), BF16)],
        grid=(n // TM,),
        in_specs=[tok(D_R), tok(D_R), tok(D_R), tok(LORA_COLS), tok(D_A), tok(D_KV), tab, tab,
                  full(2, D_R), full(2, LANES, D_R), full(2, D_R), full(2, LANES, D_R),
                  full(GATE_RANK, D_R), full(1, D_R), full(1, D_R), full(1, D_R),
                  full(1, D_A), full(1, D_KV), full(D_R, D_R), full(2, TM, TM), full(TM, TM)],
        out_specs=[pl.BlockSpec((2, TM, SCAN_OPS * D_R), lambda i: (0, i, 0)),
                   pl.BlockSpec((2, cpb, 1, D_R), lambda i: (0, i, 0, 0)),
                   tok(D_R), tok(D_R), tok(D_R), tok(D_A), tok(D_KV), tok(D_KV)],
        compiler_params=_cparams("arbitrary"),
        name="mixer_prep",
    )(z["r"], z["k"], z["v"], z["lora"], z["q"], z["ka"], cos_tab, sin_tab,
      p["w0"], p["w2"], p["a0"], p["a2"], p["g2"], p["kkw"], p["kaw"], p["rk"],
      p["qn"], p["kn"], p["ones"], p["tri"], p["chunk_ones"])


def _scan_kernel(scf_ref, scb_ref, vf_ref, vb_ref, gf_ref, gb_ref, s0_ref,
                 yf_ref, yb_ref, sfin_ref, state):
    c = pl.program_id(1)
    C = CHUNK
    pairs = D_R // LANES
    chains = [(g, d, p) for g in range(SCAN_GROUP) for d in range(2) for p in range(pairs)]

    lane = lax.broadcasted_iota(jnp.int32, (1, LANES), 1)
    low = lane < HEAD

    @pl.when(c == 0)
    def _():
        zero = jnp.zeros((HEAD, HEAD), F32)
        for i, (g, d, p) in enumerate(chains):
            top = jnp.concatenate([s0_ref[d, g, 2 * p], zero], axis=1)
            bot = jnp.concatenate([zero, s0_ref[d, g, 2 * p + 1]], axis=1)
            state[i] = jnp.concatenate([top, bot], axis=0)

    W = 2 * C
    t2 = lax.broadcasted_iota(jnp.int32, (C, W), 0)
    s2 = lax.broadcasted_iota(jnp.int32, (C, W), 1) % C
    strict = (s2 < t2, s2 > t2)
    incl = (s2 <= t2, s2 >= t2)
    eye2 = jnp.where(s2 == t2, 1.0, 0.0)
    first_mat = lax.broadcasted_iota(jnp.int32, (1, W), 1) < C
    r2 = lax.broadcasted_iota(jnp.int32, (LANES, LANES), 0)
    c2 = lax.broadcasted_iota(jnp.int32, (LANES, LANES), 1)
    same_head = (r2 // HEAD) == (c2 // HEAD)

    sc_refs = (scf_ref, scb_ref)
    v_refs = (vf_ref, vb_ref)
    g_refs = (gf_ref, gb_ref)
    y_refs = (yf_ref, yb_ref)

    def operand(chain, j):
        g, d, p = chain
        c0 = j * D_R + p * LANES
        return sc_refs[d][0, g, :, c0:c0 + LANES]

    def stack(x):
        first = low if x.shape[1] == LANES else first_mat
        zero = jnp.zeros_like(x)
        return jnp.concatenate([jnp.where(first, x, zero), jnp.where(first, zero, x)], axis=0)

    Qk = [operand(ch, 0) for ch in chains]
    Qr = [operand(ch, 1) for ch in chains]
    v_p = [v_refs[d][g, :, p * LANES:(p + 1) * LANES] for g, d, p in chains]
    v_st = [stack(x) for x in v_p]
    Z0 = [state[i] for i in range(len(chains))]
    Z0b = [z.astype(BF16) for z in Z0]
    m = [_dot_nt(jnp.concatenate([Qk[i], Qr[i]], axis=0),
                 jnp.concatenate([stack(operand(ch, 2)), stack(operand(ch, 3))], axis=0))
         for i, ch in enumerate(chains)]
    A = [jnp.where(strict[d], m[i][0:C, 0:W], 0.0) for i, (g, d, p) in enumerate(chains)]
    Bm = [jnp.where(strict[d], m[i][0:C, W:2 * W], 0.0).astype(BF16)
          for i, (g, d, p) in enumerate(chains)]
    Pb = [jnp.where(incl[d], m[i][C:2 * C, 0:W], 0.0).astype(BF16)
          for i, (g, d, p) in enumerate(chains)]
    Pk = [jnp.where(incl[d], m[i][C:2 * C, W:2 * W], 0.0).astype(BF16)
          for i, (g, d, p) in enumerate(chains)]
    BV = [_dot(Bm[i], v_st[i]) for i in range(len(chains))]
    T = [eye2 - a for a in A]
    Ap = [_dot(a.astype(BF16), stack(a.astype(BF16))).astype(BF16) for a in A]
    n = 2
    while 2 * n < C:
        R = [_dot(jnp.concatenate([Ap[i], T[i].astype(BF16)], axis=0), stack(Ap[i]))
             for i in range(len(chains))]
        Ap = [x[0:C].astype(BF16) for x in R]
        T = [T[i] + R[i][C:2 * C] for i in range(len(chains))]
        n *= 2
    T = [T[i] + _dot(T[i].astype(BF16), stack(Ap[i])) for i in range(len(chains))]
    WU = [_dot(T[i].astype(BF16),
               jnp.concatenate([stack(Qk[i]), stack(BV[i].astype(BF16))], axis=1))
          for i in range(len(chains))]
    Wq = [x[:, 0:LANES].astype(BF16) for x in WU]
    Uv = [x[:, LANES:2 * LANES] for x in WU]
    PW = [_dot(Pb[i], jnp.concatenate([stack(Wq[i]), stack(Uv[i].astype(BF16))], axis=1))
          for i in range(len(chains))]
    PkV = [_dot(Pk[i], v_st[i]) for i in range(len(chains))]
    U = [_dot_nt(Wq[i], Z0b[i]) + Uv[i] for i in range(len(chains))]
    Yq = [(Qr[i].astype(F32) - PW[i][:, 0:LANES]).astype(BF16) for i in range(len(chains))]
    for i, (g, d, p) in enumerate(chains):
        y_refs[d][g, :, p * LANES:(p + 1) * LANES] = (_dot_nt(Yq[i], Z0b[i]) + PkV[i]
                                                      - PW[i][:, LANES:2 * LANES])
    for i, (g, d, p) in enumerate(chains):
        lhs = jnp.concatenate([v_p[i], (-U[i]).astype(BF16)], axis=0)
        rhs = jnp.concatenate([operand(chains[i], 5), operand(chains[i], 4)], axis=0)
        upd = _dot_tn(lhs, rhs)
        g_tot = g_refs[d][0, g, 0][:, p * LANES:(p + 1) * LANES]
        state[i] = Z0[i] * g_tot + jnp.where(same_head, upd, 0.0)

    @pl.when(c == pl.num_programs(1) - 1)
    def _():
        for i, (g, d, p) in enumerate(chains):
            z = state[i]
            sfin_ref[d, g, 2 * p] = z[0:HEAD, 0:HEAD]
            sfin_ref[d, g, 2 * p + 1] = z[HEAD:2 * HEAD, HEAD:2 * HEAD]


def _rwkv_scan(sc, gt, vb, s0, row0, batch, t_len):
    n = vb.shape[0]
    nch = t_len // CHUNK
    assert row0 % (t_len * SCAN_GROUP) == 0 and batch % SCAN_GROUP == 0
    seq0 = row0 // (t_len * SCAN_GROUP)
    width = SCAN_OPS * D_R
    sc4 = sc.reshape(2, n // t_len, t_len, width)
    gt5 = gt.reshape(2, n // t_len, nch, 1, D_R)
    v3 = vb.reshape(n // t_len, t_len, D_R)
    G = SCAN_GROUP
    fwd = lambda c: c
    bwd = lambda c: nch - 1 - c
    sc_spec = lambda d, at: pl.BlockSpec((1, G, CHUNK, width), lambda s, c: (d, seq0 + s, at(c), 0))
    v_spec = lambda at: pl.BlockSpec((G, CHUNK, D_R), lambda s, c: (seq0 + s, at(c), 0))
    gt_spec = lambda d, at: pl.BlockSpec((1, G, 1, 1, D_R), lambda s, c: (d, seq0 + s, at(c), 0, 0))
    y_spec = lambda at: pl.BlockSpec((G, CHUNK, D_R), lambda s, c: (s, at(c), 0))
    st = pl.BlockSpec((2, G, H_R, HEAD, HEAD), lambda s, c: (0, s, 0, 0, 0))
    return pl.pallas_call(
        _scan_kernel,
        out_shape=[jax.ShapeDtypeStruct((batch, t_len, D_R), F32),
                   jax.ShapeDtypeStruct((batch, t_len, D_R), F32),
                   jax.ShapeDtypeStruct((2, batch, H_R, HEAD, HEAD), F32)],
        grid=(batch // G, nch),
        in_specs=[sc_spec(0, fwd), sc_spec(1, bwd), v_spec(fwd), v_spec(bwd),
                  gt_spec(0, fwd), gt_spec(1, bwd), st],
        out_specs=[y_spec(fwd), y_spec(bwd), st],
        scratch_shapes=[pltpu.VMEM((G * 2 * D_R // LANES, LANES, LANES), F32)],
        compiler_params=_cparams("arbitrary", "arbitrary"),
        name="rwkv_scan",
    )(sc4, sc4, v3, v3, gt5, gt5, s0)


def _attn_kernel(*refs, tq, with_cache):
    if with_cache:
        q_ref, k_ref, v_ref, ck_ref, cv_ref, o_ref = refs
        kx = jnp.concatenate([k_ref[...], ck_ref[0].astype(BF16)], axis=0)
        vx = jnp.concatenate([v_ref[...], cv_ref[0]], axis=0).astype(BF16)
    else:
        q_ref, k_ref, v_ref, o_ref = refs
        kx = k_ref[...]
        vx = v_ref[...].astype(BF16)
    lane = lax.broadcasted_iota(jnp.int32, (1, D_KV), 1)
    low = lane < HEAD
    k_sw = pltpu.roll(kx.astype(F32), HEAD, 1).astype(BF16)
    v_sw = pltpu.roll(vx.astype(F32), HEAD, 1).astype(BF16)
    group = N_HEADS // N_KV
    for p in range(D_A // LANES):
        g = (2 * p) // group
        keep = low if g == 0 else jnp.logical_not(low)
        kd = jnp.where(keep, kx, k_sw)
        vd = jnp.where(keep, vx, v_sw)
        qp = q_ref[:, p * LANES:(p + 1) * LANES]
        zero = jnp.zeros_like(qp)
        qs = jnp.concatenate([jnp.where(low, qp, zero), jnp.where(low, zero, qp)], axis=0)
        s = _dot_nt(qs, kd)
        mx = jnp.max(s, axis=-1, keepdims=True)
        e = jnp.exp(s - mx)
        den = jnp.sum(e, axis=-1, keepdims=True)
        o = _dot(e.astype(BF16), vd) / den
        o_ref[:, p * LANES:(p + 1) * LANES] = jnp.where(low, o[0:tq], o[tq:2 * tq]).astype(BF16)


def _attention(qr, kr, va, row0, batch, t_len, cache_k=None, cache_v=None):
    tq = min(t_len, 256)
    nq = t_len // tq
    qblk0 = row0 // tq
    sblk0 = row0 // t_len
    with_cache = cache_k is not None
    in_specs = [pl.BlockSpec((tq, D_A), lambda b, i: (qblk0 + b * nq + i, 0)),
                pl.BlockSpec((t_len, D_KV), lambda b, i: (sblk0 + b, 0)),
                pl.BlockSpec((t_len, D_KV), lambda b, i: (sblk0 + b, 0))]
    args = [qr, kr, va]
    if with_cache:
        past = cache_k.shape[1]
        in_specs += [pl.BlockSpec((1, past, D_KV), lambda b, i: (b, 0, 0))] * 2
        args += [cache_k, cache_v]
    return pl.pallas_call(
        functools.partial(_attn_kernel, tq=tq, with_cache=with_cache),
        out_shape=jax.ShapeDtypeStruct((batch * t_len, D_A), BF16),
        grid=(batch, nq),
        in_specs=in_specs,
        out_specs=pl.BlockSpec((tq, D_A), lambda b, i: (b * nq + i, 0)),
        compiler_params=_cparams("arbitrary", "arbitrary"),
        name="attention_cache" if with_cache else "attention_ctx",
    )(*args)


def _post_kernel(yfc_ref, yfl_ref, ybc_ref, ybl_ref, bonus_ref, g_ref, yac_ref, yal_ref, gr_ref, ga_ref,
                 xc_ref, xl_ref,
                 mod_ref, lnw_ref, lnb_ref, wur_ref, wua_ref, wo_ref, n2_ref, wr_ref, br_ref, ones_ref,
                 x1_o, h2_o, idx_o, gate_o, *, ctx_blocks):
    ones = ones_ref[...]
    is_ctx = pl.program_id(0) < ctx_blocks
    y_a = jnp.where(is_ctx, yac_ref[...], yal_ref[...])
    x = jnp.where(is_ctx, xc_ref[...], xl_ref[...])
    y = jnp.where(is_ctx, yfc_ref[...] + ybc_ref[...], yfl_ref[...] + ybl_ref[...])
    mu = _head_sum(y, ones) * (1.0 / HEAD)
    yc = y - mu
    var = _head_sum(yc * yc, ones) * (1.0 / HEAD)
    yn = yc * lax.rsqrt(var + GN_EPS) * lnw_ref[...] + lnb_ref[...] + bonus_ref[...]
    y_r = (yn * g_ref[...]).astype(BF16)
    merged = (_sigmoid(gr_ref[...]) * _dot(y_r, wur_ref[...])
              + _sigmoid(ga_ref[...]) * _dot(y_a, wua_ref[...]))
    mix = _dot(merged.astype(BF16), wo_ref[...])
    mod = mod_ref[0]
    g1 = mod[:, 2 * D_MODEL:3 * D_MODEL]
    sh2 = mod[:, 3 * D_MODEL:4 * D_MODEL]
    sc2 = mod[:, 4 * D_MODEL:5 * D_MODEL]
    x1 = x + g1 * mix
    x1_o[...] = x1
    ms = jnp.mean(x1 * x1, axis=-1, keepdims=True)
    h2 = x1 * lax.rsqrt(ms + NORM_EPS) * n2_ref[...] * (1.0 + sc2) + sh2
    _store_token_tiles(h2_o, h2)
    hh, hl = _split2(h2)
    wh, wl = _split2(wr_ref[...])
    hi_terms = _dot(hh, jnp.concatenate([wh, wl], axis=1))
    logits = hi_terms[:, 0:LANES] + hi_terms[:, LANES:2 * LANES] + _dot(hl, wh) + br_ref[...]
    lane_i = lax.broadcasted_iota(jnp.int32, logits.shape, 1)
    lane = lane_i.astype(F32)
    neg = jnp.float32(-jnp.inf)
    cur = jnp.where(lane_i < N_EXPERTS, logits, neg)
    vals, idxs = [], []
    for _ in range(TOP_K):
        mx = jnp.max(cur, axis=-1, keepdims=True)
        ix = jnp.min(jnp.where(cur == mx, lane, float(LANES)), axis=-1, keepdims=True)
        vals.append(mx)
        idxs.append(ix)
        cur = jnp.where(lane == ix, neg, cur)
    es = [jnp.exp(val - vals[0]) for val in vals]
    den = es[0] + es[1] + es[2] + es[3]
    idx_out = jnp.zeros(logits.shape, jnp.int32)
    gate_out = jnp.zeros(logits.shape, F32)
    for j in range(TOP_K):
        idx_out = jnp.where(lane_i == j, idxs[j].astype(jnp.int32), idx_out)
        gate_out = jnp.where(lane_i == j, es[j] / den, gate_out)
    idx_o[...] = idx_out
    gate_o[...] = gate_out


def _post(y_f, y_b, bonus, g, ya_ctx, ya_lat, gate_r, gate_a, x_ctx, x_lat, mods3, mod_row, p):
    n = x_ctx.shape[0] + x_lat.shape[0]
    nbc = ya_ctx.shape[0] // TM
    row = lambda i: (i, 0)
    full = lambda *shape: pl.BlockSpec(shape, lambda i: (0,) * len(shape))
    tok = lambda width: pl.BlockSpec((TM, width), row)
    return pl.pallas_call(
        functools.partial(_post_kernel, ctx_blocks=nbc),
        out_shape=[jax.ShapeDtypeStruct((n, D_MODEL), F32),
                   jax.ShapeDtypeStruct((n * TILE_ROWS, LANES), F32),
                   jax.ShapeDtypeStruct((n, LANES), jnp.int32),
                   jax.ShapeDtypeStruct((n, LANES), F32)],
        grid=(n // TM,),
        in_specs=_token_specs(nbc, D_R) + _token_specs(nbc, D_R) + [tok(D_R), tok(D_R)]
                 + _token_specs(nbc, D_A) + [tok(D_MODEL), tok(D_MODEL)] + _token_specs(nbc, D_MODEL) + [
                  pl.BlockSpec((1, 1, 6 * D_MODEL), lambda i: (mod_row(i), 0, 0)),
                  full(1, D_R), full(1, D_R), full(D_R, D_MODEL), full(D_A, D_MODEL),
                  full(D_MODEL, D_MODEL), full(1, D_MODEL), full(D_MODEL, LANES), full(1, LANES),
                  full(D_R, D_R)],
        out_specs=[tok(D_MODEL), pl.BlockSpec((TM * TILE_ROWS, LANES), row), tok(LANES), tok(LANES)],
        compiler_params=_cparams("arbitrary"),
        name="mixer_post",
    )(*y_f, *y_b, bonus, g, ya_ctx, ya_lat, gate_r, gate_a, x_ctx, x_lat, mods3,
      p["lnw"], p["lnb"], p["wur"], p["wua"], p["wo"], p["n2"], p["wr"], p["br"], p["ones"])


def _moe_kernel(blk_e_ref, blk_first_ref, next_e_ref, used_ref,
                src0_ref, src_next_ref, dst_prev_ref, h_hbm, w1_hbm, b1_ref, w2_hbm, b2_ref,
                out_hbm, w1f, w2f, w1b, w2b, xs, ys, sem_g, sem_s, sem_t, sem_w, *, n_rows):
    b = pl.program_id(0)
    used = used_ref[0]
    R = MOE_ROWS
    T = TILE_ROWS

    def tile(ref, i):
        return ref.at[pl.ds(pl.multiple_of(i * T, T), T)]

    def gather(table_ref, buf):
        for i in range(R):
            pltpu.make_async_copy(tile(h_hbm, table_ref[0, 0, i]), tile(xs.at[buf], i),
                                  sem_g.at[buf]).start(priority=i % 2)

    def scatter_prev(buf):
        for i in range(R):
            pltpu.make_async_copy(tile(ys.at[buf], i), tile(out_hbm, dst_prev_ref[0, 0, i]),
                                  sem_s.at[buf]).start(priority=i % 2)

    def fetch_weights(e):
        pltpu.make_async_copy(w1_hbm.at[e], w1f, sem_w.at[0]).start()
        pltpu.make_async_copy(w2_hbm.at[e], w2f, sem_w.at[1]).start()

    @pl.when(b == 0)
    def _():
        ys[...] = jnp.zeros_like(ys)
        for s in range(2):
            band = pltpu.make_async_copy(ys.at[s], out_hbm.at[pl.ds((n_rows + s * R) * T, R * T)], sem_t)
            band.start()
            band.wait()
        gather(src0_ref, 0)
        fetch_weights(blk_e_ref[0])

    @pl.when(jnp.logical_and(blk_first_ref[b] == 1, b < used))
    def _():
        pltpu.make_async_copy(w1_hbm.at[0], w1f, sem_w.at[0]).wait()
        pltpu.make_async_copy(w2_hbm.at[0], w2f, sem_w.at[1]).wait()
        w1b[...] = w1f[...].astype(BF16)
        w2b[...] = w2f[...].astype(BF16)

        @pl.when(next_e_ref[b] >= 0)
        def _():
            fetch_weights(next_e_ref[b])

    def step(p):
        @pl.when(b <= used)
        def _():
            pltpu.make_async_copy(h_hbm.at[pl.ds(0, R * T)], xs.at[p], sem_g.at[p]).wait()

        @pl.when(jnp.logical_and(b >= 1, b <= used + 1))
        def _():
            pltpu.make_async_copy(ys.at[p], out_hbm.at[pl.ds(0, R * T)], sem_s.at[p]).wait()

        @pl.when(b < used)
        def _():
            gather(src_next_ref, 1 - p)
            scatter_prev(1 - p)
            hb = _dot(_load_token_tiles(xs.at[p]).astype(BF16), w1b[...]) + b1_ref[0]
            glu = jnp.minimum(hb[:, 0:D_FF], SWIGLU_LIMIT)
            lin = jnp.clip(hb[:, D_FF:2 * D_FF], -SWIGLU_LIMIT, SWIGLU_LIMIT)
            act = glu * _sigmoid(SWIGLU_ALPHA * glu) * (lin + 1.0)
            _store_token_tiles(ys.at[p], _dot(act.astype(BF16), w2b[...]) + b2_ref[0])

        @pl.when(b == used)
        def _():
            scatter_prev(1 - p)

    for p in range(2):
        pl.when(b % 2 == p)(functools.partial(step, p))


def _moe(h2, src, dst_prev, blk_e, blk_first, next_e, used, w1, b1, w2, b2):
    d = D_MODEL
    n = h2.shape[0] // TILE_ROWS
    nb = src.shape[0]
    steps = nb + 2
    cur = lambda b, *_: (jnp.minimum(b, nb), 0, 0)
    nxt = lambda b, *_: (jnp.minimum(b + 1, nb - 1), 0, 0)
    table = lambda index_map: pl.BlockSpec((1, 1, MOE_ROWS), index_map, memory_space=pltpu.SMEM)
    expert = lambda b, blk_e, *_: (blk_e[jnp.minimum(b, nb - 1)], 0, 0)
    grid_spec = pltpu.PrefetchScalarGridSpec(
        num_scalar_prefetch=4,
        grid=(steps,),
        in_specs=[table(lambda b, *_: (0, 0, 0)), table(nxt), table(cur),
                  pl.BlockSpec(memory_space=pl.ANY),
                  pl.BlockSpec(memory_space=pl.ANY),
                  pl.BlockSpec((1, 1, 2 * D_FF), expert),
                  pl.BlockSpec(memory_space=pl.ANY),
                  pl.BlockSpec((1, 1, d), expert)],
        out_specs=pl.BlockSpec(memory_space=pl.ANY),
        scratch_shapes=[pltpu.VMEM((d, 2 * D_FF), F32),
                        pltpu.VMEM((D_FF, d), F32),
                        pltpu.VMEM((d, 2 * D_FF), BF16),
                        pltpu.VMEM((D_FF, d), BF16),
                        pltpu.VMEM((2, MOE_ROWS * TILE_ROWS, LANES), F32),
                        pltpu.VMEM((2, MOE_ROWS * TILE_ROWS, LANES), F32),
                        pltpu.SemaphoreType.DMA((2,)),
                        pltpu.SemaphoreType.DMA((2,)),
                        pltpu.SemaphoreType.DMA(()),
                        pltpu.SemaphoreType.DMA((2,))])
    tab3 = lambda t: t.reshape(-1, 1, MOE_ROWS)
    pad = lambda t: jnp.concatenate([t, jnp.zeros((steps - nb,), jnp.int32)])
    return pl.pallas_call(
        functools.partial(_moe_kernel, n_rows=n * TOP_K),
        out_shape=jax.ShapeDtypeStruct(((n * TOP_K + 2 * MOE_ROWS) * TILE_ROWS, LANES), F32),
        grid_spec=grid_spec,
        compiler_params=pltpu.CompilerParams(dimension_semantics=("arbitrary",),
                                             vmem_limit_bytes=VMEM_LIMIT, has_side_effects=True),
        name="moe_experts",
    )(blk_e, pad(blk_first), next_e, used, tab3(src), tab3(src), tab3(dst_prev), h2, w1,
      b1.reshape(N_EXPERTS, 1, 2 * D_FF), w2, b2.reshape(N_EXPERTS, 1, d))


def _route(top_idx, n):
    nk = n * TOP_K
    cap = nk + N_EXPERTS * MOE_ROWS
    nb = cap // MOE_ROWS
    flat_e = top_idx.reshape(nk)
    order = jnp.argsort(flat_e, stable=True).astype(jnp.int32)
    counts = jnp.sum((flat_e[:, None] == jnp.arange(N_EXPERTS, dtype=jnp.int32)[None, :])
                     .astype(jnp.int32), axis=0)
    start = jnp.cumsum(counts) - counts
    n_blk = (counts + MOE_ROWS - 1) // MOE_ROWS
    blk_end = jnp.cumsum(n_blk)
    first_blk = blk_end - n_blk
    experts = jnp.arange(N_EXPERTS, dtype=jnp.int32)
    blk = jnp.arange(nb, dtype=jnp.int32)
    blk_e = jnp.minimum(jnp.sum((blk[:, None] >= blk_end[None, :]).astype(jnp.int32), axis=1),
                        N_EXPERTS - 1)
    onehot = (blk_e[:, None] == experts[None, :]).astype(jnp.int32)
    pick = lambda t: jnp.sum(onehot * t[None, :], axis=1)
    blk_first_of_e, blk_count, blk_start = pick(first_blk), pick(counts), pick(start)
    blk_first = (blk == blk_first_of_e).astype(jnp.int32)
    blk_after = pick(blk_end)
    next_e = jnp.where(blk_after < blk_end[-1], blk_e[jnp.minimum(blk_after, nb - 1)], -1).astype(jnp.int32)
    row = jnp.arange(MOE_ROWS, dtype=jnp.int32)[None, :]
    within = (blk - blk_first_of_e)[:, None] * MOE_ROWS + row
    valid = jnp.logical_and(within < blk_count[:, None], (blk < blk_end[-1])[:, None])
    j = order[jnp.clip(blk_start[:, None] + within, 0, nk - 1)]
    src = jnp.where(valid, j // TOP_K, 0).astype(jnp.int32)
    scratch_row = nk + (blk % 2)[:, None] * MOE_ROWS + row
    dst = jnp.where(valid, (j % TOP_K) * n + j // TOP_K, scratch_row).astype(jnp.int32)
    dst_prev = jnp.concatenate([nk + MOE_ROWS + row, dst], axis=0)
    return (src, dst_prev, blk_e.astype(jnp.int32), blk_first, next_e,
            blk_end[-1:].astype(jnp.int32))


def _final_kernel(x1_ref, y0_ref, y1_ref, y2_ref, y3_ref, gate_ref, mod_ref, nf_ref, oc_ref, ol_ref,
                  *, ctx_blocks):
    gates = gate_ref[...]
    ffn = gates[:, 0:1] * _load_token_tiles(y0_ref)
    for j, ref in enumerate((y1_ref, y2_ref, y3_ref), start=1):
        ffn = ffn + gates[:, j:j + 1] * _load_token_tiles(ref)
    g2 = mod_ref[0][:, 5 * D_MODEL:6 * D_MODEL]
    x2 = x1_ref[...] + g2 * ffn
    ms = jnp.mean(x2 * x2, axis=-1, keepdims=True)
    y = x2 * lax.rsqrt(ms + NORM_EPS) * nf_ref[...]
    i = pl.program_id(0)

    @pl.when(i < ctx_blocks)
    def _():
        oc_ref[...] = y

    @pl.when(i >= ctx_blocks)
    def _():
        ol_ref[...] = y


def _final(x1, ys, gates, mods3, mod_row, norm_f, nc):
    n, d = x1.shape
    nb = n // TM
    nbc = nc // TM
    row = lambda i: (i, 0)
    return pl.pallas_call(
        functools.partial(_final_kernel, ctx_blocks=nbc),
        out_shape=[jax.ShapeDtypeStruct((nc, d), F32), jax.ShapeDtypeStruct((n - nc, d), F32)],
        grid=(nb,),
        in_specs=[pl.BlockSpec((TM, d), row)]
                 + [pl.BlockSpec((TM * TILE_ROWS, LANES), functools.partial(lambda j, i: (j * nb + i, 0), j))
                    for j in range(TOP_K)]
                 + [pl.BlockSpec((TM, LANES), row),
                    pl.BlockSpec((1, 1, 6 * d), lambda i: (mod_row(i), 0, 0)),
                    pl.BlockSpec((1, d), lambda i: (0, 0))],
        out_specs=_token_specs(nbc, d),
        compiler_params=_cparams("arbitrary"),
        name="combine_final",
    )(x1, ys, ys, ys, ys, gates, mods3, norm_f.reshape(1, d))


def _rope_tables(t_ctx, t_lat):
    rows = np.arange(t_lat) // GRID_W
    cols = np.arange(t_lat) % GRID_W
    inv = ROPE_THETA ** (-np.arange(0, AXIS_DIM, 2, dtype=np.float32) / AXIS_DIM)
    inv = jnp.asarray(inv, F32)
    ang_r = jnp.asarray(rows, F32)[:, None] * inv[None, :]
    ang_c = jnp.asarray(cols, F32)[:, None] * inv[None, :]
    cos = jnp.concatenate([jnp.cos(ang_r)] * 2 + [jnp.cos(ang_c)] * 2, axis=1)
    sin = jnp.concatenate([-jnp.sin(ang_r), jnp.sin(ang_r), -jnp.sin(ang_c), jnp.sin(ang_c)], axis=1)
    cos = jnp.concatenate([jnp.ones((t_ctx, HEAD), F32), cos], axis=0)
    sin = jnp.concatenate([jnp.zeros((t_ctx, HEAD), F32), sin], axis=0)
    return jnp.concatenate([cos, cos], axis=1), jnp.concatenate([sin, sin], axis=1)


def kernel(x_prompt, x_sample, cache_k, cache_v, state_rwkv_fwd, state_rwkv_bwd, c, c_ctx, w_ada, b_ada, norm1, norm2, w_in, rw_w0, rw_w2, rw_a0, rw_a2, rw_g2, rw_kk, rw_ka, rw_rk, rw_ln_w, rw_ln_b, q_norm, k_norm, w_up_r, w_up_a, w_out, w_router, b_router, w_moe_in, b_moe_in, w_moe_out, b_moe_out, norm_f):
    depth = w_in.shape[0]
    assert depth == 1, "single trunk layer"
    bc, tc, d = x_prompt.shape
    bl, tl, _ = x_sample.shape
    nc, nl = bc * tc, bl * tl
    n = nc + nl
    assert d == D_MODEL and tc == TM and tl % TM == 0 and nc % tl == 0
    assert nc % TM_PROJ == 0 and tl % TM_PROJ == 0
    l = 0

    x_ctx, x_lat = x_prompt.reshape(nc, d), x_sample.reshape(nl, d)
    cond =jnp.concatenate([c_ctx[None, :], c, jnp.zeros((16 - 1 - bl, d), F32)], axis=0)
    mods3 = _adaln(cond, w_ada[l], b_ada[l]).reshape(16, 1, 6 * d)
    nbc = nc // TM
    per_seq = tl // TM
    mod_row = lambda i: jnp.where(i < nbc, 0, 1 + (i - nbc) // per_seq)
    tab_row = lambda i: jnp.where(i < nbc, 0, 1 + (i - nbc) % per_seq)

    z = dict(zip([name for name, _ in IN_SPLITS],
                 _proj_in(x_ctx, x_lat, mods3, norm1[l], w_in[l].astype(BF16), tl)))

    ones_bd = jnp.asarray(np.kron(np.eye(H_R), np.ones((HEAD, HEAD))), BF16)
    pad_lo = lambda w: jnp.concatenate([w, jnp.zeros_like(w)], axis=0)
    pad_hi = lambda w: jnp.concatenate([jnp.zeros_like(w), w], axis=0)
    prep_p = dict(
        w0=rw_w0[l], a0=rw_a0[l],
        w2=jnp.stack([pad_lo(rw_w2[l, 0]), pad_hi(rw_w2[l, 1])]).astype(BF16),
        a2=jnp.stack([pad_lo(rw_a2[l, 0]), pad_hi(rw_a2[l, 1])]).astype(BF16),
        g2=rw_g2[l].astype(BF16),
        kkw=rw_kk[l].reshape(1, D_R), kaw=rw_ka[l].reshape(1, D_R), rk=rw_rk[l].reshape(1, D_R),
        qn=jnp.tile(q_norm[l], N_HEADS).reshape(1, D_A), kn=jnp.tile(k_norm[l], N_KV).reshape(1, D_KV),
        ones=ones_bd,
        tri=jnp.asarray(np.stack([np.kron(np.eye(TM // CHUNK), np.tril(np.ones((CHUNK, CHUNK)))),
                                  np.kron(np.eye(TM // CHUNK), np.triu(np.ones((CHUNK, CHUNK))))]), BF16),
        chunk_ones=jnp.asarray(np.kron(np.eye(TM // CHUNK), np.ones((CHUNK, CHUNK))), BF16))
    cos_tab, sin_tab = _rope_tables(tc, tl)
    sc, gt, vb, bonus, g, qr, kn, kr = _prep(z, cos_tab, sin_tab, tab_row, prep_p)

    yf_c, yb_c, s_fin = _rwkv_scan(sc, gt, vb, jnp.zeros((2, bc, H_R, HEAD, HEAD), F32), 0, bc, tc)
    yf_l, yb_l, _ = _rwkv_scan(sc, gt, vb, jnp.stack([state_rwkv_fwd[:, l], state_rwkv_bwd[:, l]]),
                               nc, bl, tl)
    y_f = (yf_c.reshape(nc, D_R), yf_l.reshape(nl, D_R))
    y_b = (yb_c.reshape(nc, D_R), yb_l.reshape(nl, D_R))

    ya_ctx = _attention(qr, kr, z["va"], 0, bc, tc)
    ya_lat = _attention(qr, kr, z["va"], nc, bl, tl,
                        cache_k[:, l].reshape(bl, -1, D_KV), cache_v[:, l].reshape(bl, -1, D_KV))

    post_p = dict(
        lnw=rw_ln_w[l].reshape(1, D_R), lnb=rw_ln_b[l].reshape(1, D_R),
        wur=w_up_r[l].astype(BF16), wua=w_up_a[l].astype(BF16), wo=w_out[l].astype(BF16),
        n2=norm2[l].reshape(1, d),
        wr=jnp.pad(w_router[l], ((0, 0), (0, LANES - N_EXPERTS))),
        br=jnp.pad(b_router[l], (0, LANES - N_EXPERTS)).reshape(1, LANES),
        ones=ones_bd)
    x1, h2, top_idx, gates = _post(y_f, y_b, bonus, g, ya_ctx, ya_lat, z["gate_r"], z["gate_a"],
                                   x_ctx, x_lat, mods3, mod_row, post_p)

    src, dst, blk_e, blk_first, next_e, used = _route(top_idx[:, :TOP_K], n)
    ys = _moe(h2, src, dst, blk_e, blk_first, next_e, used,
              w_moe_in[l], b_moe_in[l], w_moe_out[l], b_moe_out[l])
    y_ctx, y_lat = _final(x1, ys, gates, mods3, mod_row, norm_f, nc)

    y_prompt = y_ctx.reshape(bc, tc, d)
    y_sample = y_lat.reshape(bl, tl, d)
    new_cache_k = kn[:nc].reshape(bc, 1, tc, N_KV, HEAD)
    new_cache_v = z["va"][:nc].reshape(bc, 1, tc, N_KV, HEAD)
    new_state_fwd = s_fin[0][:, None]
    new_state_bwd = s_fin[1][:, None]
    return (y_prompt, y_sample, new_cache_k, new_cache_v, new_state_fwd, new_state_bwd)
```

```python
import functools

import numpy as np
import jax
import jax.numpy as jnp
from jax import lax
from jax.experimental import pallas as pl
from jax.experimental.pallas import tpu as pltpu

F32 = jnp.float32
BF16 = jnp.bfloat16

D_MODEL = 1024
GRID_W = 64
NORM_EPS = 1e-6
HEAD = 64
H_R = 8
D_R = H_R * HEAD
DECAY_RANK = 64
AAA_RANK = 64
GATE_RANK = 128
GN_EPS = 64e-5
N_HEADS = 8
N_KV = 2
D_A = N_HEADS * HEAD
D_KV = N_KV * HEAD
AXIS_DIM = HEAD // 2
ROPE_THETA = 10000.0
N_EXPERTS = 32
TOP_K = 4
D_FF = D_MODEL
SWIGLU_ALPHA = 1.702
SWIGLU_LIMIT = 7.0
LORA_COLS = 2 * DECAY_RANK + 2 * AAA_RANK + GATE_RANK
IN_SPLITS = (("r", D_R), ("k", D_R), ("v", D_R), ("lora", LORA_COLS), ("q", D_A),
             ("ka", D_KV), ("va", D_KV), ("gate_r", D_MODEL), ("gate_a", D_MODEL))

LANES = 128
TM = 256
TM_PROJ = 512
CHUNK = 64
SCAN_OPS = 6
SCAN_GROUP = 4
MOE_ROWS = 256
VMEM_LIMIT = 56 * 1024 * 1024


def _cparams(*sem):
    return pltpu.CompilerParams(dimension_semantics=sem, vmem_limit_bytes=VMEM_LIMIT)


def _dot(a, b):
    return jnp.dot(a, b, preferred_element_type=F32)


def _dot_nt(a, b):
    return lax.dot_general(a, b, (((1,), (1,)), ((), ())), preferred_element_type=F32)


def _dot_tn(a, b):
    return lax.dot_general(a, b, (((0,), (0,)), ((), ())), preferred_element_type=F32)


def _split2(x):
    hi = x.astype(BF16)
    lo = (x - hi.astype(F32)).astype(BF16)
    return hi, lo


def _head_sum(x, ones_bd):
    return _dot(x.astype(BF16), ones_bd)


def _sigmoid(x):
    return 0.5 * jnp.tanh(0.5 * x) + 0.5


TILE_ROWS = D_MODEL // LANES


def _store_token_tiles(ref, x):
    rows = x.shape[0]
    for j in range(TILE_ROWS):
        ref[pl.ds(j, rows, stride=TILE_ROWS), :] = x[:, j * LANES:(j + 1) * LANES]


def _load_token_tiles(ref):
    rows = ref.shape[0] // TILE_ROWS
    return jnp.concatenate([ref[pl.ds(j, rows, stride=TILE_ROWS), :] for j in range(TILE_ROWS)], axis=1)


def _adaln_kernel(c_ref, w_ref, b_ref, o_ref):
    c = c_ref[...]
    s = (c * _sigmoid(c)).astype(BF16)
    o_ref[...] = _dot(s, w_ref[...].astype(BF16)) + b_ref[...]


def _adaln(cond, w_ada, b_ada):
    rows, d = cond.shape
    cols = w_ada.shape[1]
    tn = 512
    return pl.pallas_call(
        _adaln_kernel,
        out_shape=jax.ShapeDtypeStruct((rows, cols), F32),
        grid=(cols // tn,),
        in_specs=[pl.BlockSpec((rows, d), lambda j: (0, 0)),
                  pl.BlockSpec((d, tn), lambda j: (0, j)),
                  pl.BlockSpec((1, tn), lambda j: (0, j))],
        out_specs=pl.BlockSpec((rows, tn), lambda j: (0, j)),
        compiler_params=_cparams("arbitrary"),
        name="adaln",
    )(cond, w_ada, b_ada.reshape(1, cols))


def _token_specs(nbc, width, tm=TM):
    return [pl.BlockSpec((tm, width), lambda i: (jnp.minimum(i, nbc - 1), 0)),
            pl.BlockSpec((tm, width), lambda i: (jnp.maximum(i - nbc, 0), 0))]


def _proj_in_kernel(xc_ref, xl_ref, mod_ref, n1_ref, w_ref, *out_refs, ctx_blocks):
    x = jnp.where(pl.program_id(0) < ctx_blocks, xc_ref[...], xl_ref[...])
    ms = jnp.mean(x * x, axis=-1, keepdims=True)
    hn = x * lax.rsqrt(ms + NORM_EPS) * n1_ref[...]
    mod = mod_ref[0]
    sh1 = mod[:, 0:D_MODEL]
    sc1 = mod[:, D_MODEL:2 * D_MODEL]
    h = (hn * (1.0 + sc1) + sh1).astype(BF16)
    c0 = 0
    for ref, (_, width) in zip(out_refs, IN_SPLITS):
        ref[...] = _dot(h, w_ref[:, c0:c0 + width])
        c0 += width


def _proj_in(x_ctx, x_lat, mods3, norm1, w_in_bf, t_lat):
    d = x_ctx.shape[1]
    n = x_ctx.shape[0] + x_lat.shape[0]
    tm = TM_PROJ
    nbc = x_ctx.shape[0] // tm
    per_seq = t_lat // tm
    cols = w_in_bf.shape[1]
    row = lambda i: (i, 0)
    mod_row = lambda i: jnp.where(i < nbc, 0, 1 + (i - nbc) // per_seq)
    return pl.pallas_call(
        functools.partial(_proj_in_kernel, ctx_blocks=nbc),
        out_shape=[jax.ShapeDtypeStruct((n, width), F32) for _, width in IN_SPLITS],
        grid=(n // tm,),
        in_specs=_token_specs(nbc, d, tm) + [
                  pl.BlockSpec((1, 1, 6 * d), lambda i: (mod_row(i), 0, 0)),
                  pl.BlockSpec((1, d), lambda i: (0, 0)),
                  pl.BlockSpec((d, cols), lambda i: (0, 0))],
        out_specs=[pl.BlockSpec((tm, width), row) for _, width in IN_SPLITS],
        compiler_params=_cparams("arbitrary"),
        name="proj_in",
    )(x_ctx, x_lat, mods3, norm1.reshape(1, d), w_in_bf)


def _prep_kernel(r_ref, k_ref, v_ref, lora_ref, q_ref, ka_ref, cos_ref, sin_ref,
                 w0_ref, w2_ref, a0_ref, a2_ref, g2_ref, kkw_ref, kaw_ref, rk_ref,
                 qn_ref, kn_ref, ones_ref, tri_ref, chunk_ones_ref,
                 sc_o, gt_o, vb_o, bonus_o, g_o, qr_o, kn_o, kr_o):
    ones = ones_ref[...]
    chunk_ones = chunk_ones_ref[...]
    r = r_ref[...]
    k = k_ref[...]
    v = v_ref[...]
    vb_o[...] = v.astype(BF16)
    lora = lora_ref[...]
    kk = k * kkw_ref[...]
    kk = kk * lax.rsqrt(_head_sum(kk * kk, ones) + 1e-12)
    th = jnp.tanh(lora[:, 0:LANES]).astype(BF16)
    al = lora[:, LANES:2 * LANES].astype(BF16)
    kd_sum = jnp.zeros_like(k)
    for d in range(2):
        u = w0_ref[d:d + 1, :] + _dot(th, w2_ref[d])
        lw = -float(np.exp(-0.5)) * _sigmoid(u)
        a = _sigmoid(a0_ref[d:d + 1, :] + _dot(al, a2_ref[d]))
        kd = k * (1.0 + (a - 1.0) * kaw_ref[...])
        kd_sum = kd_sum + kd
        b = kk * a
        parts = _split2(lw)
        tri = tri_ref[d]
        L = _dot(tri, parts[0]) + _dot(tri, parts[1])
        Ltot = _dot(chunk_ones, parts[0]) + _dot(chunk_ones, parts[1])
        e_inv = jnp.exp(-L)
        e_rest = jnp.exp(Ltot - L)
        scan_ops = (kk * jnp.exp(L - lw), r * jnp.exp(L), b * e_inv, kd * e_inv, b * e_rest, kd * e_rest)
        for j, op in enumerate(scan_ops):
            sc_o[d, :, j * D_R:(j + 1) * D_R] = op.astype(BF16)
        g_tot = jnp.exp(Ltot)
        for j in range(TM // CHUNK):
            gt_o[d, j] = g_tot[j * CHUNK:j * CHUNK + 1, :]
    bonus_o[...] = _head_sum(r * kd_sum * rk_ref[...], ones) * v
    g_o[...] = _dot(_sigmoid(lora[:, 2 * LANES:3 * LANES]).astype(BF16), g2_ref[...])
    q = q_ref[...]
    qn = q * lax.rsqrt(_head_sum(q * q, ones) * (1.0 / HEAD) + NORM_EPS) * qn_ref[...]
    ka = ka_ref[...]
    ones_kv = ones[0:D_KV, 0:D_KV]
    kn = ka * lax.rsqrt(_head_sum(ka * ka, ones_kv) * (1.0 / HEAD) + NORM_EPS) * kn_ref[...]
    kn_o[...] = kn
    cos = cos_ref[...]
    sin = sin_ref[...]
    half = AXIS_DIM // 2

    def rope(t, cos_t, sin_t):
        width = t.shape[1]
        lane = lax.broadcasted_iota(jnp.int32, t.shape, 1)
        first = (lane % AXIS_DIM) < half
        swapped = jnp.where(first, pltpu.roll(t, width - half, 1), pltpu.roll(t, half, 1))
        return t * cos_t + swapped * sin_t

    cos4 = jnp.concatenate([cos] * (D_A // D_KV), axis=1)
    sin4 = jnp.concatenate([sin] * (D_A // D_KV), axis=1)
    qr_o[...] = (rope(qn, cos4, sin4) * (HEAD ** -0.5)).astype(BF16)
    kr_o[...] = rope(kn, cos, sin).astype(BF16)


def _prep(z, cos_tab, sin_tab, tab_row, p):
    n = z["r"].shape[0]
    row = lambda i: (i, 0)
    full = lambda *shape: pl.BlockSpec(shape, lambda i: (0,) * len(shape))
    tok = lambda width: pl.BlockSpec((TM, width), row)
    tab = pl.BlockSpec((TM, D_KV), lambda i: (tab_row(i), 0))
    cpb = TM // CHUNK
    return pl.pallas_call(
        _prep_kernel,
        out_shape=[jax.ShapeDtypeStruct((2, n, SCAN_OPS * D_R), BF16),
                   jax.ShapeDtypeStruct((2, n // CHUNK, 1, D_R), F32),
                   jax.ShapeDtypeStruct((n, D_R), BF16),
                   jax.ShapeDtypeStruct((n, D_R), F32),
                   jax.ShapeDtypeStruct((n, D_R), F32),
                   jax.ShapeDtypeStruct((n, D_A), BF16),
                   jax.ShapeDtypeStruct((n, D_KV), F32),
                   jax.ShapeDtypeStruct((n, D_KV), BF16)],
        grid=(n // TM,),
        in_specs=[tok(D_R), tok(D_R), tok(D_R), tok(LORA_COLS), tok(D_A), tok(D_KV), tab, tab,
                  full(2, D_R), full(2, LANES, D_R), full(2, D_R), full(2, LANES, D_R),
                  full(GATE_RANK, D_R), full(1, D_R), full(1, D_R), full(1, D_R),
                  full(1, D_A), full(1, D_KV), full(D_R, D_R), full(2, TM, TM), full(TM, TM)],
        out_specs=[pl.BlockSpec((2, TM, SCAN_OPS * D_R), lambda i: (0, i, 0)),
                   pl.BlockSpec((2, cpb, 1, D_R), lambda i: (0, i, 0, 0)),
                   tok(D_R), tok(D_R), tok(D_R), tok(D_A), tok(D_KV), tok(D_KV)],
        compiler_params=_cparams("arbitrary"),
        name="mixer_prep",
    )(z["r"], z["k"], z["v"], z["lora"], z["q"], z["ka"], cos_tab, sin_tab,
      p["w0"], p["w2"], p["a0"], p["a2"], p["g2"], p["kkw"], p["kaw"], p["rk"],
      p["qn"], p["kn"], p["ones"], p["tri"], p["chunk_ones"])


def _scan_kernel(scf_ref, scb_ref, vf_ref, vb_ref, gf_ref, gb_ref, s0_ref,
                 yf_ref, yb_ref, sfin_ref, state):
    c = pl.program_id(1)
    C = CHUNK
    pairs = D_R // LANES
    chains = [(g, d, p) for g in range(SCAN_GROUP) for d in range(2) for p in range(pairs)]

    lane = lax.broadcasted_iota(jnp.int32, (1, LANES), 1)
    low = lane < HEAD

    @pl.when(c == 0)
    def _():
        zero = jnp.zeros((HEAD, HEAD), F32)
        for i, (g, d, p) in enumerate(chains):
            top = jnp.concatenate([s0_ref[d, g, 2 * p], zero], axis=1)
            bot = jnp.concatenate([zero, s0_ref[d, g, 2 * p + 1]], axis=1)
            state[i] = jnp.concatenate([top, bot], axis=0)

    W = 2 * C
    t2 = lax.broadcasted_iota(jnp.int32, (C, W), 0)
    s2 = lax.broadcasted_iota(jnp.int32, (C, W), 1) % C
    strict = (s2 < t2, s2 > t2)
    incl = (s2 <= t2, s2 >= t2)
    eye2 = jnp.where(s2 == t2, 1.0, 0.0)
    first_mat = lax.broadcasted_iota(jnp.int32, (1, W), 1) < C
    r2 = lax.broadcasted_iota(jnp.int32, (LANES, LANES), 0)
    c2 = lax.broadcasted_iota(jnp.int32, (LANES, LANES), 1)
    same_head = (r2 // HEAD) == (c2 // HEAD)

    sc_refs = (scf_ref, scb_ref)
    v_refs = (vf_ref, vb_ref)
    g_refs = (gf_ref, gb_ref)
    y_refs = (yf_ref, yb_ref)

    def operand(chain, j):
        g, d, p = chain
        c0 = j * D_R + p * LANES
        return sc_refs[d][0, g, :, c0:c0 + LANES]

    def stack(x):
        first = low if x.shape[1] == LANES else first_mat
        zero = jnp.zeros_like(x)
        return jnp.concatenate([jnp.where(first, x, zero), jnp.where(first, zero, x)], axis=0)

    Qk = [operand(ch, 0) for ch in chains]
    Qr = [operand(ch, 1) for ch in chains]
    v_p = [v_refs[d][g, :, p * LANES:(p + 1) * LANES] for g, d, p in chains]
    v_st = [stack(x) for x in v_p]
    Z0 = [state[i] for i in range(len(chains))]
    Z0b = [z.astype(BF16) for z in Z0]
    m = [_dot_nt(jnp.concatenate([Qk[i], Qr[i]], axis=0),
                 jnp.concatenate([stack(operand(ch, 2)), stack(operand(ch, 3))], axis=0))
         for i, ch in enumerate(chains)]
    A = [jnp.where(strict[d], m[i][0:C, 0:W], 0.0) for i, (g, d, p) in enumerate(chains)]
    Bm = [jnp.where(strict[d], m[i][0:C, W:2 * W], 0.0).astype(BF16)
          for i, (g, d, p) in enumerate(chains)]
    Pb = [jnp.where(incl[d], m[i][C:2 * C, 0:W], 0.0).astype(BF16)
          for i, (g, d, p) in enumerate(chains)]
    Pk = [jnp.where(incl[d], m[i][C:2 * C, W:2 * W], 0.0).astype(BF16)
          for i, (g, d, p) in enumerate(chains)]
    BV = [_dot(Bm[i], v_st[i]) for i in range(len(chains))]
    T = [eye2 - a for a in A]
    Ap = [_dot(a.astype(BF16), stack(a.astype(BF16))).astype(BF16) for a in A]
    n = 2
    while 2 * n < C:
        R = [_dot(jnp.concatenate([Ap[i], T[i].astype(BF16)], axis=0), stack(Ap[i]))
             for i in range(len(chains))]
        Ap = [x[0:C].astype(BF16) for x in R]
        T = [T[i] + R[i][C:2 * C] for i in range(len(chains))]
        n *= 2
    T = [T[i] + _dot(T[i].astype(BF16), stack(Ap[i])) for i in range(len(chains))]
    WU = [_dot(T[i].astype(BF16),
               jnp.concatenate([stack(Qk[i]), stack(BV[i].astype(BF16))], axis=1))
          for i in range(len(chains))]
    Wq = [x[:, 0:LANES].astype(BF16) for x in WU]
    Uv = [x[:, LANES:2 * LANES] for x in WU]
    PW = [_dot(Pb[i], jnp.concatenate([stack(Wq[i]), stack(Uv[i].astype(BF16))], axis=1))
          for i in range(len(chains))]
    PkV = [_dot(Pk[i], v_st[i]) for i in range(len(chains))]
    U = [_dot_nt(Wq[i], Z0b[i]) + Uv[i] for i in range(len(chains))]
    Yq = [(Qr[i].astype(F32) - PW[i][:, 0:LANES]).astype(BF16) for i in range(len(chains))]
    for i, (g, d, p) in enumerate(chains):
        y_refs[d][g, :, p * LANES:(p + 1) * LANES] = (_dot_nt(Yq[i], Z0b[i]) + PkV[i]
                                                      - PW[i][:, LANES:2 * LANES])
    for i, (g, d, p) in enumerate(chains):
        lhs = jnp.concatenate([v_p[i], (-U[i]).astype(BF16)], axis=0)
        rhs = jnp.concatenate([operand(chains[i], 5), operand(chains[i], 4)], axis=0)
        upd = _dot_tn(lhs, rhs)
        g_tot = g_refs[d][0, g, 0][:, p * LANES:(p + 1) * LANES]
        state[i] = Z0[i] * g_tot + jnp.where(same_head, upd, 0.0)

    @pl.when(c == pl.num_programs(1) - 1)
    def _():
        for i, (g, d, p) in enumerate(chains):
            z = state[i]
            sfin_ref[d, g, 2 * p] = z[0:HEAD, 0:HEAD]
            sfin_ref[d, g, 2 * p + 1] = z[HEAD:2 * HEAD, HEAD:2 * HEAD]


def _rwkv_scan(sc, gt, vb, s0, row0, batch, t_len):
    n = vb.shape[0]
    nch = t_len // CHUNK
    assert row0 % (t_len * SCAN_GROUP) == 0 and batch % SCAN_GROUP == 0
    seq0 = row0 // (t_len * SCAN_GROUP)
    width = SCAN_OPS * D_R
    sc4 = sc.reshape(2, n // t_len, t_len, width)
    gt5 = gt.reshape(2, n // t_len, nch, 1, D_R)
    v3 = vb.reshape(n // t_len, t_len, D_R)
    G = SCAN_GROUP
    fwd = lambda c: c
    bwd = lambda c: nch - 1 - c
    sc_spec = lambda d, at: pl.BlockSpec((1, G, CHUNK, width), lambda s, c: (d, seq0 + s, at(c), 0))
    v_spec = lambda at: pl.BlockSpec((G, CHUNK, D_R), lambda s, c: (seq0 + s, at(c), 0))
    gt_spec = lambda d, at: pl.BlockSpec((1, G, 1, 1, D_R), lambda s, c: (d, seq0 + s, at(c), 0, 0))
    y_spec = lambda at: pl.BlockSpec((G, CHUNK, D_R), lambda s, c: (s, at(c), 0))
    st = pl.BlockSpec((2, G, H_R, HEAD, HEAD), lambda s, c: (0, s, 0, 0, 0))
    return pl.pallas_call(
        _scan_kernel,
        out_shape=[jax.ShapeDtypeStruct((batch, t_len, D_R), F32),
                   jax.ShapeDtypeStruct((batch, t_len, D_R), F32),
                   jax.ShapeDtypeStruct((2, batch, H_R, HEAD, HEAD), F32)],
        grid=(batch // G, nch),
        in_specs=[sc_spec(0, fwd), sc_spec(1, bwd), v_spec(fwd), v_spec(bwd),
                  gt_spec(0, fwd), gt_spec(1, bwd), st],
        out_specs=[y_spec(fwd), y_spec(bwd), st],
        scratch_shapes=[pltpu.VMEM((G * 2 * D_R // LANES, LANES, LANES), F32)],
        compiler_params=_cparams("arbitrary", "arbitrary"),
        name="rwkv_scan",
    )(sc4, sc4, v3, v3, gt5, gt5, s0)


def _attn_kernel(*refs, tq, with_cache):
    if with_cache:
        q_ref, k_ref, v_ref, ck_ref, cv_ref, o_ref = refs
        kx = jnp.concatenate([k_ref[...], ck_ref[0].astype(BF16)], axis=0)
        vx = jnp.concatenate([v_ref[...], cv_ref[0]], axis=0).astype(BF16)
    else:
        q_ref, k_ref, v_ref, o_ref = refs
        kx = k_ref[...]
        vx = v_ref[...].astype(BF16)
    lane = lax.broadcasted_iota(jnp.int32, (1, D_KV), 1)
    low = lane < HEAD
    k_sw = pltpu.roll(kx.astype(F32), HEAD, 1).astype(BF16)
    v_sw = pltpu.roll(vx.astype(F32), HEAD, 1).astype(BF16)
    group = N_HEADS // N_KV
    for p in range(D_A // LANES):
        g = (2 * p) // group
        keep = low if g == 0 else jnp.logical_not(low)
        kd = jnp.where(keep, kx, k_sw)
        vd = jnp.where(keep, vx, v_sw)
        qp = q_ref[:, p * LANES:(p + 1) * LANES]
        zero = jnp.zeros_like(qp)
        qs = jnp.concatenate([jnp.where(low, qp, zero), jnp.where(low, zero, qp)], axis=0)
        s = _dot_nt(qs, kd)
        mx = jnp.max(s, axis=-1, keepdims=True)
        e = jnp.exp(s - mx)
        den = jnp.sum(e, axis=-1, keepdims=True)
        o = _dot(e.astype(BF16), vd) / den
        o_ref[:, p * LANES:(p + 1) * LANES] = jnp.where(low, o[0:tq], o[tq:2 * tq]).astype(BF16)


def _attention(qr, kr, va, row0, batch, t_len, cache_k=None, cache_v=None):
    tq = min(t_len, 256)
    nq = t_len // tq
    qblk0 = row0 // tq
    sblk0 = row0 // t_len
    with_cache = cache_k is not None
    in_specs = [pl.BlockSpec((tq, D_A), lambda b, i: (qblk0 + b * nq + i, 0)),
                pl.BlockSpec((t_len, D_KV), lambda b, i: (sblk0 + b, 0)),
                pl.BlockSpec((t_len, D_KV), lambda b, i: (sblk0 + b, 0))]
    args = [qr, kr, va]
    if with_cache:
        past = cache_k.shape[1]
        in_specs += [pl.BlockSpec((1, past, D_KV), lambda b, i: (b, 0, 0))] * 2
        args += [cache_k, cache_v]
    return pl.pallas_call(
        functools.partial(_attn_kernel, tq=tq, with_cache=with_cache),
        out_shape=jax.ShapeDtypeStruct((batch * t_len, D_A), BF16),
        grid=(batch, nq),
        in_specs=in_specs,
        out_specs=pl.BlockSpec((tq, D_A), lambda b, i: (b * nq + i, 0)),
        compiler_params=_cparams("arbitrary", "arbitrary"),
        name="attention_cache" if with_cache else "attention_ctx",
    )(*args)


def _post_kernel(yfc_ref, yfl_ref, ybc_ref, ybl_ref, bonus_ref, g_ref, yac_ref, yal_ref, gr_ref, ga_ref,
                 xc_ref, xl_ref,
                 mod_ref, lnw_ref, lnb_ref, wur_ref, wua_ref, wo_ref, n2_ref, wr_ref, br_ref, ones_ref,
                 x1_o, h2_o, idx_o, gate_o, *, ctx_blocks):
    ones = ones_ref[...]
    is_ctx = pl.program_id(0) < ctx_blocks
    y_a = jnp.where(is_ctx, yac_ref[...], yal_ref[...])
    x = jnp.where(is_ctx, xc_ref[...], xl_ref[...])
    y = jnp.where(is_ctx, yfc_ref[...] + ybc_ref[...], yfl_ref[...] + ybl_ref[...])
    mu = _head_sum(y, ones) * (1.0 / HEAD)
    yc = y - mu
    var = _head_sum(yc * yc, ones) * (1.0 / HEAD)
    yn = yc * lax.rsqrt(var + GN_EPS) * lnw_ref[...] + lnb_ref[...] + bonus_ref[...]
    y_r = (yn * g_ref[...]).astype(BF16)
    merged = (_sigmoid(gr_ref[...]) * _dot(y_r, wur_ref[...])
              + _sigmoid(ga_ref[...]) * _dot(y_a, wua_ref[...]))
    mix = _dot(merged.astype(BF16), wo_ref[...])
    mod = mod_ref[0]
    g1 = mod[:, 2 * D_MODEL:3 * D_MODEL]
    sh2 = mod[:, 3 * D_MODEL:4 * D_MODEL]
    sc2 = mod[:, 4 * D_MODEL:5 * D_MODEL]
    x1 = x + g1 * mix
    x1_o[...] = x1
    ms = jnp.mean(x1 * x1, axis=-1, keepdims=True)
    h2 = x1 * lax.rsqrt(ms + NORM_EPS) * n2_ref[...] * (1.0 + sc2) + sh2
    _store_token_tiles(h2_o, h2)
    hh, hl = _split2(h2)
    wh, wl = _split2(wr_ref[...])
    hi_terms = _dot(hh, jnp.concatenate([wh, wl], axis=1))
    logits = hi_terms[:, 0:LANES] + hi_terms[:, LANES:2 * LANES] + _dot(hl, wh) + br_ref[...]
    lane_i = lax.broadcasted_iota(jnp.int32, logits.shape, 1)
    lane = lane_i.astype(F32)
    neg = jnp.float32(-jnp.inf)
    cur = jnp.where(lane_i < N_EXPERTS, logits, neg)
    vals, idxs = [], []
    for _ in range(TOP_K):
        mx = jnp.max(cur, axis=-1, keepdims=True)
        ix = jnp.min(jnp.where(cur == mx, lane, float(LANES)), axis=-1, keepdims=True)
        vals.append(mx)
        idxs.append(ix)
        cur = jnp.where(lane == ix, neg, cur)
    es = [jnp.exp(val - vals[0]) for val in vals]
    den = es[0] + es[1] + es[2] + es[3]
    idx_out = jnp.zeros(logits.shape, jnp.int32)
    gate_out = jnp.zeros(logits.shape, F32)
    for j in range(TOP_K):
        idx_out = jnp.where(lane_i == j, idxs[j].astype(jnp.int32), idx_out)
        gate_out = jnp.where(lane_i == j, es[j] / den, gate_out)
    idx_o[...] = idx_out
    gate_o[...] = gate_out


def _post(y_f, y_b, bonus, g, ya_ctx, ya_lat, gate_r, gate_a, x_ctx, x_lat, mods3, mod_row, p):
    n = x_ctx.shape[0] + x_lat.shape[0]
    nbc = ya_ctx.shape[0] // TM
    row = lambda i: (i, 0)
    full = lambda *shape: pl.BlockSpec(shape, lambda i: (0,) * len(shape))
    tok = lambda width: pl.BlockSpec((TM, width), row)
    return pl.pallas_call(
        functools.partial(_post_kernel, ctx_blocks=nbc),
        out_shape=[jax.ShapeDtypeStruct((n, D_MODEL), F32),
                   jax.ShapeDtypeStruct((n * TILE_ROWS, LANES), F32),
                   jax.ShapeDtypeStruct((n, LANES), jnp.int32),
                   jax.ShapeDtypeStruct((n, LANES), F32)],
        grid=(n // TM,),
        in_specs=_token_specs(nbc, D_R) + _token_specs(nbc, D_R) + [tok(D_R), tok(D_R)]
                 + _token_specs(nbc, D_A) + [tok(D_MODEL), tok(D_MODEL)] + _token_specs(nbc, D_MODEL) + [
                  pl.BlockSpec((1, 1, 6 * D_MODEL), lambda i: (mod_row(i), 0, 0)),
                  full(1, D_R), full(1, D_R), full(D_R, D_MODEL), full(D_A, D_MODEL),
                  full(D_MODEL, D_MODEL), full(1, D_MODEL), full(D_MODEL, LANES), full(1, LANES),
                  full(D_R, D_R)],
        out_specs=[tok(D_MODEL), pl.BlockSpec((TM * TILE_ROWS, LANES), row), tok(LANES), tok(LANES)],
        compiler_params=_cparams("arbitrary"),
        name="mixer_post",
    )(*y_f, *y_b, bonus, g, ya_ctx, ya_lat, gate_r, gate_a, x_ctx, x_lat, mods3,
      p["lnw"], p["lnb"], p["wur"], p["wua"], p["wo"], p["n2"], p["wr"], p["br"], p["ones"])


def _moe_kernel(blk_e_ref, blk_first_ref, next_e_ref, used_ref,
                src0_ref, src_next_ref, dst_prev_ref, h_hbm, w1_hbm, b1_ref, w2_hbm, b2_ref,
                out_hbm, w1f, w2f, w1b, w2b, xs, ys, sem_g, sem_s, sem_t, sem_w, *, n_rows):
    b = pl.program_id(0)
    used = used_ref[0]
    R = MOE_ROWS
    T = TILE_ROWS

    def tile(ref, i):
        return ref.at[pl.ds(pl.multiple_of(i * T, T), T)]

    def gather(table_ref, buf):
        for i in range(R):
            pltpu.make_async_copy(tile(h_hbm, table_ref[0, 0, i]), tile(xs.at[buf], i),
                                  sem_g.at[buf]).start(priority=i % 2)

    def scatter_prev(buf):
        for i in range(R):
            pltpu.make_async_copy(tile(ys.at[buf], i), tile(out_hbm, dst_prev_ref[0, 0, i]),
                                  sem_s.at[buf]).start(priority=i % 2)

    def fetch_weights(e):
        pltpu.make_async_copy(w1_hbm.at[e], w1f, sem_w.at[0]).start()
        pltpu.make_async_copy(w2_hbm.at[e], w2f, sem_w.at[1]).start()

    @pl.when(b == 0)
    def _():
        ys[...] = jnp.zeros_like(ys)
        for s in range(2):
            band = pltpu.make_async_copy(ys.at[s], out_hbm.at[pl.ds((n_rows + s * R) * T, R * T)], sem_t)
            band.start()
            band.wait()
        gather(src0_ref, 0)
        fetch_weights(blk_e_ref[0])

    @pl.when(jnp.logical_and(blk_first_ref[b] == 1, b < used))
    def _():
        pltpu.make_async_copy(w1_hbm.at[0], w1f, sem_w.at[0]).wait()
        pltpu.make_async_copy(w2_hbm.at[0], w2f, sem_w.at[1]).wait()
        w1b[...] = w1f[...].astype(BF16)
        w2b[...] = w2f[...].astype(BF16)

        @pl.when(next_e_ref[b] >= 0)
        def _():
            fetch_weights(next_e_ref[b])

    def step(p):
        @pl.when(b <= used)
        def _():
            pltpu.make_async_copy(h_hbm.at[pl.ds(0, R * T)], xs.at[p], sem_g.at[p]).wait()

        @pl.when(jnp.logical_and(b >= 1, b <= used + 1))
        def _():
            pltpu.make_async_copy(ys.at[p], out_hbm.at[pl.ds(0, R * T)], sem_s.at[p]).wait()

        @pl.when(b < used)
        def _():
            gather(src_next_ref, 1 - p)
            scatter_prev(1 - p)
            hb = _dot(_load_token_tiles(xs.at[p]).astype(BF16), w1b[...]) + b1_ref[0]
            glu = jnp.minimum(hb[:, 0:D_FF], SWIGLU_LIMIT)
            lin = jnp.clip(hb[:, D_FF:2 * D_FF], -SWIGLU_LIMIT, SWIGLU_LIMIT)
            act = glu * _sigmoid(SWIGLU_ALPHA * glu) * (lin + 1.0)
            _store_token_tiles(ys.at[p], _dot(act.astype(BF16), w2b[...]) + b2_ref[0])

        @pl.when(b == used)
        def _():
            scatter_prev(1 - p)

    for p in range(2):
        pl.when(b % 2 == p)(functools.partial(step, p))


def _moe(h2, src, dst_prev, blk_e, blk_first, next_e, used, w1, b1, w2, b2):
    d = D_MODEL
    n = h2.shape[0] // TILE_ROWS
    nb = src.shape[0]
    steps = nb + 2
    cur = lambda b, *_: (jnp.minimum(b, nb), 0, 0)
    nxt = lambda b, *_: (jnp.minimum(b + 1, nb - 1), 0, 0)
    table = lambda index_map: pl.BlockSpec((1, 1, MOE_ROWS), index_map, memory_space=pltpu.SMEM)
    expert = lambda b, blk_e, *_: (blk_e[jnp.minimum(b, nb - 1)], 0, 0)
    grid_spec = pltpu.PrefetchScalarGridSpec(
        num_scalar_prefetch=4,
        grid=(steps,),
        in_specs=[table(lambda b, *_: (0, 0, 0)), table(nxt), table(cur),
                  pl.BlockSpec(memory_space=pl.ANY),
                  pl.BlockSpec(memory_space=pl.ANY),
                  pl.BlockSpec((1, 1, 2 * D_FF), expert),
                  pl.BlockSpec(memory_space=pl.ANY),
                  pl.BlockSpec((1, 1, d), expert)],
        out_specs=pl.BlockSpec(memory_space=pl.ANY),
        scratch_shapes=[pltpu.VMEM((d, 2 * D_FF), F32),
                        pltpu.VMEM((D_FF, d), F32),
                        pltpu.VMEM((d, 2 * D_FF), BF16),
                        pltpu.VMEM((D_FF, d), BF16),
                        pltpu.VMEM((2, MOE_ROWS * TILE_ROWS, LANES), F32),
                        pltpu.VMEM((2, MOE_ROWS * TILE_ROWS, LANES), F32),
                        pltpu.SemaphoreType.DMA((2,)),
                        pltpu.SemaphoreType.DMA((2,)),
                        pltpu.SemaphoreType.DMA(()),
                        pltpu.SemaphoreType.DMA((2,))])
    tab3 = lambda t: t.reshape(-1, 1, MOE_ROWS)
    pad = lambda t: jnp.concatenate([t, jnp.zeros((steps - nb,), jnp.int32)])
    return pl.pallas_call(
        functools.partial(_moe_kernel, n_rows=n * TOP_K),
        out_shape=jax.ShapeDtypeStruct(((n * TOP_K + 2 * MOE_ROWS) * TILE_ROWS, LANES), F32),
        grid_spec=grid_spec,
        compiler_params=pltpu.CompilerParams(dimension_semantics=("arbitrary",),
                                             vmem_limit_bytes=VMEM_LIMIT, has_side_effects=True),
        name="moe_experts",
    )(blk_e, pad(blk_first), next_e, used, tab3(src), tab3(src), tab3(dst_prev), h2, w1,
      b1.reshape(N_EXPERTS, 1, 2 * D_FF), w2, b2.reshape(N_EXPERTS, 1, d))


def _route(top_idx, n):
    nk = n * TOP_K
    cap = nk + N_EXPERTS * MOE_ROWS
    nb = cap // MOE_ROWS
    flat_e = top_idx.reshape(nk)
    order = jnp.argsort(flat_e, stable=True).astype(jnp.int32)
    counts = jnp.sum((flat_e[:, None] == jnp.arange(N_EXPERTS, dtype=jnp.int32)[None, :])
                     .astype(jnp.int32), axis=0)
    start = jnp.cumsum(counts) - counts
    n_blk = (counts + MOE_ROWS - 1) // MOE_ROWS
    blk_end = jnp.cumsum(n_blk)
    first_blk = blk_end - n_blk
    experts = jnp.arange(N_EXPERTS, dtype=jnp.int32)
    blk = jnp.arange(nb, dtype=jnp.int32)
    blk_e = jnp.minimum(jnp.sum((blk[:, None] >= blk_end[None, :]).astype(jnp.int32), axis=1),
                        N_EXPERTS - 1)
    onehot = (blk_e[:, None] == experts[None, :]).astype(jnp.int32)
    pick = lambda t: jnp.sum(onehot * t[None, :], axis=1)
    blk_first_of_e, blk_count, blk_start = pick(first_blk), pick(counts), pick(start)
    blk_first = (blk == blk_first_of_e).astype(jnp.int32)
    blk_after = pick(blk_end)
    next_e = jnp.where(blk_after < blk_end[-1], blk_e[jnp.minimum(blk_after, nb - 1)], -1).astype(jnp.int32)
    row = jnp.arange(MOE_ROWS, dtype=jnp.int32)[None, :]
    within = (blk - blk_first_of_e)[:, None] * MOE_ROWS + row
    valid = jnp.logical_and(within < blk_count[:, None], (blk < blk_end[-1])[:, None])
    j = order[jnp.clip(blk_start[:, None] + within, 0, nk - 1)]
    src = jnp.where(valid, j // TOP_K, 0).astype(jnp.int32)
    scratch_row = nk + (blk % 2)[:, None] * MOE_ROWS + row
    dst = jnp.where(valid, (j % TOP_K) * n + j // TOP_K, scratch_row).astype(jnp.int32)
    dst_prev = jnp.concatenate([nk + MOE_ROWS + row, dst], axis=0)
    return (src, dst_prev, blk_e.astype(jnp.int32), blk_first, next_e,
            blk_end[-1:].astype(jnp.int32))


def _final_kernel(x1_ref, y0_ref, y1_ref, y2_ref, y3_ref, gate_ref, mod_ref, nf_ref, oc_ref, ol_ref,
                  *, ctx_blocks):
    gates = gate_ref[...]
    ffn = gates[:, 0:1] * _load_token_tiles(y0_ref)
    for j, ref in enumerate((y1_ref, y2_ref, y3_ref), start=1):
        ffn = ffn + gates[:, j:j + 1] * _load_token_tiles(ref)
    g2 = mod_ref[0][:, 5 * D_MODEL:6 * D_MODEL]
    x2 = x1_ref[...] + g2 * ffn
    ms = jnp.mean(x2 * x2, axis=-1, keepdims=True)
    y = x2 * lax.rsqrt(ms + NORM_EPS) * nf_ref[...]
    i = pl.program_id(0)

    @pl.when(i < ctx_blocks)
    def _():
        oc_ref[...] = y

    @pl.when(i >= ctx_blocks)
    def _():
        ol_ref[...] = y


def _final(x1, ys, gates, mods3, mod_row, norm_f, nc):
    n, d = x1.shape
    nb = n // TM
    nbc = nc // TM
    row = lambda i: (i, 0)
    return pl.pallas_call(
        functools.partial(_final_kernel, ctx_blocks=nbc),
        out_shape=[jax.ShapeDtypeStruct((nc, d), F32), jax.ShapeDtypeStruct((n - nc, d), F32)],
        grid=(nb,),
        in_specs=[pl.BlockSpec((TM, d), row)]
                 + [pl.BlockSpec((TM * TILE_ROWS, LANES), functools.partial(lambda j, i: (j * nb + i, 0), j))
                    for j in range(TOP_K)]
                 + [pl.BlockSpec((TM, LANES), row),
                    pl.BlockSpec((1, 1, 6 * d), lambda i: (mod_row(i), 0, 0)),
                    pl.BlockSpec((1, d), lambda i: (0, 0))],
        out_specs=_token_specs(nbc, d),
        compiler_params=_cparams("arbitrary"),
        name="combine_final",
    )(x1, ys, ys, ys, ys, gates, mods3, norm_f.reshape(1, d))


def _rope_tables(t_ctx, t_lat):
    rows = np.arange(t_lat) // GRID_W
    cols = np.arange(t_lat) % GRID_W
    inv = ROPE_THETA ** (-np.arange(0, AXIS_DIM, 2, dtype=np.float32) / AXIS_DIM)
    inv = jnp.asarray(inv, F32)
    ang_r = jnp.asarray(rows, F32)[:, None] * inv[None, :]
    ang_c = jnp.asarray(cols, F32)[:, None] * inv[None, :]
    cos = jnp.concatenate([jnp.cos(ang_r)] * 2 + [jnp.cos(ang_c)] * 2, axis=1)
    sin = jnp.concatenate([-jnp.sin(ang_r), jnp.sin(ang_r), -jnp.sin(ang_c), jnp.sin(ang_c)], axis=1)
    cos = jnp.concatenate([jnp.ones((t_ctx, HEAD), F32), cos], axis=0)
    sin = jnp.concatenate([jnp.zeros((t_ctx, HEAD), F32), sin], axis=0)
    return jnp.concatenate([cos, cos], axis=1), jnp.concatenate([sin, sin], axis=1)


def kernel(x_prompt, x_sample, cache_k, cache_v, state_rwkv_fwd, state_rwkv_bwd, c, c_ctx, w_ada, b_ada, norm1, norm2, w_in, rw_w0, rw_w2, rw_a0, rw_a2, rw_g2, rw_kk, rw_ka, rw_rk, rw_ln_w, rw_ln_b, q_norm, k_norm, w_up_r, w_up_a, w_out, w_router, b_router, w_moe_in, b_moe_in, w_moe_out, b_moe_out, norm_f):
    depth = w_in.shape[0]
    assert depth == 1, "single trunk layer"
    bc, tc, d = x_prompt.shape
    bl, tl, _ = x_sample.shape
    nc, nl = bc * tc, bl * tl
    n = nc + nl
    assert d == D_MODEL and tc == TM and tl % TM == 0 and nc % tl == 0
    assert nc % TM_PROJ == 0 and tl % TM_PROJ == 0
    l = 0

    x_ctx, x_lat = x_prompt.reshape(nc, d), x_sample.reshape(nl, d)
    cond =jnp.concatenate([c_ctx[None, :], c, jnp.zeros((16 - 1 - bl, d), F32)], axis=0)
    mods3 = _adaln(cond, w_ada[l], b_ada[l]).reshape(16, 1, 6 * d)
    nbc = nc // TM
    per_seq = tl // TM
    mod_row = lambda i: jnp.where(i < nbc, 0, 1 + (i - nbc) // per_seq)
    tab_row = lambda i: jnp.where(i < nbc, 0, 1 + (i - nbc) % per_seq)

    z = dict(zip([name for name, _ in IN_SPLITS],
                 _proj_in(x_ctx, x_lat, mods3, norm1[l], w_in[l].astype(BF16), tl)))

    ones_bd = jnp.asarray(np.kron(np.eye(H_R), np.ones((HEAD, HEAD))), BF16)
    pad_lo = lambda w: jnp.concatenate([w, jnp.zeros_like(w)], axis=0)
    pad_hi = lambda w: jnp.concatenate([jnp.zeros_like(w), w], axis=0)
    prep_p = dict(
        w0=rw_w0[l], a0=rw_a0[l],
        w2=jnp.stack([pad_lo(rw_w2[l, 0]), pad_hi(rw_w2[l, 1])]).astype(BF16),
        a2=jnp.stack([pad_lo(rw_a2[l, 0]), pad_hi(rw_a2[l, 1])]).astype(BF16),
        g2=rw_g2[l].astype(BF16),
        kkw=rw_kk[l].reshape(1, D_R), kaw=rw_ka[l].reshape(1, D_R), rk=rw_rk[l].reshape(1, D_R),
        qn=jnp.tile(q_norm[l], N_HEADS).reshape(1, D_A), kn=jnp.tile(k_norm[l], N_KV).reshape(1, D_KV),
        ones=ones_bd,
        tri=jnp.asarray(np.stack([np.kron(np.eye(TM // CHUNK), np.tril(np.ones((CHUNK, CHUNK)))),
                                  np.kron(np.eye(TM // CHUNK), np.triu(np.ones((CHUNK, CHUNK))))]), BF16),
        chunk_ones=jnp.asarray(np.kron(np.eye(TM // CHUNK), np.ones((CHUNK, CHUNK))), BF16))
    cos_tab, sin_tab = _rope_tables(tc, tl)
    sc, gt, vb, bonus, g, qr, kn, kr = _prep(z, cos_tab, sin_tab, tab_row, prep_p)

    yf_c, yb_c, s_fin = _rwkv_scan(sc, gt, vb, jnp.zeros((2, bc, H_R, HEAD, HEAD), F32), 0, bc, tc)
    yf_l, yb_l, _ = _rwkv_scan(sc, gt, vb, jnp.stack([state_rwkv_fwd[:, l], state_rwkv_bwd[:, l]]),
                               nc, bl, tl)
    y_f = (yf_c.reshape(nc, D_R), yf_l.reshape(nl, D_R))
    y_b = (yb_c.reshape(nc, D_R), yb_l.reshape(nl, D_R))

    ya_ctx = _attention(qr, kr, z["va"], 0, bc, tc)
    ya_lat = _attention(qr, kr, z["va"], nc, bl, tl,
                        cache_k[:, l].reshape(bl, -1, D_KV), cache_v[:, l].reshape(bl, -1, D_KV))

    post_p = dict(
        lnw=rw_ln_w[l].reshape(1, D_R), lnb=rw_ln_b[l].reshape(1, D_R),
        wur=w_up_r[l].astype(BF16), wua=w_up_a[l].astype(BF16), wo=w_out[l].astype(BF16),
        n2=norm2[l].reshape(1, d),
        wr=jnp.pad(w_router[l], ((0, 0), (0, LANES - N_EXPERTS))),
        br=jnp.pad(b_router[l], (0, LANES - N_EXPERTS)).reshape(1, LANES),
        ones=ones_bd)
    x1, h2, top_idx, gates = _post(y_f, y_b, bonus, g, ya_ctx, ya_lat, z["gate_r"], z["gate_a"],
                                   x_ctx, x_lat, mods3, mod_row, post_p)

    src, dst, blk_e, blk_first, next_e, used = _route(top_idx[:, :TOP_K], n)
    ys = _moe(h2, src, dst, blk_e, blk_first, next_e, used,
              w_moe_in[l], b_moe_in[l], w_moe_out[l], b_moe_out[l])
    y_ctx, y_lat = _final(x1, ys, gates, mods3, mod_row, norm_f, nc)

    y_prompt = y_ctx.reshape(bc, tc, d)
    y_sample = y_lat.reshape(bl, tl, d)
    new_cache_k = kn[:nc].reshape(bc, 1, tc, N_KV, HEAD)
    new_cache_v = z["va"][:nc].reshape(bc, 1, tc, N_KV, HEAD)
    new_state_fwd = s_fin[0][:, None]
    new_state_bwd = s_fin[1][:, None]
    return (y_prompt, y_sample, new_cache_k, new_cache_v, new_state_fwd, new_state_bwd)
```

```python
import functools

import numpy as np
import jax
import jax.numpy as jnp
from jax import lax
from jax.experimental import pallas as pl
from jax.experimental.pallas import tpu as pltpu

F32 = jnp.float32
BF16 = jnp.bfloat16

D_MODEL = 1024
GRID_W = 64
NORM_EPS = 1e-6
HEAD = 64
H_R = 8
D_R = H_R * HEAD
DECAY_RANK = 64
AAA_RANK = 64
GATE_RANK = 128
GN_EPS = 64e-5
N_HEADS = 8
N_KV = 2
D_A = N_HEADS * HEAD
D_KV = N_KV * HEAD
AXIS_DIM = HEAD // 2
ROPE_THETA = 10000.0
N_EXPERTS = 32
TOP_K = 4
D_FF = D_MODEL
SWIGLU_ALPHA = 1.702
SWIGLU_LIMIT = 7.0
LORA_COLS = 2 * DECAY_RANK + 2 * AAA_RANK + GATE_RANK
IN_SPLITS = (("r", D_R), ("k", D_R), ("v", D_R), ("lora", LORA_COLS), ("q", D_A),
             ("ka", D_KV), ("va", D_KV), ("gate_r", D_MODEL), ("gate_a", D_MODEL))

LANES = 128
TM = 256
TM_POST = 512
TM_PROJ = 512
CHUNK = 64
SCAN_OPS = 6
SCAN_GROUP = 4
MOE_ROWS = 256
VMEM_LIMIT = 56 * 1024 * 1024


def _cparams(*sem):
    return pltpu.CompilerParams(dimension_semantics=sem, vmem_limit_bytes=VMEM_LIMIT)


def _dot(a, b):
    return jnp.dot(a, b, preferred_element_type=F32)


def _dot_nt(a, b):
    return lax.dot_general(a, b, (((1,), (1,)), ((), ())), preferred_element_type=F32)


def _dot_tn(a, b):
    return lax.dot_general(a, b, (((0,), (0,)), ((), ())), preferred_element_type=F32)


def _split2(x):
    hi = x.astype(BF16)
    lo = (x - hi.astype(F32)).astype(BF16)
    return hi, lo


def _head_sum(x, ones_bd):
    return _dot(x.astype(BF16), ones_bd)


def _sigmoid(x):
    return 0.5 * jnp.tanh(0.5 * x) + 0.5


TILE_ROWS = D_MODEL // LANES


def _store_token_tiles(ref, x):
    rows = x.shape[0]
    for j in range(TILE_ROWS):
        ref[pl.ds(j, rows, stride=TILE_ROWS), :] = x[:, j * LANES:(j + 1) * LANES]


def _load_token_tiles(ref):
    rows = ref.shape[0] // TILE_ROWS
    return jnp.concatenate([ref[pl.ds(j, rows, stride=TILE_ROWS), :] for j in range(TILE_ROWS)], axis=1)


def _adaln_kernel(c_ref, w_ref, b_ref, o_ref):
    c = c_ref[...]
    s = (c * _sigmoid(c)).astype(BF16)
    o_ref[...] = _dot(s, w_ref[...].astype(BF16)) + b_ref[...]


def _adaln(cond, w_ada, b_ada):
    rows, d = cond.shape
    cols = w_ada.shape[1]
    tn = 512
    return pl.pallas_call(
        _adaln_kernel,
        out_shape=jax.ShapeDtypeStruct((rows, cols), F32),
        grid=(cols // tn,),
        in_specs=[pl.BlockSpec((rows, d), lambda j: (0, 0)),
                  pl.BlockSpec((d, tn), lambda j: (0, j)),
                  pl.BlockSpec((1, tn), lambda j: (0, j))],
        out_specs=pl.BlockSpec((rows, tn), lambda j: (0, j)),
        compiler_params=_cparams("arbitrary"),
        name="adaln",
    )(cond, w_ada, b_ada.reshape(1, cols))


def _token_specs(nbc, width, tm=TM):
    return [pl.BlockSpec((tm, width), lambda i: (jnp.minimum(i, nbc - 1), 0)),
            pl.BlockSpec((tm, width), lambda i: (jnp.maximum(i - nbc, 0), 0))]


def _proj_in_kernel(xc_ref, xl_ref, mod_ref, n1_ref, w_ref, *out_refs, ctx_blocks):
    x = jnp.where(pl.program_id(0) < ctx_blocks, xc_ref[...], xl_ref[...])
    ms = jnp.mean(x * x, axis=-1, keepdims=True)
    hn = x * lax.rsqrt(ms + NORM_EPS) * n1_ref[...]
    mod = mod_ref[0]
    sh1 = mod[:, 0:D_MODEL]
    sc1 = mod[:, D_MODEL:2 * D_MODEL]
    h = (hn * (1.0 + sc1) + sh1).astype(BF16)
    c0 = 0
    for ref, (_, width) in zip(out_refs, IN_SPLITS):
        ref[...] = _dot(h, w_ref[:, c0:c0 + width])
        c0 += width


def _proj_in(x_ctx, x_lat, mods3, norm1, w_in_bf, t_lat):
    d = x_ctx.shape[1]
    n = x_ctx.shape[0] + x_lat.shape[0]
    tm = TM_PROJ
    nbc = x_ctx.shape[0] // tm
    per_seq = t_lat // tm
    cols = w_in_bf.shape[1]
    row = lambda i: (i, 0)
    mod_row = lambda i: jnp.where(i < nbc, 0, 1 + (i - nbc) // per_seq)
    return pl.pallas_call(
        functools.partial(_proj_in_kernel, ctx_blocks=nbc),
        out_shape=[jax.ShapeDtypeStruct((n, width), F32) for _, width in IN_SPLITS],
        grid=(n // tm,),
        in_specs=_token_specs(nbc, d, tm) + [
                  pl.BlockSpec((1, 1, 6 * d), lambda i: (mod_row(i), 0, 0)),
                  pl.BlockSpec((1, d), lambda i: (0, 0)),
                  pl.BlockSpec((d, cols), lambda i: (0, 0))],
        out_specs=[pl.BlockSpec((tm, width), row) for _, width in IN_SPLITS],
        compiler_params=_cparams("arbitrary"),
        name="proj_in",
    )(x_ctx, x_lat, mods3, norm1.reshape(1, d), w_in_bf)


def _prep_kernel(r_ref, k_ref, v_ref, lora_ref, q_ref, ka_ref, cos_ref, sin_ref,
                 w0_ref, w2_ref, a0_ref, a2_ref, g2_ref, kkw_ref, kaw_ref, rk_ref,
                 qn_ref, kn_ref, ones_ref, tri_ref, chunk_ones_ref,
                 sc_o, gt_o, vb_o, bonus_o, g_o, qr_o, kn_o, kr_o):
    ones = ones_ref[...]
    chunk_ones = chunk_ones_ref[...]
    r = r_ref[...]
    k = k_ref[...]
    v = v_ref[...]
    vb_o[...] = v.astype(BF16)
    lora = lora_ref[...]
    kk = k * kkw_ref[...]
    kk = kk * lax.rsqrt(_head_sum(kk * kk, ones) + 1e-12)
    th = jnp.tanh(lora[:, 0:LANES]).astype(BF16)
    al = lora[:, LANES:2 * LANES].astype(BF16)
    kd_sum = jnp.zeros_like(k)
    for d in range(2):
        u = w0_ref[d:d + 1, :] + _dot(th, w2_ref[d])
        lw = -float(np.exp(-0.5)) * _sigmoid(u)
        a = _sigmoid(a0_ref[d:d + 1, :] + _dot(al, a2_ref[d]))
        kd = k * (1.0 + (a - 1.0) * kaw_ref[...])
        kd_sum = kd_sum + kd
        b = kk * a
        parts = _split2(lw)
        tri = tri_ref[d]
        L = _dot(tri, parts[0]) + _dot(tri, parts[1])
        Ltot = _dot(chunk_ones, parts[0]) + _dot(chunk_ones, parts[1])
        e_inv = jnp.exp(-L)
        e_rest = jnp.exp(Ltot - L)
        scan_ops = (kk * jnp.exp(L - lw), r * jnp.exp(L), b * e_inv, kd * e_inv, b * e_rest, kd * e_rest)
        for j, op in enumerate(scan_ops):
            sc_o[d, :, j * D_R:(j + 1) * D_R] = op.astype(BF16)
        g_tot = jnp.exp(Ltot)
        for j in range(TM // CHUNK):
            gt_o[d, j] = g_tot[j * CHUNK:j * CHUNK + 1, :]
    bonus_o[...] = _head_sum(r * kd_sum * rk_ref[...], ones) * v
    g_o[...] = _dot(_sigmoid(lora[:, 2 * LANES:3 * LANES]).astype(BF16), g2_ref[...])
    q = q_ref[...]
    qn = q * lax.rsqrt(_head_sum(q * q, ones) * (1.0 / HEAD) + NORM_EPS) * qn_ref[...]
    ka = ka_ref[...]
    ones_kv = ones[0:D_KV, 0:D_KV]
    kn = ka * lax.rsqrt(_head_sum(ka * ka, ones_kv) * (1.0 / HEAD) + NORM_EPS) * kn_ref[...]
    kn_o[...] = kn
    cos = cos_ref[...]
    sin = sin_ref[...]
    half = AXIS_DIM // 2

    def rope(t, cos_t, sin_t):
        width = t.shape[1]
        lane = lax.broadcasted_iota(jnp.int32, t.shape, 1)
        first = (lane % AXIS_DIM) < half
        swapped = jnp.where(first, pltpu.roll(t, width - half, 1), pltpu.roll(t, half, 1))
        return t * cos_t + swapped * sin_t

    cos4 = jnp.concatenate([cos] * (D_A // D_KV), axis=1)
    sin4 = jnp.concatenate([sin] * (D_A // D_KV), axis=1)
    qr_o[...] = (rope(qn, cos4, sin4) * (HEAD ** -0.5)).astype(BF16)
    kr_o[...] = rope(kn, cos, sin).astype(BF16)


def _prep(z, cos_tab, sin_tab, tab_row, p):
    n = z["r"].shape[0]
    row = lambda i: (i, 0)
    full = lambda *shape: pl.BlockSpec(shape, lambda i: (0,) * len(shape))
    tok = lambda width: pl.BlockSpec((TM, width), row)
    tab = pl.BlockSpec((TM, D_KV), lambda i: (tab_row(i), 0))
    cpb = TM // CHUNK
    return pl.pallas_call(
        _prep_kernel,
        out_shape=[jax.ShapeDtypeStruct((2, n, SCAN_OPS * D_R), BF16),
                   jax.ShapeDtypeStruct((2, n // CHUNK, 1, D_R), F32),
                   jax.ShapeDtypeStruct((n, D_R), BF16),
                   jax.ShapeDtypeStruct((n, D_R), F32),
                   jax.ShapeDtypeStruct((n, D_R), F32),
                   jax.ShapeDtypeStruct((n, D_A), BF16),
                   jax.ShapeDtypeStruct((n, D_KV), F32),
                   jax.ShapeDtypeStruct((n, D_KV), BF16)],
        grid=(n // TM,),
        in_specs=[tok(D_R), tok(D_R), tok(D_R), tok(LORA_COLS), tok(D_A), tok(D_KV), tab, tab,
                  full(2, D_R), full(2, LANES, D_R), full(2, D_R), full(2, LANES, D_R),
                  full(GATE_RANK, D_R), full(1, D_R), full(1, D_R), full(1, D_R),
                  full(1, D_A), full(1, D_KV), full(D_R, D_R), full(2, TM, TM), full(TM, TM)],
        out_specs=[pl.BlockSpec((2, TM, SCAN_OPS * D_R), lambda i: (0, i, 0)),
                   pl.BlockSpec((2, cpb, 1, D_R), lambda i: (0, i, 0, 0)),
                   tok(D_R), tok(D_R), tok(D_R), tok(D_A), tok(D_KV), tok(D_KV)],
        compiler_params=_cparams("arbitrary"),
        name="mixer_prep",
    )(z["r"], z["k"], z["v"], z["lora"], z["q"], z["ka"], cos_tab, sin_tab,
      p["w0"], p["w2"], p["a0"], p["a2"], p["g2"], p["kkw"], p["kaw"], p["rk"],
      p["qn"], p["kn"], p["ones"], p["tri"], p["chunk_ones"])


def _scan_kernel(scf_ref, scb_ref, vf_ref, vb_ref, gf_ref, gb_ref, s0_ref,
                 yf_ref, yb_ref, sfin_ref, state):
    c = pl.program_id(1)
    C = CHUNK
    pairs = D_R // LANES
    chains = [(g, d, p) for g in range(SCAN_GROUP) for d in range(2) for p in range(pairs)]

    lane = lax.broadcasted_iota(jnp.int32, (1, LANES), 1)
    low = lane < HEAD

    @pl.when(c == 0)
    def _():
        zero = jnp.zeros((HEAD, HEAD), F32)
        for i, (g, d, p) in enumerate(chains):
            top = jnp.concatenate([s0_ref[d, g, 2 * p], zero], axis=1)
            bot = jnp.concatenate([zero, s0_ref[d, g, 2 * p + 1]], axis=1)
            state[i] = jnp.concatenate([top, bot], axis=0)

    W = 2 * C
    t2 = lax.broadcasted_iota(jnp.int32, (C, W), 0)
    s2 = lax.broadcasted_iota(jnp.int32, (C, W), 1) % C
    strict = (s2 < t2, s2 > t2)
    incl = (s2 <= t2, s2 >= t2)
    eye2 = jnp.where(s2 == t2, 1.0, 0.0)
    first_mat = lax.broadcasted_iota(jnp.int32, (1, W), 1) < C
    r2 = lax.broadcasted_iota(jnp.int32, (LANES, LANES), 0)
    c2 = lax.broadcasted_iota(jnp.int32, (LANES, LANES), 1)
    same_head = (r2 // HEAD) == (c2 // HEAD)

    sc_refs = (scf_ref, scb_ref)
    v_refs = (vf_ref, vb_ref)
    g_refs = (gf_ref, gb_ref)
    y_refs = (yf_ref, yb_ref)

    def operand(chain, j):
        g, d, p = chain
        c0 = j * D_R + p * LANES
        return sc_refs[d][0, g, :, c0:c0 + LANES]

    def stack(x):
        first = low if x.shape[1] == LANES else first_mat
        zero = jnp.zeros_like(x)
        return jnp.concatenate([jnp.where(first, x, zero), jnp.where(first, zero, x)], axis=0)

    Qk = [operand(ch, 0) for ch in chains]
    Qr = [operand(ch, 1) for ch in chains]
    v_p = [v_refs[d][g, :, p * LANES:(p + 1) * LANES] for g, d, p in chains]
    v_st = [stack(x) for x in v_p]
    Z0 = [state[i] for i in range(len(chains))]
    Z0b = [z.astype(BF16) for z in Z0]
    m = [_dot_nt(jnp.concatenate([Qk[i], Qr[i]], axis=0),
                 jnp.concatenate([stack(operand(ch, 2)), stack(operand(ch, 3))], axis=0))
         for i, ch in enumerate(chains)]
    A = [jnp.where(strict[d], m[i][0:C, 0:W], 0.0) for i, (g, d, p) in enumerate(chains)]
    Bm = [jnp.where(strict[d], m[i][0:C, W:2 * W], 0.0).astype(BF16)
          for i, (g, d, p) in enumerate(chains)]
    Pb = [jnp.where(incl[d], m[i][C:2 * C, 0:W], 0.0).astype(BF16)
          for i, (g, d, p) in enumerate(chains)]
    Pk = [jnp.where(incl[d], m[i][C:2 * C, W:2 * W], 0.0).astype(BF16)
          for i, (g, d, p) in enumerate(chains)]
    BV = [_dot(Bm[i], v_st[i]) for i in range(len(chains))]
    T = [eye2 - a for a in A]
    Ap = [_dot(a.astype(BF16), stack(a.astype(BF16))).astype(BF16) for a in A]
    n = 2
    while 2 * n < C:
        R = [_dot(jnp.concatenate([Ap[i], T[i].astype(BF16)], axis=0), stack(Ap[i]))
             for i in range(len(chains))]
        Ap = [x[0:C].astype(BF16) for x in R]
        T = [T[i] + R[i][C:2 * C] for i in range(len(chains))]
        n *= 2
    T = [T[i] + _dot(T[i].astype(BF16), stack(Ap[i])) for i in range(len(chains))]
    WU = [_dot(T[i].astype(BF16),
               jnp.concatenate([stack(Qk[i]), stack(BV[i].astype(BF16))], axis=1))
          for i in range(len(chains))]
    Wq = [x[:, 0:LANES].astype(BF16) for x in WU]
    Uv = [x[:, LANES:2 * LANES] for x in WU]
    PW = [_dot(Pb[i], jnp.concatenate([stack(Wq[i]), stack(Uv[i].astype(BF16))], axis=1))
          for i in range(len(chains))]
    PkV = [_dot(Pk[i], v_st[i]) for i in range(len(chains))]
    U = [_dot_nt(Wq[i], Z0b[i]) + Uv[i] for i in range(len(chains))]
    Yq = [(Qr[i].astype(F32) - PW[i][:, 0:LANES]).astype(BF16) for i in range(len(chains))]
    for i, (g, d, p) in enumerate(chains):
        y_refs[d][g, :, p * LANES:(p + 1) * LANES] = (_dot_nt(Yq[i], Z0b[i]) + PkV[i]
                                                      - PW[i][:, LANES:2 * LANES])
    for i, (g, d, p) in enumerate(chains):
        lhs = jnp.concatenate([v_p[i], (-U[i]).astype(BF16)], axis=0)
        rhs = jnp.concatenate([operand(chains[i], 5), operand(chains[i], 4)], axis=0)
        upd = _dot_tn(lhs, rhs)
        g_tot = g_refs[d][0, g, 0][:, p * LANES:(p + 1) * LANES]
        state[i] = Z0[i] * g_tot + jnp.where(same_head, upd, 0.0)

    @pl.when(c == pl.num_programs(1) - 1)
    def _():
        for i, (g, d, p) in enumerate(chains):
            z = state[i]
            sfin_ref[d, g, 2 * p] = z[0:HEAD, 0:HEAD]
            sfin_ref[d, g, 2 * p + 1] = z[HEAD:2 * HEAD, HEAD:2 * HEAD]


def _rwkv_scan(sc, gt, vb, s0, row0, batch, t_len):
    n = vb.shape[0]
    nch = t_len // CHUNK
    assert row0 % (t_len * SCAN_GROUP) == 0 and batch % SCAN_GROUP == 0
    seq0 = row0 // (t_len * SCAN_GROUP)
    width = SCAN_OPS * D_R
    sc4 = sc.reshape(2, n // t_len, t_len, width)
    gt5 = gt.reshape(2, n // t_len, nch, 1, D_R)
    v3 = vb.reshape(n // t_len, t_len, D_R)
    G = SCAN_GROUP
    fwd = lambda c: c
    bwd = lambda c: nch - 1 - c
    sc_spec = lambda d, at: pl.BlockSpec((1, G, CHUNK, width), lambda s, c: (d, seq0 + s, at(c), 0))
    v_spec = lambda at: pl.BlockSpec((G, CHUNK, D_R), lambda s, c: (seq0 + s, at(c), 0))
    gt_spec = lambda d, at: pl.BlockSpec((1, G, 1, 1, D_R), lambda s, c: (d, seq0 + s, at(c), 0, 0))
    y_spec = lambda at: pl.BlockSpec((G, CHUNK, D_R), lambda s, c: (s, at(c), 0))
    st = pl.BlockSpec((2, G, H_R, HEAD, HEAD), lambda s, c: (0, s, 0, 0, 0))
    return pl.pallas_call(
        _scan_kernel,
        out_shape=[jax.ShapeDtypeStruct((batch, t_len, D_R), F32),
                   jax.ShapeDtypeStruct((batch, t_len, D_R), F32),
                   jax.ShapeDtypeStruct((2, batch, H_R, HEAD, HEAD), F32)],
        grid=(batch // G, nch),
        in_specs=[sc_spec(0, fwd), sc_spec(1, bwd), v_spec(fwd), v_spec(bwd),
                  gt_spec(0, fwd), gt_spec(1, bwd), st],
        out_specs=[y_spec(fwd), y_spec(bwd), st],
        scratch_shapes=[pltpu.VMEM((G * 2 * D_R // LANES, LANES, LANES), F32)],
        compiler_params=_cparams("arbitrary", "arbitrary"),
        name="rwkv_scan",
    )(sc4, sc4, v3, v3, gt5, gt5, s0)


def _attn_kernel(*refs, tq, with_cache):
    if with_cache:
        q_ref, k_ref, v_ref, ck_ref, cv_ref, o_ref = refs
        kx = jnp.concatenate([k_ref[...], ck_ref[0].astype(BF16)], axis=0)
        vx = jnp.concatenate([v_ref[...], cv_ref[0]], axis=0).astype(BF16)
    else:
        q_ref, k_ref, v_ref, o_ref = refs
        kx = k_ref[...]
        vx = v_ref[...].astype(BF16)
    lane = lax.broadcasted_iota(jnp.int32, (1, D_KV), 1)
    low = lane < HEAD
    k_sw = pltpu.roll(kx.astype(F32), HEAD, 1).astype(BF16)
    v_sw = pltpu.roll(vx.astype(F32), HEAD, 1).astype(BF16)
    group = N_HEADS // N_KV
    for p in range(D_A // LANES):
        g = (2 * p) // group
        keep = low if g == 0 else jnp.logical_not(low)
        kd = jnp.where(keep, kx, k_sw)
        vd = jnp.where(keep, vx, v_sw)
        qp = q_ref[:, p * LANES:(p + 1) * LANES]
        zero = jnp.zeros_like(qp)
        qs = jnp.concatenate([jnp.where(low, qp, zero), jnp.where(low, zero, qp)], axis=0)
        s = _dot_nt(qs, kd)
        mx = jnp.max(s, axis=-1, keepdims=True)
        e = jnp.exp(s - mx)
        den = jnp.sum(e, axis=-1, keepdims=True)
        o = _dot(e.astype(BF16), vd) / den
        o_ref[:, p * LANES:(p + 1) * LANES] = jnp.where(low, o[0:tq], o[tq:2 * tq]).astype(BF16)


def _attention(qr, kr, va, row0, batch, t_len, cache_k=None, cache_v=None):
    tq = min(t_len, 256)
    nq = t_len // tq
    qblk0 = row0 // tq
    sblk0 = row0 // t_len
    with_cache = cache_k is not None
    in_specs = [pl.BlockSpec((tq, D_A), lambda b, i: (qblk0 + b * nq + i, 0)),
                pl.BlockSpec((t_len, D_KV), lambda b, i: (sblk0 + b, 0)),
                pl.BlockSpec((t_len, D_KV), lambda b, i: (sblk0 + b, 0))]
    args = [qr, kr, va]
    if with_cache:
        past = cache_k.shape[1]
        in_specs += [pl.BlockSpec((1, past, D_KV), lambda b, i: (b, 0, 0))] * 2
        args += [cache_k, cache_v]
    return pl.pallas_call(
        functools.partial(_attn_kernel, tq=tq, with_cache=with_cache),
        out_shape=jax.ShapeDtypeStruct((batch * t_len, D_A), BF16),
        grid=(batch, nq),
        in_specs=in_specs,
        out_specs=pl.BlockSpec((tq, D_A), lambda b, i: (b * nq + i, 0)),
        compiler_params=_cparams("arbitrary", "arbitrary"),
        name="attention_cache" if with_cache else "attention_ctx",
    )(*args)


def _post_kernel(yfc_ref, yfl_ref, ybc_ref, ybl_ref, bonus_ref, g_ref, yac_ref, yal_ref, gr_ref, ga_ref,
                 xc_ref, xl_ref,
                 mod_ref, lnw_ref, lnb_ref, wur_ref, wua_ref, wo_ref, n2_ref, wr_ref, br_ref, ones_ref,
                 x1_o, h2_o, idx_o, gate_o, *, ctx_blocks):
    ones = ones_ref[...]
    is_ctx = pl.program_id(0) < ctx_blocks
    y_a = jnp.where(is_ctx, yac_ref[...], yal_ref[...])
    x = jnp.where(is_ctx, xc_ref[...], xl_ref[...])
    y = jnp.where(is_ctx, yfc_ref[...] + ybc_ref[...], yfl_ref[...] + ybl_ref[...])
    mu = _head_sum(y, ones) * (1.0 / HEAD)
    yc = y - mu
    var = _head_sum(yc * yc, ones) * (1.0 / HEAD)
    yn = yc * lax.rsqrt(var + GN_EPS) * lnw_ref[...] + lnb_ref[...] + bonus_ref[...]
    y_r = (yn * g_ref[...]).astype(BF16)
    merged = (_sigmoid(gr_ref[...]) * _dot(y_r, wur_ref[...])
              + _sigmoid(ga_ref[...]) * _dot(y_a, wua_ref[...]))
    mix = _dot(merged.astype(BF16), wo_ref[...])
    mod = mod_ref[0]
    g1 = mod[:, 2 * D_MODEL:3 * D_MODEL]
    sh2 = mod[:, 3 * D_MODEL:4 * D_MODEL]
    sc2 = mod[:, 4 * D_MODEL:5 * D_MODEL]
    x1 = x + g1 * mix
    x1_o[...] = x1
    ms = jnp.mean(x1 * x1, axis=-1, keepdims=True)
    h2 = x1 * lax.rsqrt(ms + NORM_EPS) * n2_ref[...] * (1.0 + sc2) + sh2
    _store_token_tiles(h2_o, h2)
    hh, hl = _split2(h2)
    wh, wl = _split2(wr_ref[...])
    hi_terms = _dot(hh, jnp.concatenate([wh, wl], axis=1))
    logits = hi_terms[:, 0:LANES] + hi_terms[:, LANES:2 * LANES] + _dot(hl, wh) + br_ref[...]
    lane_i = lax.broadcasted_iota(jnp.int32, logits.shape, 1)
    lane = lane_i.astype(F32)
    neg = jnp.float32(-jnp.inf)
    cur = jnp.where(lane_i < N_EXPERTS, logits, neg)
    vals, idxs = [], []
    for _ in range(TOP_K):
        mx = jnp.max(cur, axis=-1, keepdims=True)
        ix = jnp.min(jnp.where(cur == mx, lane, float(LANES)), axis=-1, keepdims=True)
        vals.append(mx)
        idxs.append(ix)
        cur = jnp.where(lane == ix, neg, cur)
    es = [jnp.exp(val - vals[0]) for val in vals]
    den = es[0] + es[1] + es[2] + es[3]
    idx_out = jnp.zeros(logits.shape, jnp.int32)
    gate_out = jnp.zeros(logits.shape, F32)
    for j in range(TOP_K):
        idx_out = jnp.where(lane_i == j, idxs[j].astype(jnp.int32), idx_out)
        gate_out = jnp.where(lane_i == j, es[j] / den, gate_out)
    idx_o[...] = idx_out
    gate_o[...] = gate_out


def _post(y_f, y_b, bonus, g, ya_ctx, ya_lat, gate_r, gate_a, x_ctx, x_lat, mods3, mod_row, p):
    n = x_ctx.shape[0] + x_lat.shape[0]
    tm = TM_POST
    nbc = ya_ctx.shape[0] // tm
    row = lambda i: (i, 0)
    full = lambda *shape: pl.BlockSpec(shape, lambda i: (0,) * len(shape))
    tok = lambda width: pl.BlockSpec((tm, width), row)
    pair = lambda width: _token_specs(nbc, width, tm)
    return pl.pallas_call(
        functools.partial(_post_kernel, ctx_blocks=nbc),
        out_shape=[jax.ShapeDtypeStruct((n, D_MODEL), F32),
                   jax.ShapeDtypeStruct((n * TILE_ROWS, LANES), F32),
                   jax.ShapeDtypeStruct((n, LANES), jnp.int32),
                   jax.ShapeDtypeStruct((n, LANES), F32)],
        grid=(n // tm,),
        in_specs=pair(D_R) + pair(D_R) + [tok(D_R), tok(D_R)]
                 + pair(D_A) + [tok(D_MODEL), tok(D_MODEL)] + pair(D_MODEL) + [
                  pl.BlockSpec((1, 1, 6 * D_MODEL), lambda i: (mod_row(i), 0, 0)),
                  full(1, D_R), full(1, D_R), full(D_R, D_MODEL), full(D_A, D_MODEL),
                  full(D_MODEL, D_MODEL), full(1, D_MODEL), full(D_MODEL, LANES), full(1, LANES),
                  full(D_R, D_R)],
        out_specs=[tok(D_MODEL), pl.BlockSpec((tm * TILE_ROWS, LANES), row), tok(LANES), tok(LANES)],
        compiler_params=_cparams("arbitrary"),
        name="mixer_post",
    )(*y_f, *y_b, bonus, g, ya_ctx, ya_lat, gate_r, gate_a, x_ctx, x_lat, mods3,
      p["lnw"], p["lnb"], p["wur"], p["wua"], p["wo"], p["n2"], p["wr"], p["br"], p["ones"])


def _moe_kernel(blk_e_ref, blk_first_ref, next_e_ref, used_ref,
                src0_ref, src_next_ref, dst_prev_ref, h_hbm, w1_hbm, b1_ref, w2_hbm, b2_ref,
                out_hbm, w1f, w2f, w1b, w2b, xs, ys, sem_g, sem_s, sem_t, sem_w, *, n_rows):
    b = pl.program_id(0)
    used = used_ref[0]
    R = MOE_ROWS
    T = TILE_ROWS

    def tile(ref, i):
        return ref.at[pl.ds(pl.multiple_of(i * T, T), T)]

    def gather(table_ref, buf):
        for i in range(R):
            pltpu.make_async_copy(tile(h_hbm, table_ref[0, 0, i]), tile(xs.at[buf], i),
                                  sem_g.at[buf]).start(priority=i % 2)

    def scatter_prev(buf):
        for i in range(R):
            pltpu.make_async_copy(tile(ys.at[buf], i), tile(out_hbm, dst_prev_ref[0, 0, i]),
                                  sem_s.at[buf]).start(priority=i % 2)

    def fetch_weights(e):
        pltpu.make_async_copy(w1_hbm.at[e], w1f, sem_w.at[0]).start()
        pltpu.make_async_copy(w2_hbm.at[e], w2f, sem_w.at[1]).start()

    @pl.when(b == 0)
    def _():
        ys[...] = jnp.zeros_like(ys)
        for s in range(2):
            band = pltpu.make_async_copy(ys.at[s], out_hbm.at[pl.ds((n_rows + s * R) * T, R * T)], sem_t)
            band.start()
            band.wait()
        gather(src0_ref, 0)
        fetch_weights(blk_e_ref[0])

    @pl.when(jnp.logical_and(blk_first_ref[b] == 1, b < used))
    def _():
        pltpu.make_async_copy(w1_hbm.at[0], w1f, sem_w.at[0]).wait()
        pltpu.make_async_copy(w2_hbm.at[0], w2f, sem_w.at[1]).wait()
        w1b[...] = w1f[...].astype(BF16)
        w2b[...] = w2f[...].astype(BF16)

        @pl.when(next_e_ref[b] >= 0)
        def _():
            fetch_weights(next_e_ref[b])

    def step(p):
        @pl.when(b <= used)
        def _():
            pltpu.make_async_copy(h_hbm.at[pl.ds(0, R * T)], xs.at[p], sem_g.at[p]).wait()

        @pl.when(jnp.logical_and(b >= 1, b <= used + 1))
        def _():
            pltpu.make_async_copy(ys.at[p], out_hbm.at[pl.ds(0, R * T)], sem_s.at[p]).wait()

        @pl.when(b < used)
        def _():
            gather(src_next_ref, 1 - p)
            scatter_prev(1 - p)
            hb = _dot(_load_token_tiles(xs.at[p]).astype(BF16), w1b[...]) + b1_ref[0]
            glu = jnp.minimum(hb[:, 0:D_FF], SWIGLU_LIMIT)
            lin = jnp.clip(hb[:, D_FF:2 * D_FF], -SWIGLU_LIMIT, SWIGLU_LIMIT)
            act = glu * _sigmoid(SWIGLU_ALPHA * glu) * (lin + 1.0)
            _store_token_tiles(ys.at[p], _dot(act.astype(BF16), w2b[...]) + b2_ref[0])

        @pl.when(b == used)
        def _():
            scatter_prev(1 - p)

    for p in range(2):
        pl.when(b % 2 == p)(functools.partial(step, p))


def _moe(h2, src, dst_prev, blk_e, blk_first, next_e, used, w1, b1, w2, b2):
    d = D_MODEL
    n = h2.shape[0] // TILE_ROWS
    nb = src.shape[0]
    steps = nb + 2
    cur = lambda b, *_: (jnp.minimum(b, nb), 0, 0)
    nxt = lambda b, *_: (jnp.minimum(b + 1, nb - 1), 0, 0)
    table = lambda index_map: pl.BlockSpec((1, 1, MOE_ROWS), index_map, memory_space=pltpu.SMEM)
    expert = lambda b, blk_e, *_: (blk_e[jnp.minimum(b, nb - 1)], 0, 0)
    grid_spec = pltpu.PrefetchScalarGridSpec(
        num_scalar_prefetch=4,
        grid=(steps,),
        in_specs=[table(lambda b, *_: (0, 0, 0)), table(nxt), table(cur),
                  pl.BlockSpec(memory_space=pl.ANY),
                  pl.BlockSpec(memory_space=pl.ANY),
                  pl.BlockSpec((1, 1, 2 * D_FF), expert),
                  pl.BlockSpec(memory_space=pl.ANY),
                  pl.BlockSpec((1, 1, d), expert)],
        out_specs=pl.BlockSpec(memory_space=pl.ANY),
        scratch_shapes=[pltpu.VMEM((d, 2 * D_FF), F32),
                        pltpu.VMEM((D_FF, d), F32),
                        pltpu.VMEM((d, 2 * D_FF), BF16),
                        pltpu.VMEM((D_FF, d), BF16),
                        pltpu.VMEM((2, MOE_ROWS * TILE_ROWS, LANES), F32),
                        pltpu.VMEM((2, MOE_ROWS * TILE_ROWS, LANES), F32),
                        pltpu.SemaphoreType.DMA((2,)),
                        pltpu.SemaphoreType.DMA((2,)),
                        pltpu.SemaphoreType.DMA(()),
                        pltpu.SemaphoreType.DMA((2,))])
    tab3 = lambda t: t.reshape(-1, 1, MOE_ROWS)
    pad = lambda t: jnp.concatenate([t, jnp.zeros((steps - nb,), jnp.int32)])
    return pl.pallas_call(
        functools.partial(_moe_kernel, n_rows=n * TOP_K),
        out_shape=jax.ShapeDtypeStruct(((n * TOP_K + 2 * MOE_ROWS) * TILE_ROWS, LANES), F32),
        grid_spec=grid_spec,
        compiler_params=pltpu.CompilerParams(dimension_semantics=("arbitrary",),
                                             vmem_limit_bytes=VMEM_LIMIT, has_side_effects=True),
        name="moe_experts",
    )(blk_e, pad(blk_first), next_e, used, tab3(src), tab3(src), tab3(dst_prev), h2, w1,
      b1.reshape(N_EXPERTS, 1, 2 * D_FF), w2, b2.reshape(N_EXPERTS, 1, d))


def _route(top_idx, n):
    nk = n * TOP_K
    cap = nk + N_EXPERTS * MOE_ROWS
    nb = cap // MOE_ROWS
    flat_e = top_idx.reshape(nk)
    order = jnp.argsort(flat_e, stable=True).astype(jnp.int32)
    counts = jnp.sum((flat_e[:, None] == jnp.arange(N_EXPERTS, dtype=jnp.int32)[None, :])
                     .astype(jnp.int32), axis=0)
    start = jnp.cumsum(counts) - counts
    n_blk = (counts + MOE_ROWS - 1) // MOE_ROWS
    blk_end = jnp.cumsum(n_blk)
    first_blk = blk_end - n_blk
    experts = jnp.arange(N_EXPERTS, dtype=jnp.int32)
    blk = jnp.arange(nb, dtype=jnp.int32)
    blk_e = jnp.minimum(jnp.sum((blk[:, None] >= blk_end[None, :]).astype(jnp.int32), axis=1),
                        N_EXPERTS - 1)
    onehot = (blk_e[:, None] == experts[None, :]).astype(jnp.int32)
    pick = lambda t: jnp.sum(onehot * t[None, :], axis=1)
    blk_first_of_e, blk_count, blk_start = pick(first_blk), pick(counts), pick(start)
    blk_first = (blk == blk_first_of_e).astype(jnp.int32)
    blk_after = pick(blk_end)
    next_e = jnp.where(blk_after < blk_end[-1], blk_e[jnp.minimum(blk_after, nb - 1)], -1).astype(jnp.int32)
    row = jnp.arange(MOE_ROWS, dtype=jnp.int32)[None, :]
    within = (blk - blk_first_of_e)[:, None] * MOE_ROWS + row
    valid = jnp.logical_and(within < blk_count[:, None], (blk < blk_end[-1])[:, None])
    j = order[jnp.clip(blk_start[:, None] + within, 0, nk - 1)]
    src = jnp.where(valid, j // TOP_K, 0).astype(jnp.int32)
    scratch_row = nk + (blk % 2)[:, None] * MOE_ROWS + row
    dst = jnp.where(valid, (j % TOP_K) * n + j // TOP_K, scratch_row).astype(jnp.int32)
    dst_prev = jnp.concatenate([nk + MOE_ROWS + row, dst], axis=0)
    return (src, dst_prev, blk_e.astype(jnp.int32), blk_first, next_e,
            blk_end[-1:].astype(jnp.int32))


def _final_kernel(x1_ref, y0_ref, y1_ref, y2_ref, y3_ref, gate_ref, mod_ref, nf_ref, oc_ref, ol_ref,
                  *, ctx_blocks):
    gates = gate_ref[...]
    ffn = gates[:, 0:1] * _load_token_tiles(y0_ref)
    for j, ref in enumerate((y1_ref, y2_ref, y3_ref), start=1):
        ffn = ffn + gates[:, j:j + 1] * _load_token_tiles(ref)
    g2 = mod_ref[0][:, 5 * D_MODEL:6 * D_MODEL]
    x2 = x1_ref[...] + g2 * ffn
    ms = jnp.mean(x2 * x2, axis=-1, keepdims=True)
    y = x2 * lax.rsqrt(ms + NORM_EPS) * nf_ref[...]
    i = pl.program_id(0)

    @pl.when(i < ctx_blocks)
    def _():
        oc_ref[...] = y

    @pl.when(i >= ctx_blocks)
    def _():
        ol_ref[...] = y


def _final(x1, ys, gates, mods3, mod_row, norm_f, nc):
    n, d = x1.shape
    tm = TM_POST
    nb = n // tm
    nbc = nc // tm
    row = lambda i: (i, 0)
    return pl.pallas_call(
        functools.partial(_final_kernel, ctx_blocks=nbc),
        out_shape=[jax.ShapeDtypeStruct((nc, d), F32), jax.ShapeDtypeStruct((n - nc, d), F32)],
        grid=(nb,),
        in_specs=[pl.BlockSpec((tm, d), row)]
                 + [pl.BlockSpec((tm * TILE_ROWS, LANES), functools.partial(lambda j, i: (j * nb + i, 0), j))
                    for j in range(TOP_K)]
                 + [pl.BlockSpec((tm, LANES), row),
                    pl.BlockSpec((1, 1, 6 * d), lambda i: (mod_row(i), 0, 0)),
                    pl.BlockSpec((1, d), lambda i: (0, 0))],
        out_specs=_token_specs(nbc, d, tm),
        compiler_params=_cparams("arbitrary"),
        name="combine_final",
    )(x1, ys, ys, ys, ys, gates, mods3, norm_f.reshape(1, d))


def _rope_tables(t_ctx, t_lat):
    rows = np.arange(t_lat) // GRID_W
    cols = np.arange(t_lat) % GRID_W
    inv = ROPE_THETA ** (-np.arange(0, AXIS_DIM, 2, dtype=np.float32) / AXIS_DIM)
    inv = jnp.asarray(inv, F32)
    ang_r = jnp.asarray(rows, F32)[:, None] * inv[None, :]
    ang_c = jnp.asarray(cols, F32)[:, None] * inv[None, :]
    cos = jnp.concatenate([jnp.cos(ang_r)] * 2 + [jnp.cos(ang_c)] * 2, axis=1)
    sin = jnp.concatenate([-jnp.sin(ang_r), jnp.sin(ang_r), -jnp.sin(ang_c), jnp.sin(ang_c)], axis=1)
    cos = jnp.concatenate([jnp.ones((t_ctx, HEAD), F32), cos], axis=0)
    sin = jnp.concatenate([jnp.zeros((t_ctx, HEAD), F32), sin], axis=0)
    return jnp.concatenate([cos, cos], axis=1), jnp.concatenate([sin, sin], axis=1)


def kernel(x_prompt, x_sample, cache_k, cache_v, state_rwkv_fwd, state_rwkv_bwd, c, c_ctx, w_ada, b_ada, norm1, norm2, w_in, rw_w0, rw_w2, rw_a0, rw_a2, rw_g2, rw_kk, rw_ka, rw_rk, rw_ln_w, rw_ln_b, q_norm, k_norm, w_up_r, w_up_a, w_out, w_router, b_router, w_moe_in, b_moe_in, w_moe_out, b_moe_out, norm_f):
    depth = w_in.shape[0]
    assert depth == 1, "single trunk layer"
    bc, tc, d = x_prompt.shape
    bl, tl, _ = x_sample.shape
    nc, nl = bc * tc, bl * tl
    n = nc + nl
    assert d == D_MODEL and TM % tc == 0 and tl % TM == 0 and nc % tl == 0
    assert all(nc % tm == 0 and tl % tm == 0 for tm in (TM_PROJ, TM_POST))
    l = 0

    x_ctx, x_lat = x_prompt.reshape(nc, d), x_sample.reshape(nl, d)
    cond = jnp.concatenate([c_ctx[None, :], c, jnp.zeros((16 - 1 - bl, d), F32)], axis=0)
    mods3 = _adaln(cond, w_ada[l], b_ada[l]).reshape(16, 1, 6 * d)
    nbc = nc // TM
    per_seq = tl // TM
    tab_row = lambda i: jnp.where(i < nbc, 0, 1 + (i - nbc) % per_seq)
    mod_row = lambda i: jnp.where(i < nc // TM_POST, 0, 1 + (i - nc // TM_POST) // (tl // TM_POST))

    z = dict(zip([name for name, _ in IN_SPLITS],
                 _proj_in(x_ctx, x_lat, mods3, norm1[l], w_in[l].astype(BF16), tl)))

    ones_bd = jnp.asarray(np.kron(np.eye(H_R), np.ones((HEAD, HEAD))), BF16)
    pad_lo = lambda w: jnp.concatenate([w, jnp.zeros_like(w)], axis=0)
    pad_hi = lambda w: jnp.concatenate([jnp.zeros_like(w), w], axis=0)
    prep_p = dict(
        w0=rw_w0[l], a0=rw_a0[l],
        w2=jnp.stack([pad_lo(rw_w2[l, 0]), pad_hi(rw_w2[l, 1])]).astype(BF16),
        a2=jnp.stack([pad_lo(rw_a2[l, 0]), pad_hi(rw_a2[l, 1])]).astype(BF16),
        g2=rw_g2[l].astype(BF16),
        kkw=rw_kk[l].reshape(1, D_R), kaw=rw_ka[l].reshape(1, D_R), rk=rw_rk[l].reshape(1, D_R),
        qn=jnp.tile(q_norm[l], N_HEADS).reshape(1, D_A), kn=jnp.tile(k_norm[l], N_KV).reshape(1, D_KV),
        ones=ones_bd,
        tri=jnp.asarray(np.stack([np.kron(np.eye(TM // CHUNK), np.tril(np.ones((CHUNK, CHUNK)))),
                                  np.kron(np.eye(TM // CHUNK), np.triu(np.ones((CHUNK, CHUNK))))]), BF16),
        chunk_ones=jnp.asarray(np.kron(np.eye(TM // CHUNK), np.ones((CHUNK, CHUNK))), BF16))
    cos_tab, sin_tab = _rope_tables(TM, tl)
    sc, gt, vb, bonus, g, qr, kn, kr = _prep(z, cos_tab, sin_tab, tab_row, prep_p)

    yf_c, yb_c, s_fin = _rwkv_scan(sc, gt, vb, jnp.zeros((2, bc, H_R, HEAD, HEAD), F32), 0, bc, tc)
    yf_l, yb_l, _ = _rwkv_scan(sc, gt, vb, jnp.stack([state_rwkv_fwd[:, l], state_rwkv_bwd[:, l]]),
                               nc, bl, tl)
    y_f = (yf_c.reshape(nc, D_R), yf_l.reshape(nl, D_R))
    y_b = (yb_c.reshape(nc, D_R), yb_l.reshape(nl, D_R))

    ya_ctx = _attention(qr, kr, z["va"], 0, bc, tc)
    ya_lat = _attention(qr, kr, z["va"], nc, bl, tl,
                        cache_k[:, l].reshape(bl, -1, D_KV), cache_v[:, l].reshape(bl, -1, D_KV))

    post_p = dict(
        lnw=rw_ln_w[l].reshape(1, D_R), lnb=rw_ln_b[l].reshape(1, D_R),
        wur=w_up_r[l].astype(BF16), wua=w_up_a[l].astype(BF16), wo=w_out[l].astype(BF16),
        n2=norm2[l].reshape(1, d),
        wr=jnp.pad(w_router[l], ((0, 0), (0, LANES - N_EXPERTS))),
        br=jnp.pad(b_router[l], (0, LANES - N_EXPERTS)).reshape(1, LANES),
        ones=ones_bd)
    x1, h2, top_idx, gates = _post(y_f, y_b, bonus, g, ya_ctx, ya_lat, z["gate_r"], z["gate_a"],
                                   x_ctx, x_lat, mods3, mod_row, post_p)

    src, dst, blk_e, blk_first, next_e, used = _route(top_idx[:, :TOP_K], n)
    ys = _moe(h2, src, dst, blk_e, blk_first, next_e, used,
              w_moe_in[l], b_moe_in[l], w_moe_out[l], b_moe_out[l])
    y_ctx, y_lat = _final(x1, ys, gates, mods3, mod_row, norm_f, nc)

    y_prompt = y_ctx.reshape(bc, tc, d)
    y_sample = y_lat.reshape(bl, tl, d)
    new_cache_k = kn[:nc].reshape(bc, 1, tc, N_KV, HEAD)
    new_cache_v = z["va"][:nc].reshape(bc, 1, tc, N_KV, HEAD)
    new_state_fwd = s_fin[0][:, None]
    new_state_bwd = s_fin[1][:, None]
    return (y_prompt, y_sample, new_cache_k, new_cache_v, new_state_fwd, new_state_bwd)
```

```python
import functools

import numpy as np
import jax
import jax.numpy as jnp
from jax import lax
from jax.experimental import pallas as pl
from jax.experimental.pallas import tpu as pltpu

F32 = jnp.float32
BF16 = jnp.bfloat16

D_MODEL = 1024
GRID_W = 64
NORM_EPS = 1e-6
HEAD = 64
H_R = 8
D_R = H_R * HEAD
DECAY_RANK = 64
AAA_RANK = 64
GATE_RANK = 128
GN_EPS = 64e-5
N_HEADS = 8
N_KV = 2
D_A = N_HEADS * HEAD
D_KV = N_KV * HEAD
AXIS_DIM = HEAD // 2
ROPE_THETA = 10000.0
N_EXPERTS = 32
TOP_K = 4
D_FF = D_MODEL
SWIGLU_ALPHA = 1.702
SWIGLU_LIMIT = 7.0
LORA_COLS = 2 * DECAY_RANK + 2 * AAA_RANK + GATE_RANK
IN_SPLITS = (("r", D_R), ("k", D_R), ("v", D_R), ("lora", LORA_COLS), ("q", D_A),
             ("ka", D_KV), ("va", D_KV), ("gate_r", D_MODEL), ("gate_a", D_MODEL))

LANES = 128
TM = 256
TM_POST = 512
TM_PROJ = 512
ATTN_ROWS = 512
CHUNK = 64
SCAN_OPS = 4
SCAN_GROUP = 4
MOE_ROWS = 256
VMEM_LIMIT = 56 * 1024 * 1024


def _cparams(*sem):
    return pltpu.CompilerParams(dimension_semantics=sem, vmem_limit_bytes=VMEM_LIMIT)


def _dot(a, b):
    return jnp.dot(a, b, preferred_element_type=F32)


def _dot_nt(a, b):
    return lax.dot_general(a, b, (((1,), (1,)), ((), ())), preferred_element_type=F32)


def _dot_tn(a, b):
    return lax.dot_general(a, b, (((0,), (0,)), ((), ())), preferred_element_type=F32)


def _split2(x):
    hi = x.astype(BF16)
    lo = (x - hi.astype(F32)).astype(BF16)
    return hi, lo


def _head_sum(x, ones_bd):
    return _dot(x.astype(BF16), ones_bd)


def _sigmoid(x):
    return 0.5 * jnp.tanh(0.5 * x) + 0.5


TILE_ROWS = D_MODEL // LANES


def _store_token_tiles(ref, x):
    rows = x.shape[0]
    for j in range(TILE_ROWS):
        ref[pl.ds(j, rows, stride=TILE_ROWS), :] = x[:, j * LANES:(j + 1) * LANES]


def _load_token_tiles(ref):
    rows = ref.shape[0] // TILE_ROWS
    return jnp.concatenate([ref[pl.ds(j, rows, stride=TILE_ROWS), :] for j in range(TILE_ROWS)], axis=1)


def _adaln_kernel(c_ref, w_ref, b_ref, o_ref):
    c = c_ref[...]
    s = (c * _sigmoid(c)).astype(BF16)
    o_ref[...] = _dot(s, w_ref[...].astype(BF16)) + b_ref[...]


def _adaln(cond, w_ada, b_ada):
    rows, d = cond.shape
    cols = w_ada.shape[1]
    tn = 512
    return pl.pallas_call(
        _adaln_kernel,
        out_shape=jax.ShapeDtypeStruct((rows, cols), F32),
        grid=(cols // tn,),
        in_specs=[pl.BlockSpec((rows, d), lambda j: (0, 0)),
                  pl.BlockSpec((d, tn), lambda j: (0, j)),
                  pl.BlockSpec((1, tn), lambda j: (0, j))],
        out_specs=pl.BlockSpec((rows, tn), lambda j: (0, j)),
        compiler_params=_cparams("arbitrary"),
        name="adaln",
    )(cond, w_ada, b_ada.reshape(1, cols))


def _token_specs(nbc, width, tm=TM):
    return [pl.BlockSpec((tm, width), lambda i: (jnp.minimum(i, nbc - 1), 0)),
            pl.BlockSpec((tm, width), lambda i: (jnp.maximum(i - nbc, 0), 0))]


def _proj_in_kernel(xc_ref, xl_ref, mod_ref, n1_ref, w_ref, *out_refs, ctx_blocks):
    x = jnp.where(pl.program_id(0) < ctx_blocks, xc_ref[...], xl_ref[...])
    ms = jnp.mean(x * x, axis=-1, keepdims=True)
    hn = x * lax.rsqrt(ms + NORM_EPS) * n1_ref[...]
    mod = mod_ref[0]
    sh1 = mod[:, 0:D_MODEL]
    sc1 = mod[:, D_MODEL:2 * D_MODEL]
    h = (hn * (1.0 + sc1) + sh1).astype(BF16)
    c0 = 0
    for ref, (_, width) in zip(out_refs, IN_SPLITS):
        ref[...] = _dot(h, w_ref[:, c0:c0 + width])
        c0 += width


def _proj_in(x_ctx, x_lat, mods3, norm1, w_in_bf, t_lat):
    d = x_ctx.shape[1]
    n = x_ctx.shape[0] + x_lat.shape[0]
    tm = TM_PROJ
    nbc = x_ctx.shape[0] // tm
    per_seq = t_lat // tm
    cols = w_in_bf.shape[1]
    row = lambda i: (i, 0)
    mod_row = lambda i: jnp.where(i < nbc, 0, 1 + (i - nbc) // per_seq)
    return pl.pallas_call(
        functools.partial(_proj_in_kernel, ctx_blocks=nbc),
        out_shape=[jax.ShapeDtypeStruct((n, width), F32) for _, width in IN_SPLITS],
        grid=(n // tm,),
        in_specs=_token_specs(nbc, d, tm) + [
                  pl.BlockSpec((1, 1, 6 * d), lambda i: (mod_row(i), 0, 0)),
                  pl.BlockSpec((1, d), lambda i: (0, 0)),
                  pl.BlockSpec((d, cols), lambda i: (0, 0))],
        out_specs=[pl.BlockSpec((tm, width), row) for _, width in IN_SPLITS],
        compiler_params=_cparams("arbitrary"),
        name="proj_in",
    )(x_ctx, x_lat, mods3, norm1.reshape(1, d), w_in_bf)


def _prep_kernel(r_ref, k_ref, v_ref, lora_ref, q_ref, ka_ref, cos_ref, sin_ref,
                 w0_ref, w2_ref, a0_ref, a2_ref, g2_ref, kkw_ref, kaw_ref, rk_ref,
                 qn_ref, kn_ref, ones_ref, tri_ref, chunk_ones_ref,
                 sc_o, gt_o, vb_o, bonus_o, g_o, qr_o, kn_o, kr_o):
    ones = ones_ref[...]
    chunk_ones = chunk_ones_ref[...]
    r = r_ref[...]
    k = k_ref[...]
    v = v_ref[...]
    vb_o[...] = v.astype(BF16)
    lora = lora_ref[...]
    kk = k * kkw_ref[...]
    kk = kk * lax.rsqrt(_head_sum(kk * kk, ones) + 1e-12)
    th = jnp.tanh(lora[:, 0:LANES]).astype(BF16)
    al = lora[:, LANES:2 * LANES].astype(BF16)
    kd_sum = jnp.zeros_like(k)
    for d in range(2):
        u = w0_ref[d:d + 1, :] + _dot(th, w2_ref[d])
        lw = -float(np.exp(-0.5)) * _sigmoid(u)
        a = _sigmoid(a0_ref[d:d + 1, :] + _dot(al, a2_ref[d]))
        kd = k * (1.0 + (a - 1.0) * kaw_ref[...])
        kd_sum = kd_sum + kd
        b = kk * a
        parts = _split2(lw)
        tri = tri_ref[d]
        L = _dot(tri, parts[0]) + _dot(tri, parts[1])
        Ltot = _dot(chunk_ones, parts[0]) + _dot(chunk_ones, parts[1])
        e_inv = jnp.exp(-L)
        scan_ops = (kk * jnp.exp(L - lw), r * jnp.exp(L), b * e_inv, kd * e_inv)
        for j, op in enumerate(scan_ops):
            sc_o[d, :, j * D_R:(j + 1) * D_R] = op.astype(BF16)
        g_tot = jnp.exp(Ltot)
        for j in range(TM // CHUNK):
            gt_o[d, j] = g_tot[j * CHUNK:j * CHUNK + 1, :]
    bonus_o[...] = _head_sum(r * kd_sum * rk_ref[...], ones) * v
    g_o[...] = _dot(_sigmoid(lora[:, 2 * LANES:3 * LANES]).astype(BF16), g2_ref[...])
    q = q_ref[...]
    qn = q * lax.rsqrt(_head_sum(q * q, ones) * (1.0 / HEAD) + NORM_EPS) * qn_ref[...]
    ka = ka_ref[...]
    ones_kv = ones[0:D_KV, 0:D_KV]
    kn = ka * lax.rsqrt(_head_sum(ka * ka, ones_kv) * (1.0 / HEAD) + NORM_EPS) * kn_ref[...]
    kn_o[...] = kn
    cos = cos_ref[...]
    sin = sin_ref[...]
    half = AXIS_DIM // 2

    def rope(t, cos_t, sin_t):
        width = t.shape[1]
        lane = lax.broadcasted_iota(jnp.int32, t.shape, 1)
        first = (lane % AXIS_DIM) < half
        swapped = jnp.where(first, pltpu.roll(t, width - half, 1), pltpu.roll(t, half, 1))
        return t * cos_t + swapped * sin_t

    cos4 = jnp.concatenate([cos] * (D_A // D_KV), axis=1)
    sin4 = jnp.concatenate([sin] * (D_A // D_KV), axis=1)
    qr_o[...] = (rope(qn, cos4, sin4) * (HEAD ** -0.5)).astype(BF16)
    kr_o[...] = rope(kn, cos, sin).astype(BF16)


def _prep(z, cos_tab, sin_tab, tab_row, p):
    n = z["r"].shape[0]
    row = lambda i: (i, 0)
    full = lambda *shape: pl.BlockSpec(shape, lambda i: (0,) * len(shape))
    tok = lambda width: pl.BlockSpec((TM, width), row)
    tab = pl.BlockSpec((TM, D_KV), lambda i: (tab_row(i), 0))
    cpb = TM // CHUNK
    return pl.pallas_call(
        _prep_kernel,
        out_shape=[jax.ShapeDtypeStruct((2, n, SCAN_OPS * D_R), BF16),
                   jax.ShapeDtypeStruct((2, n // CHUNK, 1, D_R), F32),
                   jax.ShapeDtypeStruct((n, D_R), BF16),
                   jax.ShapeDtypeStruct((n, D_R), F32),
                   jax.ShapeDtypeStruct((n, D_R), F32),
                   jax.ShapeDtypeStruct((n, D_A), BF16),
                   jax.ShapeDtypeStruct((n, D_KV), F32),
                   jax.ShapeDtypeStruct((n, D_KV), BF16)],
        grid=(n // TM,),
        in_specs=[tok(D_R), tok(D_R), tok(D_R), tok(LORA_COLS), tok(D_A), tok(D_KV), tab, tab,
                  full(2, D_R), full(2, LANES, D_R), full(2, D_R), full(2, LANES, D_R),
                  full(GATE_RANK, D_R), full(1, D_R), full(1, D_R), full(1, D_R),
                  full(1, D_A), full(1, D_KV), full(D_R, D_R), full(2, TM, TM), full(TM, TM)],
        out_specs=[pl.BlockSpec((2, TM, SCAN_OPS * D_R), lambda i: (0, i, 0)),
                   pl.BlockSpec((2, cpb, 1, D_R), lambda i: (0, i, 0, 0)),
                   tok(D_R), tok(D_R), tok(D_R), tok(D_A), tok(D_KV), tok(D_KV)],
        compiler_params=_cparams("arbitrary"),
        name="mixer_prep",
    )(z["r"], z["k"], z["v"], z["lora"], z["q"], z["ka"], cos_tab, sin_tab,
      p["w0"], p["w2"], p["a0"], p["a2"], p["g2"], p["kkw"], p["kaw"], p["rk"],
      p["qn"], p["kn"], p["ones"], p["tri"], p["chunk_ones"])


def _scan_kernel(scf_ref, scb_ref, vf_ref, vb_ref, gf_ref, gb_ref, s0_ref,
                 yf_ref, yb_ref, sfin_ref, state):
    c = pl.program_id(1)
    C = CHUNK
    pairs = D_R // LANES
    chains = [(g, d, p) for g in range(SCAN_GROUP) for d in range(2) for p in range(pairs)]

    lane = lax.broadcasted_iota(jnp.int32, (1, LANES), 1)
    low = lane < HEAD

    @pl.when(c == 0)
    def _():
        zero = jnp.zeros((HEAD, HEAD), F32)
        for i, (g, d, p) in enumerate(chains):
            top = jnp.concatenate([s0_ref[d, g, 2 * p], zero], axis=1)
            bot = jnp.concatenate([zero, s0_ref[d, g, 2 * p + 1]], axis=1)
            state[i] = jnp.concatenate([top, bot], axis=0)

    W = 2 * C
    t2 = lax.broadcasted_iota(jnp.int32, (C, W), 0)
    s2 = lax.broadcasted_iota(jnp.int32, (C, W), 1) % C
    strict = (s2 < t2, s2 > t2)
    incl = (s2 <= t2, s2 >= t2)
    eye2 = jnp.where(s2 == t2, 1.0, 0.0)
    first_mat = lax.broadcasted_iota(jnp.int32, (1, W), 1) < C
    r2 = lax.broadcasted_iota(jnp.int32, (LANES, LANES), 0)
    c2 = lax.broadcasted_iota(jnp.int32, (LANES, LANES), 1)
    same_head = (r2 // HEAD) == (c2 // HEAD)

    sc_refs = (scf_ref, scb_ref)
    v_refs = (vf_ref, vb_ref)
    g_refs = (gf_ref, gb_ref)
    y_refs = (yf_ref, yb_ref)

    def operand(chain, j):
        g, d, p = chain
        c0 = j * D_R + p * LANES
        return sc_refs[d][0, g, :, c0:c0 + LANES]

    def stack(x):
        first = low if x.shape[1] == LANES else first_mat
        zero = jnp.zeros_like(x)
        return jnp.concatenate([jnp.where(first, x, zero), jnp.where(first, zero, x)], axis=0)

    Qk = [operand(ch, 0) for ch in chains]
    Qr = [operand(ch, 1) for ch in chains]
    v_p = [v_refs[d][g, :, p * LANES:(p + 1) * LANES] for g, d, p in chains]
    v_st = [stack(x) for x in v_p]
    Z0 = [state[i] for i in range(len(chains))]
    Z0b = [z.astype(BF16) for z in Z0]
    m = [_dot_nt(jnp.concatenate([Qk[i], Qr[i]], axis=0),
                 jnp.concatenate([stack(operand(ch, 2)), stack(operand(ch, 3))], axis=0))
         for i, ch in enumerate(chains)]
    A = [jnp.where(strict[d], m[i][0:C, 0:W], 0.0) for i, (g, d, p) in enumerate(chains)]
    Bm = [jnp.where(strict[d], m[i][0:C, W:2 * W], 0.0).astype(BF16)
          for i, (g, d, p) in enumerate(chains)]
    Pb = [jnp.where(incl[d], m[i][C:2 * C, 0:W], 0.0).astype(BF16)
          for i, (g, d, p) in enumerate(chains)]
    Pk = [jnp.where(incl[d], m[i][C:2 * C, W:2 * W], 0.0).astype(BF16)
          for i, (g, d, p) in enumerate(chains)]
    BV = [_dot(Bm[i], v_st[i]) for i in range(len(chains))]
    T = [eye2 - a for a in A]
    Ap = [_dot(a.astype(BF16), stack(a.astype(BF16))).astype(BF16) for a in A]
    n = 2
    while 2 * n < C:
        R = [_dot(jnp.concatenate([Ap[i], T[i].astype(BF16)], axis=0), stack(Ap[i]))
             for i in range(len(chains))]
        Ap = [x[0:C].astype(BF16) for x in R]
        T = [T[i] + R[i][C:2 * C] for i in range(len(chains))]
        n *= 2
    T = [T[i] + _dot(T[i].astype(BF16), stack(Ap[i])) for i in range(len(chains))]
    WU = [_dot(T[i].astype(BF16),
               jnp.concatenate([stack(Qk[i]), stack(BV[i].astype(BF16))], axis=1))
          for i in range(len(chains))]
    Wq = [x[:, 0:LANES].astype(BF16) for x in WU]
    Uv = [x[:, LANES:2 * LANES] for x in WU]
    PW = [_dot(Pb[i], jnp.concatenate([stack(Wq[i]), stack(Uv[i].astype(BF16))], axis=1))
          for i in range(len(chains))]
    PkV = [_dot(Pk[i], v_st[i]) for i in range(len(chains))]
    U = [_dot_nt(Wq[i], Z0b[i]) + Uv[i] for i in range(len(chains))]
    Yq = [(Qr[i].astype(F32) - PW[i][:, 0:LANES]).astype(BF16) for i in range(len(chains))]
    for i, (g, d, p) in enumerate(chains):
        y_refs[d][g, :, p * LANES:(p + 1) * LANES] = (_dot_nt(Yq[i], Z0b[i]) + PkV[i]
                                                      - PW[i][:, LANES:2 * LANES])
    for i, (g, d, p) in enumerate(chains):
        g_tot = g_refs[d][0, g, 0][:, p * LANES:(p + 1) * LANES]
        lhs = jnp.concatenate([v_p[i], (-U[i]).astype(BF16)], axis=0)
        rhs = (jnp.concatenate([operand(chains[i], 3), operand(chains[i], 2)], axis=0).astype(F32)
               * g_tot).astype(BF16)
        upd = _dot_tn(lhs, rhs)
        state[i] = Z0[i] * g_tot + jnp.where(same_head, upd, 0.0)

    @pl.when(c == pl.num_programs(1) - 1)
    def _():
        for i, (g, d, p) in enumerate(chains):
            z = state[i]
            sfin_ref[d, g, 2 * p] = z[0:HEAD, 0:HEAD]
            sfin_ref[d, g, 2 * p + 1] = z[HEAD:2 * HEAD, HEAD:2 * HEAD]


def _rwkv_scan(sc, gt, vb, s0, row0, batch, t_len):
    n = vb.shape[0]
    nch = t_len // CHUNK
    assert row0 % (t_len * SCAN_GROUP) == 0 and batch % SCAN_GROUP == 0
    seq0 = row0 // (t_len * SCAN_GROUP)
    width = SCAN_OPS * D_R
    sc4 = sc.reshape(2, n // t_len, t_len, width)
    gt5 = gt.reshape(2, n // t_len, nch, 1, D_R)
    v3 = vb.reshape(n // t_len, t_len, D_R)
    G = SCAN_GROUP
    fwd = lambda c: c
    bwd = lambda c: nch - 1 - c
    sc_spec = lambda d, at: pl.BlockSpec((1, G, CHUNK, width), lambda s, c: (d, seq0 + s, at(c), 0))
    v_spec = lambda at: pl.BlockSpec((G, CHUNK, D_R), lambda s, c: (seq0 + s, at(c), 0))
    gt_spec = lambda d, at: pl.BlockSpec((1, G, 1, 1, D_R), lambda s, c: (d, seq0 + s, at(c), 0, 0))
    y_spec = lambda at: pl.BlockSpec((G, CHUNK, D_R), lambda s, c: (s, at(c), 0))
    st = pl.BlockSpec((2, G, H_R, HEAD, HEAD), lambda s, c: (0, s, 0, 0, 0))
    return pl.pallas_call(
        _scan_kernel,
        out_shape=[jax.ShapeDtypeStruct((batch, t_len, D_R), F32),
                   jax.ShapeDtypeStruct((batch, t_len, D_R), F32),
                   jax.ShapeDtypeStruct((2, batch, H_R, HEAD, HEAD), F32)],
        grid=(batch // G, nch),
        in_specs=[sc_spec(0, fwd), sc_spec(1, bwd), v_spec(fwd), v_spec(bwd),
                  gt_spec(0, fwd), gt_spec(1, bwd), st],
        out_specs=[y_spec(fwd), y_spec(bwd), st],
        scratch_shapes=[pltpu.VMEM((G * 2 * D_R // LANES, LANES, LANES), F32)],
        compiler_params=_cparams("arbitrary", "arbitrary"),
        name="rwkv_scan",
    )(sc4, sc4, v3, v3, gt5, gt5, s0)


def _attn_kernel(*refs, tq, with_cache):
    if with_cache:
        q_ref, k_ref, v_ref, ck_ref, cv_ref, o_ref = refs
        kx = jnp.concatenate([k_ref[...], ck_ref[0].astype(BF16)], axis=0)
        vx = jnp.concatenate([v_ref[...], cv_ref[0]], axis=0).astype(BF16)
    else:
        q_ref, k_ref, v_ref, o_ref = refs
        kx = k_ref[...]
        vx = v_ref[...].astype(BF16)
    lane = lax.broadcasted_iota(jnp.int32, (1, D_KV), 1)
    low = lane < HEAD
    k_sw = pltpu.roll(kx.astype(F32), HEAD, 1).astype(BF16)
    v_sw = pltpu.roll(vx.astype(F32), HEAD, 1).astype(BF16)
    group = N_HEADS // N_KV
    for p in range(D_A // LANES):
        g = (2 * p) // group
        keep = low if g == 0 else jnp.logical_not(low)
        kd = jnp.where(keep, kx, k_sw)
        vd = jnp.where(keep, vx, v_sw)
        qp = q_ref[:, p * LANES:(p + 1) * LANES]
        zero = jnp.zeros_like(qp)
        qs = jnp.concatenate([jnp.where(low, qp, zero), jnp.where(low, zero, qp)], axis=0)
        s = _dot_nt(qs, kd)
        mx = jnp.max(s, axis=-1, keepdims=True)
        e = jnp.exp(s - mx)
        den = jnp.sum(e, axis=-1, keepdims=True)
        o = _dot(e.astype(BF16), vd) / den
        o_ref[:, p * LANES:(p + 1) * LANES] = jnp.where(low, o[0:tq], o[tq:2 * tq]).astype(BF16)


def _attention(qr, kr, va, row0, batch, t_len, cache_k=None, cache_v=None):
    tq = min(t_len, ATTN_ROWS)
    nq = t_len // tq
    qblk0 = row0 // tq
    sblk0 = row0 // t_len
    with_cache = cache_k is not None
    in_specs = [pl.BlockSpec((tq, D_A), lambda b, i: (qblk0 + b * nq + i, 0)),
                pl.BlockSpec((t_len, D_KV), lambda b, i: (sblk0 + b, 0)),
                pl.BlockSpec((t_len, D_KV), lambda b, i: (sblk0 + b, 0))]
    args = [qr, kr, va]
    if with_cache:
        past = cache_k.shape[1]
        in_specs += [pl.BlockSpec((1, past, D_KV), lambda b, i: (b, 0, 0))] * 2
        args += [cache_k, cache_v]
    return pl.pallas_call(
        functools.partial(_attn_kernel, tq=tq, with_cache=with_cache),
        out_shape=jax.ShapeDtypeStruct((batch * t_len, D_A), BF16),
        grid=(batch, nq),
        in_specs=in_specs,
        out_specs=pl.BlockSpec((tq, D_A), lambda b, i: (b * nq + i, 0)),
        compiler_params=_cparams("arbitrary", "arbitrary"),
        name="attention_cache" if with_cache else "attention_ctx",
    )(*args)


def _post_kernel(yfc_ref, yfl_ref, ybc_ref, ybl_ref, bonus_ref, g_ref, yac_ref, yal_ref, gr_ref, ga_ref,
                 xc_ref, xl_ref,
                 mod_ref, lnw_ref, lnb_ref, wur_ref, wua_ref, wo_ref, n2_ref, wr_ref, br_ref, ones_ref,
                 x1_o, h2_o, idx_o, gate_o, *, ctx_blocks):
    ones = ones_ref[...]
    is_ctx = pl.program_id(0) < ctx_blocks
    y_a = jnp.where(is_ctx, yac_ref[...], yal_ref[...])
    x = jnp.where(is_ctx, xc_ref[...], xl_ref[...])
    y = jnp.where(is_ctx, yfc_ref[...] + ybc_ref[...], yfl_ref[...] + ybl_ref[...])
    mu = _head_sum(y, ones) * (1.0 / HEAD)
    yc = y - mu
    var = _head_sum(yc * yc, ones) * (1.0 / HEAD)
    yn = yc * lax.rsqrt(var + GN_EPS) * lnw_ref[...] + lnb_ref[...] + bonus_ref[...]
    y_r = (yn * g_ref[...]).astype(BF16)
    merged = (_sigmoid(gr_ref[...]) * _dot(y_r, wur_ref[...])
              + _sigmoid(ga_ref[...]) * _dot(y_a, wua_ref[...]))
    mix = _dot(merged.astype(BF16), wo_ref[...])
    mod = mod_ref[0]
    g1 = mod[:, 2 * D_MODEL:3 * D_MODEL]
    sh2 = mod[:, 3 * D_MODEL:4 * D_MODEL]
    sc2 = mod[:, 4 * D_MODEL:5 * D_MODEL]
    x1 = x + g1 * mix
    x1_o[...] = x1
    ms = jnp.mean(x1 * x1, axis=-1, keepdims=True)
    h2 = x1 * lax.rsqrt(ms + NORM_EPS) * n2_ref[...] * (1.0 + sc2) + sh2
    _store_token_tiles(h2_o, h2)
    hh, hl = _split2(h2)
    wh, wl = _split2(wr_ref[...])
    hi_terms = _dot(hh, jnp.concatenate([wh, wl], axis=1))
    logits = hi_terms[:, 0:LANES] + hi_terms[:, LANES:2 * LANES] + _dot(hl, wh) + br_ref[...]
    lane_i = lax.broadcasted_iota(jnp.int32, logits.shape, 1)
    lane = lane_i.astype(F32)
    neg = jnp.float32(-jnp.inf)
    cur = jnp.where(lane_i < N_EXPERTS, logits, neg)
    vals, idxs = [], []
    for _ in range(TOP_K):
        mx = jnp.max(cur, axis=-1, keepdims=True)
        ix = jnp.min(jnp.where(cur == mx, lane, float(LANES)), axis=-1, keepdims=True)
        vals.append(mx)
        idxs.append(ix)
        cur = jnp.where(lane == ix, neg, cur)
    es = [jnp.exp(val - vals[0]) for val in vals]
    den = es[0] + es[1] + es[2] + es[3]
    idx_out = jnp.zeros(logits.shape, jnp.int32)
    gate_out = jnp.zeros(logits.shape, F32)
    for j in range(TOP_K):
        idx_out = jnp.where(lane_i == j, idxs[j].astype(jnp.int32), idx_out)
        gate_out = jnp.where(lane_i == j, es[j] / den, gate_out)
    idx_o[...] = idx_out
    gate_o[...] = gate_out


def _post(y_f, y_b, bonus, g, ya_ctx, ya_lat, gate_r, gate_a, x_ctx, x_lat, mods3, mod_row, p):
    n = x_ctx.shape[0] + x_lat.shape[0]
    tm = TM_POST
    nbc = ya_ctx.shape[0] // tm
    row = lambda i: (i, 0)
    full = lambda *shape: pl.BlockSpec(shape, lambda i: (0,) * len(shape))
    tok = lambda width: pl.BlockSpec((tm, width), row)
    pair = lambda width: _token_specs(nbc, width, tm)
    return pl.pallas_call(
        functools.partial(_post_kernel, ctx_blocks=nbc),
        out_shape=[jax.ShapeDtypeStruct((n, D_MODEL), F32),
                   jax.ShapeDtypeStruct((n * TILE_ROWS, LANES), F32),
                   jax.ShapeDtypeStruct((n, LANES), jnp.int32),
                   jax.ShapeDtypeStruct((n, LANES), F32)],
        grid=(n // tm,),
        in_specs=pair(D_R) + pair(D_R) + [tok(D_R), tok(D_R)]
                 + pair(D_A) + [tok(D_MODEL), tok(D_MODEL)] + pair(D_MODEL) + [
                  pl.BlockSpec((1, 1, 6 * D_MODEL), lambda i: (mod_row(i), 0, 0)),
                  full(1, D_R), full(1, D_R), full(D_R, D_MODEL), full(D_A, D_MODEL),
                  full(D_MODEL, D_MODEL), full(1, D_MODEL), full(D_MODEL, LANES), full(1, LANES),
                  full(D_R, D_R)],
        out_specs=[tok(D_MODEL), pl.BlockSpec((tm * TILE_ROWS, LANES), row), tok(LANES), tok(LANES)],
        compiler_params=_cparams("arbitrary"),
        name="mixer_post",
    )(*y_f, *y_b, bonus, g, ya_ctx, ya_lat, gate_r, gate_a, x_ctx, x_lat, mods3,
      p["lnw"], p["lnb"], p["wur"], p["wua"], p["wo"], p["n2"], p["wr"], p["br"], p["ones"])


def _moe_kernel(blk_e_ref, blk_first_ref, next_e_ref, used_ref,
                src0_ref, src_next_ref, dst_prev_ref, h_hbm, w1_hbm, b1_ref, w2_hbm, b2_ref,
                out_hbm, w1f, w2f, w1b, w2b, xs, ys, sem_g, sem_s, sem_t, sem_w, *, n_rows):
    b = pl.program_id(0)
    used = used_ref[0]
    R = MOE_ROWS
    T = TILE_ROWS

    def tile(ref, i):
        return ref.at[pl.ds(pl.multiple_of(i * T, T), T)]

    def gather(table_ref, buf):
        for i in range(R):
            pltpu.make_async_copy(tile(h_hbm, table_ref[0, 0, i]), tile(xs.at[buf], i),
                                  sem_g.at[buf]).start(priority=i % 2)

    def scatter_prev(buf):
        for i in range(R):
            pltpu.make_async_copy(tile(ys.at[buf], i), tile(out_hbm, dst_prev_ref[0, 0, i]),
                                  sem_s.at[buf]).start(priority=i % 2)

    def fetch_weights(e):
        pltpu.make_async_copy(w1_hbm.at[e], w1f, sem_w.at[0]).start()
        pltpu.make_async_copy(w2_hbm.at[e], w2f, sem_w.at[1]).start()

    @pl.when(b == 0)
    def _():
        ys[...] = jnp.zeros_like(ys)
        for s in range(2):
            band = pltpu.make_async_copy(ys.at[s], out_hbm.at[pl.ds((n_rows + s * R) * T, R * T)], sem_t)
            band.start()
            band.wait()
        gather(src0_ref, 0)
        fetch_weights(blk_e_ref[0])

    @pl.when(jnp.logical_and(blk_first_ref[b] == 1, b < used))
    def _():
        pltpu.make_async_copy(w1_hbm.at[0], w1f, sem_w.at[0]).wait()
        pltpu.make_async_copy(w2_hbm.at[0], w2f, sem_w.at[1]).wait()
        w1b[...] = w1f[...].astype(BF16)
        w2b[...] = w2f[...].astype(BF16)

        @pl.when(next_e_ref[b] >= 0)
        def _():
            fetch_weights(next_e_ref[b])

    def step(p):
        @pl.when(b <= used)
        def _():
            pltpu.make_async_copy(h_hbm.at[pl.ds(0, R * T)], xs.at[p], sem_g.at[p]).wait()

        @pl.when(jnp.logical_and(b >= 1, b <= used + 1))
        def _():
            pltpu.make_async_copy(ys.at[p], out_hbm.at[pl.ds(0, R * T)], sem_s.at[p]).wait()

        @pl.when(b < used)
        def _():
            gather(src_next_ref, 1 - p)
            scatter_prev(1 - p)
            hb = _dot(_load_token_tiles(xs.at[p]).astype(BF16), w1b[...]) + b1_ref[0]
            glu = jnp.minimum(hb[:, 0:D_FF], SWIGLU_LIMIT)
            lin = jnp.clip(hb[:, D_FF:2 * D_FF], -SWIGLU_LIMIT, SWIGLU_LIMIT)
            act = glu * _sigmoid(SWIGLU_ALPHA * glu) * (lin + 1.0)
            _store_token_tiles(ys.at[p], _dot(act.astype(BF16), w2b[...]) + b2_ref[0])

        @pl.when(b == used)
        def _():
            scatter_prev(1 - p)

    for p in range(2):
        pl.when(b % 2 == p)(functools.partial(step, p))


def _moe(h2, src, dst_prev, blk_e, blk_first, next_e, used, w1, b1, w2, b2):
    d = D_MODEL
    n = h2.shape[0] // TILE_ROWS
    nb = src.shape[0]
    steps = nb + 2
    cur = lambda b, *_: (jnp.minimum(b, nb), 0, 0)
    nxt = lambda b, *_: (jnp.minimum(b + 1, nb - 1), 0, 0)
    table = lambda index_map: pl.BlockSpec((1, 1, MOE_ROWS), index_map, memory_space=pltpu.SMEM)
    expert = lambda b, blk_e, *_: (blk_e[jnp.minimum(b, nb - 1)], 0, 0)
    grid_spec = pltpu.PrefetchScalarGridSpec(
        num_scalar_prefetch=4,
        grid=(steps,),
        in_specs=[table(lambda b, *_: (0, 0, 0)), table(nxt), table(cur),
                  pl.BlockSpec(memory_space=pl.ANY),
                  pl.BlockSpec(memory_space=pl.ANY),
                  pl.BlockSpec((1, 1, 2 * D_FF), expert),
                  pl.BlockSpec(memory_space=pl.ANY),
                  pl.BlockSpec((1, 1, d), expert)],
        out_specs=pl.BlockSpec(memory_space=pl.ANY),
        scratch_shapes=[pltpu.VMEM((d, 2 * D_FF), F32),
                        pltpu.VMEM((D_FF, d), F32),
                        pltpu.VMEM((d, 2 * D_FF), BF16),
                        pltpu.VMEM((D_FF, d), BF16),
                        pltpu.VMEM((2, MOE_ROWS * TILE_ROWS, LANES), F32),
                        pltpu.VMEM((2, MOE_ROWS * TILE_ROWS, LANES), F32),
                        pltpu.SemaphoreType.DMA((2,)),
                        pltpu.SemaphoreType.DMA((2,)),
                        pltpu.SemaphoreType.DMA(()),
                        pltpu.SemaphoreType.DMA((2,))])
    tab3 = lambda t: t.reshape(-1, 1, MOE_ROWS)
    pad = lambda t: jnp.concatenate([t, jnp.zeros((steps - nb,), jnp.int32)])
    return pl.pallas_call(
        functools.partial(_moe_kernel, n_rows=n * TOP_K),
        out_shape=jax.ShapeDtypeStruct(((n * TOP_K + 2 * MOE_ROWS) * TILE_ROWS, LANES), F32),
        grid_spec=grid_spec,
        compiler_params=pltpu.CompilerParams(dimension_semantics=("arbitrary",),
                                             vmem_limit_bytes=VMEM_LIMIT, has_side_effects=True),
        name="moe_experts",
    )(blk_e, pad(blk_first), next_e, used, tab3(src), tab3(src), tab3(dst_prev), h2, w1,
      b1.reshape(N_EXPERTS, 1, 2 * D_FF), w2, b2.reshape(N_EXPERTS, 1, d))


def _route(top_idx, n):
    nk = n * TOP_K
    cap = nk + N_EXPERTS * MOE_ROWS
    nb = cap // MOE_ROWS
    flat_e = top_idx.reshape(nk)
    order = jnp.argsort(flat_e, stable=True).astype(jnp.int32)
    counts = jnp.sum((flat_e[:, None] == jnp.arange(N_EXPERTS, dtype=jnp.int32)[None, :])
                     .astype(jnp.int32), axis=0)
    start = jnp.cumsum(counts) - counts
    n_blk = (counts + MOE_ROWS - 1) // MOE_ROWS
    blk_end = jnp.cumsum(n_blk)
    first_blk = blk_end - n_blk
    experts = jnp.arange(N_EXPERTS, dtype=jnp.int32)
    blk = jnp.arange(nb, dtype=jnp.int32)
    blk_e = jnp.minimum(jnp.sum((blk[:, None] >= blk_end[None, :]).astype(jnp.int32), axis=1),
                        N_EXPERTS - 1)
    onehot = (blk_e[:, None] == experts[None, :]).astype(jnp.int32)
    pick = lambda t: jnp.sum(onehot * t[None, :], axis=1)
    blk_first_of_e, blk_count, blk_start = pick(first_blk), pick(counts), pick(start)
    blk_first = (blk == blk_first_of_e).astype(jnp.int32)
    blk_after = pick(blk_end)
    next_e = jnp.where(blk_after < blk_end[-1], blk_e[jnp.minimum(blk_after, nb - 1)], -1).astype(jnp.int32)
    row = jnp.arange(MOE_ROWS, dtype=jnp.int32)[None, :]
    within = (blk - blk_first_of_e)[:, None] * MOE_ROWS + row
    valid = jnp.logical_and(within < blk_count[:, None], (blk < blk_end[-1])[:, None])
    j = order[jnp.clip(blk_start[:, None] + within, 0, nk - 1)]
    src = jnp.where(valid, j // TOP_K, 0).astype(jnp.int32)
    scratch_row = nk + (blk % 2)[:, None] * MOE_ROWS + row
    dst = jnp.where(valid, (j % TOP_K) * n + j // TOP_K, scratch_row).astype(jnp.int32)
    dst_prev = jnp.concatenate([nk + MOE_ROWS + row, dst], axis=0)
    return (src, dst_prev, blk_e.astype(jnp.int32), blk_first, next_e,
            blk_end[-1:].astype(jnp.int32))


def _final_kernel(x1_ref, y0_ref, y1_ref, y2_ref, y3_ref, gate_ref, mod_ref, nf_ref, oc_ref, ol_ref,
                  *, ctx_blocks):
    gates = gate_ref[...]
    ffn = gates[:, 0:1] * _load_token_tiles(y0_ref)
    for j, ref in enumerate((y1_ref, y2_ref, y3_ref), start=1):
        ffn = ffn + gates[:, j:j + 1] * _load_token_tiles(ref)
    g2 = mod_ref[0][:, 5 * D_MODEL:6 * D_MODEL]
    x2 = x1_ref[...] + g2 * ffn
    ms = jnp.mean(x2 * x2, axis=-1, keepdims=True)
    y = x2 * lax.rsqrt(ms + NORM_EPS) * nf_ref[...]
    i = pl.program_id(0)

    @pl.when(i < ctx_blocks)
    def _():
        oc_ref[...] = y

    @pl.when(i >= ctx_blocks)
    def _():
        ol_ref[...] = y


def _final(x1, ys, gates, mods3, mod_row, norm_f, nc):
    n, d = x1.shape
    tm = TM_POST
    nb = n // tm
    nbc = nc // tm
    row = lambda i: (i, 0)
    return pl.pallas_call(
        functools.partial(_final_kernel, ctx_blocks=nbc),
        out_shape=[jax.ShapeDtypeStruct((nc, d), F32), jax.ShapeDtypeStruct((n - nc, d), F32)],
        grid=(nb,),
        in_specs=[pl.BlockSpec((tm, d), row)]
                 + [pl.BlockSpec((tm * TILE_ROWS, LANES), functools.partial(lambda j, i: (j * nb + i, 0), j))
                    for j in range(TOP_K)]
                 + [pl.BlockSpec((tm, LANES), row),
                    pl.BlockSpec((1, 1, 6 * d), lambda i: (mod_row(i), 0, 0)),
                    pl.BlockSpec((1, d), lambda i: (0, 0))],
        out_specs=_token_specs(nbc, d, tm),
        compiler_params=_cparams("arbitrary"),
        name="combine_final",
    )(x1, ys, ys, ys, ys, gates, mods3, norm_f.reshape(1, d))


def _rope_tables(t_ctx, t_lat):
    rows = np.arange(t_lat) // GRID_W
    cols = np.arange(t_lat) % GRID_W
    inv = ROPE_THETA ** (-np.arange(0, AXIS_DIM, 2, dtype=np.float32) / AXIS_DIM)
    inv = jnp.asarray(inv, F32)
    ang_r = jnp.asarray(rows, F32)[:, None] * inv[None, :]
    ang_c = jnp.asarray(cols, F32)[:, None] * inv[None, :]
    cos = jnp.concatenate([jnp.cos(ang_r)] * 2 + [jnp.cos(ang_c)] * 2, axis=1)
    sin = jnp.concatenate([-jnp.sin(ang_r), jnp.sin(ang_r), -jnp.sin(ang_c), jnp.sin(ang_c)], axis=1)
    cos = jnp.concatenate([jnp.ones((t_ctx, HEAD), F32), cos], axis=0)
    sin = jnp.concatenate([jnp.zeros((t_ctx, HEAD), F32), sin], axis=0)
    return jnp.concatenate([cos, cos], axis=1), jnp.concatenate([sin, sin], axis=1)


def kernel(x_prompt, x_sample, cache_k, cache_v, state_rwkv_fwd, state_rwkv_bwd, c, c_ctx, w_ada, b_ada, norm1, norm2, w_in, rw_w0, rw_w2, rw_a0, rw_a2, rw_g2, rw_kk, rw_ka, rw_rk, rw_ln_w, rw_ln_b, q_norm, k_norm, w_up_r, w_up_a, w_out, w_router, b_router, w_moe_in, b_moe_in, w_moe_out, b_moe_out, norm_f):
    depth = w_in.shape[0]
    assert depth == 1, "single trunk layer"
    bc, tc, d = x_prompt.shape
    bl, tl, _ = x_sample.shape
    nc, nl = bc * tc, bl * tl
    n = nc + nl
    assert d == D_MODEL and TM % tc == 0 and tl % TM == 0 and nc % tl == 0
    assert all(nc % tm == 0 and tl % tm == 0 for tm in (TM_PROJ, TM_POST))
    l = 0

    x_ctx, x_lat = x_prompt.reshape(nc, d), x_sample.reshape(nl, d)
    cond = jnp.concatenate([c_ctx[None, :], c, jnp.zeros((16 - 1 - bl, d), F32)], axis=0)
    mods3 = _adaln(cond, w_ada[l], b_ada[l]).reshape(16, 1, 6 * d)
    nbc = nc // TM
    per_seq = tl // TM
    tab_row = lambda i: jnp.where(i < nbc, 0, 1 + (i - nbc) % per_seq)
    mod_row = lambda i: jnp.where(i < nc // TM_POST, 0, 1 + (i - nc // TM_POST) // (tl // TM_POST))

    z = dict(zip([name for name, _ in IN_SPLITS],
                 _proj_in(x_ctx, x_lat, mods3, norm1[l], w_in[l].astype(BF16), tl)))

    ones_bd = jnp.asarray(np.kron(np.eye(H_R), np.ones((HEAD, HEAD))), BF16)
    pad_lo = lambda w: jnp.concatenate([w, jnp.zeros_like(w)], axis=0)
    pad_hi = lambda w: jnp.concatenate([jnp.zeros_like(w), w], axis=0)
    prep_p = dict(
        w0=rw_w0[l], a0=rw_a0[l],
        w2=jnp.stack([pad_lo(rw_w2[l, 0]), pad_hi(rw_w2[l, 1])]).astype(BF16),
        a2=jnp.stack([pad_lo(rw_a2[l, 0]), pad_hi(rw_a2[l, 1])]).astype(BF16),
        g2=rw_g2[l].astype(BF16),
        kkw=rw_kk[l].reshape(1, D_R), kaw=rw_ka[l].reshape(1, D_R), rk=rw_rk[l].reshape(1, D_R),
        qn=jnp.tile(q_norm[l], N_HEADS).reshape(1, D_A), kn=jnp.tile(k_norm[l], N_KV).reshape(1, D_KV),
        ones=ones_bd,
        tri=jnp.asarray(np.stack([np.kron(np.eye(TM // CHUNK), np.tril(np.ones((CHUNK, CHUNK)))),
                                  np.kron(np.eye(TM // CHUNK), np.triu(np.ones((CHUNK, CHUNK))))]), BF16),
        chunk_ones=jnp.asarray(np.kron(np.eye(TM // CHUNK), np.ones((CHUNK, CHUNK))), BF16))
    cos_tab, sin_tab = _rope_tables(TM, tl)
    sc, gt, vb, bonus, g, qr, kn, kr = _prep(z, cos_tab, sin_tab, tab_row, prep_p)

    yf_c, yb_c, s_fin = _rwkv_scan(sc, gt, vb, jnp.zeros((2, bc, H_R, HEAD, HEAD), F32), 0, bc, tc)
    yf_l, yb_l, _ = _rwkv_scan(sc, gt, vb, jnp.stack([state_rwkv_fwd[:, l], state_rwkv_bwd[:, l]]),
                               nc, bl, tl)
    y_f = (yf_c.reshape(nc, D_R), yf_l.reshape(nl, D_R))
    y_b = (yb_c.reshape(nc, D_R), yb_l.reshape(nl, D_R))

    ya_ctx = _attention(qr, kr, z["va"], 0, bc, tc)
    ya_lat = _attention(qr, kr, z["va"], nc, bl, tl,
                        cache_k[:, l].reshape(bl, -1, D_KV), cache_v[:, l].reshape(bl, -1, D_KV))

    post_p = dict(
        lnw=rw_ln_w[l].reshape(1, D_R), lnb=rw_ln_b[l].reshape(1, D_R),
        wur=w_up_r[l].astype(BF16), wua=w_up_a[l].astype(BF16), wo=w_out[l].astype(BF16),
        n2=norm2[l].reshape(1, d),
        wr=jnp.pad(w_router[l], ((0, 0), (0, LANES - N_EXPERTS))),
        br=jnp.pad(b_router[l], (0, LANES - N_EXPERTS)).reshape(1, LANES),
        ones=ones_bd)
    x1, h2, top_idx, gates = _post(y_f, y_b, bonus, g, ya_ctx, ya_lat, z["gate_r"], z["gate_a"],
                                   x_ctx, x_lat, mods3, mod_row, post_p)

    src, dst, blk_e, blk_first, next_e, used = _route(top_idx[:, :TOP_K], n)
    ys = _moe(h2, src, dst, blk_e, blk_first, next_e, used,
              w_moe_in[l], b_moe_in[l], w_moe_out[l], b_moe_out[l])
    y_ctx, y_lat = _final(x1, ys, gates, mods3, mod_row, norm_f, nc)

    y_prompt = y_ctx.reshape(bc, tc, d)
    y_sample = y_lat.reshape(bl, tl, d)
    new_cache_k = kn[:nc].reshape(bc, 1, tc, N_KV, HEAD)
    new_cache_v = z["va"][:nc].reshape(bc, 1, tc, N_KV, HEAD)
    new_state_fwd = s_fin[0][:, None]
    new_state_bwd = s_fin[1][:, None]
    return (y_prompt, y_sample, new_cache_k, new_cache_v, new_state_fwd, new_state_bwd)
```

```python
import functools

import numpy as np
import jax
import jax.numpy as jnp
from jax import lax
from jax.experimental import pallas as pl
from jax.experimental.pallas import tpu as pltpu

F32 = jnp.float32
BF16 = jnp.bfloat16

D_MODEL = 1024
GRID_W = 64
NORM_EPS = 1e-6
HEAD = 64
H_R = 8
D_R = H_R * HEAD
DECAY_RANK = 64
AAA_RANK = 64
GATE_RANK = 128
GN_EPS = 64e-5
N_HEADS = 8
N_KV = 2
D_A = N_HEADS * HEAD
D_KV = N_KV * HEAD
AXIS_DIM = HEAD // 2
ROPE_THETA = 10000.0
N_EXPERTS = 32
TOP_K = 4
D_FF = D_MODEL
SWIGLU_ALPHA = 1.702
SWIGLU_LIMIT = 7.0
LORA_COLS = 2 * DECAY_RANK + 2 * AAA_RANK + GATE_RANK
IN_SPLITS = (("r", D_R), ("k", D_R), ("v", D_R), ("lora", LORA_COLS), ("q", D_A),
             ("ka", D_KV), ("va", D_KV), ("gate_r", D_MODEL), ("gate_a", D_MODEL))

LANES = 128
TM = 256
TM_POST = 512
TM_PROJ = 512
ATTN_ROWS = 512
CHUNK = 64
SCAN_OPS = 6
SCAN_GROUP = 4
MOE_ROWS = 256
VMEM_LIMIT = 56 * 1024 * 1024


def _cparams(*sem):
    return pltpu.CompilerParams(dimension_semantics=sem, vmem_limit_bytes=VMEM_LIMIT)


def _dot(a, b):
    return jnp.dot(a, b, preferred_element_type=F32)


def _dot_nt(a, b):
    return lax.dot_general(a, b, (((1,), (1,)), ((), ())), preferred_element_type=F32)


def _dot_tn(a, b):
    return lax.dot_general(a, b, (((0,), (0,)), ((), ())), preferred_element_type=F32)


def _split2(x):
    hi = x.astype(BF16)
    lo = (x - hi.astype(F32)).astype(BF16)
    return hi, lo


def _head_sum(x, ones_bd):
    return _dot(x.astype(BF16), ones_bd)


def _sigmoid(x):
    return 0.5 * jnp.tanh(0.5 * x) + 0.5


TILE_ROWS = D_MODEL // LANES


def _store_token_tiles(ref, x):
    rows = x.shape[0]
    for j in range(TILE_ROWS):
        ref[pl.ds(j, rows, stride=TILE_ROWS), :] = x[:, j * LANES:(j + 1) * LANES]


def _load_token_tiles(ref):
    rows = ref.shape[0] // TILE_ROWS
    return jnp.concatenate([ref[pl.ds(j, rows, stride=TILE_ROWS), :] for j in range(TILE_ROWS)], axis=1)


def _adaln_kernel(c_ref, w_ref, b_ref, o_ref):
    c = c_ref[...]
    s = (c * _sigmoid(c)).astype(BF16)
    o_ref[...] = _dot(s, w_ref[...].astype(BF16)) + b_ref[...]


def _adaln(cond, w_ada, b_ada):
    rows, d = cond.shape
    cols = w_ada.shape[1]
    tn = 512
    return pl.pallas_call(
        _adaln_kernel,
        out_shape=jax.ShapeDtypeStruct((rows, cols), F32),
        grid=(cols // tn,),
        in_specs=[pl.BlockSpec((rows, d), lambda j: (0, 0)),
                  pl.BlockSpec((d, tn), lambda j: (0, j)),
                  pl.BlockSpec((1, tn), lambda j: (0, j))],
        out_specs=pl.BlockSpec((rows, tn), lambda j: (0, j)),
        compiler_params=_cparams("arbitrary"),
        name="adaln",
    )(cond, w_ada, b_ada.reshape(1, cols))


def _token_specs(nbc, width, tm=TM):
    return [pl.BlockSpec((tm, width), lambda i: (jnp.minimum(i, nbc - 1), 0)),
            pl.BlockSpec((tm, width), lambda i: (jnp.maximum(i - nbc, 0), 0))]


def _proj_in_kernel(xc_ref, xl_ref, mod_ref, n1_ref, w_ref, *out_refs, ctx_blocks):
    x = jnp.where(pl.program_id(0) < ctx_blocks, xc_ref[...], xl_ref[...])
    ms = jnp.mean(x * x, axis=-1, keepdims=True)
    hn = x * lax.rsqrt(ms + NORM_EPS) * n1_ref[...]
    mod = mod_ref[0]
    sh1 = mod[:, 0:D_MODEL]
    sc1 = mod[:, D_MODEL:2 * D_MODEL]
    h = (hn * (1.0 + sc1) + sh1).astype(BF16)
    c0 = 0
    for ref, (_, width) in zip(out_refs, IN_SPLITS):
        ref[...] = _dot(h, w_ref[:, c0:c0 + width])
        c0 += width


def _proj_in(x_ctx, x_lat, mods3, norm1, w_in_bf, t_lat):
    d = x_ctx.shape[1]
    n = x_ctx.shape[0] + x_lat.shape[0]
    tm = TM_PROJ
    nbc = x_ctx.shape[0] // tm
    per_seq = t_lat // tm
    cols = w_in_bf.shape[1]
    row = lambda i: (i, 0)
    mod_row = lambda i: jnp.where(i < nbc, 0, 1 + (i - nbc) // per_seq)
    return pl.pallas_call(
        functools.partial(_proj_in_kernel, ctx_blocks=nbc),
        out_shape=[jax.ShapeDtypeStruct((n, width), F32) for _, width in IN_SPLITS],
        grid=(n // tm,),
        in_specs=_token_specs(nbc, d, tm) + [
                  pl.BlockSpec((1, 1, 6 * d), lambda i: (mod_row(i), 0, 0)),
                  pl.BlockSpec((1, d), lambda i: (0, 0)),
                  pl.BlockSpec((d, cols), lambda i: (0, 0))],
        out_specs=[pl.BlockSpec((tm, width), row) for _, width in IN_SPLITS],
        compiler_params=_cparams("arbitrary"),
        name="proj_in",
    )(x_ctx, x_lat, mods3, norm1.reshape(1, d), w_in_bf)


def _prep_kernel(r_ref, k_ref, v_ref, lora_ref, q_ref, ka_ref, cos_ref, sin_ref,
                 w0_ref, w2_ref, a0_ref, a2_ref, g2_ref, kkw_ref, kaw_ref, rk_ref,
                 qn_ref, kn_ref, ones_ref, tri_ref, chunk_ones_ref,
                 sc_o, gt_o, vb_o, bonus_o, g_o, qr_o, kn_o, kr_o):
    ones = ones_ref[...]
    chunk_ones = chunk_ones_ref[...]
    r = r_ref[...]
    k = k_ref[...]
    v = v_ref[...]
    vb_o[...] = v.astype(BF16)
    lora = lora_ref[...]
    kk = k * kkw_ref[...]
    kk = kk * lax.rsqrt(_head_sum(kk * kk, ones) + 1e-12)
    th = jnp.tanh(lora[:, 0:LANES]).astype(BF16)
    al = lora[:, LANES:2 * LANES].astype(BF16)
    kd_sum = jnp.zeros_like(k)
    for d in range(2):
        u = w0_ref[d:d + 1, :] + _dot(th, w2_ref[d])
        lw = -float(np.exp(-0.5)) * _sigmoid(u)
        a = _sigmoid(a0_ref[d:d + 1, :] + _dot(al, a2_ref[d]))
        kd = k * (1.0 + (a - 1.0) * kaw_ref[...])
        kd_sum = kd_sum + kd
        b = kk * a
        parts = _split2(lw)
        tri = tri_ref[d]
        L = _dot(tri, parts[0]) + _dot(tri, parts[1])
        Ltot = _dot(chunk_ones, parts[0]) + _dot(chunk_ones, parts[1])
        e_inv = jnp.exp(-L)
        e_rest = jnp.exp(Ltot - L)
        scan_ops = (kk * jnp.exp(L - lw), r * jnp.exp(L), b * e_inv, kd * e_inv, b * e_rest, kd * e_rest)
        for j, op in enumerate(scan_ops):
            sc_o[d, :, j * D_R:(j + 1) * D_R] = op.astype(BF16)
        g_tot = jnp.exp(Ltot)
        for j in range(TM // CHUNK):
            gt_o[d, j] = g_tot[j * CHUNK:j * CHUNK + 1, :]
    bonus_o[...] = _head_sum(r * kd_sum * rk_ref[...], ones) * v
    g_o[...] = _dot(_sigmoid(lora[:, 2 * LANES:3 * LANES]).astype(BF16), g2_ref[...])
    q = q_ref[...]
    qn = q * lax.rsqrt(_head_sum(q * q, ones) * (1.0 / HEAD) + NORM_EPS) * qn_ref[...]
    ka = ka_ref[...]
    ones_kv = ones[0:D_KV, 0:D_KV]
    kn = ka * lax.rsqrt(_head_sum(ka * ka, ones_kv) * (1.0 / HEAD) + NORM_EPS) * kn_ref[...]
    kn_o[...] = kn
    cos = cos_ref[...]
    sin = sin_ref[...]
    half = AXIS_DIM // 2

    def rope(t, cos_t, sin_t):
        width = t.shape[1]
        lane = lax.broadcasted_iota(jnp.int32, t.shape, 1)
        first = (lane % AXIS_DIM) < half
        swapped = jnp.where(first, pltpu.roll(t, width - half, 1), pltpu.roll(t, half, 1))
        return t * cos_t + swapped * sin_t

    cos4 = jnp.concatenate([cos] * (D_A // D_KV), axis=1)
    sin4 = jnp.concatenate([sin] * (D_A // D_KV), axis=1)
    qr_o[...] = (rope(qn, cos4, sin4) * (HEAD ** -0.5)).astype(BF16)
    kr_o[...] = rope(kn, cos, sin).astype(BF16)


def _prep(z, cos_tab, sin_tab, tab_row, p):
    n = z["r"].shape[0]
    row = lambda i: (i, 0)
    full = lambda *shape: pl.BlockSpec(shape, lambda i: (0,) * len(shape))
    tok = lambda width: pl.BlockSpec((TM, width), row)
    tab = pl.BlockSpec((TM, D_KV), lambda i: (tab_row(i), 0))
    cpb = TM // CHUNK
    return pl.pallas_call(
        _prep_kernel,
        out_shape=[jax.ShapeDtypeStruct((2, n, SCAN_OPS * D_R), BF16),
                   jax.ShapeDtypeStruct((2, n // CHUNK, 1, D_R), F32),
                   jax.ShapeDtypeStruct((n, D_R), BF16),
                   jax.ShapeDtypeStruct((n, D_R), F32),
                   jax.ShapeDtypeStruct((n, D_R), F32),
                   jax.ShapeDtypeStruct((n, D_A), BF16),
                   jax.ShapeDtypeStruct((n, D_KV), F32),
                   jax.ShapeDtypeStruct((n, D_KV), BF16)],
        grid=(n // TM,),
        in_specs=[tok(D_R), tok(D_R), tok(D_R), tok(LORA_COLS), tok(D_A), tok(D_KV), tab, tab,
                  full(2, D_R), full(2, LANES, D_R), full(2, D_R), full(2, LANES, D_R),
                  full(GATE_RANK, D_R), full(1, D_R), full(1, D_R), full(1, D_R),
                  full(1, D_A), full(1, D_KV), full(D_R, D_R), full(2, TM, TM), full(TM, TM)],
        out_specs=[pl.BlockSpec((2, TM, SCAN_OPS * D_R), lambda i: (0, i, 0)),
                   pl.BlockSpec((2, cpb, 1, D_R), lambda i: (0, i, 0, 0)),
                   tok(D_R), tok(D_R), tok(D_R), tok(D_A), tok(D_KV), tok(D_KV)],
        compiler_params=_cparams("arbitrary"),
        name="mixer_prep",
    )(z["r"], z["k"], z["v"], z["lora"], z["q"], z["ka"], cos_tab, sin_tab,
      p["w0"], p["w2"], p["a0"], p["a2"], p["g2"], p["kkw"], p["kaw"], p["rk"],
      p["qn"], p["kn"], p["ones"], p["tri"], p["chunk_ones"])


def _scan_kernel(scf_ref, scb_ref, vf_ref, vb_ref, gf_ref, gb_ref, s0_ref,
                 yf_ref, yb_ref, sfin_ref, state):
    c = pl.program_id(1)
    C = CHUNK
    pairs = D_R // LANES
    chains = [(g, d, p) for g in range(SCAN_GROUP) for d in range(2) for p in range(pairs)]

    lane = lax.broadcasted_iota(jnp.int32, (1, LANES), 1)
    low = lane < HEAD

    @pl.when(c == 0)
    def _():
        zero = jnp.zeros((HEAD, HEAD), F32)
        for i, (g, d, p) in enumerate(chains):
            top = jnp.concatenate([s0_ref[d, g, 2 * p], zero], axis=1)
            bot = jnp.concatenate([zero, s0_ref[d, g, 2 * p + 1]], axis=1)
            state[i] = jnp.concatenate([top, bot], axis=0)

    W = 2 * C
    t2 = lax.broadcasted_iota(jnp.int32, (C, W), 0)
    s2 = lax.broadcasted_iota(jnp.int32, (C, W), 1) % C
    strict = (s2 < t2, s2 > t2)
    incl = (s2 <= t2, s2 >= t2)
    eye2 = jnp.where(s2 == t2, 1.0, 0.0)
    first_mat = lax.broadcasted_iota(jnp.int32, (1, W), 1) < C
    r2 = lax.broadcasted_iota(jnp.int32, (LANES, LANES), 0)
    c2 = lax.broadcasted_iota(jnp.int32, (LANES, LANES), 1)
    same_head = (r2 // HEAD) == (c2 // HEAD)

    sc_refs = (scf_ref, scb_ref)
    v_refs = (vf_ref, vb_ref)
    g_refs = (gf_ref, gb_ref)
    y_refs = (yf_ref, yb_ref)

    def operand(chain, j):
        g, d, p = chain
        c0 = j * D_R + p * LANES
        return sc_refs[d][0, g, :, c0:c0 + LANES]

    def stack(x):
        first = low if x.shape[1] == LANES else first_mat
        zero = jnp.zeros_like(x)
        return jnp.concatenate([jnp.where(first, x, zero), jnp.where(first, zero, x)], axis=0)

    Qk = [operand(ch, 0) for ch in chains]
    Qr = [operand(ch, 1) for ch in chains]
    v_p = [v_refs[d][g, :, p * LANES:(p + 1) * LANES] for g, d, p in chains]
    v_st = [stack(x) for x in v_p]
    Z0 = [state[i] for i in range(len(chains))]
    Z0b = [z.astype(BF16) for z in Z0]
    m = [_dot_nt(jnp.concatenate([Qk[i], Qr[i]], axis=0),
                 jnp.concatenate([stack(operand(ch, 2)), stack(operand(ch, 3))], axis=0))
         for i, ch in enumerate(chains)]
    A = [jnp.where(strict[d], m[i][0:C, 0:W], 0.0) for i, (g, d, p) in enumerate(chains)]
    Bm = [jnp.where(strict[d], m[i][0:C, W:2 * W], 0.0).astype(BF16)
          for i, (g, d, p) in enumerate(chains)]
    Pb = [jnp.where(incl[d], m[i][C:2 * C, 0:W], 0.0).astype(BF16)
          for i, (g, d, p) in enumerate(chains)]
    Pk = [jnp.where(incl[d], m[i][C:2 * C, W:2 * W], 0.0).astype(BF16)
          for i, (g, d, p) in enumerate(chains)]
    BV = [_dot(Bm[i], v_st[i]) for i in range(len(chains))]
    T = [eye2 - a for a in A]
    Ap = [_dot(a.astype(BF16), stack(a.astype(BF16))).astype(BF16) for a in A]
    n = 2
    while 2 * n < C:
        R = [_dot(jnp.concatenate([Ap[i], T[i].astype(BF16)], axis=0), stack(Ap[i]))
             for i in range(len(chains))]
        Ap = [x[0:C].astype(BF16) for x in R]
        T = [T[i] + R[i][C:2 * C] for i in range(len(chains))]
        n *= 2
    T = [T[i] + _dot(T[i].astype(BF16), stack(Ap[i])) for i in range(len(chains))]
    WU = [_dot(T[i].astype(BF16),
               jnp.concatenate([stack(Qk[i]), stack(BV[i].astype(BF16))], axis=1))
          for i in range(len(chains))]
    Wq = [x[:, 0:LANES].astype(BF16) for x in WU]
    Uv = [x[:, LANES:2 * LANES] for x in WU]
    PW = [_dot(Pb[i], jnp.concatenate([stack(Wq[i]), stack(Uv[i].astype(BF16))], axis=1))
          for i in range(len(chains))]
    PkV = [_dot(Pk[i], v_st[i]) for i in range(len(chains))]
    U = [_dot_nt(Wq[i], Z0b[i]) + Uv[i] for i in range(len(chains))]
    Yq = [(Qr[i].astype(F32) - PW[i][:, 0:LANES]).astype(BF16) for i in range(len(chains))]
    for i, (g, d, p) in enumerate(chains):
        y_refs[d][g, :, p * LANES:(p + 1) * LANES] = (_dot_nt(Yq[i], Z0b[i]) + PkV[i]
                                                      - PW[i][:, LANES:2 * LANES])
    for i, (g, d, p) in enumerate(chains):
        lhs = jnp.concatenate([v_p[i], (-U[i]).astype(BF16)], axis=0)
        rhs = jnp.concatenate([operand(chains[i], 5), operand(chains[i], 4)], axis=0)
        upd = _dot_tn(lhs, rhs)
        g_tot = g_refs[d][0, g, 0][:, p * LANES:(p + 1) * LANES]
        state[i] = Z0[i] * g_tot + jnp.where(same_head, upd, 0.0)

    @pl.when(c == pl.num_programs(1) - 1)
    def _():
        for i, (g, d, p) in enumerate(chains):
            z = state[i]
            sfin_ref[d, g, 2 * p] = z[0:HEAD, 0:HEAD]
            sfin_ref[d, g, 2 * p + 1] = z[HEAD:2 * HEAD, HEAD:2 * HEAD]


def _rwkv_scan(sc, gt, vb, s0, row0, batch, t_len):
    n = vb.shape[0]
    nch = t_len // CHUNK
    assert row0 % (t_len * SCAN_GROUP) == 0 and batch % SCAN_GROUP == 0
    seq0 = row0 // (t_len * SCAN_GROUP)
    width = SCAN_OPS * D_R
    sc4 = sc.reshape(2, n // t_len, t_len, width)
    gt5 = gt.reshape(2, n // t_len, nch, 1, D_R)
    v3 = vb.reshape(n // t_len, t_len, D_R)
    G = SCAN_GROUP
    fwd = lambda c: c
    bwd = lambda c: nch - 1 - c
    sc_spec = lambda d, at: pl.BlockSpec((1, G, CHUNK, width), lambda s, c: (d, seq0 + s, at(c), 0))
    v_spec = lambda at: pl.BlockSpec((G, CHUNK, D_R), lambda s, c: (seq0 + s, at(c), 0))
    gt_spec = lambda d, at: pl.BlockSpec((1, G, 1, 1, D_R), lambda s, c: (d, seq0 + s, at(c), 0, 0))
    y_spec = lambda at: pl.BlockSpec((G, CHUNK, D_R), lambda s, c: (s, at(c), 0))
    st = pl.BlockSpec((2, G, H_R, HEAD, HEAD), lambda s, c: (0, s, 0, 0, 0))
    return pl.pallas_call(
        _scan_kernel,
        out_shape=[jax.ShapeDtypeStruct((batch, t_len, D_R), F32),
                   jax.ShapeDtypeStruct((batch, t_len, D_R), F32),
                   jax.ShapeDtypeStruct((2, batch, H_R, HEAD, HEAD), F32)],
        grid=(batch // G, nch),
        in_specs=[sc_spec(0, fwd), sc_spec(1, bwd), v_spec(fwd), v_spec(bwd),
                  gt_spec(0, fwd), gt_spec(1, bwd), st],
        out_specs=[y_spec(fwd), y_spec(bwd), st],
        scratch_shapes=[pltpu.VMEM((G * 2 * D_R // LANES, LANES, LANES), F32)],
        compiler_params=_cparams("arbitrary", "arbitrary"),
        name="rwkv_scan",
    )(sc4, sc4, v3, v3, gt5, gt5, s0)


def _attn_kernel(*refs, tq, with_cache):
    if with_cache:
        q_ref, k_ref, v_ref, ck_ref, cv_ref, o_ref = refs
        kx = jnp.concatenate([k_ref[...], ck_ref[0].astype(BF16)], axis=0)
        vx = jnp.concatenate([v_ref[...], cv_ref[0]], axis=0).astype(BF16)
    else:
        q_ref, k_ref, v_ref, o_ref = refs
        kx = k_ref[...]
        vx = v_ref[...].astype(BF16)
    lane = lax.broadcasted_iota(jnp.int32, (1, D_KV), 1)
    low = lane < HEAD
    k_sw = pltpu.roll(kx.astype(F32), HEAD, 1).astype(BF16)
    v_sw = pltpu.roll(vx.astype(F32), HEAD, 1).astype(BF16)
    group = N_HEADS // N_KV
    for p in range(D_A // LANES):
        g = (2 * p) // group
        keep = low if g == 0 else jnp.logical_not(low)
        kd = jnp.where(keep, kx, k_sw)
        vd = jnp.where(keep, vx, v_sw)
        qp = q_ref[:, p * LANES:(p + 1) * LANES]
        zero = jnp.zeros_like(qp)
        qs = jnp.concatenate([jnp.where(low, qp, zero), jnp.where(low, zero, qp)], axis=0)
        s = _dot_nt(qs, kd)
        mx = jnp.max(s, axis=-1, keepdims=True)
        e = jnp.exp(s - mx)
        den = jnp.sum(e, axis=-1, keepdims=True)
        o = _dot(e.astype(BF16), vd) / den
        o_ref[:, p * LANES:(p + 1) * LANES] = jnp.where(low, o[0:tq], o[tq:2 * tq]).astype(BF16)


def _attention(qr, kr, va, row0, batch, t_len, cache_k=None, cache_v=None):
    tq = min(t_len, ATTN_ROWS)
    nq = t_len // tq
    qblk0 = row0 // tq
    sblk0 = row0 // t_len
    with_cache = cache_k is not None
    in_specs = [pl.BlockSpec((tq, D_A), lambda b, i: (qblk0 + b * nq + i, 0)),
                pl.BlockSpec((t_len, D_KV), lambda b, i: (sblk0 + b, 0)),
                pl.BlockSpec((t_len, D_KV), lambda b, i: (sblk0 + b, 0))]
    args = [qr, kr, va]
    if with_cache:
        past = cache_k.shape[1]
        in_specs += [pl.BlockSpec((1, past, D_KV), lambda b, i: (b, 0, 0))] * 2
        args += [cache_k, cache_v]
    return pl.pallas_call(
        functools.partial(_attn_kernel, tq=tq, with_cache=with_cache),
        out_shape=jax.ShapeDtypeStruct((batch * t_len, D_A), BF16),
        grid=(batch, nq),
        in_specs=in_specs,
        out_specs=pl.BlockSpec((tq, D_A), lambda b, i: (b * nq + i, 0)),
        compiler_params=_cparams("arbitrary", "arbitrary"),
        name="attention_cache" if with_cache else "attention_ctx",
    )(*args)


def _post_kernel(yfc_ref, yfl_ref, ybc_ref, ybl_ref, bonus_ref, g_ref, yac_ref, yal_ref, gr_ref, ga_ref,
                 xc_ref, xl_ref,
                 mod_ref, lnw_ref, lnb_ref, wur_ref, wua_ref, wo_ref, n2_ref, wr_ref, br_ref, ones_ref,
                 x1_o, h2_o, idx_o, gate_o, *, ctx_blocks):
    ones = ones_ref[...]
    is_ctx = pl.program_id(0) < ctx_blocks
    y_a = jnp.where(is_ctx, yac_ref[...], yal_ref[...])
    x = jnp.where(is_ctx, xc_ref[...], xl_ref[...])
    y = jnp.where(is_ctx, yfc_ref[...] + ybc_ref[...], yfl_ref[...] + ybl_ref[...])
    mu = _head_sum(y, ones) * (1.0 / HEAD)
    yc = y - mu
    var = _head_sum(yc * yc, ones) * (1.0 / HEAD)
    yn = yc * lax.rsqrt(var + GN_EPS) * lnw_ref[...] + lnb_ref[...] + bonus_ref[...]
    y_r = (yn * g_ref[...]).astype(BF16)
    merged = (_sigmoid(gr_ref[...]) * _dot(y_r, wur_ref[...])
              + _sigmoid(ga_ref[...]) * _dot(y_a, wua_ref[...]))
    mix = _dot(merged.astype(BF16), wo_ref[...])
    mod = mod_ref[0]
    g1 = mod[:, 2 * D_MODEL:3 * D_MODEL]
    sh2 = mod[:, 3 * D_MODEL:4 * D_MODEL]
    sc2 = mod[:, 4 * D_MODEL:5 * D_MODEL]
    x1 = x + g1 * mix
    x1_o[...] = x1
    ms = jnp.mean(x1 * x1, axis=-1, keepdims=True)
    h2 = x1 * lax.rsqrt(ms + NORM_EPS) * n2_ref[...] * (1.0 + sc2) + sh2
    _store_token_tiles(h2_o, h2)
    hh, hl = _split2(h2)
    wh, wl = _split2(wr_ref[...])
    hi_terms = _dot(hh, jnp.concatenate([wh, wl], axis=1))
    logits = hi_terms[:, 0:LANES] + hi_terms[:, LANES:2 * LANES] + _dot(hl, wh) + br_ref[...]
    lane_i = lax.broadcasted_iota(jnp.int32, logits.shape, 1)
    lane = lane_i.astype(F32)
    neg = jnp.float32(-jnp.inf)
    cur = jnp.where(lane_i < N_EXPERTS, logits, neg)
    vals, idxs = [], []
    for _ in range(TOP_K):
        mx = jnp.max(cur, axis=-1, keepdims=True)
        ix = jnp.min(jnp.where(cur == mx, lane, float(LANES)), axis=-1, keepdims=True)
        vals.append(mx)
        idxs.append(ix)
        cur = jnp.where(lane == ix, neg, cur)
    es = [jnp.exp(val - vals[0]) for val in vals]
    den = es[0] + es[1] + es[2] + es[3]
    idx_out = jnp.zeros(logits.shape, jnp.int32)
    gate_out = jnp.zeros(logits.shape, F32)
    for j in range(TOP_K):
        idx_out = jnp.where(lane_i == j, idxs[j].astype(jnp.int32), idx_out)
        gate_out = jnp.where(lane_i == j, es[j] / den, gate_out)
    idx_o[...] = idx_out
    gate_o[...] = gate_out


def _post(y_f, y_b, bonus, g, ya_ctx, ya_lat, gate_r, gate_a, x_ctx, x_lat, mods3, mod_row, p):
    n = x_ctx.shape[0] + x_lat.shape[0]
    tm = TM_POST
    nbc = ya_ctx.shape[0] // tm
    row = lambda i: (i, 0)
    full = lambda *shape: pl.BlockSpec(shape, lambda i: (0,) * len(shape))
    tok = lambda width: pl.BlockSpec((tm, width), row)
    pair = lambda width: _token_specs(nbc, width, tm)
    return pl.pallas_call(
        functools.partial(_post_kernel, ctx_blocks=nbc),
        out_shape=[jax.ShapeDtypeStruct((n, D_MODEL), F32),
                   jax.ShapeDtypeStruct((n * TILE_ROWS, LANES), F32),
                   jax.ShapeDtypeStruct((n, LANES), jnp.int32),
                   jax.ShapeDtypeStruct((n, LANES), F32)],
        grid=(n // tm,),
        in_specs=pair(D_R) + pair(D_R) + [tok(D_R), tok(D_R)]
                 + pair(D_A) + [tok(D_MODEL), tok(D_MODEL)] + pair(D_MODEL) + [
                  pl.BlockSpec((1, 1, 6 * D_MODEL), lambda i: (mod_row(i), 0, 0)),
                  full(1, D_R), full(1, D_R), full(D_R, D_MODEL), full(D_A, D_MODEL),
                  full(D_MODEL, D_MODEL), full(1, D_MODEL), full(D_MODEL, LANES), full(1, LANES),
                  full(D_R, D_R)],
        out_specs=[tok(D_MODEL), pl.BlockSpec((tm * TILE_ROWS, LANES), row), tok(LANES), tok(LANES)],
        compiler_params=_cparams("arbitrary"),
        name="mixer_post",
    )(*y_f, *y_b, bonus, g, ya_ctx, ya_lat, gate_r, gate_a, x_ctx, x_lat, mods3,
      p["lnw"], p["lnb"], p["wur"], p["wua"], p["wo"], p["n2"], p["wr"], p["br"], p["ones"])


def _moe_kernel(blk_e_ref, blk_first_ref, next_e_ref, used_ref,
                src0_ref, src_next_ref, dst_prev_ref, h_hbm, w1_hbm, b1_ref, w2_hbm, b2_ref,
                out_hbm, w1f, w2f, w1b, w2b, xs, ys, sem_g, sem_s, sem_t, sem_w, *, n_rows):
    b = pl.program_id(0)
    used = used_ref[0]
    R = MOE_ROWS
    T = TILE_ROWS

    def tile(ref, i):
        return ref.at[pl.ds(pl.multiple_of(i * T, T), T)]

    def gather(table_ref, buf):
        for i in range(R):
            pltpu.make_async_copy(tile(h_hbm, table_ref[0, 0, i]), tile(xs.at[buf], i),
                                  sem_g.at[buf]).start(priority=i % 2)

    def scatter_prev(buf):
        for i in range(R):
            pltpu.make_async_copy(tile(ys.at[buf], i), tile(out_hbm, dst_prev_ref[0, 0, i]),
                                  sem_s.at[buf]).start(priority=i % 2)

    def fetch_weights(e):
        pltpu.make_async_copy(w1_hbm.at[e], w1f, sem_w.at[0]).start()
        pltpu.make_async_copy(w2_hbm.at[e], w2f, sem_w.at[1]).start()

    @pl.when(b == 0)
    def _():
        ys[...] = jnp.zeros_like(ys)
        for s in range(2):
            band = pltpu.make_async_copy(ys.at[s], out_hbm.at[pl.ds((n_rows + s * R) * T, R * T)], sem_t)
            band.start()
            band.wait()
        gather(src0_ref, 0)
        fetch_weights(blk_e_ref[0])

    @pl.when(jnp.logical_and(blk_first_ref[b] == 1, b < used))
    def _():
        pltpu.make_async_copy(w1_hbm.at[0], w1f, sem_w.at[0]).wait()
        pltpu.make_async_copy(w2_hbm.at[0], w2f, sem_w.at[1]).wait()
        w1b[...] = w1f[...].astype(BF16)
        w2b[...] = w2f[...].astype(BF16)

        @pl.when(next_e_ref[b] >= 0)
        def _():
            fetch_weights(next_e_ref[b])

    def step(p):
        @pl.when(b <= used)
        def _():
            pltpu.make_async_copy(h_hbm.at[pl.ds(0, R * T)], xs.at[p], sem_g.at[p]).wait()

        @pl.when(jnp.logical_and(b >= 1, b <= used + 1))
        def _():
            pltpu.make_async_copy(ys.at[p], out_hbm.at[pl.ds(0, R * T)], sem_s.at[p]).wait()

        @pl.when(b < used)
        def _():
            gather(src_next_ref, 1 - p)
            scatter_prev(1 - p)
            hb = _dot(_load_token_tiles(xs.at[p]).astype(BF16), w1b[...]) + b1_ref[0]
            glu = jnp.minimum(hb[:, 0:D_FF], SWIGLU_LIMIT)
            lin = jnp.clip(hb[:, D_FF:2 * D_FF], -SWIGLU_LIMIT, SWIGLU_LIMIT)
            act = glu * _sigmoid(SWIGLU_ALPHA * glu) * (lin + 1.0)
            _store_token_tiles(ys.at[p], _dot(act.astype(BF16), w2b[...]) + b2_ref[0])

        @pl.when(b == used)
        def _():
            scatter_prev(1 - p)

    for p in range(2):
        pl.when(b % 2 == p)(functools.partial(step, p))


def _moe(h2, src, dst_prev, blk_e, blk_first, next_e, used, w1, b1, w2, b2):
    d = D_MODEL
    n = h2.shape[0] // TILE_ROWS
    nb = src.shape[0]
    steps = nb + 2
    cur = lambda b, *_: (jnp.minimum(b, nb), 0, 0)
    nxt = lambda b, *_: (jnp.minimum(b + 1, nb - 1), 0, 0)
    table = lambda index_map: pl.BlockSpec((1, 1, MOE_ROWS), index_map, memory_space=pltpu.SMEM)
    expert = lambda b, blk_e, *_: (blk_e[jnp.minimum(b, nb - 1)], 0, 0)
    grid_spec = pltpu.PrefetchScalarGridSpec(
        num_scalar_prefetch=4,
        grid=(steps,),
        in_specs=[table(lambda b, *_: (0, 0, 0)), table(nxt), table(cur),
                  pl.BlockSpec(memory_space=pl.ANY),
                  pl.BlockSpec(memory_space=pl.ANY),
                  pl.BlockSpec((1, 1, 2 * D_FF), expert),
                  pl.BlockSpec(memory_space=pl.ANY),
                  pl.BlockSpec((1, 1, d), expert)],
        out_specs=pl.BlockSpec(memory_space=pl.ANY),
        scratch_shapes=[pltpu.VMEM((d, 2 * D_FF), F32),
                        pltpu.VMEM((D_FF, d), F32),
                        pltpu.VMEM((d, 2 * D_FF), BF16),
                        pltpu.VMEM((D_FF, d), BF16),
                        pltpu.VMEM((2, MOE_ROWS * TILE_ROWS, LANES), F32),
                        pltpu.VMEM((2, MOE_ROWS * TILE_ROWS, LANES), F32),
                        pltpu.SemaphoreType.DMA((2,)),
                        pltpu.SemaphoreType.DMA((2,)),
                        pltpu.SemaphoreType.DMA(()),
                        pltpu.SemaphoreType.DMA((2,))])
    tab3 = lambda t: t.reshape(-1, 1, MOE_ROWS)
    pad = lambda t: jnp.concatenate([t, jnp.zeros((steps - nb,), jnp.int32)])
    return pl.pallas_call(
        functools.partial(_moe_kernel, n_rows=n * TOP_K),
        out_shape=jax.ShapeDtypeStruct(((n * TOP_K + 2 * MOE_ROWS) * TILE_ROWS, LANES), F32),
        grid_spec=grid_spec,
        compiler_params=pltpu.CompilerParams(dimension_semantics=("arbitrary",),
                                             vmem_limit_bytes=VMEM_LIMIT, has_side_effects=True),
        name="moe_experts",
    )(blk_e, pad(blk_first), next_e, used, tab3(src), tab3(src), tab3(dst_prev), h2, w1,
      b1.reshape(N_EXPERTS, 1, 2 * D_FF), w2, b2.reshape(N_EXPERTS, 1, d))


def _route(top_idx, n):
    nk = n * TOP_K
    cap = nk + N_EXPERTS * MOE_ROWS
    nb = cap // MOE_ROWS
    flat_e = top_idx.reshape(nk)
    order = jnp.argsort(flat_e, stable=True).astype(jnp.int32)
    counts = jnp.sum((flat_e[:, None] == jnp.arange(N_EXPERTS, dtype=jnp.int32)[None, :])
                     .astype(jnp.int32), axis=0)
    start = jnp.cumsum(counts) - counts
    n_blk = (counts + MOE_ROWS - 1) // MOE_ROWS
    blk_end = jnp.cumsum(n_blk)
    first_blk = blk_end - n_blk
    experts = jnp.arange(N_EXPERTS, dtype=jnp.int32)
    blk = jnp.arange(nb, dtype=jnp.int32)
    blk_e = jnp.minimum(jnp.sum((blk[:, None] >= blk_end[None, :]).astype(jnp.int32), axis=1),
                        N_EXPERTS - 1)
    onehot = (blk_e[:, None] == experts[None, :]).astype(jnp.int32)
    pick = lambda t: jnp.sum(onehot * t[None, :], axis=1)
    blk_first_of_e, blk_count, blk_start = pick(first_blk), pick(counts), pick(start)
    blk_first = (blk == blk_first_of_e).astype(jnp.int32)
    blk_after = pick(blk_end)
    next_e = jnp.where(blk_after < blk_end[-1], blk_e[jnp.minimum(blk_after, nb - 1)], -1).astype(jnp.int32)
    row = jnp.arange(MOE_ROWS, dtype=jnp.int32)[None, :]
    within = (blk - blk_first_of_e)[:, None] * MOE_ROWS + row
    valid = jnp.logical_and(within < blk_count[:, None], (blk < blk_end[-1])[:, None])
    j = order[jnp.clip(blk_start[:, None] + within, 0, nk - 1)]
    src = jnp.where(valid, j // TOP_K, 0).astype(jnp.int32)
    scratch_row = nk + (blk % 2)[:, None] * MOE_ROWS + row
    dst = jnp.where(valid, (j % TOP_K) * n + j // TOP_K, scratch_row).astype(jnp.int32)
    dst_prev = jnp.concatenate([nk + MOE_ROWS + row, dst], axis=0)
    return (src, dst_prev, blk_e.astype(jnp.int32), blk_first, next_e,
            blk_end[-1:].astype(jnp.int32))


def _final_kernel(x1_ref, y0_ref, y1_ref, y2_ref, y3_ref, gate_ref, mod_ref, nf_ref, oc_ref, ol_ref,
                  *, ctx_blocks):
    gates = gate_ref[...]
    ffn = gates[:, 0:1] * _load_token_tiles(y0_ref)
    for j, ref in enumerate((y1_ref, y2_ref, y3_ref), start=1):
        ffn = ffn + gates[:, j:j + 1] * _load_token_tiles(ref)
    g2 = mod_ref[0][:, 5 * D_MODEL:6 * D_MODEL]
    x2 = x1_ref[...] + g2 * ffn
    ms = jnp.mean(x2 * x2, axis=-1, keepdims=True)
    y = x2 * lax.rsqrt(ms + NORM_EPS) * nf_ref[...]
    i = pl.program_id(0)

    @pl.when(i < ctx_blocks)
    def _():
        oc_ref[...] = y

    @pl.when(i >= ctx_blocks)
    def _():
        ol_ref[...] = y


def _final(x1, ys, gates, mods3, mod_row, norm_f, nc):
    n, d = x1.shape
    tm = TM_POST
    nb = n // tm
    nbc = nc // tm
    row = lambda i: (i, 0)
    return pl.pallas_call(
        functools.partial(_final_kernel, ctx_blocks=nbc),
        out_shape=[jax.ShapeDtypeStruct((nc, d), F32), jax.ShapeDtypeStruct((n - nc, d), F32)],
        grid=(nb,),
        in_specs=[pl.BlockSpec((tm, d), row)]
                 + [pl.BlockSpec((tm * TILE_ROWS, LANES), functools.partial(lambda j, i: (j * nb + i, 0), j))
                    for j in range(TOP_K)]
                 + [pl.BlockSpec((tm, LANES), row),
                    pl.BlockSpec((1, 1, 6 * d), lambda i: (mod_row(i), 0, 0)),
                    pl.BlockSpec((1, d), lambda i: (0, 0))],
        out_specs=_token_specs(nbc, d, tm),
        compiler_params=_cparams("arbitrary"),
        name="combine_final",
    )(x1, ys, ys, ys, ys, gates, mods3, norm_f.reshape(1, d))


def _rope_tables(t_ctx, t_lat):
    rows = np.arange(t_lat) // GRID_W
    cols = np.arange(t_lat) % GRID_W
    inv = ROPE_THETA ** (-np.arange(0, AXIS_DIM, 2, dtype=np.float32) / AXIS_DIM)
    inv = jnp.asarray(inv, F32)
    ang_r = jnp.asarray(rows, F32)[:, None] * inv[None, :]
    ang_c = jnp.asarray(cols, F32)[:, None] * inv[None, :]
    cos = jnp.concatenate([jnp.cos(ang_r)] * 2 + [jnp.cos(ang_c)] * 2, axis=1)
    sin = jnp.concatenate([-jnp.sin(ang_r), jnp.sin(ang_r), -jnp.sin(ang_c), jnp.sin(ang_c)], axis=1)
    cos = jnp.concatenate([jnp.ones((t_ctx, HEAD), F32), cos], axis=0)
    sin = jnp.concatenate([jnp.zeros((t_ctx, HEAD), F32), sin], axis=0)
    return jnp.concatenate([cos, cos], axis=1), jnp.concatenate([sin, sin], axis=1)


def kernel(x_prompt, x_sample, cache_k, cache_v, state_rwkv_fwd, state_rwkv_bwd, c, c_ctx, w_ada, b_ada, norm1, norm2, w_in, rw_w0, rw_w2, rw_a0, rw_a2, rw_g2, rw_kk, rw_ka, rw_rk, rw_ln_w, rw_ln_b, q_norm, k_norm, w_up_r, w_up_a, w_out, w_router, b_router, w_moe_in, b_moe_in, w_moe_out, b_moe_out, norm_f):
    depth = w_in.shape[0]
    assert depth == 1, "single trunk layer"
    bc, tc, d = x_prompt.shape
    bl, tl, _ = x_sample.shape
    nc, nl = bc * tc, bl * tl
    n = nc + nl
    assert d == D_MODEL and TM % tc == 0 and tl % TM == 0 and nc % tl == 0
    assert all(nc % tm == 0 and tl % tm == 0 for tm in (TM_PROJ, TM_POST))
    l = 0

    x_ctx, x_lat = x_prompt.reshape(nc, d), x_sample.reshape(nl, d)
    cond = jnp.concatenate([c_ctx[None, :], c, jnp.zeros((16 - 1 - bl, d), F32)], axis=0)
    mods3 = _adaln(cond, w_ada[l], b_ada[l]).reshape(16, 1, 6 * d)
    nbc = nc // TM
    per_seq = tl // TM
    tab_row = lambda i: jnp.where(i < nbc, 0, 1 + (i - nbc) % per_seq)
    mod_row = lambda i: jnp.where(i < nc // TM_POST, 0, 1 + (i - nc // TM_POST) // (tl // TM_POST))

    z = dict(zip([name for name, _ in IN_SPLITS],
                 _proj_in(x_ctx, x_lat, mods3, norm1[l], w_in[l].astype(BF16), tl)))

    ones_bd = jnp.asarray(np.kron(np.eye(H_R), np.ones((HEAD, HEAD))), BF16)
    pad_lo = lambda w: jnp.concatenate([w, jnp.zeros_like(w)], axis=0)
    pad_hi = lambda w: jnp.concatenate([jnp.zeros_like(w), w], axis=0)
    prep_p = dict(
        w0=rw_w0[l], a0=rw_a0[l],
        w2=jnp.stack([pad_lo(rw_w2[l, 0]), pad_hi(rw_w2[l, 1])]).astype(BF16),
        a2=jnp.stack([pad_lo(rw_a2[l, 0]), pad_hi(rw_a2[l, 1])]).astype(BF16),
        g2=rw_g2[l].astype(BF16),
        kkw=rw_kk[l].reshape(1, D_R), kaw=rw_ka[l].reshape(1, D_R), rk=rw_rk[l].reshape(1, D_R),
        qn=jnp.tile(q_norm[l], N_HEADS).reshape(1, D_A), kn=jnp.tile(k_norm[l], N_KV).reshape(1, D_KV),
        ones=ones_bd,
        tri=jnp.asarray(np.stack([np.kron(np.eye(TM // CHUNK), np.tril(np.ones((CHUNK, CHUNK)))),
                                  np.kron(np.eye(TM // CHUNK), np.triu(np.ones((CHUNK, CHUNK))))]), BF16),
        chunk_ones=jnp.asarray(np.kron(np.eye(TM // CHUNK), np.ones((CHUNK, CHUNK))), BF16))
    cos_tab, sin_tab = _rope_tables(TM, tl)
    sc, gt, vb, bonus, g, qr, kn, kr = _prep(z, cos_tab, sin_tab, tab_row, prep_p)

    yf_c, yb_c, s_fin = _rwkv_scan(sc, gt, vb, jnp.zeros((2, bc, H_R, HEAD, HEAD), F32), 0, bc, tc)
    yf_l, yb_l, _ = _rwkv_scan(sc, gt, vb, jnp.stack([state_rwkv_fwd[:, l], state_rwkv_bwd[:, l]]),
                               nc, bl, tl)
    y_f = (yf_c.reshape(nc, D_R), yf_l.reshape(nl, D_R))
    y_b = (yb_c.reshape(nc, D_R), yb_l.reshape(nl, D_R))

    ya_ctx = _attention(qr, kr, z["va"], 0, bc, tc)
    ya_lat = _attention(qr, kr, z["va"], nc, bl, tl,
                        cache_k[:, l].reshape(bl, -1, D_KV), cache_v[:, l].reshape(bl, -1, D_KV))

    post_p = dict(
        lnw=rw_ln_w[l].reshape(1, D_R), lnb=rw_ln_b[l].reshape(1, D_R),
        wur=w_up_r[l].astype(BF16), wua=w_up_a[l].astype(BF16), wo=w_out[l].astype(BF16),
        n2=norm2[l].reshape(1, d),
        wr=jnp.pad(w_router[l], ((0, 0), (0, LANES - N_EXPERTS))),
        br=jnp.pad(b_router[l], (0, LANES - N_EXPERTS)).reshape(1, LANES),
        ones=ones_bd)
    x1, h2, top_idx, gates = _post(y_f, y_b, bonus, g, ya_ctx, ya_lat, z["gate_r"], z["gate_a"],
                                   x_ctx, x_lat, mods3, mod_row, post_p)

    src, dst, blk_e, blk_first, next_e, used = _route(top_idx[:, :TOP_K], n)
    ys = _moe(h2, src, dst, blk_e, blk_first, next_e, used,
              w_moe_in[l], b_moe_in[l], w_moe_out[l], b_moe_out[l])
    y_ctx, y_lat = _final(x1, ys, gates, mods3, mod_row, norm_f, nc)

    y_prompt = y_ctx.reshape(bc, tc, d)
    y_sample = y_lat.reshape(bl, tl, d)
    new_cache_k = kn[:nc].reshape(bc, 1, tc, N_KV, HEAD)
    new_cache_v = z["va"][:nc].reshape(bc, 1, tc, N_KV, HEAD)
    new_state_fwd = s_fin[0][:, None]
    new_state_bwd = s_fin[1][:, None]
    return (y_prompt, y_sample, new_cache_k, new_cache_v, new_state_fwd, new_state_bwd)
```

```python
import functools

import numpy as np
import jax
import jax.numpy as jnp
from jax import lax
from jax.experimental import pallas as pl
from jax.experimental.pallas import tpu as pltpu

F32 = jnp.float32
BF16 = jnp.bfloat16

D_MODEL = 1024
GRID_W = 64
NORM_EPS = 1e-6
HEAD = 64
H_R = 8
D_R = H_R * HEAD
DECAY_RANK = 64
AAA_RANK = 64
GATE_RANK = 128
GN_EPS = 64e-5
N_HEADS = 8
N_KV = 2
D_A = N_HEADS * HEAD
D_KV = N_KV * HEAD
AXIS_DIM = HEAD // 2
ROPE_THETA = 10000.0
N_EXPERTS = 32
TOP_K = 4
D_FF = D_MODEL
SWIGLU_ALPHA = 1.702
SWIGLU_LIMIT = 7.0
LORA_COLS = 2 * DECAY_RANK + 2 * AAA_RANK + GATE_RANK
IN_SPLITS = (("r", D_R), ("k", D_R), ("v", D_R), ("lora", LORA_COLS), ("q", D_A),
             ("ka", D_KV), ("va", D_KV), ("gate_r", D_MODEL), ("gate_a", D_MODEL))

LANES = 128
TM = 256
TM_POST = 512
TM_PROJ = 512
ATTN_ROWS = 512
CHUNK = 64
SCAN_OPS = 6
SCAN_GROUP = 4
MOE_ROWS = 256
VMEM_LIMIT = 56 * 1024 * 1024


def _cparams(*sem):
    return pltpu.CompilerParams(dimension_semantics=sem, vmem_limit_bytes=VMEM_LIMIT)


def _dot(a, b):
    return jnp.dot(a, b, preferred_element_type=F32)


def _dot_nt(a, b):
    return lax.dot_general(a, b, (((1,), (1,)), ((), ())), preferred_element_type=F32)


def _dot_tn(a, b):
    return lax.dot_general(a, b, (((0,), (0,)), ((), ())), preferred_element_type=F32)


def _split2(x):
    hi = x.astype(BF16)
    lo = (x - hi.astype(F32)).astype(BF16)
    return hi, lo


def _head_sum(x, ones_bd):
    return _dot(x.astype(BF16), ones_bd)


def _sigmoid(x):
    return 0.5 * jnp.tanh(0.5 * x) + 0.5


TILE_ROWS = D_MODEL // LANES


def _store_token_tiles(ref, x):
    rows = x.shape[0]
    for j in range(TILE_ROWS):
        ref[pl.ds(j, rows, stride=TILE_ROWS), :] = x[:, j * LANES:(j + 1) * LANES]


def _load_token_tiles(ref):
    rows = ref.shape[0] // TILE_ROWS
    return jnp.concatenate([ref[pl.ds(j, rows, stride=TILE_ROWS), :] for j in range(TILE_ROWS)], axis=1)


def _adaln_kernel(c_ref, w_ref, b_ref, o_ref):
    c = c_ref[...]
    s = (c * _sigmoid(c)).astype(BF16)
    o_ref[...] = _dot(s, w_ref[...].astype(BF16)) + b_ref[...]


def _adaln(cond, w_ada, b_ada):
    rows, d = cond.shape
    cols = w_ada.shape[1]
    tn = 512
    return pl.pallas_call(
        _adaln_kernel,
        out_shape=jax.ShapeDtypeStruct((rows, cols), F32),
        grid=(cols // tn,),
        in_specs=[pl.BlockSpec((rows, d), lambda j: (0, 0)),
                  pl.BlockSpec((d, tn), lambda j: (0, j)),
                  pl.BlockSpec((1, tn), lambda j: (0, j))],
        out_specs=pl.BlockSpec((rows, tn), lambda j: (0, j)),
        compiler_params=_cparams("arbitrary"),
        name="adaln",
    )(cond, w_ada, b_ada.reshape(1, cols))


def _token_specs(nbc, width, tm=TM):
    return [pl.BlockSpec((tm, width), lambda i: (jnp.minimum(i, nbc - 1), 0)),
            pl.BlockSpec((tm, width), lambda i: (jnp.maximum(i - nbc, 0), 0))]


def _proj_in_kernel(xc_ref, xl_ref, mod_ref, n1_ref, w_ref, *out_refs, ctx_blocks):
    x = jnp.where(pl.program_id(0) < ctx_blocks, xc_ref[...], xl_ref[...])
    ms = jnp.mean(x * x, axis=-1, keepdims=True)
    hn = x * lax.rsqrt(ms + NORM_EPS) * n1_ref[...]
    mod = mod_ref[0]
    sh1 = mod[:, 0:D_MODEL]
    sc1 = mod[:, D_MODEL:2 * D_MODEL]
    h = (hn * (1.0 + sc1) + sh1).astype(BF16)
    c0 = 0
    for ref, (_, width) in zip(out_refs, IN_SPLITS):
        ref[...] = _dot(h, w_ref[:, c0:c0 + width])
        c0 += width


def _proj_in(x_ctx, x_lat, mods3, norm1, w_in_bf, t_lat):
    d = x_ctx.shape[1]
    n = x_ctx.shape[0] + x_lat.shape[0]
    tm = TM_PROJ
    nbc = x_ctx.shape[0] // tm
    per_seq = t_lat // tm
    cols = w_in_bf.shape[1]
    row = lambda i: (i, 0)
    mod_row = lambda i: jnp.where(i < nbc, 0, 1 + (i - nbc) // per_seq)
    return pl.pallas_call(
        functools.partial(_proj_in_kernel, ctx_blocks=nbc),
        out_shape=[jax.ShapeDtypeStruct((n, width), F32) for _, width in IN_SPLITS],
        grid=(n // tm,),
        in_specs=_token_specs(nbc, d, tm) + [
                  pl.BlockSpec((1, 1, 6 * d), lambda i: (mod_row(i), 0, 0)),
                  pl.BlockSpec((1, d), lambda i: (0, 0)),
                  pl.BlockSpec((d, cols), lambda i: (0, 0))],
        out_specs=[pl.BlockSpec((tm, width), row) for _, width in IN_SPLITS],
        compiler_params=_cparams("arbitrary"),
        name="proj_in",
    )(x_ctx, x_lat, mods3, norm1.reshape(1, d), w_in_bf)


def _prep_kernel(r_ref, k_ref, v_ref, lora_ref, q_ref, ka_ref, cos_ref, sin_ref,
                 w0_ref, w2_ref, a0_ref, a2_ref, g2_ref, kkw_ref, kaw_ref, rk_ref,
                 qn_ref, kn_ref, ones_ref, tri_ref, chunk_ones_ref,
                 sc_o, gt_o, vb_o, bonus_o, g_o, qr_o, kn_o, kr_o):
    ones = ones_ref[...]
    chunk_ones = chunk_ones_ref[...]
    r = r_ref[...]
    k = k_ref[...]
    v = v_ref[...]
    vb_o[...] = v.astype(BF16)
    lora = lora_ref[...]
    kk = k * kkw_ref[...]
    kk = kk * lax.rsqrt(_head_sum(kk * kk, ones) + 1e-12)
    th = jnp.tanh(lora[:, 0:LANES]).astype(BF16)
    al = lora[:, LANES:2 * LANES].astype(BF16)
    kd_sum = jnp.zeros_like(k)
    for d in range(2):
        u = w0_ref[d:d + 1, :] + _dot(th, w2_ref[d])
        lw = -float(np.exp(-0.5)) * _sigmoid(u)
        a = _sigmoid(a0_ref[d:d + 1, :] + _dot(al, a2_ref[d]))
        kd = k * (1.0 + (a - 1.0) * kaw_ref[...])
        kd_sum = kd_sum + kd
        b = kk * a
        parts = _split2(lw)
        tri = tri_ref[d]
        L = _dot(tri, parts[0]) + _dot(tri, parts[1])
        Ltot = _dot(chunk_ones, parts[0]) + _dot(chunk_ones, parts[1])
        e_inv = jnp.exp(-L)
        e_rest = jnp.exp(Ltot - L)
        scan_ops = (kk * jnp.exp(L - lw), r * jnp.exp(L), b * e_inv, kd * e_inv, b * e_rest, kd * e_rest)
        for j, op in enumerate(scan_ops):
            sc_o[d, :, j * D_R:(j + 1) * D_R] = op.astype(BF16)
        g_tot = jnp.exp(Ltot)
        for j in range(TM // CHUNK):
            gt_o[d, j] = g_tot[j * CHUNK:j * CHUNK + 1, :]
    bonus_o[...] = _head_sum(r * kd_sum * rk_ref[...], ones) * v
    g_o[...] = _dot(_sigmoid(lora[:, 2 * LANES:3 * LANES]).astype(BF16), g2_ref[...])
    q = q_ref[...]
    qn = q * lax.rsqrt(_head_sum(q * q, ones) * (1.0 / HEAD) + NORM_EPS) * qn_ref[...]
    ka = ka_ref[...]
    ones_kv = ones[0:D_KV, 0:D_KV]
    kn = ka * lax.rsqrt(_head_sum(ka * ka, ones_kv) * (1.0 / HEAD) + NORM_EPS) * kn_ref[...]
    kn_o[...] = kn
    cos = cos_ref[...]
    sin = sin_ref[...]
    half = AXIS_DIM // 2

    def rope(t, cos_t, sin_t):
        width = t.shape[1]
        lane = lax.broadcasted_iota(jnp.int32, t.shape, 1)
        first = (lane % AXIS_DIM) < half
        swapped = jnp.where(first, pltpu.roll(t, width - half, 1), pltpu.roll(t, half, 1))
        return t * cos_t + swapped * sin_t

    cos4 = jnp.concatenate([cos] * (D_A // D_KV), axis=1)
    sin4 = jnp.concatenate([sin] * (D_A // D_KV), axis=1)
    qr_o[...] = (rope(qn, cos4, sin4) * (HEAD ** -0.5)).astype(BF16)
    kr_o[...] = rope(kn, cos, sin).astype(BF16)


def _prep(z, cos_tab, sin_tab, tab_row, p):
    n = z["r"].shape[0]
    row = lambda i: (i, 0)
    full = lambda *shape: pl.BlockSpec(shape, lambda i: (0,) * len(shape))
    tok = lambda width: pl.BlockSpec((TM, width), row)
    tab = pl.BlockSpec((TM, D_KV), lambda i: (tab_row(i), 0))
    cpb = TM // CHUNK
    return pl.pallas_call(
        _prep_kernel,
        out_shape=[jax.ShapeDtypeStruct((2, n, SCAN_OPS * D_R), BF16),
                   jax.ShapeDtypeStruct((2, n // CHUNK, 1, D_R), F32),
                   jax.ShapeDtypeStruct((n, D_R), BF16),
                   jax.ShapeDtypeStruct((n, D_R), F32),
                   jax.ShapeDtypeStruct((n, D_R), F32),
                   jax.ShapeDtypeStruct((n, D_A), BF16),
                   jax.ShapeDtypeStruct((n, D_KV), F32),
                   jax.ShapeDtypeStruct((n, D_KV), BF16)],
        grid=(n // TM,),
        in_specs=[tok(D_R), tok(D_R), tok(D_R), tok(LORA_COLS), tok(D_A), tok(D_KV), tab, tab,
                  full(2, D_R), full(2, LANES, D_R), full(2, D_R), full(2, LANES, D_R),
                  full(GATE_RANK, D_R), full(1, D_R), full(1, D_R), full(1, D_R),
                  full(1, D_A), full(1, D_KV), full(D_R, D_R), full(2, TM, TM), full(TM, TM)],
        out_specs=[pl.BlockSpec((2, TM, SCAN_OPS * D_R), lambda i: (0, i, 0)),
                   pl.BlockSpec((2, cpb, 1, D_R), lambda i: (0, i, 0, 0)),
                   tok(D_R), tok(D_R), tok(D_R), tok(D_A), tok(D_KV), tok(D_KV)],
        compiler_params=_cparams("arbitrary"),
        name="mixer_prep",
    )(z["r"], z["k"], z["v"], z["lora"], z["q"], z["ka"], cos_tab, sin_tab,
      p["w0"], p["w2"], p["a0"], p["a2"], p["g2"], p["kkw"], p["kaw"], p["rk"],
      p["qn"], p["kn"], p["ones"], p["tri"], p["chunk_ones"])


def _scan_kernel(scf_ref, scb_ref, vf_ref, vb_ref, gf_ref, gb_ref, s0_ref,
                 yf_ref, yb_ref, sfin_ref, state):
    c = pl.program_id(1)
    C = CHUNK
    pairs = D_R // LANES
    chains = [(g, d, p) for g in range(SCAN_GROUP) for d in range(2) for p in range(pairs)]

    lane = lax.broadcasted_iota(jnp.int32, (1, LANES), 1)
    low = lane < HEAD

    @pl.when(c == 0)
    def _():
        zero = jnp.zeros((HEAD, HEAD), F32)
        for i, (g, d, p) in enumerate(chains):
            top = jnp.concatenate([s0_ref[d, g, 2 * p], zero], axis=1)
            bot = jnp.concatenate([zero, s0_ref[d, g, 2 * p + 1]], axis=1)
            state[i] = jnp.concatenate([top, bot], axis=0)

    W = 2 * C
    t2 = lax.broadcasted_iota(jnp.int32, (C, W), 0)
    s2 = lax.broadcasted_iota(jnp.int32, (C, W), 1) % C
    strict = (s2 < t2, s2 > t2)
    incl = (s2 <= t2, s2 >= t2)
    eye2 = jnp.where(s2 == t2, 1.0, 0.0)
    first_mat = lax.broadcasted_iota(jnp.int32, (1, W), 1) < C
    r2 = lax.broadcasted_iota(jnp.int32, (LANES, LANES), 0)
    c2 = lax.broadcasted_iota(jnp.int32, (LANES, LANES), 1)
    same_head = (r2 // HEAD) == (c2 // HEAD)

    sc_refs = (scf_ref, scb_ref)
    v_refs = (vf_ref, vb_ref)
    g_refs = (gf_ref, gb_ref)
    y_refs = (yf_ref, yb_ref)

    def operand(chain, j):
        g, d, p = chain
        c0 = j * D_R + p * LANES
        return sc_refs[d][0, g, :, c0:c0 + LANES]

    def stack(x):
        first = low if x.shape[1] == LANES else first_mat
        zero = jnp.zeros_like(x)
        return jnp.concatenate([jnp.where(first, x, zero), jnp.where(first, zero, x)], axis=0)

    Qk = [operand(ch, 0) for ch in chains]
    Qr = [operand(ch, 1) for ch in chains]
    v_p = [v_refs[d][g, :, p * LANES:(p + 1) * LANES] for g, d, p in chains]
    v_st = [stack(x) for x in v_p]
    Z0 = [state[i] for i in range(len(chains))]
    Z0b = [z.astype(BF16) for z in Z0]
    m = [_dot_nt(jnp.concatenate([Qk[i], Qr[i]], axis=0),
                 jnp.concatenate([stack(operand(ch, 2)), stack(operand(ch, 3))], axis=0))
         for i, ch in enumerate(chains)]
    A = [jnp.where(strict[d], m[i][0:C, 0:W], 0.0) for i, (g, d, p) in enumerate(chains)]
    Bm = [jnp.where(strict[d], m[i][0:C, W:2 * W], 0.0).astype(BF16)
          for i, (g, d, p) in enumerate(chains)]
    Pb = [jnp.where(incl[d], m[i][C:2 * C, 0:W], 0.0).astype(BF16)
          for i, (g, d, p) in enumerate(chains)]
    Pk = [jnp.where(incl[d], m[i][C:2 * C, W:2 * W], 0.0).astype(BF16)
          for i, (g, d, p) in enumerate(chains)]
    BV = [_dot(Bm[i], v_st[i]) for i in range(len(chains))]
    T = [eye2 - a for a in A]
    Ap = [_dot(a.astype(BF16), stack(a.astype(BF16))).astype(BF16) for a in A]
    n = 2
    while 2 * n < C:
        R = [_dot(jnp.concatenate([Ap[i], T[i].astype(BF16)], axis=0), stack(Ap[i]))
             for i in range(len(chains))]
        Ap = [x[0:C].astype(BF16) for x in R]
        T = [T[i] + R[i][C:2 * C] for i in range(len(chains))]
        n *= 2
    T = [T[i] + _dot(T[i].astype(BF16), stack(Ap[i])) for i in range(len(chains))]
    WU = [_dot(T[i].astype(BF16),
               jnp.concatenate([stack(Qk[i]), stack(BV[i].astype(BF16))], axis=1))
          for i in range(len(chains))]
    Wq = [x[:, 0:LANES].astype(BF16) for x in WU]
    Uv = [x[:, LANES:2 * LANES] for x in WU]
    PW = [_dot(Pb[i], jnp.concatenate([stack(Wq[i]), stack(Uv[i].astype(BF16))], axis=1))
          for i in range(len(chains))]
    PkV = [_dot(Pk[i], v_st[i]) for i in range(len(chains))]
    U = [_dot_nt(Wq[i], Z0b[i]) + Uv[i] for i in range(len(chains))]
    Yq = [(Qr[i].astype(F32) - PW[i][:, 0:LANES]).astype(BF16) for i in range(len(chains))]
    for i, (g, d, p) in enumerate(chains):
        y_refs[d][g, :, p * LANES:(p + 1) * LANES] = (_dot_nt(Yq[i], Z0b[i]) + PkV[i]
                                                      - PW[i][:, LANES:2 * LANES])
    for i, (g, d, p) in enumerate(chains):
        lhs = jnp.concatenate([v_p[i], (-U[i]).astype(BF16)], axis=0)
        rhs = jnp.concatenate([operand(chains[i], 5), operand(chains[i], 4)], axis=0)
        upd = _dot_tn(lhs, rhs)
        g_tot = g_refs[d][0, g, 0][:, p * LANES:(p + 1) * LANES]
        state[i] = Z0[i] * g_tot + jnp.where(same_head, upd, 0.0)

    @pl.when(c == pl.num_programs(1) - 1)
    def _():
        for i, (g, d, p) in enumerate(chains):
            z = state[i]
            sfin_ref[d, g, 2 * p] = z[0:HEAD, 0:HEAD]
            sfin_ref[d, g, 2 * p + 1] = z[HEAD:2 * HEAD, HEAD:2 * HEAD]


def _rwkv_scan(sc, gt, vb, s0, row0, batch, t_len):
    n = vb.shape[0]
    nch = t_len // CHUNK
    assert row0 % (t_len * SCAN_GROUP) == 0 and batch % SCAN_GROUP == 0
    seq0 = row0 // (t_len * SCAN_GROUP)
    width = SCAN_OPS * D_R
    sc4 = sc.reshape(2, n // t_len, t_len, width)
    gt5 = gt.reshape(2, n // t_len, nch, 1, D_R)
    v3 = vb.reshape(n // t_len, t_len, D_R)
    G = SCAN_GROUP
    fwd = lambda c: c
    bwd = lambda c: nch - 1 - c
    sc_spec = lambda d, at: pl.BlockSpec((1, G, CHUNK, width), lambda s, c: (d, seq0 + s, at(c), 0))
    v_spec = lambda at: pl.BlockSpec((G, CHUNK, D_R), lambda s, c: (seq0 + s, at(c), 0))
    gt_spec = lambda d, at: pl.BlockSpec((1, G, 1, 1, D_R), lambda s, c: (d, seq0 + s, at(c), 0, 0))
    y_spec = lambda at: pl.BlockSpec((G, CHUNK, D_R), lambda s, c: (s, at(c), 0))
    st = pl.BlockSpec((2, G, H_R, HEAD, HEAD), lambda s, c: (0, s, 0, 0, 0))
    return pl.pallas_call(
        _scan_kernel,
        out_shape=[jax.ShapeDtypeStruct((batch, t_len, D_R), F32),
                   jax.ShapeDtypeStruct((batch, t_len, D_R), F32),
                   jax.ShapeDtypeStruct((2, batch, H_R, HEAD, HEAD), F32)],
        grid=(batch // G, nch),
        in_specs=[sc_spec(0, fwd), sc_spec(1, bwd), v_spec(fwd), v_spec(bwd),
                  gt_spec(0, fwd), gt_spec(1, bwd), st],
        out_specs=[y_spec(fwd), y_spec(bwd), st],
        scratch_shapes=[pltpu.VMEM((G * 2 * D_R // LANES, LANES, LANES), F32)],
        compiler_params=_cparams("arbitrary", "arbitrary"),
        name="rwkv_scan",
    )(sc4, sc4, v3, v3, gt5, gt5, s0)


def _attn_kernel(*refs, tq, with_cache):
    if with_cache:
        q_ref, k_ref, v_ref, ck_ref, cv_ref, o_ref = refs
        kx = jnp.concatenate([k_ref[...], ck_ref[0].astype(BF16)], axis=0)
        vx = jnp.concatenate([v_ref[...], cv_ref[0]], axis=0).astype(BF16)
    else:
        q_ref, k_ref, v_ref, o_ref = refs
        kx = k_ref[...]
        vx = v_ref[...].astype(BF16)
    lane = lax.broadcasted_iota(jnp.int32, (1, D_KV), 1)
    low = lane < HEAD
    k_sw = pltpu.roll(kx.astype(F32), HEAD, 1).astype(BF16)
    v_sw = pltpu.roll(vx.astype(F32), HEAD, 1).astype(BF16)
    group = N_HEADS // N_KV
    for p in range(D_A // LANES):
        g = (2 * p) // group
        keep = low if g == 0 else jnp.logical_not(low)
        kd = jnp.where(keep, kx, k_sw)
        vd = jnp.where(low, vx if g == 0 else v_sw, jnp.ones_like(vx))
        qp = q_ref[:, p * LANES:(p + 1) * LANES]
        zero = jnp.zeros_like(qp)
        qs = jnp.concatenate([jnp.where(low, qp, zero), jnp.where(low, zero, qp)], axis=0)
        s = _dot_nt(qs, kd)
        mx = jnp.max(s, axis=-1, keepdims=True)
        e = jnp.exp(s - mx).astype(BF16)
        num_den = _dot(e, vd)
        den_num = pltpu.roll(num_den, HEAD, 1)
        first = num_den[0:tq] / den_num[0:tq]
        second = den_num[tq:2 * tq] / num_den[tq:2 * tq]
        o_ref[:, p * LANES:(p + 1) * LANES] = jnp.where(low, first, second).astype(BF16)


def _attention(qr, kr, va, row0, batch, t_len, cache_k=None, cache_v=None):
    tq = min(t_len, ATTN_ROWS)
    nq = t_len // tq
    qblk0 = row0 // tq
    sblk0 = row0 // t_len
    with_cache = cache_k is not None
    in_specs = [pl.BlockSpec((tq, D_A), lambda b, i: (qblk0 + b * nq + i, 0)),
                pl.BlockSpec((t_len, D_KV), lambda b, i: (sblk0 + b, 0)),
                pl.BlockSpec((t_len, D_KV), lambda b, i: (sblk0 + b, 0))]
    args = [qr, kr, va]
    if with_cache:
        past = cache_k.shape[1]
        in_specs += [pl.BlockSpec((1, past, D_KV), lambda b, i: (b, 0, 0))] * 2
        args += [cache_k, cache_v]
    return pl.pallas_call(
        functools.partial(_attn_kernel, tq=tq, with_cache=with_cache),
        out_shape=jax.ShapeDtypeStruct((batch * t_len, D_A), BF16),
        grid=(batch, nq),
        in_specs=in_specs,
        out_specs=pl.BlockSpec((tq, D_A), lambda b, i: (b * nq + i, 0)),
        compiler_params=_cparams("arbitrary", "arbitrary"),
        name="attention_cache" if with_cache else "attention_ctx",
    )(*args)


def _post_kernel(yfc_ref, yfl_ref, ybc_ref, ybl_ref, bonus_ref, g_ref, yac_ref, yal_ref, gr_ref, ga_ref,
                 xc_ref, xl_ref,
                 mod_ref, lnw_ref, lnb_ref, wur_ref, wua_ref, wo_ref, n2_ref, wr_ref, br_ref, ones_ref,
                 x1_o, h2_o, idx_o, gate_o, *, ctx_blocks):
    ones = ones_ref[...]
    is_ctx = pl.program_id(0) < ctx_blocks
    y_a = jnp.where(is_ctx, yac_ref[...], yal_ref[...])
    x = jnp.where(is_ctx, xc_ref[...], xl_ref[...])
    y = jnp.where(is_ctx, yfc_ref[...] + ybc_ref[...], yfl_ref[...] + ybl_ref[...])
    mu = _head_sum(y, ones) * (1.0 / HEAD)
    yc = y - mu
    var = _head_sum(yc * yc, ones) * (1.0 / HEAD)
    yn = yc * lax.rsqrt(var + GN_EPS) * lnw_ref[...] + lnb_ref[...] + bonus_ref[...]
    y_r = (yn * g_ref[...]).astype(BF16)
    merged = (_sigmoid(gr_ref[...]) * _dot(y_r, wur_ref[...])
              + _sigmoid(ga_ref[...]) * _dot(y_a, wua_ref[...]))
    mix = _dot(merged.astype(BF16), wo_ref[...])
    mod = mod_ref[0]
    g1 = mod[:, 2 * D_MODEL:3 * D_MODEL]
    sh2 = mod[:, 3 * D_MODEL:4 * D_MODEL]
    sc2 = mod[:, 4 * D_MODEL:5 * D_MODEL]
    x1 = x + g1 * mix
    x1_o[...] = x1
    ms = jnp.mean(x1 * x1, axis=-1, keepdims=True)
    h2 = x1 * lax.rsqrt(ms + NORM_EPS) * n2_ref[...] * (1.0 + sc2) + sh2
    _store_token_tiles(h2_o, h2)
    hh, hl = _split2(h2)
    wh, wl = _split2(wr_ref[...])
    hi_terms = _dot(hh, jnp.concatenate([wh, wl], axis=1))
    logits = hi_terms[:, 0:LANES] + hi_terms[:, LANES:2 * LANES] + _dot(hl, wh) + br_ref[...]
    lane_i = lax.broadcasted_iota(jnp.int32, logits.shape, 1)
    lane = lane_i.astype(F32)
    neg = jnp.float32(-jnp.inf)
    cur = jnp.where(lane_i < N_EXPERTS, logits, neg)
    vals, idxs = [], []
    for _ in range(TOP_K):
        mx = jnp.max(cur, axis=-1, keepdims=True)
        ix = jnp.min(jnp.where(cur == mx, lane, float(LANES)), axis=-1, keepdims=True)
        vals.append(mx)
        idxs.append(ix)
        cur = jnp.where(lane == ix, neg, cur)
    es = [jnp.exp(val - vals[0]) for val in vals]
    den = es[0] + es[1] + es[2] + es[3]
    idx_out = jnp.zeros(logits.shape, jnp.int32)
    gate_out = jnp.zeros(logits.shape, F32)
    for j in range(TOP_K):
        idx_out = jnp.where(lane_i == j, idxs[j].astype(jnp.int32), idx_out)
        gate_out = jnp.where(lane_i == j, es[j] / den, gate_out)
    idx_o[...] = idx_out
    gate_o[...] = gate_out


def _post(y_f, y_b, bonus, g, ya_ctx, ya_lat, gate_r, gate_a, x_ctx, x_lat, mods3, mod_row, p):
    n = x_ctx.shape[0] + x_lat.shape[0]
    tm = TM_POST
    nbc = ya_ctx.shape[0] // tm
    row = lambda i: (i, 0)
    full = lambda *shape: pl.BlockSpec(shape, lambda i: (0,) * len(shape))
    tok = lambda width: pl.BlockSpec((tm, width), row)
    pair = lambda width: _token_specs(nbc, width, tm)
    return pl.pallas_call(
        functools.partial(_post_kernel, ctx_blocks=nbc),
        out_shape=[jax.ShapeDtypeStruct((n, D_MODEL), F32),
                   jax.ShapeDtypeStruct((n * TILE_ROWS, LANES), F32),
                   jax.ShapeDtypeStruct((n, LANES), jnp.int32),
                   jax.ShapeDtypeStruct((n, LANES), F32)],
        grid=(n // tm,),
        in_specs=pair(D_R) + pair(D_R) + [tok(D_R), tok(D_R)]
                 + pair(D_A) + [tok(D_MODEL), tok(D_MODEL)] + pair(D_MODEL) + [
                  pl.BlockSpec((1, 1, 6 * D_MODEL), lambda i: (mod_row(i), 0, 0)),
                  full(1, D_R), full(1, D_R), full(D_R, D_MODEL), full(D_A, D_MODEL),
                  full(D_MODEL, D_MODEL), full(1, D_MODEL), full(D_MODEL, LANES), full(1, LANES),
                  full(D_R, D_R)],
        out_specs=[tok(D_MODEL), pl.BlockSpec((tm * TILE_ROWS, LANES), row), tok(LANES), tok(LANES)],
        compiler_params=_cparams("arbitrary"),
        name="mixer_post",
    )(*y_f, *y_b, bonus, g, ya_ctx, ya_lat, gate_r, gate_a, x_ctx, x_lat, mods3,
      p["lnw"], p["lnb"], p["wur"], p["wua"], p["wo"], p["n2"], p["wr"], p["br"], p["ones"])


def _moe_kernel(blk_e_ref, blk_first_ref, next_e_ref, used_ref,
                src0_ref, src_next_ref, dst_prev_ref, h_hbm, w1_hbm, b1_ref, w2_hbm, b2_ref,
                out_hbm, w1f, w2f, w1b, w2b, xs, ys, sem_g, sem_s, sem_t, sem_w, *, n_rows):
    b = pl.program_id(0)
    used = used_ref[0]
    R = MOE_ROWS
    T = TILE_ROWS

    def tile(ref, i):
        return ref.at[pl.ds(pl.multiple_of(i * T, T), T)]

    def gather(table_ref, buf):
        for i in range(R):
            pltpu.make_async_copy(tile(h_hbm, table_ref[0, 0, i]), tile(xs.at[buf], i),
                                  sem_g.at[buf]).start(priority=i % 2)

    def scatter_prev(buf):
        for i in range(R):
            pltpu.make_async_copy(tile(ys.at[buf], i), tile(out_hbm, dst_prev_ref[0, 0, i]),
                                  sem_s.at[buf]).start(priority=i % 2)

    def fetch_weights(e):
        pltpu.make_async_copy(w1_hbm.at[e], w1f, sem_w.at[0]).start()
        pltpu.make_async_copy(w2_hbm.at[e], w2f, sem_w.at[1]).start()

    @pl.when(b == 0)
    def _():
        ys[...] = jnp.zeros_like(ys)
        for s in range(2):
            band = pltpu.make_async_copy(ys.at[s], out_hbm.at[pl.ds((n_rows + s * R) * T, R * T)], sem_t)
            band.start()
            band.wait()
        gather(src0_ref, 0)
        fetch_weights(blk_e_ref[0])

    @pl.when(jnp.logical_and(blk_first_ref[b] == 1, b < used))
    def _():
        pltpu.make_async_copy(w1_hbm.at[0], w1f, sem_w.at[0]).wait()
        pltpu.make_async_copy(w2_hbm.at[0], w2f, sem_w.at[1]).wait()
        w1b[...] = w1f[...].astype(BF16)
        w2b[...] = w2f[...].astype(BF16)

        @pl.when(next_e_ref[b] >= 0)
        def _():
            fetch_weights(next_e_ref[b])

    def step(p):
        @pl.when(b <= used)
        def _():
            pltpu.make_async_copy(h_hbm.at[pl.ds(0, R * T)], xs.at[p], sem_g.at[p]).wait()

        @pl.when(jnp.logical_and(b >= 1, b <= used + 1))
        def _():
            pltpu.make_async_copy(ys.at[p], out_hbm.at[pl.ds(0, R * T)], sem_s.at[p]).wait()

        @pl.when(b < used)
        def _():
            gather(src_next_ref, 1 - p)
            scatter_prev(1 - p)
            hb = _dot(_load_token_tiles(xs.at[p]).astype(BF16), w1b[...]) + b1_ref[0]
            glu = jnp.minimum(hb[:, 0:D_FF], SWIGLU_LIMIT)
            lin = jnp.clip(hb[:, D_FF:2 * D_FF], -SWIGLU_LIMIT, SWIGLU_LIMIT)
            act = glu * _sigmoid(SWIGLU_ALPHA * glu) * (lin + 1.0)
            _store_token_tiles(ys.at[p], _dot(act.astype(BF16), w2b[...]) + b2_ref[0])

        @pl.when(b == used)
        def _():
            scatter_prev(1 - p)

    for p in range(2):
        pl.when(b % 2 == p)(functools.partial(step, p))


def _moe(h2, src, dst_prev, blk_e, blk_first, next_e, used, w1, b1, w2, b2):
    d = D_MODEL
    n = h2.shape[0] // TILE_ROWS
    nb = src.shape[0]
    steps = nb + 2
    cur = lambda b, *_: (jnp.minimum(b, nb), 0, 0)
    nxt = lambda b, *_: (jnp.minimum(b + 1, nb - 1), 0, 0)
    table = lambda index_map: pl.BlockSpec((1, 1, MOE_ROWS), index_map, memory_space=pltpu.SMEM)
    expert = lambda b, blk_e, *_: (blk_e[jnp.minimum(b, nb - 1)], 0, 0)
    grid_spec = pltpu.PrefetchScalarGridSpec(
        num_scalar_prefetch=4,
        grid=(steps,),
        in_specs=[table(lambda b, *_: (0, 0, 0)), table(nxt), table(cur),
                  pl.BlockSpec(memory_space=pl.ANY),
                  pl.BlockSpec(memory_space=pl.ANY),
                  pl.BlockSpec((1, 1, 2 * D_FF), expert),
                  pl.BlockSpec(memory_space=pl.ANY),
                  pl.BlockSpec((1, 1, d), expert)],
        out_specs=pl.BlockSpec(memory_space=pl.ANY),
        scratch_shapes=[pltpu.VMEM((d, 2 * D_FF), F32),
                        pltpu.VMEM((D_FF, d), F32),
                        pltpu.VMEM((d, 2 * D_FF), BF16),
                        pltpu.VMEM((D_FF, d), BF16),
                        pltpu.VMEM((2, MOE_ROWS * TILE_ROWS, LANES), F32),
                        pltpu.VMEM((2, MOE_ROWS * TILE_ROWS, LANES), F32),
                        pltpu.SemaphoreType.DMA((2,)),
                        pltpu.SemaphoreType.DMA((2,)),
                        pltpu.SemaphoreType.DMA(()),
                        pltpu.SemaphoreType.DMA((2,))])
    tab3 = lambda t: t.reshape(-1, 1, MOE_ROWS)
    pad = lambda t: jnp.concatenate([t, jnp.zeros((steps - nb,), jnp.int32)])
    return pl.pallas_call(
        functools.partial(_moe_kernel, n_rows=n * TOP_K),
        out_shape=jax.ShapeDtypeStruct(((n * TOP_K + 2 * MOE_ROWS) * TILE_ROWS, LANES), F32),
        grid_spec=grid_spec,
        compiler_params=pltpu.CompilerParams(dimension_semantics=("arbitrary",),
                                             vmem_limit_bytes=VMEM_LIMIT, has_side_effects=True),
        name="moe_experts",
    )(blk_e, pad(blk_first), next_e, used, tab3(src), tab3(src), tab3(dst_prev), h2, w1,
      b1.reshape(N_EXPERTS, 1, 2 * D_FF), w2, b2.reshape(N_EXPERTS, 1, d))


def _route(top_idx, n):
    nk = n * TOP_K
    cap = nk + N_EXPERTS * MOE_ROWS
    nb = cap // MOE_ROWS
    flat_e = top_idx.reshape(nk)
    order = jnp.argsort(flat_e, stable=True).astype(jnp.int32)
    counts = jnp.sum((flat_e[:, None] == jnp.arange(N_EXPERTS, dtype=jnp.int32)[None, :])
                     .astype(jnp.int32), axis=0)
    start = jnp.cumsum(counts) - counts
    n_blk = (counts + MOE_ROWS - 1) // MOE_ROWS
    blk_end = jnp.cumsum(n_blk)
    first_blk = blk_end - n_blk
    experts = jnp.arange(N_EXPERTS, dtype=jnp.int32)
    blk = jnp.arange(nb, dtype=jnp.int32)
    blk_e = jnp.minimum(jnp.sum((blk[:, None] >= blk_end[None, :]).astype(jnp.int32), axis=1),
                        N_EXPERTS - 1)
    onehot = (blk_e[:, None] == experts[None, :]).astype(jnp.int32)
    pick = lambda t: jnp.sum(onehot * t[None, :], axis=1)
    blk_first_of_e, blk_count, blk_start = pick(first_blk), pick(counts), pick(start)
    blk_first = (blk == blk_first_of_e).astype(jnp.int32)
    blk_after = pick(blk_end)
    next_e = jnp.where(blk_after < blk_end[-1], blk_e[jnp.minimum(blk_after, nb - 1)], -1).astype(jnp.int32)
    row = jnp.arange(MOE_ROWS, dtype=jnp.int32)[None, :]
    within = (blk - blk_first_of_e)[:, None] * MOE_ROWS + row
    valid = jnp.logical_and(within < blk_count[:, None], (blk < blk_end[-1])[:, None])
    j = order[jnp.clip(blk_start[:, None] + within, 0, nk - 1)]
    src = jnp.where(valid, j // TOP_K, 0).astype(jnp.int32)
    scratch_row = nk + (blk % 2)[:, None] * MOE_ROWS + row
    dst = jnp.where(valid, (j % TOP_K) * n + j // TOP_K, scratch_row).astype(jnp.int32)
    dst_prev = jnp.concatenate([nk + MOE_ROWS + row, dst], axis=0)
    return (src, dst_prev, blk_e.astype(jnp.int32), blk_first, next_e,
            blk_end[-1:].astype(jnp.int32))


def _final_kernel(x1_ref, y0_ref, y1_ref, y2_ref, y3_ref, gate_ref, mod_ref, nf_ref, oc_ref, ol_ref,
                  *, ctx_blocks):
    gates = gate_ref[...]
    ffn = gates[:, 0:1] * _load_token_tiles(y0_ref)
    for j, ref in enumerate((y1_ref, y2_ref, y3_ref), start=1):
        ffn = ffn + gates[:, j:j + 1] * _load_token_tiles(ref)
    g2 = mod_ref[0][:, 5 * D_MODEL:6 * D_MODEL]
    x2 = x1_ref[...] + g2 * ffn
    ms = jnp.mean(x2 * x2, axis=-1, keepdims=True)
    y = x2 * lax.rsqrt(ms + NORM_EPS) * nf_ref[...]
    i = pl.program_id(0)

    @pl.when(i < ctx_blocks)
    def _():
        oc_ref[...] = y

    @pl.when(i >= ctx_blocks)
    def _():
        ol_ref[...] = y


def _final(x1, ys, gates, mods3, mod_row, norm_f, nc):
    n, d = x1.shape
    tm = TM_POST
    nb = n // tm
    nbc = nc // tm
    row = lambda i: (i, 0)
    return pl.pallas_call(
        functools.partial(_final_kernel, ctx_blocks=nbc),
        out_shape=[jax.ShapeDtypeStruct((nc, d), F32), jax.ShapeDtypeStruct((n - nc, d), F32)],
        grid=(nb,),
        in_specs=[pl.BlockSpec((tm, d), row)]
                 + [pl.BlockSpec((tm * TILE_ROWS, LANES), functools.partial(lambda j, i: (j * nb + i, 0), j))
                    for j in range(TOP_K)]
                 + [pl.BlockSpec((tm, LANES), row),
                    pl.BlockSpec((1, 1, 6 * d), lambda i: (mod_row(i), 0, 0)),
                    pl.BlockSpec((1, d), lambda i: (0, 0))],
        out_specs=_token_specs(nbc, d, tm),
        compiler_params=_cparams("arbitrary"),
        name="combine_final",
    )(x1, ys, ys, ys, ys, gates, mods3, norm_f.reshape(1, d))


def _rope_tables(t_ctx, t_lat):
    rows = np.arange(t_lat) // GRID_W
    cols = np.arange(t_lat) % GRID_W
    inv = ROPE_THETA ** (-np.arange(0, AXIS_DIM, 2, dtype=np.float32) / AXIS_DIM)
    inv = jnp.asarray(inv, F32)
    ang_r = jnp.asarray(rows, F32)[:, None] * inv[None, :]
    ang_c = jnp.asarray(cols, F32)[:, None] * inv[None, :]
    cos = jnp.concatenate([jnp.cos(ang_r)] * 2 + [jnp.cos(ang_c)] * 2, axis=1)
    sin = jnp.concatenate([-jnp.sin(ang_r), jnp.sin(ang_r), -jnp.sin(ang_c), jnp.sin(ang_c)], axis=1)
    cos = jnp.concatenate([jnp.ones((t_ctx, HEAD), F32), cos], axis=0)
    sin = jnp.concatenate([jnp.zeros((t_ctx, HEAD), F32), sin], axis=0)
    return jnp.concatenate([cos, cos], axis=1), jnp.concatenate([sin, sin], axis=1)


def kernel(x_prompt, x_sample, cache_k, cache_v, state_rwkv_fwd, state_rwkv_bwd, c, c_ctx, w_ada, b_ada, norm1, norm2, w_in, rw_w0, rw_w2, rw_a0, rw_a2, rw_g2, rw_kk, rw_ka, rw_rk, rw_ln_w, rw_ln_b, q_norm, k_norm, w_up_r, w_up_a, w_out, w_router, b_router, w_moe_in, b_moe_in, w_moe_out, b_moe_out, norm_f):
    depth = w_in.shape[0]
    assert depth == 1, "single trunk layer"
    bc, tc, d = x_prompt.shape
    bl, tl, _ = x_sample.shape
    nc, nl = bc * tc, bl * tl
    n = nc + nl
    assert d == D_MODEL and TM % tc == 0 and tl % TM == 0 and nc % tl == 0
    assert all(nc % tm == 0 and tl % tm == 0 for tm in (TM_PROJ, TM_POST))
    l = 0

    x_ctx, x_lat = x_prompt.reshape(nc, d), x_sample.reshape(nl, d)
    cond = jnp.concatenate([c_ctx[None, :], c, jnp.zeros((16 - 1 - bl, d), F32)], axis=0)
    mods3 = _adaln(cond, w_ada[l], b_ada[l]).reshape(16, 1, 6 * d)
    nbc = nc // TM
    per_seq = tl // TM
    tab_row = lambda i: jnp.where(i < nbc, 0, 1 + (i - nbc) % per_seq)
    mod_row = lambda i: jnp.where(i < nc // TM_POST, 0, 1 + (i - nc // TM_POST) // (tl // TM_POST))

    z = dict(zip([name for name, _ in IN_SPLITS],
                 _proj_in(x_ctx, x_lat, mods3, norm1[l], w_in[l].astype(BF16), tl)))

    ones_bd = jnp.asarray(np.kron(np.eye(H_R), np.ones((HEAD, HEAD))), BF16)
    pad_lo = lambda w: jnp.concatenate([w, jnp.zeros_like(w)], axis=0)
    pad_hi = lambda w: jnp.concatenate([jnp.zeros_like(w), w], axis=0)
    prep_p = dict(
        w0=rw_w0[l], a0=rw_a0[l],
        w2=jnp.stack([pad_lo(rw_w2[l, 0]), pad_hi(rw_w2[l, 1])]).astype(BF16),
        a2=jnp.stack([pad_lo(rw_a2[l, 0]), pad_hi(rw_a2[l, 1])]).astype(BF16),
        g2=rw_g2[l].astype(BF16),
        kkw=rw_kk[l].reshape(1, D_R), kaw=rw_ka[l].reshape(1, D_R), rk=rw_rk[l].reshape(1, D_R),
        qn=jnp.tile(q_norm[l], N_HEADS).reshape(1, D_A), kn=jnp.tile(k_norm[l], N_KV).reshape(1, D_KV),
        ones=ones_bd,
        tri=jnp.asarray(np.stack([np.kron(np.eye(TM // CHUNK), np.tril(np.ones((CHUNK, CHUNK)))),
                                  np.kron(np.eye(TM // CHUNK), np.triu(np.ones((CHUNK, CHUNK))))]), BF16),
        chunk_ones=jnp.asarray(np.kron(np.eye(TM // CHUNK), np.ones((CHUNK, CHUNK))), BF16))
    cos_tab, sin_tab = _rope_tables(TM, tl)
    sc, gt, vb, bonus, g, qr, kn, kr = _prep(z, cos_tab, sin_tab, tab_row, prep_p)

    yf_c, yb_c, s_fin = _rwkv_scan(sc, gt, vb, jnp.zeros((2, bc, H_R, HEAD, HEAD), F32), 0, bc, tc)
    yf_l, yb_l, _ = _rwkv_scan(sc, gt, vb, jnp.stack([state_rwkv_fwd[:, l], state_rwkv_bwd[:, l]]),
                               nc, bl, tl)
    y_f = (yf_c.reshape(nc, D_R), yf_l.reshape(nl, D_R))
    y_b = (yb_c.reshape(nc, D_R), yb_l.reshape(nl, D_R))

    ya_ctx = _attention(qr, kr, z["va"], 0, bc, tc)
    ya_lat = _attention(qr, kr, z["va"], nc, bl, tl,
                        cache_k[:, l].reshape(bl, -1, D_KV), cache_v[:, l].reshape(bl, -1, D_KV))

    post_p = dict(
        lnw=rw_ln_w[l].reshape(1, D_R), lnb=rw_ln_b[l].reshape(1, D_R),
        wur=w_up_r[l].astype(BF16), wua=w_up_a[l].astype(BF16), wo=w_out[l].astype(BF16),
        n2=norm2[l].reshape(1, d),
        wr=jnp.pad(w_router[l], ((0, 0), (0, LANES - N_EXPERTS))),
        br=jnp.pad(b_router[l], (0, LANES - N_EXPERTS)).reshape(1, LANES),
        ones=ones_bd)
    x1, h2, top_idx, gates = _post(y_f, y_b, bonus, g, ya_ctx, ya_lat, z["gate_r"], z["gate_a"],
                                   x_ctx, x_lat, mods3, mod_row, post_p)

    src, dst, blk_e, blk_first, next_e, used = _route(top_idx[:, :TOP_K], n)
    ys = _moe(h2, src, dst, blk_e, blk_first, next_e, used,
              w_moe_in[l], b_moe_in[l], w_moe_out[l], b_moe_out[l])
    y_ctx, y_lat = _final(x1, ys, gates, mods3, mod_row, norm_f, nc)

    y_prompt = y_ctx.reshape(bc, tc, d)
    y_sample = y_lat.reshape(bl, tl, d)
    new_cache_k = kn[:nc].reshape(bc, 1, tc, N_KV, HEAD)
    new_cache_v = z["va"][:nc].reshape(bc, 1, tc, N_KV, HEAD)
    new_state_fwd = s_fin[0][:, None]
    new_state_bwd = s_fin[1][:, None]
    return (y_prompt, y_sample, new_cache_k, new_cache_v, new_state_fwd, new_state_bwd)
```

```python
import functools

import numpy as np
import jax
import jax.numpy as jnp
from jax import lax
from jax.experimental import pallas as pl
from jax.experimental.pallas import tpu as pltpu

F32 = jnp.float32
BF16 = jnp.bfloat16

D_MODEL = 1024
GRID_W = 64
NORM_EPS = 1e-6
HEAD = 64
H_R = 8
D_R = H_R * HEAD
DECAY_RANK = 64
AAA_RANK = 64
GATE_RANK = 128
GN_EPS = 64e-5
N_HEADS = 8
N_KV = 2
D_A = N_HEADS * HEAD
D_KV = N_KV * HEAD
AXIS_DIM = HEAD // 2
ROPE_THETA = 10000.0
N_EXPERTS = 32
TOP_K = 4
D_FF = D_MODEL
SWIGLU_ALPHA = 1.702
SWIGLU_LIMIT = 7.0
LORA_COLS = 2 * DECAY_RANK + 2 * AAA_RANK + GATE_RANK
IN_SPLITS = (("r", D_R), ("k", D_R), ("v", D_R), ("lora", LORA_COLS), ("q", D_A),
             ("ka", D_KV), ("va", D_KV), ("gate_r", D_MODEL), ("gate_a", D_MODEL))

LANES = 128
TM = 256
TM_POST = 512
TM_PROJ = 512
ATTN_ROWS = 512
CHUNK = 64
SCAN_OPS = 6
SCAN_GROUP = 4
MOE_ROWS = 256
VMEM_LIMIT = 56 * 1024 * 1024


def _cparams(*sem):
    return pltpu.CompilerParams(dimension_semantics=sem, vmem_limit_bytes=VMEM_LIMIT)


def _dot(a, b):
    return jnp.dot(a, b, preferred_element_type=F32)


def _dot_nt(a, b):
    return lax.dot_general(a, b, (((1,), (1,)), ((), ())), preferred_element_type=F32)


def _dot_tn(a, b):
    return lax.dot_general(a, b, (((0,), (0,)), ((), ())), preferred_element_type=F32)


def _split2(x):
    hi = x.astype(BF16)
    lo = (x - hi.astype(F32)).astype(BF16)
    return hi, lo


def _head_sum(x, ones_bd):
    return _dot(x.astype(BF16), ones_bd)


def _sigmoid(x):
    return 0.5 * jnp.tanh(0.5 * x) + 0.5


TILE_ROWS = D_MODEL // LANES


def _store_token_tiles(ref, x):
    rows = x.shape[0]
    for j in range(TILE_ROWS):
        ref[pl.ds(j, rows, stride=TILE_ROWS), :] = x[:, j * LANES:(j + 1) * LANES]


def _load_token_tiles(ref):
    rows = ref.shape[0] // TILE_ROWS
    return jnp.concatenate([ref[pl.ds(j, rows, stride=TILE_ROWS), :] for j in range(TILE_ROWS)], axis=1)


def _adaln_kernel(c_ref, w_ref, b_ref, o_ref):
    c = c_ref[...]
    s = (c * _sigmoid(c)).astype(BF16)
    o_ref[...] = _dot(s, w_ref[...].astype(BF16)) + b_ref[...]


def _adaln(cond, w_ada, b_ada):
    rows, d = cond.shape
    cols = w_ada.shape[1]
    tn = 512
    return pl.pallas_call(
        _adaln_kernel,
        out_shape=jax.ShapeDtypeStruct((rows, cols), F32),
        grid=(cols // tn,),
        in_specs=[pl.BlockSpec((rows, d), lambda j: (0, 0)),
                  pl.BlockSpec((d, tn), lambda j: (0, j)),
                  pl.BlockSpec((1, tn), lambda j: (0, j))],
        out_specs=pl.BlockSpec((rows, tn), lambda j: (0, j)),
        compiler_params=_cparams("arbitrary"),
        name="adaln",
    )(cond, w_ada, b_ada.reshape(1, cols))


def _token_specs(nbc, width, tm=TM):
    return [pl.BlockSpec((tm, width), lambda i: (jnp.minimum(i, nbc - 1), 0)),
            pl.BlockSpec((tm, width), lambda i: (jnp.maximum(i - nbc, 0), 0))]


def _proj_in_kernel(xc_ref, xl_ref, mod_ref, n1_ref, w_ref, *out_refs, ctx_blocks):
    x = jnp.where(pl.program_id(0) < ctx_blocks, xc_ref[...], xl_ref[...])
    ms = jnp.mean(x * x, axis=-1, keepdims=True)
    hn = x * lax.rsqrt(ms + NORM_EPS) * n1_ref[...]
    mod = mod_ref[0]
    sh1 = mod[:, 0:D_MODEL]
    sc1 = mod[:, D_MODEL:2 * D_MODEL]
    h = (hn * (1.0 + sc1) + sh1).astype(BF16)
    c0 = 0
    for ref, (_, width) in zip(out_refs, IN_SPLITS):
        ref[...] = _dot(h, w_ref[:, c0:c0 + width])
        c0 += width


def _proj_in(x_ctx, x_lat, mods3, norm1, w_in_bf, t_lat):
    d = x_ctx.shape[1]
    n = x_ctx.shape[0] + x_lat.shape[0]
    tm = TM_PROJ
    nbc = x_ctx.shape[0] // tm
    per_seq = t_lat // tm
    cols = w_in_bf.shape[1]
    row = lambda i: (i, 0)
    mod_row = lambda i: jnp.where(i < nbc, 0, 1 + (i - nbc) // per_seq)
    return pl.pallas_call(
        functools.partial(_proj_in_kernel, ctx_blocks=nbc),
        out_shape=[jax.ShapeDtypeStruct((n, width), F32) for _, width in IN_SPLITS],
        grid=(n // tm,),
        in_specs=_token_specs(nbc, d, tm) + [
                  pl.BlockSpec((1, 1, 6 * d), lambda i: (mod_row(i), 0, 0)),
                  pl.BlockSpec((1, d), lambda i: (0, 0)),
                  pl.BlockSpec((d, cols), lambda i: (0, 0))],
        out_specs=[pl.BlockSpec((tm, width), row) for _, width in IN_SPLITS],
        compiler_params=_cparams("arbitrary"),
        name="proj_in",
    )(x_ctx, x_lat, mods3, norm1.reshape(1, d), w_in_bf)


def _prep_kernel(r_ref, k_ref, v_ref, lora_ref, q_ref, ka_ref, cos_ref, sin_ref,
                 w0_ref, w2_ref, a0_ref, a2_ref, g2_ref, kkw_ref, kaw_ref, rk_ref,
                 qn_ref, kn_ref, ones_ref, tri_ref, chunk_ones_ref,
                 sc_o, gt_o, vb_o, bonus_o, g_o, qr_o, kn_o, kr_o):
    ones = ones_ref[...]
    chunk_ones = chunk_ones_ref[...]
    r = r_ref[...]
    k = k_ref[...]
    v = v_ref[...]
    vb_o[...] = v.astype(BF16)
    lora = lora_ref[...]
    kk = k * kkw_ref[...]
    kk = kk * lax.rsqrt(_head_sum(kk * kk, ones) + 1e-12)
    th = jnp.tanh(lora[:, 0:LANES]).astype(BF16)
    al = lora[:, LANES:2 * LANES].astype(BF16)
    kd_sum = jnp.zeros_like(k)
    for d in range(2):
        u = w0_ref[d:d + 1, :] + _dot(th, w2_ref[d])
        lw = -float(np.exp(-0.5)) * _sigmoid(u)
        a = _sigmoid(a0_ref[d:d + 1, :] + _dot(al, a2_ref[d]))
        kd = k * (1.0 + (a - 1.0) * kaw_ref[...])
        kd_sum = kd_sum + kd
        b = kk * a
        parts = _split2(lw)
        tri = tri_ref[d]
        L = _dot(tri, parts[0]) + _dot(tri, parts[1])
        Ltot = _dot(chunk_ones, parts[0]) + _dot(chunk_ones, parts[1])
        e_inv = jnp.exp(-L)
        e_rest = jnp.exp(Ltot - L)
        scan_ops = (kk * jnp.exp(L - lw), r * jnp.exp(L), b * e_inv, kd * e_inv, b * e_rest, kd * e_rest)
        for j, op in enumerate(scan_ops):
            sc_o[d, :, j * D_R:(j + 1) * D_R] = op.astype(BF16)
        g_tot = jnp.exp(Ltot)
        for j in range(TM // CHUNK):
            gt_o[d, j] = g_tot[j * CHUNK:j * CHUNK + 1, :]
    bonus_o[...] = _head_sum(r * kd_sum * rk_ref[...], ones) * v
    g_o[...] = _dot(_sigmoid(lora[:, 2 * LANES:3 * LANES]).astype(BF16), g2_ref[...])
    q = q_ref[...]
    qn = q * lax.rsqrt(_head_sum(q * q, ones) * (1.0 / HEAD) + NORM_EPS) * qn_ref[...]
    ka = ka_ref[...]
    ones_kv = ones[0:D_KV, 0:D_KV]
    kn = ka * lax.rsqrt(_head_sum(ka * ka, ones_kv) * (1.0 / HEAD) + NORM_EPS) * kn_ref[...]
    kn_o[...] = kn
    cos = cos_ref[...]
    sin = sin_ref[...]
    half = AXIS_DIM // 2

    def rope(t, cos_t, sin_t):
        width = t.shape[1]
        lane = lax.broadcasted_iota(jnp.int32, t.shape, 1)
        first = (lane % AXIS_DIM) < half
        swapped = jnp.where(first, pltpu.roll(t, width - half, 1), pltpu.roll(t, half, 1))
        return t * cos_t + swapped * sin_t

    cos4 = jnp.concatenate([cos] * (D_A // D_KV), axis=1)
    sin4 = jnp.concatenate([sin] * (D_A // D_KV), axis=1)
    qr_o[...] = (rope(qn, cos4, sin4) * (HEAD ** -0.5)).astype(BF16)
    kr_o[...] = rope(kn, cos, sin).astype(BF16)


def _prep(z, cos_tab, sin_tab, tab_row, p):
    n = z["r"].shape[0]
    row = lambda i: (i, 0)
    full = lambda *shape: pl.BlockSpec(shape, lambda i: (0,) * len(shape))
    tok = lambda width: pl.BlockSpec((TM, width), row)
    tab = pl.BlockSpec((TM, D_KV), lambda i: (tab_row(i), 0))
    cpb = TM // CHUNK
    return pl.pallas_call(
        _prep_kernel,
        out_shape=[jax.ShapeDtypeStruct((2, n, SCAN_OPS * D_R), BF16),
                   jax.ShapeDtypeStruct((2, n // CHUNK, 1, D_R), F32),
                   jax.ShapeDtypeStruct((n, D_R), BF16),
                   jax.ShapeDtypeStruct((n, D_R), F32),
                   jax.ShapeDtypeStruct((n, D_R), F32),
                   jax.ShapeDtypeStruct((n, D_A), BF16),
                   jax.ShapeDtypeStruct((n, D_KV), F32),
                   jax.ShapeDtypeStruct((n, D_KV), BF16)],
        grid=(n // TM,),
        in_specs=[tok(D_R), tok(D_R), tok(D_R), tok(LORA_COLS), tok(D_A), tok(D_KV), tab, tab,
                  full(2, D_R), full(2, LANES, D_R), full(2, D_R), full(2, LANES, D_R),
                  full(GATE_RANK, D_R), full(1, D_R), full(1, D_R), full(1, D_R),
                  full(1, D_A), full(1, D_KV), full(D_R, D_R), full(2, TM, TM), full(TM, TM)],
        out_specs=[pl.BlockSpec((2, TM, SCAN_OPS * D_R), lambda i: (0, i, 0)),
                   pl.BlockSpec((2, cpb, 1, D_R), lambda i: (0, i, 0, 0)),
                   tok(D_R), tok(D_R), tok(D_R), tok(D_A), tok(D_KV), tok(D_KV)],
        compiler_params=_cparams("arbitrary"),
        name="mixer_prep",
    )(z["r"], z["k"], z["v"], z["lora"], z["q"], z["ka"], cos_tab, sin_tab,
      p["w0"], p["w2"], p["a0"], p["a2"], p["g2"], p["kkw"], p["kaw"], p["rk"],
      p["qn"], p["kn"], p["ones"], p["tri"], p["chunk_ones"])


def _scan_kernel(scf_ref, scb_ref, vf_ref, vb_ref, gf_ref, gb_ref, s0_ref,
                 yf_ref, yb_ref, sfin_ref, state):
    c = pl.program_id(1)
    C = CHUNK
    pairs = D_R // LANES
    chains = [(g, d, p) for g in range(SCAN_GROUP) for d in range(2) for p in range(pairs)]

    lane = lax.broadcasted_iota(jnp.int32, (1, LANES), 1)
    low = lane < HEAD

    @pl.when(c == 0)
    def _():
        zero = jnp.zeros((HEAD, HEAD), F32)
        for i, (g, d, p) in enumerate(chains):
            top = jnp.concatenate([s0_ref[d, g, 2 * p], zero], axis=1)
            bot = jnp.concatenate([zero, s0_ref[d, g, 2 * p + 1]], axis=1)
            state[i] = jnp.concatenate([top, bot], axis=0)

    W = 2 * C
    t2 = lax.broadcasted_iota(jnp.int32, (C, W), 0)
    s2 = lax.broadcasted_iota(jnp.int32, (C, W), 1) % C
    strict = (s2 < t2, s2 > t2)
    incl = (s2 <= t2, s2 >= t2)
    eye2 = jnp.where(s2 == t2, 1.0, 0.0)
    first_mat = lax.broadcasted_iota(jnp.int32, (1, W), 1) < C
    r2 = lax.broadcasted_iota(jnp.int32, (LANES, LANES), 0)
    c2 = lax.broadcasted_iota(jnp.int32, (LANES, LANES), 1)
    same_head = (r2 // HEAD) == (c2 // HEAD)

    sc_refs = (scf_ref, scb_ref)
    v_refs = (vf_ref, vb_ref)
    g_refs = (gf_ref, gb_ref)
    y_refs = (yf_ref, yb_ref)

    def operand(chain, j):
        g, d, p = chain
        c0 = j * D_R + p * LANES
        return sc_refs[d][0, g, :, c0:c0 + LANES]

    def stack(x):
        first = low if x.shape[1] == LANES else first_mat
        zero = jnp.zeros_like(x)
        return jnp.concatenate([jnp.where(first, x, zero), jnp.where(first, zero, x)], axis=0)

    Qk = [operand(ch, 0) for ch in chains]
    Qr = [operand(ch, 1) for ch in chains]
    v_p = [v_refs[d][g, :, p * LANES:(p + 1) * LANES] for g, d, p in chains]
    v_st = [stack(x) for x in v_p]
    Z0 = [state[i] for i in range(len(chains))]
    Z0b = [z.astype(BF16) for z in Z0]
    m = [_dot_nt(jnp.concatenate([Qk[i], Qr[i]], axis=0),
                 jnp.concatenate([stack(operand(ch, 2)), stack(operand(ch, 3))], axis=0))
         for i, ch in enumerate(chains)]
    A = [jnp.where(strict[d], m[i][0:C, 0:W], 0.0) for i, (g, d, p) in enumerate(chains)]
    Bm = [jnp.where(strict[d], m[i][0:C, W:2 * W], 0.0).astype(BF16)
          for i, (g, d, p) in enumerate(chains)]
    Pb = [jnp.where(incl[d], m[i][C:2 * C, 0:W], 0.0).astype(BF16)
          for i, (g, d, p) in enumerate(chains)]
    Pk = [jnp.where(incl[d], m[i][C:2 * C, W:2 * W], 0.0).astype(BF16)
          for i, (g, d, p) in enumerate(chains)]
    BV = [_dot(Bm[i], v_st[i]) for i in range(len(chains))]
    T = [eye2 - a for a in A]
    Ap = [_dot(a.astype(BF16), stack(a.astype(BF16))).astype(BF16) for a in A]
    n = 2
    while 2 * n < C:
        R = [_dot(jnp.concatenate([Ap[i], T[i].astype(BF16)], axis=0), stack(Ap[i]))
             for i in range(len(chains))]
        Ap = [x[0:C].astype(BF16) for x in R]
        T = [T[i] + R[i][C:2 * C] for i in range(len(chains))]
        n *= 2
    T = [T[i] + _dot(T[i].astype(BF16), stack(Ap[i])) for i in range(len(chains))]
    WU = [_dot(T[i].astype(BF16),
               jnp.concatenate([stack(Qk[i]), stack(BV[i].astype(BF16))], axis=1))
          for i in range(len(chains))]
    Wq = [x[:, 0:LANES].astype(BF16) for x in WU]
    Uv = [x[:, LANES:2 * LANES] for x in WU]
    PW = [_dot(Pb[i], jnp.concatenate([stack(Wq[i]), stack(Uv[i].astype(BF16))], axis=1))
          for i in range(len(chains))]
    PkV = [_dot(Pk[i], v_st[i]) for i in range(len(chains))]
    U = [_dot_nt(Wq[i], Z0b[i]) + Uv[i] for i in range(len(chains))]
    Yq = [(Qr[i].astype(F32) - PW[i][:, 0:LANES]).astype(BF16) for i in range(len(chains))]
    for i, (g, d, p) in enumerate(chains):
        y_refs[d][g, :, p * LANES:(p + 1) * LANES] = (_dot_nt(Yq[i], Z0b[i]) + PkV[i]
                                                      - PW[i][:, LANES:2 * LANES])
    for i, (g, d, p) in enumerate(chains):
        lhs = jnp.concatenate([v_p[i], (-U[i]).astype(BF16)], axis=0)
        rhs = jnp.concatenate([operand(chains[i], 5), operand(chains[i], 4)], axis=0)
        upd = _dot_tn(lhs, rhs)
        g_tot = g_refs[d][0, g, 0][:, p * LANES:(p + 1) * LANES]
        state[i] = Z0[i] * g_tot + jnp.where(same_head, upd, 0.0)

    @pl.when(c == pl.num_programs(1) - 1)
    def _():
        for i, (g, d, p) in enumerate(chains):
            z = state[i]
            sfin_ref[d, g, 2 * p] = z[0:HEAD, 0:HEAD]
            sfin_ref[d, g, 2 * p + 1] = z[HEAD:2 * HEAD, HEAD:2 * HEAD]


def _rwkv_scan(sc, gt, vb, s0, row0, batch, t_len):
    n = vb.shape[0]
    nch = t_len // CHUNK
    assert row0 % (t_len * SCAN_GROUP) == 0 and batch % SCAN_GROUP == 0
    seq0 = row0 // (t_len * SCAN_GROUP)
    width = SCAN_OPS * D_R
    sc4 = sc.reshape(2, n // t_len, t_len, width)
    gt5 = gt.reshape(2, n // t_len, nch, 1, D_R)
    v3 = vb.reshape(n // t_len, t_len, D_R)
    G = SCAN_GROUP
    fwd = lambda c: c
    bwd = lambda c: nch - 1 - c
    sc_spec = lambda d, at: pl.BlockSpec((1, G, CHUNK, width), lambda s, c: (d, seq0 + s, at(c), 0))
    v_spec = lambda at: pl.BlockSpec((G, CHUNK, D_R), lambda s, c: (seq0 + s, at(c), 0))
    gt_spec = lambda d, at: pl.BlockSpec((1, G, 1, 1, D_R), lambda s, c: (d, seq0 + s, at(c), 0, 0))
    y_spec = lambda at: pl.BlockSpec((G, CHUNK, D_R), lambda s, c: (s, at(c), 0))
    st = pl.BlockSpec((2, G, H_R, HEAD, HEAD), lambda s, c: (0, s, 0, 0, 0))
    return pl.pallas_call(
        _scan_kernel,
        out_shape=[jax.ShapeDtypeStruct((batch, t_len, D_R), F32),
                   jax.ShapeDtypeStruct((batch, t_len, D_R), F32),
                   jax.ShapeDtypeStruct((2, batch, H_R, HEAD, HEAD), F32)],
        grid=(batch // G, nch),
        in_specs=[sc_spec(0, fwd), sc_spec(1, bwd), v_spec(fwd), v_spec(bwd),
                  gt_spec(0, fwd), gt_spec(1, bwd), st],
        out_specs=[y_spec(fwd), y_spec(bwd), st],
        scratch_shapes=[pltpu.VMEM((G * 2 * D_R // LANES, LANES, LANES), F32)],
        compiler_params=_cparams("arbitrary", "arbitrary"),
        name="rwkv_scan",
    )(sc4, sc4, v3, v3, gt5, gt5, s0)


def _attn_kernel(*refs, tq, with_cache):
    if with_cache:
        q_ref, k_ref, v_ref, ck_ref, cv_ref, o_ref = refs
        kx = jnp.concatenate([k_ref[...], ck_ref[0].astype(BF16)], axis=0)
        vx = jnp.concatenate([v_ref[...], cv_ref[0]], axis=0).astype(BF16)
    else:
        q_ref, k_ref, v_ref, o_ref = refs
        kx = k_ref[...]
        vx = v_ref[...].astype(BF16)
    lane = lax.broadcasted_iota(jnp.int32, (1, D_KV), 1)
    low = lane < HEAD
    k_sw = pltpu.roll(kx.astype(F32), HEAD, 1).astype(BF16)
    v_sw = pltpu.roll(vx.astype(F32), HEAD, 1).astype(BF16)
    group = N_HEADS // N_KV
    for p in range(D_A // LANES):
        g = (2 * p) // group
        keep = low if g == 0 else jnp.logical_not(low)
        kd = jnp.where(keep, kx, k_sw)
        vd = jnp.where(keep, vx, v_sw)
        qp = q_ref[:, p * LANES:(p + 1) * LANES]
        zero = jnp.zeros_like(qp)
        qs = jnp.concatenate([jnp.where(low, qp, zero), jnp.where(low, zero, qp)], axis=0)
        s = _dot_nt(qs, kd)
        mx = jnp.max(s, axis=-1, keepdims=True)
        e = jnp.exp(s - mx)
        den = jnp.sum(e, axis=-1, keepdims=True)
        o = _dot(e.astype(BF16), vd) / den
        o_ref[:, p * LANES:(p + 1) * LANES] = jnp.where(low, o[0:tq], o[tq:2 * tq]).astype(BF16)


def _attention(qr, kr, va, row0, batch, t_len, cache_k=None, cache_v=None):
    tq = min(t_len, ATTN_ROWS)
    nq = t_len // tq
    qblk0 = row0 // tq
    sblk0 = row0 // t_len
    with_cache = cache_k is not None
    in_specs = [pl.BlockSpec((tq, D_A), lambda b, i: (qblk0 + b * nq + i, 0)),
                pl.BlockSpec((t_len, D_KV), lambda b, i: (sblk0 + b, 0)),
                pl.BlockSpec((t_len, D_KV), lambda b, i: (sblk0 + b, 0))]
    args = [qr, kr, va]
    if with_cache:
        past = cache_k.shape[1]
        in_specs += [pl.BlockSpec((1, past, D_KV), lambda b, i: (b, 0, 0))] * 2
        args += [cache_k, cache_v]
    return pl.pallas_call(
        functools.partial(_attn_kernel, tq=tq, with_cache=with_cache),
        out_shape=jax.ShapeDtypeStruct((batch * t_len, D_A), BF16),
        grid=(batch, nq),
        in_specs=in_specs,
        out_specs=pl.BlockSpec((tq, D_A), lambda b, i: (b * nq + i, 0)),
        compiler_params=_cparams("arbitrary", "arbitrary"),
        name="attention_cache" if with_cache else "attention_ctx",
    )(*args)


def _post_kernel(yfc_ref, yfl_ref, ybc_ref, ybl_ref, bonus_ref, g_ref, yac_ref, yal_ref, gr_ref, ga_ref,
                 xc_ref, xl_ref,
                 mod_ref, lnw_ref, lnb_ref, wur_ref, wua_ref, wo_ref, n2_ref, wr_ref, br_ref, ones_ref,
                 x1_o, h2_o, idx_o, gate_o, *, ctx_blocks):
    ones = ones_ref[...]
    is_ctx = pl.program_id(0) < ctx_blocks
    y_a = jnp.where(is_ctx, yac_ref[...], yal_ref[...])
    x = jnp.where(is_ctx, xc_ref[...], xl_ref[...])
    y = jnp.where(is_ctx, yfc_ref[...] + ybc_ref[...], yfl_ref[...] + ybl_ref[...])
    mu = _head_sum(y, ones) * (1.0 / HEAD)
    yc = y - mu
    var = _head_sum(yc * yc, ones) * (1.0 / HEAD)
    yn = yc * lax.rsqrt(var + GN_EPS) * lnw_ref[...] + lnb_ref[...] + bonus_ref[...]
    y_r = (yn * g_ref[...]).astype(BF16)
    merged = (_sigmoid(gr_ref[...]) * _dot(y_r, wur_ref[...])
              + _sigmoid(ga_ref[...]) * _dot(y_a, wua_ref[...]))
    mix = _dot(merged.astype(BF16), wo_ref[...])
    mod = mod_ref[0]
    g1 = mod[:, 2 * D_MODEL:3 * D_MODEL]
    sh2 = mod[:, 3 * D_MODEL:4 * D_MODEL]
    sc2 = mod[:, 4 * D_MODEL:5 * D_MODEL]
    x1 = x + g1 * mix
    x1_o[...] = x1
    ms = jnp.mean(x1 * x1, axis=-1, keepdims=True)
    h2 = x1 * lax.rsqrt(ms + NORM_EPS) * n2_ref[...] * (1.0 + sc2) + sh2
    _store_token_tiles(h2_o, h2)
    hh, hl = _split2(h2)
    wh, wl = _split2(wr_ref[...])
    hi_terms = _dot(hh, jnp.concatenate([wh, wl], axis=1))
    logits = hi_terms[:, 0:LANES] + hi_terms[:, LANES:2 * LANES] + _dot(hl, wh) + br_ref[...]
    lane_i = lax.broadcasted_iota(jnp.int32, logits.shape, 1)
    lane = lane_i.astype(F32)
    neg = jnp.float32(-jnp.inf)
    cur = jnp.where(lane_i < N_EXPERTS, logits, neg)
    vals, idxs = [], []
    for _ in range(TOP_K):
        mx = jnp.max(cur, axis=-1, keepdims=True)
        ix = jnp.min(jnp.where(cur == mx, lane, float(LANES)), axis=-1, keepdims=True)
        vals.append(mx)
        idxs.append(ix)
        cur = jnp.where(lane == ix, neg, cur)
    es = [jnp.exp(val - vals[0]) for val in vals]
    den = es[0] + es[1] + es[2] + es[3]
    idx_out = jnp.zeros(logits.shape, jnp.int32)
    gate_out = jnp.zeros(logits.shape, F32)
    for j in range(TOP_K):
        idx_out = jnp.where(lane_i == j, idxs[j].astype(jnp.int32), idx_out)
        gate_out = jnp.where(lane_i == j, es[j] / den, gate_out)
    idx_o[...] = idx_out
    gate_o[...] = gate_out


def _post(y_f, y_b, bonus, g, ya_ctx, ya_lat, gate_r, gate_a, x_ctx, x_lat, mods3, mod_row, p):
    n = x_ctx.shape[0] + x_lat.shape[0]
    tm = TM_POST
    nbc = ya_ctx.shape[0] // tm
    row = lambda i: (i, 0)
    full = lambda *shape: pl.BlockSpec(shape, lambda i: (0,) * len(shape))
    tok = lambda width: pl.BlockSpec((tm, width), row)
    pair = lambda width: _token_specs(nbc, width, tm)
    return pl.pallas_call(
        functools.partial(_post_kernel, ctx_blocks=nbc),
        out_shape=[jax.ShapeDtypeStruct((n, D_MODEL), F32),
                   jax.ShapeDtypeStruct((n * TILE_ROWS, LANES), F32),
                   jax.ShapeDtypeStruct((n, LANES), jnp.int32),
                   jax.ShapeDtypeStruct((n, LANES), F32)],
        grid=(n // tm,),
        in_specs=pair(D_R) + pair(D_R) + [tok(D_R), tok(D_R)]
                 + pair(D_A) + [tok(D_MODEL), tok(D_MODEL)] + pair(D_MODEL) + [
                  pl.BlockSpec((1, 1, 6 * D_MODEL), lambda i: (mod_row(i), 0, 0)),
                  full(1, D_R), full(1, D_R), full(D_R, D_MODEL), full(D_A, D_MODEL),
                  full(D_MODEL, D_MODEL), full(1, D_MODEL), full(D_MODEL, LANES), full(1, LANES),
                  full(D_R, D_R)],
        out_specs=[tok(D_MODEL), pl.BlockSpec((tm * TILE_ROWS, LANES), row), tok(LANES), tok(LANES)],
        compiler_params=_cparams("arbitrary"),
        name="mixer_post",
    )(*y_f, *y_b, bonus, g, ya_ctx, ya_lat, gate_r, gate_a, x_ctx, x_lat, mods3,
      p["lnw"], p["lnb"], p["wur"], p["wua"], p["wo"], p["n2"], p["wr"], p["br"], p["ones"])


def _moe_kernel(blk_e_ref, blk_first_ref, next_e_ref, used_ref,
                src0_ref, src_next_ref, dst_prev_ref, h_hbm, w1_hbm, b1_ref, w2_hbm, b2_ref,
                out_hbm, w1f, w2f, w1b, w2b, xs, ys, sem_g, sem_s, sem_t, sem_w, *, n_rows):
    b = pl.program_id(0)
    used = used_ref[0]
    R = MOE_ROWS
    T = TILE_ROWS

    def tile(ref, i):
        return ref.at[pl.ds(pl.multiple_of(i * T, T), T)]

    def gather(table_ref, buf):
        for i in range(R):
            pltpu.make_async_copy(tile(h_hbm, table_ref[0, 0, i]), tile(xs.at[buf], i),
                                  sem_g.at[buf]).start(priority=i % 2)

    def scatter_prev(buf):
        for i in range(R):
            pltpu.make_async_copy(tile(ys.at[buf], i), tile(out_hbm, dst_prev_ref[0, 0, i]),
                                  sem_s.at[buf]).start(priority=i % 2)

    def fetch_weights(e):
        pltpu.make_async_copy(w1_hbm.at[e], w1f, sem_w.at[0]).start()
        pltpu.make_async_copy(w2_hbm.at[e], w2f, sem_w.at[1]).start()

    @pl.when(b == 0)
    def _():
        ys[...] = jnp.zeros_like(ys)
        for s in range(2):
            band = pltpu.make_async_copy(ys.at[s], out_hbm.at[pl.ds((n_rows + s * R) * T, R * T)], sem_t)
            band.start()
            band.wait()
        gather(src0_ref, 0)
        fetch_weights(blk_e_ref[0])

    @pl.when(jnp.logical_and(blk_first_ref[b] == 1, b < used))
    def _():
        pltpu.make_async_copy(w1_hbm.at[0], w1f, sem_w.at[0]).wait()
        pltpu.make_async_copy(w2_hbm.at[0], w2f, sem_w.at[1]).wait()
        w1b[...] = w1f[...].astype(BF16)
        w2b[...] = w2f[...].astype(BF16)

        @pl.when(next_e_ref[b] >= 0)
        def _():
            fetch_weights(next_e_ref[b])

    def step(p):
        @pl.when(b <= used)
        def _():
            pltpu.make_async_copy(h_hbm.at[pl.ds(0, R * T)], xs.at[p], sem_g.at[p]).wait()

        @pl.when(jnp.logical_and(b >= 1, b <= used + 1))
        def _():
            pltpu.make_async_copy(ys.at[p], out_hbm.at[pl.ds(0, R * T)], sem_s.at[p]).wait()

        @pl.when(b < used)
        def _():
            gather(src_next_ref, 1 - p)
            scatter_prev(1 - p)
            hb = _dot(_load_token_tiles(xs.at[p]).astype(BF16), w1b[...]) + b1_ref[0]
            glu = jnp.minimum(hb[:, 0:D_FF], SWIGLU_LIMIT)
            lin = jnp.clip(hb[:, D_FF:2 * D_FF], -SWIGLU_LIMIT, SWIGLU_LIMIT)
            act = glu * _sigmoid(SWIGLU_ALPHA * glu) * (lin + 1.0)
            _store_token_tiles(ys.at[p], _dot(act.astype(BF16), w2b[...]) + b2_ref[0])

        @pl.when(b == used)
        def _():
            scatter_prev(1 - p)

    for p in range(2):
        pl.when(b % 2 == p)(functools.partial(step, p))


def _moe(h2, src, dst_prev, blk_e, blk_first, next_e, used, w1, b1, w2, b2):
    d = D_MODEL
    n = h2.shape[0] // TILE_ROWS
    nb = src.shape[0]
    steps = nb + 2
    cur = lambda b, *_: (jnp.minimum(b, nb), 0, 0)
    nxt = lambda b, *_: (jnp.minimum(b + 1, nb - 1), 0, 0)
    table = lambda index_map: pl.BlockSpec((1, 1, MOE_ROWS), index_map, memory_space=pltpu.SMEM)
    expert = lambda b, blk_e, *_: (blk_e[jnp.minimum(b, nb - 1)], 0, 0)
    grid_spec = pltpu.PrefetchScalarGridSpec(
        num_scalar_prefetch=4,
        grid=(steps,),
        in_specs=[table(lambda b, *_: (0, 0, 0)), table(nxt), table(cur),
                  pl.BlockSpec(memory_space=pl.ANY),
                  pl.BlockSpec(memory_space=pl.ANY),
                  pl.BlockSpec((1, 1, 2 * D_FF), expert),
                  pl.BlockSpec(memory_space=pl.ANY),
                  pl.BlockSpec((1, 1, d), expert)],
        out_specs=pl.BlockSpec(memory_space=pl.ANY),
        scratch_shapes=[pltpu.VMEM((d, 2 * D_FF), F32),
                        pltpu.VMEM((D_FF, d), F32),
                        pltpu.VMEM((d, 2 * D_FF), BF16),
                        pltpu.VMEM((D_FF, d), BF16),
                        pltpu.VMEM((2, MOE_ROWS * TILE_ROWS, LANES), F32),
                        pltpu.VMEM((2, MOE_ROWS * TILE_ROWS, LANES), F32),
                        pltpu.SemaphoreType.DMA((2,)),
                        pltpu.SemaphoreType.DMA((2,)),
                        pltpu.SemaphoreType.DMA(()),
                        pltpu.SemaphoreType.DMA((2,))])
    tab3 = lambda t: t.reshape(-1, 1, MOE_ROWS)
    pad = lambda t: jnp.concatenate([t, jnp.zeros((steps - nb,), jnp.int32)])
    return pl.pallas_call(
        functools.partial(_moe_kernel, n_rows=n * TOP_K),
        out_shape=jax.ShapeDtypeStruct(((n * TOP_K + 2 * MOE_ROWS) * TILE_ROWS, LANES), F32),
        grid_spec=grid_spec,
        compiler_params=pltpu.CompilerParams(dimension_semantics=("arbitrary",),
                                             vmem_limit_bytes=VMEM_LIMIT, has_side_effects=True),
        name="moe_experts",
    )(blk_e, pad(blk_first), next_e, used, tab3(src), tab3(src), tab3(dst_prev), h2, w1,
      b1.reshape(N_EXPERTS, 1, 2 * D_FF), w2, b2.reshape(N_EXPERTS, 1, d))


def _route(top_idx, n):
    nk = n * TOP_K
    cap = nk + N_EXPERTS * MOE_ROWS
    nb = cap // MOE_ROWS
    flat_e = top_idx.reshape(nk)
    assert nk <= 1 << 16
    keys = jnp.sort(flat_e * (1 << 16) + jnp.arange(nk, dtype=jnp.int32))
    order = keys & ((1 << 16) - 1)
    counts = jnp.sum((flat_e[:, None] == jnp.arange(N_EXPERTS, dtype=jnp.int32)[None, :])
                     .astype(jnp.int32), axis=0)
    start = jnp.cumsum(counts) - counts
    n_blk = (counts + MOE_ROWS - 1) // MOE_ROWS
    blk_end = jnp.cumsum(n_blk)
    first_blk = blk_end - n_blk
    experts = jnp.arange(N_EXPERTS, dtype=jnp.int32)
    blk = jnp.arange(nb, dtype=jnp.int32)
    blk_e = jnp.minimum(jnp.sum((blk[:, None] >= blk_end[None, :]).astype(jnp.int32), axis=1),
                        N_EXPERTS - 1)
    onehot = (blk_e[:, None] == experts[None, :]).astype(jnp.int32)
    pick = lambda t: jnp.sum(onehot * t[None, :], axis=1)
    blk_first_of_e, blk_count, blk_start = pick(first_blk), pick(counts), pick(start)
    blk_first = (blk == blk_first_of_e).astype(jnp.int32)
    blk_after = pick(blk_end)
    next_e = jnp.where(blk_after < blk_end[-1], blk_e[jnp.minimum(blk_after, nb - 1)], -1).astype(jnp.int32)
    row = jnp.arange(MOE_ROWS, dtype=jnp.int32)[None, :]
    within = (blk - blk_first_of_e)[:, None] * MOE_ROWS + row
    valid = jnp.logical_and(within < blk_count[:, None], (blk < blk_end[-1])[:, None])
    j = order[jnp.clip(blk_start[:, None] + within, 0, nk - 1)]
    src = jnp.where(valid, j // TOP_K, 0).astype(jnp.int32)
    scratch_row = nk + (blk % 2)[:, None] * MOE_ROWS + row
    dst = jnp.where(valid, (j % TOP_K) * n + j // TOP_K, scratch_row).astype(jnp.int32)
    dst_prev = jnp.concatenate([nk + MOE_ROWS + row, dst], axis=0)
    return (src, dst_prev, blk_e.astype(jnp.int32), blk_first, next_e,
            blk_end[-1:].astype(jnp.int32))


def _final_kernel(x1_ref, y0_ref, y1_ref, y2_ref, y3_ref, gate_ref, mod_ref, nf_ref, oc_ref, ol_ref,
                  *, ctx_blocks):
    gates = gate_ref[...]
    ffn = gates[:, 0:1] * _load_token_tiles(y0_ref)
    for j, ref in enumerate((y1_ref, y2_ref, y3_ref), start=1):
        ffn = ffn + gates[:, j:j + 1] * _load_token_tiles(ref)
    g2 = mod_ref[0][:, 5 * D_MODEL:6 * D_MODEL]
    x2 = x1_ref[...] + g2 * ffn
    ms = jnp.mean(x2 * x2, axis=-1, keepdims=True)
    y = x2 * lax.rsqrt(ms + NORM_EPS) * nf_ref[...]
    i = pl.program_id(0)

    @pl.when(i < ctx_blocks)
    def _():
        oc_ref[...] = y

    @pl.when(i >= ctx_blocks)
    def _():
        ol_ref[...] = y


def _final(x1, ys, gates, mods3, mod_row, norm_f, nc):
    n, d = x1.shape
    tm = TM_POST
    nb = n // tm
    nbc = nc // tm
    row = lambda i: (i, 0)
    return pl.pallas_call(
        functools.partial(_final_kernel, ctx_blocks=nbc),
        out_shape=[jax.ShapeDtypeStruct((nc, d), F32), jax.ShapeDtypeStruct((n - nc, d), F32)],
        grid=(nb,),
        in_specs=[pl.BlockSpec((tm, d), row)]
                 + [pl.BlockSpec((tm * TILE_ROWS, LANES), functools.partial(lambda j, i: (j * nb + i, 0), j))
                    for j in range(TOP_K)]
                 + [pl.BlockSpec((tm, LANES), row),
                    pl.BlockSpec((1, 1, 6 * d), lambda i: (mod_row(i), 0, 0)),
                    pl.BlockSpec((1, d), lambda i: (0, 0))],
        out_specs=_token_specs(nbc, d, tm),
        compiler_params=_cparams("arbitrary"),
        name="combine_final",
    )(x1, ys, ys, ys, ys, gates, mods3, norm_f.reshape(1, d))


def _rope_tables(t_ctx, t_lat):
    rows = np.arange(t_lat) // GRID_W
    cols = np.arange(t_lat) % GRID_W
    inv = ROPE_THETA ** (-np.arange(0, AXIS_DIM, 2, dtype=np.float32) / AXIS_DIM)
    inv = jnp.asarray(inv, F32)
    ang_r = jnp.asarray(rows, F32)[:, None] * inv[None, :]
    ang_c = jnp.asarray(cols, F32)[:, None] * inv[None, :]
    cos = jnp.concatenate([jnp.cos(ang_r)] * 2 + [jnp.cos(ang_c)] * 2, axis=1)
    sin = jnp.concatenate([-jnp.sin(ang_r), jnp.sin(ang_r), -jnp.sin(ang_c), jnp.sin(ang_c)], axis=1)
    cos = jnp.concatenate([jnp.ones((t_ctx, HEAD), F32), cos], axis=0)
    sin = jnp.concatenate([jnp.zeros((t_ctx, HEAD), F32), sin], axis=0)
    return jnp.concatenate([cos, cos], axis=1), jnp.concatenate([sin, sin], axis=1)


def kernel(x_prompt, x_sample, cache_k, cache_v, state_rwkv_fwd, state_rwkv_bwd, c, c_ctx, w_ada, b_ada, norm1, norm2, w_in, rw_w0, rw_w2, rw_a0, rw_a2, rw_g2, rw_kk, rw_ka, rw_rk, rw_ln_w, rw_ln_b, q_norm, k_norm, w_up_r, w_up_a, w_out, w_router, b_router, w_moe_in, b_moe_in, w_moe_out, b_moe_out, norm_f):
    depth = w_in.shape[0]
    assert depth == 1, "single trunk layer"
    bc, tc, d = x_prompt.shape
    bl, tl, _ = x_sample.shape
    nc, nl = bc * tc, bl * tl
    n = nc + nl
    assert d == D_MODEL and TM % tc == 0 and tl % TM == 0 and nc % tl == 0
    assert all(nc % tm == 0 and tl % tm == 0 for tm in (TM_PROJ, TM_POST))
    l = 0

    x_ctx, x_lat = x_prompt.reshape(nc, d), x_sample.reshape(nl, d)
    cond = jnp.concatenate([c_ctx[None, :], c, jnp.zeros((16 - 1 - bl, d), F32)], axis=0)
    mods3 = _adaln(cond, w_ada[l], b_ada[l]).reshape(16, 1, 6 * d)
    nbc = nc // TM
    per_seq = tl // TM
    tab_row = lambda i: jnp.where(i < nbc, 0, 1 + (i - nbc) % per_seq)
    mod_row = lambda i: jnp.where(i < nc // TM_POST, 0, 1 + (i - nc // TM_POST) // (tl // TM_POST))

    z = dict(zip([name for name, _ in IN_SPLITS],
                 _proj_in(x_ctx, x_lat, mods3, norm1[l], w_in[l].astype(BF16), tl)))

    ones_bd = jnp.asarray(np.kron(np.eye(H_R), np.ones((HEAD, HEAD))), BF16)
    pad_lo = lambda w: jnp.concatenate([w, jnp.zeros_like(w)], axis=0)
    pad_hi = lambda w: jnp.concatenate([jnp.zeros_like(w), w], axis=0)
    prep_p = dict(
        w0=rw_w0[l], a0=rw_a0[l],
        w2=jnp.stack([pad_lo(rw_w2[l, 0]), pad_hi(rw_w2[l, 1])]).astype(BF16),
        a2=jnp.stack([pad_lo(rw_a2[l, 0]), pad_hi(rw_a2[l, 1])]).astype(BF16),
        g2=rw_g2[l].astype(BF16),
        kkw=rw_kk[l].reshape(1, D_R), kaw=rw_ka[l].reshape(1, D_R), rk=rw_rk[l].reshape(1, D_R),
        qn=jnp.tile(q_norm[l], N_HEADS).reshape(1, D_A), kn=jnp.tile(k_norm[l], N_KV).reshape(1, D_KV),
        ones=ones_bd,
        tri=jnp.asarray(np.stack([np.kron(np.eye(TM // CHUNK), np.tril(np.ones((CHUNK, CHUNK)))),
                                  np.kron(np.eye(TM // CHUNK), np.triu(np.ones((CHUNK, CHUNK))))]), BF16),
        chunk_ones=jnp.asarray(np.kron(np.eye(TM // CHUNK), np.ones((CHUNK, CHUNK))), BF16))
    cos_tab, sin_tab = _rope_tables(TM, tl)
    sc, gt, vb, bonus, g, qr, kn, kr = _prep(z, cos_tab, sin_tab, tab_row, prep_p)

    yf_c, yb_c, s_fin = _rwkv_scan(sc, gt, vb, jnp.zeros((2, bc, H_R, HEAD, HEAD), F32), 0, bc, tc)
    yf_l, yb_l, _ = _rwkv_scan(sc, gt, vb, jnp.stack([state_rwkv_fwd[:, l], state_rwkv_bwd[:, l]]),
                               nc, bl, tl)
    y_f = (yf_c.reshape(nc, D_R), yf_l.reshape(nl, D_R))
    y_b = (yb_c.reshape(nc, D_R), yb_l.reshape(nl, D_R))

    ya_ctx = _attention(qr, kr, z["va"], 0, bc, tc)
    ya_lat = _attention(qr, kr, z["va"], nc, bl, tl,
                        cache_k[:, l].reshape(bl, -1, D_KV), cache_v[:, l].reshape(bl, -1, D_KV))

    post_p = dict(
        lnw=rw_ln_w[l].reshape(1, D_R), lnb=rw_ln_b[l].reshape(1, D_R),
        wur=w_up_r[l].astype(BF16), wua=w_up_a[l].astype(BF16), wo=w_out[l].astype(BF16),
        n2=norm2[l].reshape(1, d),
        wr=jnp.pad(w_router[l], ((0, 0), (0, LANES - N_EXPERTS))),
        br=jnp.pad(b_router[l], (0, LANES - N_EXPERTS)).reshape(1, LANES),
        ones=ones_bd)
    x1, h2, top_idx, gates = _post(y_f, y_b, bonus, g, ya_ctx, ya_lat, z["gate_r"], z["gate_a"],
                                   x_ctx, x_lat, mods3, mod_row, post_p)

    src, dst, blk_e, blk_first, next_e, used = _route(top_idx[:, :TOP_K], n)
    ys = _moe(h2, src, dst, blk_e, blk_first, next_e, used,
              w_moe_in[l], b_moe_in[l], w_moe_out[l], b_moe_out[l])
    y_ctx, y_lat = _final(x1, ys, gates, mods3, mod_row, norm_f, nc)

    y_prompt = y_ctx.reshape(bc, tc, d)
    y_sample = y_lat.reshape(bl, tl, d)
    new_cache_k = kn[:nc].reshape(bc, 1, tc, N_KV, HEAD)
    new_cache_v = z["va"][:nc].reshape(bc, 1, tc, N_KV, HEAD)
    new_state_fwd = s_fin[0][:, None]
    new_state_bwd = s_fin[1][:, None]
    return (y_prompt, y_sample, new_cache_k, new_cache_v, new_state_fwd, new_state_bwd)
```
